```python
import math
import jax, jax.numpy as jnp
from jax import lax
import numpy as np

D_MODEL = 1024
BATCH = 16
SEQ = 256
DEPTH = 2
DEC_BATCH = 2
DEC_SEQ = 1024
PAST_LEN = 512

GRID_W = 64
HEAD_DIM = 64
MIX_A = D_MODEL // 2
MIX_B = D_MODEL // 4
MIX_C = D_MODEL - MIX_A - MIX_B
N_HEADS_A = MIX_A // (2 * HEAD_DIM)
KEY_DIM_B = HEAD_DIM
VAL_DIM_B = HEAD_DIM
N_HEADS_B = MIX_B // VAL_DIM_B
N_HEADS_C = MIX_C // HEAD_DIM
N_KV_C = N_HEADS_C // 2
GQA_GROUP = N_HEADS_C // N_KV_C
D_FF = 4 * D_MODEL
Q_BLOCK = 128
CHUNK = 64
ROPE_THETA = 10000.0
ROPE_AXIS_PAIRS = HEAD_DIM // 4
ALPHA = (2 * DEPTH) ** 0.25
BETA = (8 * DEPTH) ** -0.25
LN_EPS = 1e-6
RMS_EPS = 1e-6
F_MIN = 1e-6
PROJ_SIZES = (MIX_A, MIX_A, MIX_A,
              N_HEADS_B * KEY_DIM_B, N_HEADS_B * KEY_DIM_B,
              N_HEADS_B * KEY_DIM_B,
              N_HEADS_B * VAL_DIM_B, N_HEADS_B * VAL_DIM_B,
              MIX_C, N_KV_C * HEAD_DIM, N_KV_C * HEAD_DIM)
N_IN = sum(PROJ_SIZES)

kernel_name = 'hybrid_diffusion_diffattn_hgrn2_gqa_step'


def _layernorm(x, g, b):
    xf = x.astype(jnp.float32)
    mu = jnp.mean(xf, axis=-1, keepdims=True)
    var = jnp.mean(jnp.square(xf - mu), axis=-1, keepdims=True)
    return ((xf - mu) * lax.rsqrt(var + LN_EPS)).astype(x.dtype) * g + b


def _rmsnorm(x, g):
    xf = x.astype(jnp.float32)
    y = xf * lax.rsqrt(jnp.mean(jnp.square(xf), axis=-1, keepdims=True) + RMS_EPS)
    return y.astype(x.dtype) * g


def _axial_rope_tables(n_tokens):
    rows = n_tokens // GRID_W
    row = jnp.repeat(jnp.arange(rows, dtype=jnp.float32), GRID_W)
    col = (jnp.arange(rows * GRID_W) % GRID_W).astype(jnp.float32)
    inv = ROPE_THETA ** (-jnp.arange(ROPE_AXIS_PAIRS, dtype=jnp.float32) / ROPE_AXIS_PAIRS)
    ang = jnp.concatenate([row[:, None] * inv, col[:, None] * inv], axis=-1)
    return jnp.cos(ang), jnp.sin(ang)


def _apply_rope(x, cos, sin):
    shp = x.shape
    bshape = (1, cos.shape[0]) + (1,) * (x.ndim - 3) + (cos.shape[1],)
    c, s = cos.reshape(bshape), sin.reshape(bshape)
    xf = x.astype(jnp.float32).reshape(shp[:-1] + (HEAD_DIM // 2, 2))
    x1, x2 = xf[..., 0], xf[..., 1]
    out = jnp.stack([x1 * c - x2 * s, x1 * s + x2 * c], axis=-1)
    return out.reshape(shp).astype(x.dtype)


def _query_block_sweep(fn, q):
    B, T = q.shape[:2]
    nb = T // Q_BLOCK
    qb = jnp.moveaxis(q.reshape((B, nb, Q_BLOCK) + q.shape[2:]), 1, 0)
    out = jnp.moveaxis(lax.map(fn, qb), 0, 1)
    return out.reshape((B, T) + out.shape[3:])


def _diff_attn_block(qb, k, v, lam):
    s = jnp.einsum('bqhmd,bkhmd->bhmqk', qb, k).astype(jnp.float32) * HEAD_DIM ** -0.5
    p = jax.nn.softmax(s, axis=-1)
    w = p[:, :, 0] - lam * p[:, :, 1]
    return jnp.einsum('bhqk,bkhe->bqhe', w.astype(v.dtype), v)


def _gqa_block(qb, k, v):
    s = jnp.einsum('bqngd,bknd->bngqk', qb, k).astype(jnp.float32) * HEAD_DIM ** -0.5
    p = jax.nn.softmax(s, axis=-1).astype(v.dtype)
    return jnp.einsum('bngqk,bknd->bqngd', p, v)


def _log_forget(x, lb):
    lb = lb.astype(jnp.float32)
    f = lb + (1.0 - lb) * jax.nn.sigmoid(x.astype(jnp.float32))
    return jnp.log(jnp.maximum(f, F_MIN))


def _hgrn2_scan(q, k, v, g, s0):
    B, T, H, DK = q.shape
    DV = v.shape[-1]
    nc = T // CHUNK

    def chunks(a):
        a = a.astype(jnp.float32).reshape((B, nc, CHUNK) + a.shape[2:])
        return jnp.moveaxis(a, 1, 0)

    causal = jnp.tril(jnp.ones((CHUNK, CHUNK), dtype=bool))[None, :, :, None, None]

    def step(S, xs):
        qc, kc, vc, gc = xs
        b = jnp.cumsum(gc, axis=1)
        o_inter = jnp.einsum('bthk,bhkv->bthv', qc * jnp.exp(b), S)
        diff = b[:, :, None] - b[:, None, :]
        decay = jnp.where(causal, jnp.exp(jnp.where(causal, diff, 0.0)), 0.0)
        attn = jnp.einsum('bthk,bshk,btshk->bths', qc, kc, decay)
        o_intra = jnp.einsum('bths,bshv->bthv', attn, vc)
        b_end = b[:, -1]
        S = jnp.exp(b_end)[..., None] * S + jnp.einsum(
            'bshk,bshv->bhkv', kc * jnp.exp(b_end[:, None] - b), vc)
        return S, o_inter + o_intra

    s_fin, o = lax.scan(step, s0.astype(jnp.float32), (chunks(q), chunks(k), chunks(v), chunks(g)))
    return jnp.moveaxis(o, 0, 1).reshape(B, T, H, DV), s_fin


def _token_mixer(h, lp, li, rope, ctx):
    B, T, _ = h.shape
    z = h @ lp['w_in']
    qa, ka, va, qb, fb_f, fb_b, ib, gb, qc, kc, vc = jnp.split(
        z, np.cumsum(PROJ_SIZES)[:-1].tolist(), axis=-1)
    qa = qa.reshape(B, T, N_HEADS_A, 2, HEAD_DIM)
    ka = ka.reshape(B, T, N_HEADS_A, 2, HEAD_DIM)
    va = va.reshape(B, T, N_HEADS_A, 2 * HEAD_DIM)
    qc = _rmsnorm(qc.reshape(B, T, N_HEADS_C, HEAD_DIM), lp['qnorm_g'])
    kc = _rmsnorm(kc.reshape(B, T, N_KV_C, HEAD_DIM), lp['knorm_g'])
    vc = vc.reshape(B, T, N_KV_C, HEAD_DIM)
    if ctx is None:
        ka_all, va_all, kc_all, vc_all = ka, va, kc, vc
        s0_f = s0_b = jnp.zeros((B, N_HEADS_B, KEY_DIM_B, VAL_DIM_B), jnp.float32)
    else:
        cos, sin = rope
        qa = _apply_rope(qa, cos, sin)
        qc = _apply_rope(qc, cos, sin)
        ka_all = jnp.concatenate([_apply_rope(ka, cos, sin), ctx[0]], axis=1)
        va_all = jnp.concatenate([va, ctx[1]], axis=1)
        kc_all = jnp.concatenate([_apply_rope(kc, cos, sin), ctx[2]], axis=1)
        vc_all = jnp.concatenate([vc, ctx[3]], axis=1)
        s0_f, s0_b = ctx[4], ctx[5]

    lam_init = 0.8 - 0.6 * math.exp(-0.3 * li)
    lam = (jnp.exp(jnp.sum(lp['lam_q1'].astype(jnp.float32) * lp['lam_k1'].astype(jnp.float32)))
           - jnp.exp(jnp.sum(lp['lam_q2'].astype(jnp.float32) * lp['lam_k2'].astype(jnp.float32)))
           + lam_init)
    oa = _query_block_sweep(lambda q: _diff_attn_block(q, ka_all, va_all, lam), qa)
    oa = _rmsnorm(oa, lp['subln_g']) * (1.0 - lam_init)

    g_f = _log_forget(fb_f, lp['lb_fwd']).reshape(B, T, N_HEADS_B, KEY_DIM_B)
    g_b = _log_forget(fb_b, lp['lb_bwd']).reshape(B, T, N_HEADS_B, KEY_DIM_B)
    qb = jax.nn.silu(qb).reshape(B, T, N_HEADS_B, KEY_DIM_B)
    vb = ib.reshape(B, T, N_HEADS_B, VAL_DIM_B)
    flip = lambda a: jnp.flip(a, axis=1)
    o_f, s_f = _hgrn2_scan(qb, -jnp.expm1(g_f), vb, g_f, s0_f)
    o_b, s_b = _hgrn2_scan(flip(qb), -jnp.expm1(flip(g_b)), flip(vb), flip(g_b), s0_b)
    ob = (o_f + flip(o_b)).astype(h.dtype)
    ob = _rmsnorm(ob, lp['gnorm_g']) * jax.nn.silu(gb.reshape(B, T, N_HEADS_B, VAL_DIM_B))

    oc = _query_block_sweep(lambda q: _gqa_block(q, kc_all, vc_all),
                            qc.reshape(B, T, N_KV_C, GQA_GROUP, HEAD_DIM))

    mixed = jnp.concatenate([oa.reshape(B, T, MIX_A), ob.reshape(B, T, MIX_B),
                             oc.reshape(B, T, MIX_C)], axis=-1)
    own = (ka, va, kc, vc, s_f.astype(h.dtype), s_b.astype(h.dtype))
    return mixed @ lp['w_out'], own


def _layer(x, mod, lp, li, rope, ctx):
    sh1, sc1, g1, sh2, sc2, g2 = jnp.split(mod, 6, axis=-1)
    m, own = _token_mixer(x * (1.0 + sc1) + sh1, lp, li, rope, ctx)
    x = _layernorm(ALPHA * x + g1 * m, lp['ln1_g'], lp['ln1_b'])
    hid = jnp.square(jax.nn.relu((x * (1.0 + sc2) + sh2) @ lp['w_ff1']))
    x = _layernorm(ALPHA * x + g2 * (hid @ lp['w_ff2']), lp['ln2_g'], lp['ln2_b'])
    return x, own


def setup_inputs(seed: int = 0) -> dict:
    key = jax.random.key(seed)
    ks = jax.random.split(key, 32)
    f32 = jnp.float32
    nrm = lambda k, shape, s=1.0: s * jax.random.normal(k, shape, f32)
    return {
        'x_prompt': nrm(ks[0], (BATCH, SEQ, D_MODEL)),
        'x_sample': nrm(ks[1], (DEC_BATCH, DEC_SEQ, D_MODEL)),
        'cache_a_k': nrm(ks[2], (DEC_BATCH, DEPTH, PAST_LEN, N_HEADS_A, 2, HEAD_DIM)),
        'cache_a_v': nrm(ks[3], (DEC_BATCH, DEPTH, PAST_LEN, N_HEADS_A, 2 * HEAD_DIM)),
        'cache_c_k': nrm(ks[4], (DEC_BATCH, DEPTH, PAST_LEN, N_KV_C, HEAD_DIM)),
        'cache_c_v': nrm(ks[5], (DEC_BATCH, DEPTH, PAST_LEN, N_KV_C, HEAD_DIM)),
        'state_b_fwd': nrm(ks[6], (DEC_BATCH, DEPTH, N_HEADS_B, KEY_DIM_B, VAL_DIM_B)),
        'state_b_bwd': nrm(ks[7], (DEC_BATCH, DEPTH, N_HEADS_B, KEY_DIM_B, VAL_DIM_B)),
        'c': nrm(ks[8], (DEC_BATCH, D_MODEL)),
        'c_ctx': nrm(ks[9], (D_MODEL,)),
        'w_ada': nrm(ks[10], (DEPTH, D_MODEL, 6 * D_MODEL), 0.5 * D_MODEL ** -0.5),
        'b_ada': nrm(ks[11], (DEPTH, 6 * D_MODEL), 0.02),
        'w_in': nrm(ks[12], (DEPTH, D_MODEL, N_IN), D_MODEL ** -0.5),
        'w_out': nrm(ks[13], (DEPTH, MIX_A + MIX_B + MIX_C, D_MODEL), BETA * (MIX_A + MIX_B + MIX_C) ** -0.5),
        'lam_q1': nrm(ks[14], (DEPTH, HEAD_DIM), 0.1),
        'lam_k1': nrm(ks[15], (DEPTH, HEAD_DIM), 0.1),
        'lam_q2': nrm(ks[16], (DEPTH, HEAD_DIM), 0.1),
        'lam_k2': nrm(ks[17], (DEPTH, HEAD_DIM), 0.1),
        'subln_g': 1.0 + nrm(ks[18], (DEPTH, 2 * HEAD_DIM), 0.02),
        'lb_logits_fwd': nrm(ks[19], (DEPTH, N_HEADS_B * KEY_DIM_B), 0.5),
        'lb_logits_bwd': nrm(ks[20], (DEPTH, N_HEADS_B * KEY_DIM_B), 0.5),
        'gnorm_g': 1.0 + nrm(ks[21], (DEPTH, VAL_DIM_B), 0.02),
        'qnorm_g': 1.0 + nrm(ks[22], (DEPTH, HEAD_DIM), 0.02),
        'knorm_g': 1.0 + nrm(ks[23], (DEPTH, HEAD_DIM), 0.02),
        'ln1_g': 1.0 + nrm(ks[24], (DEPTH, D_MODEL), 0.02),
        'ln1_b': nrm(ks[25], (DEPTH, D_MODEL), 0.02),
        'ln2_g': 1.0 + nrm(ks[26], (DEPTH, D_MODEL), 0.02),
        'ln2_b': nrm(ks[27], (DEPTH, D_MODEL), 0.02),
        'w_ff1': nrm(ks[28], (DEPTH, D_MODEL, D_FF), D_MODEL ** -0.5),
        'w_ff2': nrm(ks[29], (DEPTH, D_FF, D_MODEL), BETA * D_FF ** -0.5),
    }


def reference(x_prompt, x_sample, cache_a_k, cache_a_v, cache_c_k, cache_c_v, state_b_fwd,
              state_b_bwd, c, c_ctx, w_ada, b_ada, w_in, w_out, lam_q1, lam_k1, lam_q2, lam_k2,
              subln_g, lb_logits_fwd, lb_logits_bwd, gnorm_g, qnorm_g, knorm_g, ln1_g, ln1_b,
              ln2_g, ln2_b, w_ff1, w_ff2):
    sm_f = jax.nn.softmax(lb_logits_fwd.astype(jnp.float32), axis=0)
    sm_b = jax.nn.softmax(lb_logits_bwd.astype(jnp.float32), axis=0)
    lb_f = jnp.cumsum(sm_f, axis=0) - sm_f[0]
    lb_b = jnp.cumsum(sm_b, axis=0) - sm_b[0]
    rope = _axial_rope_tables(x_sample.shape[1])

    y_prompt, y_sample = x_prompt, x_sample
    ctx_layers = []
    for li in range(DEPTH):
        lp = {'w_in': w_in[li], 'w_out': w_out[li], 'lam_q1': lam_q1[li], 'lam_k1': lam_k1[li],
              'lam_q2': lam_q2[li], 'lam_k2': lam_k2[li], 'subln_g': subln_g[li],
              'lb_fwd': lb_f[li], 'lb_bwd': lb_b[li], 'gnorm_g': gnorm_g[li],
              'qnorm_g': qnorm_g[li], 'knorm_g': knorm_g[li], 'ln1_g': ln1_g[li],
              'ln1_b': ln1_b[li], 'ln2_g': ln2_g[li], 'ln2_b': ln2_b[li],
              'w_ff1': w_ff1[li], 'w_ff2': w_ff2[li]}
        mod_ctx = (jax.nn.silu(c_ctx) @ w_ada[li] + b_ada[li])[None, None, :]
        y_prompt, own = _layer(y_prompt, mod_ctx, lp, li, None, None)
        ctx_layers.append(own)
        mod_lat = (jax.nn.silu(c) @ w_ada[li] + b_ada[li])[:, None, :]
        cache = (cache_a_k[:, li], cache_a_v[:, li], cache_c_k[:, li], cache_c_v[:, li],
                 state_b_fwd[:, li], state_b_bwd[:, li])
        y_sample, _ = _layer(y_sample, mod_lat, lp, li, rope, cache)

    new_a_k = jnp.stack([t[0] for t in ctx_layers], axis=1)
    new_a_v = jnp.stack([t[1] for t in ctx_layers], axis=1)
    new_c_k = jnp.stack([t[2] for t in ctx_layers], axis=1)
    new_c_v = jnp.stack([t[3] for t in ctx_layers], axis=1)
    new_state_fwd = jnp.stack([t[4] for t in ctx_layers], axis=1)
    new_state_bwd = jnp.stack([t[5] for t in ctx_layers], axis=1)
    return (y_prompt, y_sample, new_a_k, new_a_v, new_c_k, new_c_v, new_state_fwd, new_state_bwd)
```

```python
import functools
import math

import jax
import jax.numpy as jnp
from jax import lax
from jax.experimental import pallas as pl
from jax.experimental.pallas import tpu as pltpu

GRID_W = 64
HEAD_DIM = 64
ROPE_THETA = 10000.0
LN_EPS = 1e-6
RMS_EPS = 1e-6
F_MIN = 1e-6
CHUNK = 64
DIAG_BLOCK = 8
LANES = 128
ROW_TILE = 256
VMEM_LIMIT = 56 * 1024 * 1024

F32 = jnp.float32
BF16 = jnp.bfloat16
NT = (((1,), (1,)), ((), ()))
TN = (((0,), (0,)), ((), ()))


def _params(n_grid):
    return pltpu.CompilerParams(dimension_semantics=("arbitrary",) * n_grid,
                                vmem_limit_bytes=VMEM_LIMIT)


def _dot(a, b):
    return jnp.dot(a, b, preferred_element_type=F32)


def _split_dot(a, b_bf16, passes, dims=None):
    acc = None
    rem = a
    for _ in range(passes):
        piece = rem.astype(BF16)
        rem = rem - piece.astype(F32)
        part = (_dot(piece, b_bf16) if dims is None
                else lax.dot_general(piece, b_bf16, dims, preferred_element_type=F32))
        acc = part if acc is None else acc + part
    return acc


def _group_ones(n, group):
    r = lax.broadcasted_iota(jnp.int32, (n, n), 0) // group
    c = lax.broadcasted_iota(jnp.int32, (n, n), 1) // group
    return (r == c).astype(BF16)


def _group_rms(x, g_row, group):
    n = x.shape[-1]
    ms = _split_dot(x * x, _group_ones(n, group), 2) * (1.0 / group)
    return x * lax.rsqrt(ms + RMS_EPS) * g_row


def _pair_swap(x):
    lane = lax.broadcasted_iota(jnp.int32, x.shape, 1)
    return jnp.where(lane % 2 == 0, pltpu.roll(x, LANES - 1, 1), pltpu.roll(x, 1, 1))


def _rope(x, cos, sin):
    blocks = []
    for j in range(x.shape[-1] // LANES):
        blk = x[:, j * LANES:(j + 1) * LANES]
        blocks.append(blk * cos + _pair_swap(blk) * sin)
    return blocks[0] if len(blocks) == 1 else jnp.concatenate(blocks, axis=-1)


def _silu(x):
    return x * jax.nn.sigmoid(x)


def _layernorm(x, g, b):
    mu = jnp.mean(x, axis=-1, keepdims=True)
    xc = x - mu
    var = jnp.mean(xc * xc, axis=-1, keepdims=True)
    return xc * lax.rsqrt(var + LN_EPS) * g + b


def _mod_kernel(c_ref, w_ref, b_ref, o_ref):
    s = _silu(c_ref[...]).astype(BF16)
    o_ref[0] = _dot(s, w_ref[0].astype(BF16)) + b_ref[0]


def _modulation(cond, w_ada, b_ada):
    depth, d, n = w_ada.shape
    tn = 1536
    rows = cond.shape[0]
    return pl.pallas_call(
        _mod_kernel,
        grid=(depth, n // tn),
        in_specs=[pl.BlockSpec((rows, d), lambda l, j: (0, 0)),
                  pl.BlockSpec((1, d, tn), lambda l, j: (l, 0, j)),
                  pl.BlockSpec((1, 1, tn), lambda l, j: (l, 0, j))],
        out_specs=pl.BlockSpec((1, rows, tn), lambda l, j: (l, 0, j)),
        out_shape=jax.ShapeDtypeStruct((depth, rows, n), F32),
        compiler_params=_params(2),
        name="adaln_modulation",
    )(cond, w_ada, b_ada.reshape(depth, 1, n))


def _in_proj_kernel(*refs, li, d, latent):
    if latent:
        (x_ref, mod_ref, w_ref, lbf_ref, lbb_ref, qn_ref, kn_ref, cos_ref, sin_ref,
         qa_o, ka_o, va_o, hq_o, ff_o, fb_o, hv_o, hg_o, qc_o, kc_o, vc_o) = refs
        cos, sin = cos_ref[...], sin_ref[...]
    else:
        (x_ref, mod_ref, w_ref, lbf_ref, lbb_ref, qn_ref, kn_ref,
         qa_o, ka_o, va_o, hq_o, ff_o, fb_o, hv_o, hg_o, qc_o, kc_o, vc_o) = refs
    mix_a, mix_b, mix_c = d // 2, d // 4, d // 4
    kv_c = mix_c // 2
    scale = HEAD_DIM ** -0.5

    shift = mod_ref[0, :, 0:d]
    gain = mod_ref[0, :, d:2 * d]
    h = (x_ref[...] * (1.0 + gain) + shift).astype(BF16)

    def proj(start, width):
        return _dot(h, w_ref[:, start:start + width])

    off = 0
    qa = proj(off, mix_a); off += mix_a
    ka = proj(off, mix_a); off += mix_a
    va = proj(off, mix_a); off += mix_a
    if latent:
        qa = _rope(qa, cos, sin)
        ka = _rope(ka, cos, sin)
    qa_o[...] = (qa * scale).astype(qa_o.dtype)
    ka_o[0, 0] = ka.astype(ka_o.dtype)
    va_o[0, 0] = va.astype(va_o.dtype)

    def lower_bound(ref):
        logits = ref[...]
        e = jnp.exp(logits - jnp.max(logits, axis=0, keepdims=True))
        sm = e / jnp.sum(e, axis=0, keepdims=True)
        return jnp.sum(sm[0:li + 1], axis=0, keepdims=True) - sm[0:1]

    def forget(x, lb):
        return jnp.maximum(lb + (1.0 - lb) * jax.nn.sigmoid(x), F_MIN)

    hq_o[0] = _silu(proj(off, mix_b)); off += mix_b
    ff_o[0] = forget(proj(off, mix_b), lower_bound(lbf_ref)); off += mix_b
    fb_o[0] = forget(proj(off, mix_b), lower_bound(lbb_ref)); off += mix_b
    hv_o[0] = proj(off, mix_b); off += mix_b
    hg_o[0] = _silu(proj(off, mix_b)); off += mix_b

    qc = _group_rms(proj(off, mix_c), qn_ref[...], HEAD_DIM); off += mix_c
    kc = _group_rms(proj(off, kv_c), kn_ref[...], HEAD_DIM); off += kv_c
    vc = proj(off, kv_c)
    if latent:
        kc = _rope(kc, cos, sin)
        qc = _rope(qc, cos, sin)
    kc_o[0, 0] = kc.astype(kc_o.dtype)
    vc_o[0, 0] = vc.astype(vc_o.dtype)
    qc = qc * scale
    lane = lax.broadcasted_iota(jnp.int32, (1, LANES), 1)
    for n in range(2):
        blk = qc[:, n * LANES:(n + 1) * LANES]
        in_half = (lane // HEAD_DIM) == n
        for g in range(2):
            src = blk if g == n else pltpu.roll(blk, HEAD_DIM, 1)
            hc = 2 * n + g
            qc_o[:, hc * LANES:(hc + 1) * LANES] = jnp.where(in_half, src, 0.0).astype(qc_o.dtype)


def _in_proj(x, mod, mod_row, w_in, lb_f, lb_b, qn, kn, rope, *, li):
    bsz, t, d = x.shape
    latent = rope is not None
    n_in = w_in.shape[-1]
    tiles = t // ROW_TILE
    mix_a, mix_b, mix_c = d // 2, d // 4, d // 4
    kv_c = mix_c // 2
    kdt = BF16 if latent else F32
    x2 = x.reshape(bsz * t, d)

    row = lambda i: (i, 0)
    brow = lambda i: (i // tiles, i % tiles, 0)
    krow = lambda i: (i // tiles, 0, i % tiles, 0)
    const = lambda i: (0, 0)
    in_specs = [pl.BlockSpec((ROW_TILE, d), row),
                pl.BlockSpec((1, 1, mod.shape[-1]), lambda i: (mod_row(i // tiles), 0, 0)),
                pl.BlockSpec((d, n_in), const),
                pl.BlockSpec(lb_f.shape, const), pl.BlockSpec(lb_b.shape, const),
                pl.BlockSpec(qn.shape, const), pl.BlockSpec(kn.shape, const)]
    args = [x2, mod, w_in, lb_f, lb_b, qn, kn]
    if latent:
        in_specs += [pl.BlockSpec((ROW_TILE, LANES), lambda i: (i % tiles, 0))] * 2
        args += list(rope)

    def hspec():
        return pl.BlockSpec((1, ROW_TILE, mix_b), brow)

    out_specs = [pl.BlockSpec((ROW_TILE, mix_a), row),
                 pl.BlockSpec((1, 1, ROW_TILE, mix_a), krow),
                 pl.BlockSpec((1, 1, ROW_TILE, mix_a), krow),
                 hspec(), hspec(), hspec(), hspec(), hspec(),
                 pl.BlockSpec((ROW_TILE, 2 * mix_c), row),
                 pl.BlockSpec((1, 1, ROW_TILE, kv_c), krow),
                 pl.BlockSpec((1, 1, ROW_TILE, kv_c), krow)]
    out_shape = [jax.ShapeDtypeStruct((bsz * t, mix_a), BF16),
                 jax.ShapeDtypeStruct((bsz, 1, t, mix_a), kdt),
                 jax.ShapeDtypeStruct((bsz, 1, t, mix_a), kdt)]
    out_shape += [jax.ShapeDtypeStruct((bsz, t, mix_b), F32)] * 5
    out_shape += [jax.ShapeDtypeStruct((bsz * t, 2 * mix_c), BF16),
                  jax.ShapeDtypeStruct((bsz, 1, t, kv_c), kdt),
                  jax.ShapeDtypeStruct((bsz, 1, t, kv_c), kdt)]
    return pl.pallas_call(
        functools.partial(_in_proj_kernel, li=li, d=d, latent=latent),
        grid=(bsz * tiles,),
        in_specs=in_specs, out_specs=out_specs, out_shape=out_shape,
        compiler_params=_params(1),
        name="in_proj_latent" if latent else "in_proj_context",
    )(*args)


def _scores(q, keys):
    return [lax.dot_general(q, k, NT, preferred_element_type=F32) for k in keys]


def _softmax_parts(scores):
    m = functools.reduce(jnp.maximum, [jnp.max(s, axis=-1, keepdims=True) for s in scores])
    es = [jnp.exp(s - m) for s in scores]
    denom = functools.reduce(lambda a, b: a + b, [jnp.sum(e, axis=-1, keepdims=True) for e in es])
    return es, 1.0 / denom


def _diff_attn_kernel(*refs, li, cached):
    if cached:
        (q_ref, k_ref, v_ref, ck_ref, cv_ref, lq1, lk1, lq2, lk2, sub_ref, o_ref) = refs
    else:
        (q_ref, k_ref, v_ref, lq1, lk1, lq2, lk2, sub_ref, o_ref) = refs
    lam_init = 0.8 - 0.6 * math.exp(-0.3 * li)

    def lam_term(a, b):
        return jnp.exp(jnp.sum(a[li:li + 1, :] * b[li:li + 1, :], axis=-1, keepdims=True))

    lam = lam_term(lq1, lk1) - lam_term(lq2, lk2) + lam_init

    q = q_ref[...]
    lane = lax.broadcasted_iota(jnp.int32, (1, LANES), 1)
    keys = [k_ref[0, 0].astype(BF16)]
    vals = [v_ref[0, 0].astype(BF16)]
    if cached:
        keys.append(ck_ref[0, 0].astype(BF16))
        vals.append(cv_ref[0, 0].astype(BF16))
    zero = jnp.zeros_like(q)
    e0, r0 = _softmax_parts(_scores(jnp.where(lane < HEAD_DIM, q, zero), keys))
    e1, r1 = _softmax_parts(_scores(jnp.where(lane >= HEAD_DIM, q, zero), keys))
    r1 = r1 * lam
    o = None
    for a, b, v in zip(e0, e1, vals):
        part = _dot((a * r0 - b * r1).astype(BF16), v)
        o = part if o is None else o + part
    ms = jnp.mean(o * o, axis=-1, keepdims=True)
    o = o * lax.rsqrt(ms + RMS_EPS) * sub_ref[li:li + 1, :] * (1.0 - lam_init)
    o_ref[...] = o.astype(o_ref.dtype)


def _diff_attn(q, k, v, cache, lam, subln, *, li, bsz, tq):
    t = k.shape[2]
    width = q.shape[-1]
    heads = width // LANES
    nq = t // tq
    cached = cache is not None
    in_specs = [pl.BlockSpec((tq, LANES), lambda b, h, i: (b * nq + i, h)),
                pl.BlockSpec((1, 1, t, LANES), lambda b, h, i: (b, 0, 0, h)),
                pl.BlockSpec((1, 1, t, LANES), lambda b, h, i: (b, 0, 0, h))]
    args = [q, k, v]
    if cached:
        p = cache[0].shape[2]
        in_specs += [pl.BlockSpec((1, 1, p, LANES), lambda b, h, i: (b, li, 0, h))] * 2
        args += list(cache)
    in_specs += [pl.BlockSpec(a.shape, lambda b, h, i: (0, 0)) for a in (*lam, subln)]
    args += [*lam, subln]
    return pl.pallas_call(
        functools.partial(_diff_attn_kernel, li=li, cached=cached),
        grid=(bsz, heads, nq),
        in_specs=in_specs,
        out_specs=pl.BlockSpec((tq, LANES), lambda b, h, i: (b * nq + i, h)),
        out_shape=jax.ShapeDtypeStruct((bsz * t, width), BF16),
        compiler_params=_params(3),
        name="diff_attn_latent" if cached else "diff_attn_context",
    )(*args)


def _gqa_kernel(*refs, li, cached):
    if cached:
        q_ref, k_ref, v_ref, ck_ref, cv_ref, o_ref = refs
    else:
        q_ref, k_ref, v_ref, o_ref = refs
    lane = lax.broadcasted_iota(jnp.int32, (1, LANES), 1)
    keys = [k_ref[0, 0].astype(BF16)]
    vals = [v_ref[0, 0].astype(BF16)]
    if cached:
        keys.append(ck_ref[0, 0].astype(BF16))
        vals.append(cv_ref[0, 0].astype(BF16))
    for n in range(2):
        in_half = (lane // HEAD_DIM) == n
        vals_n = [jnp.where(in_half, v, jnp.zeros_like(v)) for v in vals]
        acc = None
        for g in range(2):
            hc = 2 * n + g
            es, r = _softmax_parts(_scores(q_ref[:, hc * LANES:(hc + 1) * LANES], keys))
            o = None
            for e, v in zip(es, vals_n):
                part = _dot((e * r).astype(BF16), v)
                o = part if o is None else o + part
            if g != n:
                o = pltpu.roll(o, HEAD_DIM, 1)
            acc = o if acc is None else acc + o
        o_ref[:, n * LANES:(n + 1) * LANES] = acc.astype(o_ref.dtype)


def _gqa(q, k, v, cache, *, li, bsz, tq):
    t = k.shape[2]
    kvw = k.shape[-1]
    nq = t // tq
    cached = cache is not None
    in_specs = [pl.BlockSpec((tq, q.shape[-1]), lambda b, i: (b * nq + i, 0)),
                pl.BlockSpec((1, 1, t, kvw), lambda b, i: (b, 0, 0, 0)),
                pl.BlockSpec((1, 1, t, kvw), lambda b, i: (b, 0, 0, 0))]
    args = [q, k, v]
    if cached:
        p = cache[0].shape[2]
        in_specs += [pl.BlockSpec((1, 1, p, kvw), lambda b, i: (b, li, 0, 0))] * 2
        args += list(cache)
    return pl.pallas_call(
        functools.partial(_gqa_kernel, li=li, cached=cached),
        grid=(bsz, nq),
        in_specs=in_specs,
        out_specs=pl.BlockSpec((tq, 2 * kvw), lambda b, i: (b * nq + i, 0)),
        out_shape=jax.ShapeDtypeStruct((bsz * t, 2 * kvw), BF16),
        compiler_params=_params(2),
        name="gqa_latent" if cached else "gqa_context",
    )(*args)


def _head_masks(width):
    lane_head = lax.broadcasted_iota(jnp.int32, (1, width), 1) // HEAD_DIM
    return [lane_head == h for h in range(width // HEAD_DIM)]


def _stack_heads(x, masks):
    return jnp.concatenate([jnp.where(m, x, jnp.zeros_like(x)) for m in masks], axis=0)


def _block_diag_mask(width):
    r = lax.broadcasted_iota(jnp.int32, (width, width), 0) // HEAD_DIM
    c = lax.broadcasted_iota(jnp.int32, (width, width), 1) // HEAD_DIM
    return r == c


def _ref_rows(b, offsets, span):
    width = b.shape[-1]
    return jnp.concatenate([jnp.broadcast_to(b[o:o + 1], (span, width)) for o in offsets], axis=0)


def _hgrn_chunk(q, f, v, st, reverse):
    c, width = q.shape
    masks = _head_masks(width)
    g = jnp.log(f)
    k = 1.0 - f
    ti = lax.broadcasted_iota(jnp.int32, (c, c), 0)
    si = lax.broadcasted_iota(jnp.int32, (c, c), 1)
    tri = ((si >= ti) if reverse else (si <= ti)).astype(BF16)
    b = None
    rem = g
    for _ in range(3):
        piece = rem.astype(BF16)
        rem = rem - piece.astype(F32)
        part = _dot(tri, piece)
        b = part if b is None else b + part
    b_end = b[0:1] if reverse else b[c - 1:c]

    st_b = st.astype(BF16)
    o = lax.dot_general((q * jnp.exp(b)).astype(BF16), st_b, NT, preferred_element_type=F32)

    trow = lax.broadcasted_iota(jnp.int32, (c, 1), 0)
    t_full = lax.broadcasted_iota(jnp.int32, (c, width), 0)
    s_full = lax.broadcasted_iota(jnp.int32, (c, width), 1) % c
    a = jnp.zeros((c, width), F32)
    m = c // 2
    while m >= DIAG_BLOCK:
        blocks = c // (2 * m)
        ref = _ref_rows(b, [j * 2 * m + (m if reverse else m - 1) for j in range(blocks)], 2 * m)
        is_q = ((trow % (2 * m)) < m) if reverse else ((trow % (2 * m)) >= m)
        e = jnp.exp(jnp.where(is_q, b - ref, ref - b))
        ql = jnp.where(is_q, q * e, 0.0).astype(BF16)
        kl = jnp.where(is_q, 0.0, k * e).astype(BF16)
        al = lax.dot_general(ql, _stack_heads(kl, masks), NT, preferred_element_type=F32)
        if blocks > 1:
            al = jnp.where((t_full // (2 * m)) == (s_full // (2 * m)), al, 0.0)
        a = a + al
        m //= 2
    blocks = c // DIAG_BLOCK
    mid = DIAG_BLOCK // 2
    ref = _ref_rows(b, [j * DIAG_BLOCK + (mid if reverse else mid - 1) for j in range(blocks)],
                    DIAG_BLOCK)
    d = b - ref
    ql = (q * jnp.exp(d)).astype(BF16)
    kl = (k * jnp.exp(-d)).astype(BF16)
    al = lax.dot_general(ql, _stack_heads(kl, masks), NT, preferred_element_type=F32)
    same = (t_full // DIAG_BLOCK) == (s_full // DIAG_BLOCK)
    causal = (s_full >= t_full) if reverse else (s_full <= t_full)
    a = a + jnp.where(same & causal, al, 0.0)

    v_b = v.astype(BF16)
    o = o + _dot(a.astype(BF16), _stack_heads(v_b, masks))

    k_end = (k * jnp.exp(b_end - b)).astype(BF16)
    upd = lax.dot_general(v_b, k_end, TN, preferred_element_type=F32)
    st_new = st * jnp.exp(b_end) + jnp.where(_block_diag_mask(width), upd, 0.0)
    return o, st_new


def _mxu_transpose(x):
    n = x.shape[1]
    r = lax.broadcasted_iota(jnp.int32, (n, n), 0)
    c = lax.broadcasted_iota(jnp.int32, (n, n), 1)
    eye = (r == c).astype(BF16)
    acc = None
    rem = x
    for _ in range(3):
        piece = rem.astype(BF16)
        rem = rem - piece.astype(F32)
        part = lax.dot_general(eye, piece, NT, preferred_element_type=F32)
        acc = part if acc is None else acc + part
    return acc


def _hgrn_kernel(*refs, has_state, want_state, heads):
    refs = list(refs)
    q_ref, ff_ref, fb_ref, v_ref, gate_ref, gn_ref = refs[:6]
    pos = 6
    if has_state:
        s0f_ref, s0b_ref = refs[pos:pos + 2]
        pos += 2
    o_ref = refs[pos]
    pos += 1
    if want_state:
        sf_ref, sb_ref = refs[pos:pos + 2]
        pos += 2
    stf_ref, stb_ref, of_ref, ob_ref = refs[pos:pos + 4]
    t, width = q_ref.shape[1], q_ref.shape[2]
    nc = t // CHUNK
    bd = _block_diag_mask(width)

    if has_state:
        for src, dst in ((s0f_ref, stf_ref), (s0b_ref, stb_ref)):
            x = src[0, 0].reshape(width, HEAD_DIM)
            xt = _mxu_transpose(x)
            dst[...] = jnp.where(bd, jnp.concatenate([xt] * heads, axis=0), 0.0)
    else:
        stf_ref[...] = jnp.zeros_like(stf_ref)
        stb_ref[...] = jnp.zeros_like(stb_ref)

    def body(ci, carry):
        rf = pl.ds(pl.multiple_of(ci * CHUNK, CHUNK), CHUNK)
        rb = pl.ds(pl.multiple_of((nc - 1 - ci) * CHUNK, CHUNK), CHUNK)
        o, st = _hgrn_chunk(q_ref[0, rf, :], ff_ref[0, rf, :], v_ref[0, rf, :], stf_ref[...], False)
        of_ref[rf, :] = o
        stf_ref[...] = st
        o, st = _hgrn_chunk(q_ref[0, rb, :], fb_ref[0, rb, :], v_ref[0, rb, :], stb_ref[...], True)
        ob_ref[rb, :] = o
        stb_ref[...] = st
        return carry

    lax.fori_loop(0, nc, body, 0)

    o = of_ref[...] + ob_ref[...]
    o_ref[0] = (_group_rms(o, gn_ref[...], HEAD_DIM) * gate_ref[0]).astype(o_ref.dtype)

    if want_state:
        for src, dst in ((stf_ref, sf_ref), (stb_ref, sb_ref)):
            st = src[...]
            rows = st[0:HEAD_DIM]
            for h in range(1, heads):
                rows = rows + st[h * HEAD_DIM:(h + 1) * HEAD_DIM]
            dst[0] = _mxu_transpose(rows).reshape(heads, HEAD_DIM, HEAD_DIM)


def _hgrn(hq, ff, fb, hv, hg, gn, state, *, li, want_state):
    bsz, t, width = hq.shape
    heads = width // HEAD_DIM
    has_state = state is not None
    seq = pl.BlockSpec((1, t, width), lambda b: (b, 0, 0))
    in_specs = [seq] * 5 + [pl.BlockSpec(gn.shape, lambda b: (0, 0))]
    args = [hq, ff, fb, hv, hg, gn]
    if has_state:
        in_specs += [pl.BlockSpec((1, 1, heads, HEAD_DIM, HEAD_DIM), lambda b: (b, li, 0, 0, 0))] * 2
        args += list(state)
    out_specs = [seq]
    out_shape = [jax.ShapeDtypeStruct((bsz, t, width), BF16)]
    if want_state:
        out_specs += [pl.BlockSpec((1, heads, HEAD_DIM, HEAD_DIM), lambda b: (b, 0, 0, 0))] * 2
        out_shape += [jax.ShapeDtypeStruct((bsz, heads, HEAD_DIM, HEAD_DIM), F32)] * 2
    return pl.pallas_call(
        functools.partial(_hgrn_kernel, has_state=has_state, want_state=want_state, heads=heads),
        grid=(bsz,),
        in_specs=in_specs, out_specs=out_specs, out_shape=out_shape,
        scratch_shapes=[pltpu.VMEM((width, width), F32), pltpu.VMEM((width, width), F32),
                        pltpu.VMEM((t, width), F32), pltpu.VMEM((t, width), F32)],
        compiler_params=_params(1),
        name="hgrn2_latent" if has_state else "hgrn2_context",
    )(*args)


def _out_mlp_kernel(x_ref, oa_ref, ob_ref, oc_ref, mod_ref, wo_ref, w1_ref, w2_ref,
                    g1_ref, b1_ref, g2_ref, b2_ref, y_ref, *, d, alpha, ff_chunk):
    wa, wb = oa_ref.shape[-1], ob_ref.shape[-1]
    m = (_dot(oa_ref[...], wo_ref[0:wa, :]) + _dot(ob_ref[...], wo_ref[wa:wa + wb, :])
         + _dot(oc_ref[...], wo_ref[wa + wb:, :]))
    gate1 = mod_ref[0, :, 2 * d:3 * d]
    shift2 = mod_ref[0, :, 3 * d:4 * d]
    gain2 = mod_ref[0, :, 4 * d:5 * d]
    gate2 = mod_ref[0, :, 5 * d:6 * d]
    x1 = _layernorm(alpha * x_ref[...] + gate1 * m, g1_ref[...], b1_ref[...])
    h2 = (x1 * (1.0 + gain2) + shift2).astype(BF16)
    acc = None
    for j in range(w1_ref.shape[-1] // ff_chunk):
        cols = slice(j * ff_chunk, (j + 1) * ff_chunk)
        hid = jnp.maximum(_dot(h2, w1_ref[:, cols]), 0.0)
        part = _dot((hid * hid).astype(BF16), w2_ref[cols, :])
        acc = part if acc is None else acc + part
    y_ref[...] = _layernorm(alpha * x1 + gate2 * acc, g2_ref[...], b2_ref[...])


def _out_mlp(x, oa, ob, oc, mod, mod_row, w_out, w_ff1, w_ff2, ln, *, alpha):
    bsz, t, d = x.shape
    tiles = t // ROW_TILE
    row = lambda i: (i, 0)
    const = lambda i: (0, 0)
    resident = lambda a: pl.BlockSpec(a.shape, const, pipeline_mode=pl.Buffered(1))
    in_specs = [pl.BlockSpec((ROW_TILE, d), row),
                pl.BlockSpec((ROW_TILE, oa.shape[-1]), row),
                pl.BlockSpec((ROW_TILE, ob.shape[-1]), row),
                pl.BlockSpec((ROW_TILE, oc.shape[-1]), row),
                pl.BlockSpec((1, 1, mod.shape[-1]), lambda i: (mod_row(i // tiles), 0, 0)),
                resident(w_out), resident(w_ff1), resident(w_ff2)]
    in_specs += [pl.BlockSpec(a.shape, const) for a in ln]
    y = pl.pallas_call(
        functools.partial(_out_mlp_kernel, d=d, alpha=alpha, ff_chunk=1024),
        grid=(bsz * tiles,),
        in_specs=in_specs,
        out_specs=pl.BlockSpec((ROW_TILE, d), row),
        out_shape=jax.ShapeDtypeStruct((bsz * t, d), F32),
        compiler_params=_params(1),
        name="out_mlp",
    )(x.reshape(bsz * t, d), oa, ob.reshape(bsz * t, -1), oc, mod, w_out, w_ff1, w_ff2, *ln)
    return y.reshape(bsz, t, d)


def _rope_tables(n_tokens):
    pairs = HEAD_DIM // 4
    tok = jnp.arange(n_tokens)
    row = (tok // GRID_W).astype(F32)
    col = (tok % GRID_W).astype(F32)
    inv = ROPE_THETA ** (-jnp.arange(pairs, dtype=F32) / pairs)
    ang = jnp.concatenate([row[:, None] * inv, col[:, None] * inv], axis=-1)
    lane = jnp.arange(LANES)
    pair = (lane % HEAD_DIM) // 2
    sign = jnp.where(lane % 2 == 0, -1.0, 1.0).astype(F32)
    return jnp.cos(ang)[:, pair], jnp.sin(ang)[:, pair] * sign


def kernel(x_prompt, x_sample, cache_a_k, cache_a_v, cache_c_k, cache_c_v, state_b_fwd, state_b_bwd, c, c_ctx, w_ada, b_ada, w_in, w_out, lam_q1, lam_k1, lam_q2, lam_k2, subln_g, lb_logits_fwd, lb_logits_bwd, gnorm_g, qnorm_g, knorm_g, ln1_g, ln1_b, ln2_g, ln2_b, w_ff1, w_ff2):
    depth = w_in.shape[0]
    bsz, seq, d = x_prompt.shape
    dec_bsz, dec_seq, _ = x_sample.shape
    past = cache_a_k.shape[2]
    alpha = (2 * depth) ** 0.25
    mix_a, mix_b, mix_c = d // 2, d // 4, d // 4

    cond = jnp.concatenate([c_ctx[None, :], c, jnp.zeros((8 - 1 - dec_bsz, d), F32)], axis=0)
    mod = _modulation(cond, w_ada, b_ada)
    rope = _rope_tables(dec_seq)

    cache = (cache_a_k.reshape(dec_bsz, depth, past, mix_a),
             cache_a_v.reshape(dec_bsz, depth, past, mix_a),
             cache_c_k.reshape(dec_bsz, depth, past, mix_c // 2),
             cache_c_v.reshape(dec_bsz, depth, past, mix_c // 2))
    lam = (lam_q1, lam_k1, lam_q2, lam_k2)

    def stream(x, li, weights, latent):
        w_in_l, w_out_l, w1_l, w2_l = weights
        n, t, _ = x.shape
        mod_l = mod[li].reshape(8, 1, 6 * d)
        mod_row = (lambda b: b + 1) if latent else (lambda b: 0)
        qn = jnp.tile(qnorm_g[li], mix_c // HEAD_DIM)[None, :]
        kn = jnp.tile(knorm_g[li], mix_c // 2 // HEAD_DIM)[None, :]
        gn = jnp.tile(gnorm_g[li], mix_b // HEAD_DIM)[None, :]
        (qa, ka, va, hq, ff, fb, hv, hg, qc, kc, vc) = _in_proj(
            x, mod_l, mod_row, w_in_l, lb_logits_fwd, lb_logits_bwd, qn, kn,
            rope if latent else None, li=li)
        tq = 256
        oa = _diff_attn(qa, ka, va, cache[0:2] if latent else None, lam, subln_g,
                        li=li, bsz=n, tq=tq)
        oc = _gqa(qc, kc, vc, cache[2:4] if latent else None, li=li, bsz=n, tq=tq)
        hres = _hgrn(hq, ff, fb, hv, hg, gn, (state_b_fwd, state_b_bwd) if latent else None,
                     li=li, want_state=not latent)
        ln = [a[li][None, :] for a in (ln1_g, ln1_b, ln2_g, ln2_b)]
        y = _out_mlp(x, oa, hres[0], oc, mod_l, mod_row, w_out_l, w1_l, w2_l, ln, alpha=alpha)
        own = None if latent else (ka, va, kc, vc, hres[1], hres[2])
        return y, own

    y_prompt, y_sample = x_prompt, x_sample
    owns = []
    for li in range(depth):
        weights = (w_in[li].astype(BF16), w_out[li].astype(BF16),
                   w_ff1[li].astype(BF16), w_ff2[li].astype(BF16))
        y_prompt, own = stream(y_prompt, li, weights, latent=False)
        owns.append(own)
        y_sample, _ = stream(y_sample, li, weights, latent=True)

    stack = lambda j: jnp.concatenate([o[j] for o in owns], axis=1)
    heads_a = mix_a // (2 * HEAD_DIM)
    new_a_k = stack(0).reshape(bsz, depth, seq, heads_a, 2, HEAD_DIM)
    new_a_v = stack(1).reshape(bsz, depth, seq, heads_a, 2 * HEAD_DIM)
    new_c_k = stack(2).reshape(bsz, depth, seq, mix_c // 2 // HEAD_DIM, HEAD_DIM)
    new_c_v = stack(3).reshape(bsz, depth, seq, mix_c // 2 // HEAD_DIM, HEAD_DIM)
    new_sf = jnp.stack([o[4] for o in owns], axis=1)
    new_sb = jnp.stack([o[5] for o in owns], axis=1)
    return (y_prompt, y_sample, new_a_k, new_a_v, new_c_k, new_c_v, new_sf, new_sb)
```

```python
import functools
import math

import jax
import jax.numpy as jnp
from jax import lax
from jax.experimental import pallas as pl
from jax.experimental.pallas import tpu as pltpu

GRID_W = 64
HEAD_DIM = 64
ROPE_THETA = 10000.0
LN_EPS = 1e-6
RMS_EPS = 1e-6
F_MIN = 1e-6
CHUNK = 64
DIAG_BLOCK = 8
LANES = 128
ROW_TILE = 256
VMEM_LIMIT = 56 * 1024 * 1024

F32 = jnp.float32
BF16 = jnp.bfloat16
NT = (((1,), (1,)), ((), ()))
TN = (((0,), (0,)), ((), ()))


def _params(n_grid):
    return pltpu.CompilerParams(dimension_semantics=("arbitrary",) * n_grid,
                                vmem_limit_bytes=VMEM_LIMIT)


def _dot(a, b):
    return jnp.dot(a, b, preferred_element_type=F32)


def _split_dot(a, b_bf16, passes, dims=None):
    acc = None
    rem = a
    for _ in range(passes):
        piece = rem.astype(BF16)
        rem = rem - piece.astype(F32)
        part = (_dot(piece, b_bf16) if dims is None
                else lax.dot_general(piece, b_bf16, dims, preferred_element_type=F32))
        acc = part if acc is None else acc + part
    return acc


def _group_ones(n, group):
    r = lax.broadcasted_iota(jnp.int32, (n, n), 0) // group
    c = lax.broadcasted_iota(jnp.int32, (n, n), 1) // group
    return (r == c).astype(BF16)


def _group_rms(x, g_row, group):
    n = x.shape[-1]
    ms = _split_dot(x * x, _group_ones(n, group), 2) * (1.0 / group)
    return x * lax.rsqrt(ms + RMS_EPS) * g_row


def _pair_swap(x):
    lane = lax.broadcasted_iota(jnp.int32, x.shape, 1)
    return jnp.where(lane % 2 == 0, pltpu.roll(x, LANES - 1, 1), pltpu.roll(x, 1, 1))


def _rope(x, cos, sin):
    blocks = []
    for j in range(x.shape[-1] // LANES):
        blk = x[:, j * LANES:(j + 1) * LANES]
        blocks.append(blk * cos + _pair_swap(blk) * sin)
    return blocks[0] if len(blocks) == 1 else jnp.concatenate(blocks, axis=-1)


def _silu(x):
    return x * jax.nn.sigmoid(x)


def _layernorm(x, g, b):
    mu = jnp.mean(x, axis=-1, keepdims=True)
    xc = x - mu
    var = jnp.mean(xc * xc, axis=-1, keepdims=True)
    return xc * lax.rsqrt(var + LN_EPS) * g + b


def _mod_kernel(c_ref, w_ref, b_ref, o_ref):
    s = _silu(c_ref[...]).astype(BF16)
    o_ref[0] = _dot(s, w_ref[0].astype(BF16)) + b_ref[0]


def _modulation(cond, w_ada, b_ada):
    depth, d, n = w_ada.shape
    tn = 1536
    rows = cond.shape[0]
    return pl.pallas_call(
        _mod_kernel,
        grid=(depth, n // tn),
        in_specs=[pl.BlockSpec((rows, d), lambda l, j: (0, 0)),
                  pl.BlockSpec((1, d, tn), lambda l, j: (l, 0, j)),
                  pl.BlockSpec((1, 1, tn), lambda l, j: (l, 0, j))],
        out_specs=pl.BlockSpec((1, rows, tn), lambda l, j: (l, 0, j)),
        out_shape=jax.ShapeDtypeStruct((depth, rows, n), F32),
        compiler_params=_params(2),
        name="adaln_modulation",
    )(cond, w_ada, b_ada.reshape(depth, 1, n))


def _in_proj_kernel(*refs, li, d, latent):
    if latent:
        (x_ref, mod_ref, w_ref, lbf_ref, lbb_ref, qn_ref, kn_ref, cos_ref, sin_ref,
         qa_o, ka_o, va_o, hq_o, ff_o, fb_o, hv_o, hg_o, qc_o, kc_o, vc_o) = refs
        cos, sin = cos_ref[...], sin_ref[...]
    else:
        (x_ref, mod_ref, w_ref, lbf_ref, lbb_ref, qn_ref, kn_ref,
         qa_o, ka_o, va_o, hq_o, ff_o, fb_o, hv_o, hg_o, qc_o, kc_o, vc_o) = refs
    mix_a, mix_b, mix_c = d // 2, d // 4, d // 4
    kv_c = mix_c // 2
    scale = HEAD_DIM ** -0.5

    shift = mod_ref[0, :, 0:d]
    gain = mod_ref[0, :, d:2 * d]
    h = (x_ref[...] * (1.0 + gain) + shift).astype(BF16)

    def proj(start, width):
        return _dot(h, w_ref[:, start:start + width])

    off = 0
    qa = proj(off, mix_a); off += mix_a
    ka = proj(off, mix_a); off += mix_a
    va = proj(off, mix_a); off += mix_a
    if latent:
        qa = _rope(qa, cos, sin)
        ka = _rope(ka, cos, sin)
    qa_o[...] = (qa * scale).astype(qa_o.dtype)
    ka_o[0, 0] = ka.astype(ka_o.dtype)
    va_o[0, 0] = va.astype(va_o.dtype)

    def lower_bound(ref):
        logits = ref[...]
        e = jnp.exp(logits - jnp.max(logits, axis=0, keepdims=True))
        sm = e / jnp.sum(e, axis=0, keepdims=True)
        return jnp.sum(sm[0:li + 1], axis=0, keepdims=True) - sm[0:1]

    def forget(x, lb):
        return jnp.maximum(lb + (1.0 - lb) * jax.nn.sigmoid(x), F_MIN)

    hq_o[0] = _silu(proj(off, mix_b)); off += mix_b
    ff_o[0] = forget(proj(off, mix_b), lower_bound(lbf_ref)); off += mix_b
    fb_o[0] = forget(proj(off, mix_b), lower_bound(lbb_ref)); off += mix_b
    hv_o[0] = proj(off, mix_b); off += mix_b
    hg_o[0] = _silu(proj(off, mix_b)); off += mix_b

    qc = _group_rms(proj(off, mix_c), qn_ref[...], HEAD_DIM); off += mix_c
    kc = _group_rms(proj(off, kv_c), kn_ref[...], HEAD_DIM); off += kv_c
    vc = proj(off, kv_c)
    if latent:
        kc = _rope(kc, cos, sin)
        qc = _rope(qc, cos, sin)
    kc_o[0, 0] = kc.astype(kc_o.dtype)
    vc_o[0, 0] = vc.astype(vc_o.dtype)
    qc = qc * scale
    lane = lax.broadcasted_iota(jnp.int32, (1, LANES), 1)
    for n in range(2):
        blk = qc[:, n * LANES:(n + 1) * LANES]
        in_half = (lane // HEAD_DIM) == n
        for g in range(2):
            src = blk if g == n else pltpu.roll(blk, HEAD_DIM, 1)
            hc = 2 * n + g
            qc_o[:, hc * LANES:(hc + 1) * LANES] = jnp.where(in_half, src, 0.0).astype(qc_o.dtype)


def _in_proj(x, mod, mod_row, w_in, lb_f, lb_b, qn, kn, rope, *, li):
    bsz, t, d = x.shape
    latent = rope is not None
    n_in = w_in.shape[-1]
    tiles = t // ROW_TILE
    mix_a, mix_b, mix_c = d // 2, d // 4, d // 4
    kv_c = mix_c // 2
    kdt = BF16 if latent else F32
    x2 = x.reshape(bsz * t, d)

    row = lambda i: (i, 0)
    brow = lambda i: (i // tiles, i % tiles, 0)
    krow = lambda i: (i // tiles, 0, i % tiles, 0)
    const = lambda i: (0, 0)
    in_specs = [pl.BlockSpec((ROW_TILE, d), row),
                pl.BlockSpec((1, 1, mod.shape[-1]), lambda i: (mod_row(i // tiles), 0, 0)),
                pl.BlockSpec((d, n_in), const),
                pl.BlockSpec(lb_f.shape, const), pl.BlockSpec(lb_b.shape, const),
                pl.BlockSpec(qn.shape, const), pl.BlockSpec(kn.shape, const)]
    args = [x2, mod, w_in, lb_f, lb_b, qn, kn]
    if latent:
        in_specs += [pl.BlockSpec((ROW_TILE, LANES), lambda i: (i % tiles, 0))] * 2
        args += list(rope)

    def hspec():
        return pl.BlockSpec((1, ROW_TILE, mix_b), brow)

    out_specs = [pl.BlockSpec((ROW_TILE, mix_a), row),
                 pl.BlockSpec((1, 1, ROW_TILE, mix_a), krow),
                 pl.BlockSpec((1, 1, ROW_TILE, mix_a), krow),
                 hspec(), hspec(), hspec(), hspec(), hspec(),
                 pl.BlockSpec((ROW_TILE, 2 * mix_c), row),
                 pl.BlockSpec((1, 1, ROW_TILE, kv_c), krow),
                 pl.BlockSpec((1, 1, ROW_TILE, kv_c), krow)]
    out_shape = [jax.ShapeDtypeStruct((bsz * t, mix_a), BF16),
                 jax.ShapeDtypeStruct((bsz, 1, t, mix_a), kdt),
                 jax.ShapeDtypeStruct((bsz, 1, t, mix_a), kdt)]
    out_shape += [jax.ShapeDtypeStruct((bsz, t, mix_b), F32)] * 5
    out_shape += [jax.ShapeDtypeStruct((bsz * t, 2 * mix_c), BF16),
                  jax.ShapeDtypeStruct((bsz, 1, t, kv_c), kdt),
                  jax.ShapeDtypeStruct((bsz, 1, t, kv_c), kdt)]
    return pl.pallas_call(
        functools.partial(_in_proj_kernel, li=li, d=d, latent=latent),
        grid=(bsz * tiles,),
        in_specs=in_specs, out_specs=out_specs, out_shape=out_shape,
        compiler_params=_params(1),
        name="in_proj_latent" if latent else "in_proj_context",
    )(*args)


def _scores(q, keys):
    return [lax.dot_general(q, k, NT, preferred_element_type=F32) for k in keys]


def _softmax_parts(scores):
    m = functools.reduce(jnp.maximum, [jnp.max(s, axis=-1, keepdims=True) for s in scores])
    es = [jnp.exp(s - m) for s in scores]
    denom = functools.reduce(lambda a, b: a + b, [jnp.sum(e, axis=-1, keepdims=True) for e in es])
    return es, 1.0 / denom


def _diff_attn_kernel(*refs, li, cached):
    if cached:
        (q_ref, k_ref, v_ref, ck_ref, cv_ref, lq1, lk1, lq2, lk2, sub_ref, o_ref) = refs
    else:
        (q_ref, k_ref, v_ref, lq1, lk1, lq2, lk2, sub_ref, o_ref) = refs
    lam_init = 0.8 - 0.6 * math.exp(-0.3 * li)

    def lam_term(a, b):
        return jnp.exp(jnp.sum(a[li:li + 1, :] * b[li:li + 1, :], axis=-1, keepdims=True))

    lam = lam_term(lq1, lk1) - lam_term(lq2, lk2) + lam_init

    q = q_ref[...]
    lane = lax.broadcasted_iota(jnp.int32, (1, LANES), 1)
    keys = [k_ref[0, 0].astype(BF16)]
    vals = [v_ref[0, 0].astype(BF16)]
    if cached:
        keys.append(ck_ref[0, 0].astype(BF16))
        vals.append(cv_ref[0, 0].astype(BF16))
    zero = jnp.zeros_like(q)
    e0, r0 = _softmax_parts(_scores(jnp.where(lane < HEAD_DIM, q, zero), keys))
    e1, r1 = _softmax_parts(_scores(jnp.where(lane >= HEAD_DIM, q, zero), keys))
    r1 = r1 * lam
    o = None
    for a, b, v in zip(e0, e1, vals):
        part = _dot((a * r0 - b * r1).astype(BF16), v)
        o = part if o is None else o + part
    ms = jnp.mean(o * o, axis=-1, keepdims=True)
    o = o * lax.rsqrt(ms + RMS_EPS) * sub_ref[li:li + 1, :] * (1.0 - lam_init)
    o_ref[...] = o.astype(o_ref.dtype)


def _diff_attn(q, k, v, cache, lam, subln, *, li, bsz, tq):
    t = k.shape[2]
    width = q.shape[-1]
    heads = width // LANES
    nq = t // tq
    cached = cache is not None
    in_specs = [pl.BlockSpec((tq, LANES), lambda b, h, i: (b * nq + i, h)),
                pl.BlockSpec((1, 1, t, LANES), lambda b, h, i: (b, 0, 0, h)),
                pl.BlockSpec((1, 1, t, LANES), lambda b, h, i: (b, 0, 0, h))]
    args = [q, k, v]
    if cached:
        p = cache[0].shape[2]
        in_specs += [pl.BlockSpec((1, 1, p, LANES), lambda b, h, i: (b, li, 0, h))] * 2
        args += list(cache)
    in_specs += [pl.BlockSpec(a.shape, lambda b, h, i: (0, 0)) for a in (*lam, subln)]
    args += [*lam, subln]
    return pl.pallas_call(
        functools.partial(_diff_attn_kernel, li=li, cached=cached),
        grid=(bsz, heads, nq),
        in_specs=in_specs,
        out_specs=pl.BlockSpec((tq, LANES), lambda b, h, i: (b * nq + i, h)),
        out_shape=jax.ShapeDtypeStruct((bsz * t, width), BF16),
        compiler_params=_params(3),
        name="diff_attn_latent" if cached else "diff_attn_context",
    )(*args)


def _gqa_kernel(*refs, li, cached):
    if cached:
        q_ref, k_ref, v_ref, ck_ref, cv_ref, o_ref = refs
    else:
        q_ref, k_ref, v_ref, o_ref = refs
    lane = lax.broadcasted_iota(jnp.int32, (1, LANES), 1)
    keys = [k_ref[0, 0].astype(BF16)]
    vals = [v_ref[0, 0].astype(BF16)]
    if cached:
        keys.append(ck_ref[0, 0].astype(BF16))
        vals.append(cv_ref[0, 0].astype(BF16))
    for n in range(2):
        in_half = (lane // HEAD_DIM) == n
        vals_n = [jnp.where(in_half, v, jnp.zeros_like(v)) for v in vals]
        acc = None
        for g in range(2):
            hc = 2 * n + g
            es, r = _softmax_parts(_scores(q_ref[:, hc * LANES:(hc + 1) * LANES], keys))
            o = None
            for e, v in zip(es, vals_n):
                part = _dot((e * r).astype(BF16), v)
                o = part if o is None else o + part
            if g != n:
                o = pltpu.roll(o, HEAD_DIM, 1)
            acc = o if acc is None else acc + o
        o_ref[:, n * LANES:(n + 1) * LANES] = acc.astype(o_ref.dtype)


def _gqa(q, k, v, cache, *, li, bsz, tq):
    t = k.shape[2]
    kvw = k.shape[-1]
    nq = t // tq
    cached = cache is not None
    in_specs = [pl.BlockSpec((tq, q.shape[-1]), lambda b, i: (b * nq + i, 0)),
                pl.BlockSpec((1, 1, t, kvw), lambda b, i: (b, 0, 0, 0)),
                pl.BlockSpec((1, 1, t, kvw), lambda b, i: (b, 0, 0, 0))]
    args = [q, k, v]
    if cached:
        p = cache[0].shape[2]
        in_specs += [pl.BlockSpec((1, 1, p, kvw), lambda b, i: (b, li, 0, 0))] * 2
        args += list(cache)
    return pl.pallas_call(
        functools.partial(_gqa_kernel, li=li, cached=cached),
        grid=(bsz, nq),
        in_specs=in_specs,
        out_specs=pl.BlockSpec((tq, 2 * kvw), lambda b, i: (b * nq + i, 0)),
        out_shape=jax.ShapeDtypeStruct((bsz * t, 2 * kvw), BF16),
        compiler_params=_params(2),
        name="gqa_latent" if cached else "gqa_context",
    )(*args)


def _head_masks(width):
    lane_head = lax.broadcasted_iota(jnp.int32, (1, width), 1) // HEAD_DIM
    return [lane_head == h for h in range(width // HEAD_DIM)]


def _stack_heads(x, masks):
    return jnp.concatenate([jnp.where(m, x, jnp.zeros_like(x)) for m in masks], axis=0)


def _block_diag_mask(width):
    r = lax.broadcasted_iota(jnp.int32, (width, width), 0) // HEAD_DIM
    c = lax.broadcasted_iota(jnp.int32, (width, width), 1) // HEAD_DIM
    return r == c


def _ref_rows(b, offsets, span):
    width = b.shape[-1]
    return jnp.concatenate([jnp.broadcast_to(b[o:o + 1], (span, width)) for o in offsets], axis=0)


def _hgrn_chunks(problems):
    n = len(problems)
    c, width = problems[0][0].shape
    qs = [p[0] for p in problems]
    vs = [p[2] for p in problems]
    sts = [p[3] for p in problems]
    rev = [p[4] for p in problems]
    chains = range(n)
    masks = _head_masks(width)
    ti = lax.broadcasted_iota(jnp.int32, (c, c), 0)
    si = lax.broadcasted_iota(jnp.int32, (c, c), 1)
    tri = {False: (si <= ti).astype(BF16), True: (si >= ti).astype(BF16)}
    trow = lax.broadcasted_iota(jnp.int32, (c, 1), 0)
    t_full = lax.broadcasted_iota(jnp.int32, (c, width), 0)
    s_full = lax.broadcasted_iota(jnp.int32, (c, width), 1) % c

    ks = [1.0 - p[1] for p in problems]
    rem = [jnp.log(p[1]) for p in problems]
    b = [None] * n
    for _ in range(3):
        for j in chains:
            piece = rem[j].astype(BF16)
            rem[j] = rem[j] - piece.astype(F32)
            part = _dot(tri[rev[j]], piece)
            b[j] = part if b[j] is None else b[j] + part
    b_end = [b[j][0:1] if rev[j] else b[j][c - 1:c] for j in chains]

    o = [lax.dot_general((qs[j] * jnp.exp(b[j])).astype(BF16), sts[j].astype(BF16), NT,
                         preferred_element_type=F32) for j in chains]

    a = [None] * n
    m = c // 2
    while m >= DIAG_BLOCK:
        blocks = c // (2 * m)
        same = (t_full // (2 * m)) == (s_full // (2 * m))
        for j in chains:
            ref = _ref_rows(b[j], [i * 2 * m + (m if rev[j] else m - 1) for i in range(blocks)],
                            2 * m)
            is_q = ((trow % (2 * m)) < m) if rev[j] else ((trow % (2 * m)) >= m)
            e = jnp.exp(jnp.where(is_q, b[j] - ref, ref - b[j]))
            ql = jnp.where(is_q, qs[j] * e, 0.0).astype(BF16)
            kl = jnp.where(is_q, 0.0, ks[j] * e).astype(BF16)
            al = lax.dot_general(ql, _stack_heads(kl, masks), NT, preferred_element_type=F32)
            if blocks > 1:
                al = jnp.where(same, al, 0.0)
            a[j] = al if a[j] is None else a[j] + al
        m //= 2
    blocks = c // DIAG_BLOCK
    mid = DIAG_BLOCK // 2
    same = (t_full // DIAG_BLOCK) == (s_full // DIAG_BLOCK)
    for j in chains:
        ref = _ref_rows(b[j], [i * DIAG_BLOCK + (mid if rev[j] else mid - 1) for i in range(blocks)],
                        DIAG_BLOCK)
        d = b[j] - ref
        ql = (qs[j] * jnp.exp(d)).astype(BF16)
        kl = (ks[j] * jnp.exp(-d)).astype(BF16)
        al = lax.dot_general(ql, _stack_heads(kl, masks), NT, preferred_element_type=F32)
        causal = (s_full >= t_full) if rev[j] else (s_full <= t_full)
        a[j] = a[j] + jnp.where(same & causal, al, 0.0)

    v_b = [v.astype(BF16) for v in vs]
    o = [o[j] + _dot(a[j].astype(BF16), _stack_heads(v_b[j], masks)) for j in chains]

    bd = _block_diag_mask(width)
    upd = [lax.dot_general(v_b[j], (ks[j] * jnp.exp(b_end[j] - b[j])).astype(BF16), TN,
                           preferred_element_type=F32) for j in chains]
    st_new = [sts[j] * jnp.exp(b_end[j]) + jnp.where(bd, upd[j], 0.0) for j in chains]
    return list(zip(o, st_new))


def _mxu_transpose(x):
    n = x.shape[1]
    r = lax.broadcasted_iota(jnp.int32, (n, n), 0)
    c = lax.broadcasted_iota(jnp.int32, (n, n), 1)
    eye = (r == c).astype(BF16)
    acc = None
    rem = x
    for _ in range(3):
        piece = rem.astype(BF16)
        rem = rem - piece.astype(F32)
        part = lax.dot_general(eye, piece, NT, preferred_element_type=F32)
        acc = part if acc is None else acc + part
    return acc


def _hgrn_kernel(*refs, has_state, want_state, heads):
    refs = list(refs)
    q_ref, ff_ref, fb_ref, v_ref, gate_ref, gn_ref = refs[:6]
    pos = 6
    if has_state:
        s0f_ref, s0b_ref = refs[pos:pos + 2]
        pos += 2
    o_ref = refs[pos]
    pos += 1
    if want_state:
        sf_ref, sb_ref = refs[pos:pos + 2]
        pos += 2
    st_ref, of_ref, ob_ref = refs[pos:pos + 3]
    nb, t, width = q_ref.shape
    nc = t // CHUNK
    bd = _block_diag_mask(width)

    for n in range(nb):
        for d in range(2):
            if has_state:
                x = (s0b_ref if d else s0f_ref)[n, 0].reshape(width, HEAD_DIM)
                xt = _mxu_transpose(x)
                st_ref[2 * n + d] = jnp.where(bd, jnp.concatenate([xt] * heads, axis=0), 0.0)
            else:
                st_ref[2 * n + d] = jnp.zeros((width, width), F32)

    def body(ci, carry):
        rows = (pl.ds(pl.multiple_of(ci * CHUNK, CHUNK), CHUNK),
                pl.ds(pl.multiple_of((nc - 1 - ci) * CHUNK, CHUNK), CHUNK))
        loaded = []
        for n in range(nb):
            for d, f_ref in enumerate((ff_ref, fb_ref)):
                r = rows[d]
                loaded.append((q_ref[n, r, :], f_ref[n, r, :], v_ref[n, r, :], st_ref[2 * n + d],
                               bool(d)))
        for j, (o, st) in enumerate(_hgrn_chunks(loaded)):
            n, d = divmod(j, 2)
            (ob_ref if d else of_ref)[n, rows[d], :] = o
            st_ref[j] = st
        return carry

    lax.fori_loop(0, nc, body, 0)

    for n in range(nb):
        o = of_ref[n] + ob_ref[n]
        o_ref[n] = (_group_rms(o, gn_ref[...], HEAD_DIM) * gate_ref[n]).astype(o_ref.dtype)

    if want_state:
        for n in range(nb):
            for d, dst in enumerate((sf_ref, sb_ref)):
                st = st_ref[2 * n + d]
                rows = st[0:HEAD_DIM]
                for h in range(1, heads):
                    rows = rows + st[h * HEAD_DIM:(h + 1) * HEAD_DIM]
                dst[n] = _mxu_transpose(rows).reshape(heads, HEAD_DIM, HEAD_DIM)


def _hgrn(hq, ff, fb, hv, hg, gn, state, *, li, want_state, nb):
    bsz, t, width = hq.shape
    heads = width // HEAD_DIM
    has_state = state is not None
    seq = pl.BlockSpec((nb, t, width), lambda b: (b, 0, 0))
    in_specs = [seq] * 5 + [pl.BlockSpec(gn.shape, lambda b: (0, 0))]
    args = [hq, ff, fb, hv, hg, gn]
    if has_state:
        in_specs += [pl.BlockSpec((nb, 1, heads, HEAD_DIM, HEAD_DIM), lambda b: (b, li, 0, 0, 0))] * 2
        args += list(state)
    out_specs = [seq]
    out_shape = [jax.ShapeDtypeStruct((bsz, t, width), BF16)]
    if want_state:
        out_specs += [pl.BlockSpec((nb, heads, HEAD_DIM, HEAD_DIM), lambda b: (b, 0, 0, 0))] * 2
        out_shape += [jax.ShapeDtypeStruct((bsz, heads, HEAD_DIM, HEAD_DIM), F32)] * 2
    return pl.pallas_call(
        functools.partial(_hgrn_kernel, has_state=has_state, want_state=want_state, heads=heads),
        grid=(bsz // nb,),
        in_specs=in_specs, out_specs=out_specs, out_shape=out_shape,
        scratch_shapes=[pltpu.VMEM((2 * nb, width, width), F32),
                        pltpu.VMEM((nb, t, width), F32), pltpu.VMEM((nb, t, width), F32)],
        compiler_params=_params(1),
        name="hgrn2_latent" if has_state else "hgrn2_context",
    )(*args)


def _out_mlp_kernel(x_ref, oa_ref, ob_ref, oc_ref, mod_ref, wo_ref, w1_ref, w2_ref,
                    g1_ref, b1_ref, g2_ref, b2_ref, y_ref, *, d, alpha, ff_chunk):
    wa, wb = oa_ref.shape[-1], ob_ref.shape[-1]
    m = (_dot(oa_ref[...], wo_ref[0:wa, :]) + _dot(ob_ref[...], wo_ref[wa:wa + wb, :])
         + _dot(oc_ref[...], wo_ref[wa + wb:, :]))
    gate1 = mod_ref[0, :, 2 * d:3 * d]
    shift2 = mod_ref[0, :, 3 * d:4 * d]
    gain2 = mod_ref[0, :, 4 * d:5 * d]
    gate2 = mod_ref[0, :, 5 * d:6 * d]
    x1 = _layernorm(alpha * x_ref[...] + gate1 * m, g1_ref[...], b1_ref[...])
    h2 = (x1 * (1.0 + gain2) + shift2).astype(BF16)
    acc = None
    for j in range(w1_ref.shape[-1] // ff_chunk):
        cols = slice(j * ff_chunk, (j + 1) * ff_chunk)
        hid = jnp.maximum(_dot(h2, w1_ref[:, cols]), 0.0)
        part = _dot((hid * hid).astype(BF16), w2_ref[cols, :])
        acc = part if acc is None else acc + part
    y_ref[...] = _layernorm(alpha * x1 + gate2 * acc, g2_ref[...], b2_ref[...])


def _out_mlp(x, oa, ob, oc, mod, mod_row, w_out, w_ff1, w_ff2, ln, *, alpha):
    bsz, t, d = x.shape
    tiles = t // ROW_TILE
    row = lambda i: (i, 0)
    const = lambda i: (0, 0)
    resident = lambda a: pl.BlockSpec(a.shape, const, pipeline_mode=pl.Buffered(1))
    in_specs = [pl.BlockSpec((ROW_TILE, d), row),
                pl.BlockSpec((ROW_TILE, oa.shape[-1]), row),
                pl.BlockSpec((ROW_TILE, ob.shape[-1]), row),
                pl.BlockSpec((ROW_TILE, oc.shape[-1]), row),
                pl.BlockSpec((1, 1, mod.shape[-1]), lambda i: (mod_row(i // tiles), 0, 0)),
                resident(w_out), resident(w_ff1), resident(w_ff2)]
    in_specs += [pl.BlockSpec(a.shape, const) for a in ln]
    y = pl.pallas_call(
        functools.partial(_out_mlp_kernel, d=d, alpha=alpha, ff_chunk=1024),
        grid=(bsz * tiles,),
        in_specs=in_specs,
        out_specs=pl.BlockSpec((ROW_TILE, d), row),
        out_shape=jax.ShapeDtypeStruct((bsz * t, d), F32),
        compiler_params=_params(1),
        name="out_mlp",
    )(x.reshape(bsz * t, d), oa, ob.reshape(bsz * t, -1), oc, mod, w_out, w_ff1, w_ff2, *ln)
    return y.reshape(bsz, t, d)


def _rope_tables(n_tokens):
    pairs = HEAD_DIM // 4
    tok = jnp.arange(n_tokens)
    row = (tok // GRID_W).astype(F32)
    col = (tok % GRID_W).astype(F32)
    inv = ROPE_THETA ** (-jnp.arange(pairs, dtype=F32) / pairs)
    ang = jnp.concatenate([row[:, None] * inv, col[:, None] * inv], axis=-1)
    lane = jnp.arange(LANES)
    pair = (lane % HEAD_DIM) // 2
    sign = jnp.where(lane % 2 == 0, -1.0, 1.0).astype(F32)
    return jnp.cos(ang)[:, pair], jnp.sin(ang)[:, pair] * sign


def kernel(x_prompt, x_sample, cache_a_k, cache_a_v, cache_c_k, cache_c_v, state_b_fwd, state_b_bwd, c, c_ctx, w_ada, b_ada, w_in, w_out, lam_q1, lam_k1, lam_q2, lam_k2, subln_g, lb_logits_fwd, lb_logits_bwd, gnorm_g, qnorm_g, knorm_g, ln1_g, ln1_b, ln2_g, ln2_b, w_ff1, w_ff2):
    depth = w_in.shape[0]
    bsz, seq, d = x_prompt.shape
    dec_bsz, dec_seq, _ = x_sample.shape
    past = cache_a_k.shape[2]
    alpha = (2 * depth) ** 0.25
    mix_a, mix_b, mix_c = d // 2, d // 4, d // 4

    cond = jnp.concatenate([c_ctx[None, :], c, jnp.zeros((8 - 1 - dec_bsz, d), F32)], axis=0)
    mod = _modulation(cond, w_ada, b_ada)
    rope = _rope_tables(dec_seq)

    cache = (cache_a_k.reshape(dec_bsz, depth, past, mix_a),
             cache_a_v.reshape(dec_bsz, depth, past, mix_a),
             cache_c_k.reshape(dec_bsz, depth, past, mix_c // 2),
             cache_c_v.reshape(dec_bsz, depth, past, mix_c // 2))
    lam = (lam_q1, lam_k1, lam_q2, lam_k2)

    def stream(x, li, weights, latent):
        w_in_l, w_out_l, w1_l, w2_l = weights
        n, t, _ = x.shape
        mod_l = mod[li].reshape(8, 1, 6 * d)
        mod_row = (lambda b: b + 1) if latent else (lambda b: 0)
        qn = jnp.tile(qnorm_g[li], mix_c // HEAD_DIM)[None, :]
        kn = jnp.tile(knorm_g[li], mix_c // 2 // HEAD_DIM)[None, :]
        gn = jnp.tile(gnorm_g[li], mix_b // HEAD_DIM)[None, :]
        (qa, ka, va, hq, ff, fb, hv, hg, qc, kc, vc) = _in_proj(
            x, mod_l, mod_row, w_in_l, lb_logits_fwd, lb_logits_bwd, qn, kn,
            rope if latent else None, li=li)
        tq = 256
        oa = _diff_attn(qa, ka, va, cache[0:2] if latent else None, lam, subln_g,
                        li=li, bsz=n, tq=tq)
        oc = _gqa(qc, kc, vc, cache[2:4] if latent else None, li=li, bsz=n, tq=tq)
        hres = _hgrn(hq, ff, fb, hv, hg, gn, (state_b_fwd, state_b_bwd) if latent else None,
                     li=li, want_state=not latent, nb=2)
        ln = [a[li][None, :] for a in (ln1_g, ln1_b, ln2_g, ln2_b)]
        y = _out_mlp(x, oa, hres[0], oc, mod_l, mod_row, w_out_l, w1_l, w2_l, ln, alpha=alpha)
        own = None if latent else (ka, va, kc, vc, hres[1], hres[2])
        return y, own

    y_prompt, y_sample = x_prompt, x_sample
    owns = []
    for li in range(depth):
        weights = (w_in[li].astype(BF16), w_out[li].astype(BF16),
                   w_ff1[li].astype(BF16), w_ff2[li].astype(BF16))
        y_prompt, own = stream(y_prompt, li, weights, latent=False)
        owns.append(own)
        y_sample, _ = stream(y_sample, li, weights, latent=True)

    stack = lambda j: jnp.concatenate([o[j] for o in owns], axis=1)
    heads_a = mix_a // (2 * HEAD_DIM)
    new_a_k = stack(0).reshape(bsz, depth, seq, heads_a, 2, HEAD_DIM)
    new_a_v = stack(1).reshape(bsz, depth, seq, heads_a, 2 * HEAD_DIM)
    new_c_k = stack(2).reshape(bsz, depth, seq, mix_c // 2 // HEAD_DIM, HEAD_DIM)
    new_c_v = stack(3).reshape(bsz, depth, seq, mix_c // 2 // HEAD_DIM, HEAD_DIM)
    new_sf = jnp.stack([o[4] for o in owns], axis=1)
    new_sb = jnp.stack([o[5] for o in owns], axis=1)
    return (y_prompt, y_sample, new_a_k, new_a_v, new_c_k, new_c_v, new_sf, new_sb)
```

```python
import functools
import math

import jax
import jax.numpy as jnp
from jax import lax
from jax.experimental import pallas as pl
from jax.experimental.pallas import tpu as pltpu

GRID_W = 64
HEAD_DIM = 64
ROPE_THETA = 10000.0
LN_EPS = 1e-6
RMS_EPS = 1e-6
F_MIN = 1e-6
CHUNK = 64
DIAG_BLOCK = 8
LANES = 128
ROW_TILE = 256
VMEM_LIMIT = 56 * 1024 * 1024

F32 = jnp.float32
BF16 = jnp.bfloat16
NT = (((1,), (1,)), ((), ()))
TN = (((0,), (0,)), ((), ()))


def _params(n_grid):
    return pltpu.CompilerParams(dimension_semantics=("arbitrary",) * n_grid,
                                vmem_limit_bytes=VMEM_LIMIT)


def _dot(a, b):
    return jnp.dot(a, b, preferred_element_type=F32)


def _split_dot(a, b_bf16, passes, dims=None):
    acc = None
    rem = a
    for _ in range(passes):
        piece = rem.astype(BF16)
        rem = rem - piece.astype(F32)
        part = (_dot(piece, b_bf16) if dims is None
                else lax.dot_general(piece, b_bf16, dims, preferred_element_type=F32))
        acc = part if acc is None else acc + part
    return acc


def _group_ones(n, group):
    r = lax.broadcasted_iota(jnp.int32, (n, n), 0) // group
    c = lax.broadcasted_iota(jnp.int32, (n, n), 1) // group
    return (r == c).astype(BF16)


def _group_rms(x, g_row, group):
    n = x.shape[-1]
    ms = _split_dot(x * x, _group_ones(n, group), 2) * (1.0 / group)
    return x * lax.rsqrt(ms + RMS_EPS) * g_row


def _pair_swap(x):
    lane = lax.broadcasted_iota(jnp.int32, x.shape, 1)
    return jnp.where(lane % 2 == 0, pltpu.roll(x, LANES - 1, 1), pltpu.roll(x, 1, 1))


def _rope(x, cos, sin):
    blocks = []
    for j in range(x.shape[-1] // LANES):
        blk = x[:, j * LANES:(j + 1) * LANES]
        blocks.append(blk * cos + _pair_swap(blk) * sin)
    return blocks[0] if len(blocks) == 1 else jnp.concatenate(blocks, axis=-1)


def _silu(x):
    return x * jax.nn.sigmoid(x)


def _layernorm(x, g, b):
    mu = jnp.mean(x, axis=-1, keepdims=True)
    xc = x - mu
    var = jnp.mean(xc * xc, axis=-1, keepdims=True)
    return xc * lax.rsqrt(var + LN_EPS) * g + b


def _mod_kernel(c_ref, w_ref, b_ref, o_ref):
    s = _silu(c_ref[...]).astype(BF16)
    o_ref[0] = _dot(s, w_ref[0].astype(BF16)) + b_ref[0]


def _modulation(cond, w_ada, b_ada):
    depth, d, n = w_ada.shape
    tn = 1536
    rows = cond.shape[0]
    return pl.pallas_call(
        _mod_kernel,
        grid=(depth, n // tn),
        in_specs=[pl.BlockSpec((rows, d), lambda l, j: (0, 0)),
                  pl.BlockSpec((1, d, tn), lambda l, j: (l, 0, j)),
                  pl.BlockSpec((1, 1, tn), lambda l, j: (l, 0, j))],
        out_specs=pl.BlockSpec((1, rows, tn), lambda l, j: (l, 0, j)),
        out_shape=jax.ShapeDtypeStruct((depth, rows, n), F32),
        compiler_params=_params(2),
        name="adaln_modulation",
    )(cond, w_ada, b_ada.reshape(depth, 1, n))


def _in_proj_kernel(*refs, li, d, latent):
    if latent:
        (x_ref, mod_ref, w_ref, lbf_ref, lbb_ref, qn_ref, kn_ref, cos_ref, sin_ref,
         qa_o, ka_o, va_o, hq_o, ff_o, fb_o, hv_o, hg_o, qc_o, kc_o, vc_o) = refs
        cos, sin = cos_ref[...], sin_ref[...]
    else:
        (x_ref, mod_ref, w_ref, lbf_ref, lbb_ref, qn_ref, kn_ref,
         qa_o, ka_o, va_o, hq_o, ff_o, fb_o, hv_o, hg_o, qc_o, kc_o, vc_o) = refs
    mix_a, mix_b, mix_c = d // 2, d // 4, d // 4
    kv_c = mix_c // 2
    scale = HEAD_DIM ** -0.5 * math.log2(math.e)

    shift = mod_ref[0, :, 0:d]
    gain = mod_ref[0, :, d:2 * d]
    h = (x_ref[...] * (1.0 + gain) + shift).astype(BF16)

    def proj(start, width):
        return _dot(h, w_ref[:, start:start + width])

    off = 0
    qa = proj(off, mix_a); off += mix_a
    ka = proj(off, mix_a); off += mix_a
    va = proj(off, mix_a); off += mix_a
    if latent:
        qa = _rope(qa, cos, sin)
        ka = _rope(ka, cos, sin)
    qa_o[...] = (qa * scale).astype(qa_o.dtype)
    ka_o[0, 0] = ka.astype(ka_o.dtype)
    va_o[0, 0] = va.astype(va_o.dtype)

    def lower_bound(ref):
        logits = ref[...]
        e = jnp.exp(logits - jnp.max(logits, axis=0, keepdims=True))
        sm = e / jnp.sum(e, axis=0, keepdims=True)
        return jnp.sum(sm[0:li + 1], axis=0, keepdims=True) - sm[0:1]

    def forget(x, lb):
        return jnp.maximum(lb + (1.0 - lb) * jax.nn.sigmoid(x), F_MIN)

    hq_o[0] = _silu(proj(off, mix_b)); off += mix_b
    ff_o[0] = forget(proj(off, mix_b), lower_bound(lbf_ref)); off += mix_b
    fb_o[0] = forget(proj(off, mix_b), lower_bound(lbb_ref)); off += mix_b
    hv_o[0] = proj(off, mix_b); off += mix_b
    hg_o[0] = _silu(proj(off, mix_b)); off += mix_b

    qc = _group_rms(proj(off, mix_c), qn_ref[...], HEAD_DIM); off += mix_c
    kc = _group_rms(proj(off, kv_c), kn_ref[...], HEAD_DIM); off += kv_c
    vc = proj(off, kv_c)
    if latent:
        kc = _rope(kc, cos, sin)
        qc = _rope(qc, cos, sin)
    kc_o[0, 0] = kc.astype(kc_o.dtype)
    vc_o[0, 0] = vc.astype(vc_o.dtype)
    qc = qc * scale
    lane = lax.broadcasted_iota(jnp.int32, (1, LANES), 1)
    for n in range(2):
        blk = qc[:, n * LANES:(n + 1) * LANES]
        in_half = (lane // HEAD_DIM) == n
        for g in range(2):
            src = blk if g == n else pltpu.roll(blk, HEAD_DIM, 1)
            hc = 2 * n + g
            qc_o[:, hc * LANES:(hc + 1) * LANES] = jnp.where(in_half, src, 0.0).astype(qc_o.dtype)


def _in_proj(x, mod, mod_row, w_in, lb_f, lb_b, qn, kn, rope, *, li):
    bsz, t, d = x.shape
    latent = rope is not None
    n_in = w_in.shape[-1]
    tiles = t // ROW_TILE
    mix_a, mix_b, mix_c = d // 2, d // 4, d // 4
    kv_c = mix_c // 2
    kdt = BF16 if latent else F32
    x2 = x.reshape(bsz * t, d)

    row = lambda i: (i, 0)
    brow = lambda i: (i // tiles, i % tiles, 0)
    krow = lambda i: (i // tiles, 0, i % tiles, 0)
    const = lambda i: (0, 0)
    in_specs = [pl.BlockSpec((ROW_TILE, d), row),
                pl.BlockSpec((1, 1, mod.shape[-1]), lambda i: (mod_row(i // tiles), 0, 0)),
                pl.BlockSpec((d, n_in), const),
                pl.BlockSpec(lb_f.shape, const), pl.BlockSpec(lb_b.shape, const),
                pl.BlockSpec(qn.shape, const), pl.BlockSpec(kn.shape, const)]
    args = [x2, mod, w_in, lb_f, lb_b, qn, kn]
    if latent:
        in_specs += [pl.BlockSpec((ROW_TILE, LANES), lambda i: (i % tiles, 0))] * 2
        args += list(rope)

    def hspec():
        return pl.BlockSpec((1, ROW_TILE, mix_b), brow)

    out_specs = [pl.BlockSpec((ROW_TILE, mix_a), row),
                 pl.BlockSpec((1, 1, ROW_TILE, mix_a), krow),
                 pl.BlockSpec((1, 1, ROW_TILE, mix_a), krow),
                 hspec(), hspec(), hspec(), hspec(), hspec(),
                 pl.BlockSpec((ROW_TILE, 2 * mix_c), row),
                 pl.BlockSpec((1, 1, ROW_TILE, kv_c), krow),
                 pl.BlockSpec((1, 1, ROW_TILE, kv_c), krow)]
    out_shape = [jax.ShapeDtypeStruct((bsz * t, mix_a), BF16),
                 jax.ShapeDtypeStruct((bsz, 1, t, mix_a), kdt),
                 jax.ShapeDtypeStruct((bsz, 1, t, mix_a), kdt)]
    out_shape += [jax.ShapeDtypeStruct((bsz, t, mix_b), F32)] * 5
    out_shape += [jax.ShapeDtypeStruct((bsz * t, 2 * mix_c), BF16),
                  jax.ShapeDtypeStruct((bsz, 1, t, kv_c), kdt),
                  jax.ShapeDtypeStruct((bsz, 1, t, kv_c), kdt)]
    return pl.pallas_call(
        functools.partial(_in_proj_kernel, li=li, d=d, latent=latent),
        grid=(bsz * tiles,),
        in_specs=in_specs, out_specs=out_specs, out_shape=out_shape,
        compiler_params=_params(1),
        name="in_proj_latent" if latent else "in_proj_context",
    )(*args)


def _softmax_parts(scores):
    m = functools.reduce(jnp.maximum, [jnp.max(s, axis=-1, keepdims=True) for s in scores])
    es = [jnp.exp2(s - m) for s in scores]
    denom = functools.reduce(lambda a, b: a + b, [jnp.sum(e, axis=-1, keepdims=True) for e in es])
    return es, 1.0 / denom


def _diff_attn_kernel(*refs, li, cached, heads):
    if cached:
        (q_ref, k_ref, v_ref, ck_ref, cv_ref, lq1, lk1, lq2, lk2, sub_ref, o_ref) = refs
    else:
        (q_ref, k_ref, v_ref, lq1, lk1, lq2, lk2, sub_ref, o_ref) = refs
    lam_init = 0.8 - 0.6 * math.exp(-0.3 * li)

    def lam_term(a, b):
        return jnp.exp(jnp.sum(a[li:li + 1, :] * b[li:li + 1, :], axis=-1, keepdims=True))

    lam = lam_term(lq1, lk1) - lam_term(lq2, lk2) + lam_init
    tq = q_ref.shape[0]
    lane = lax.broadcasted_iota(jnp.int32, (1, LANES), 1)
    cols = [slice(h * LANES, (h + 1) * LANES) for h in range(heads)]

    scores, vals = [], []
    for c in cols:
        q = q_ref[:, c]
        zero = jnp.zeros_like(q)
        q2 = jnp.concatenate([jnp.where(lane < HEAD_DIM, q, zero),
                              jnp.where(lane >= HEAD_DIM, q, zero)], axis=0)
        keys = [k_ref[0, 0, :, c].astype(BF16)]
        vals.append([v_ref[0, 0, :, c].astype(BF16)])
        if cached:
            keys.append(ck_ref[0, 0, :, c].astype(BF16))
            vals[-1].append(cv_ref[0, 0, :, c].astype(BF16))
        scores.append([lax.dot_general(q2, k, NT, preferred_element_type=F32) for k in keys])
    parts = [_softmax_parts(s) for s in scores]
    outs = []
    for (es, r), vs in zip(parts, vals):
        r0 = r[0:tq]
        r1 = r[tq:2 * tq] * lam
        o = None
        for e, v in zip(es, vs):
            part = _dot((e[0:tq] * r0 - e[tq:2 * tq] * r1).astype(BF16), v)
            o = part if o is None else o + part
        outs.append(o)
    gain = sub_ref[li:li + 1, :] * (1.0 - lam_init)
    for c, o in zip(cols, outs):
        ms = jnp.mean(o * o, axis=-1, keepdims=True)
        o_ref[:, c] = (o * lax.rsqrt(ms + RMS_EPS) * gain).astype(o_ref.dtype)


def _diff_attn(q, k, v, cache, lam, subln, *, li, kv_layer, bsz, tq, heads_per_step):
    t = k.shape[2]
    width = q.shape[-1]
    wstep = heads_per_step * LANES
    nq = t // tq
    cached = cache is not None
    in_specs = [pl.BlockSpec((tq, wstep), lambda b, h, i: (b * nq + i, h)),
                pl.BlockSpec((1, 1, t, wstep), lambda b, h, i: (b, kv_layer, 0, h)),
                pl.BlockSpec((1, 1, t, wstep), lambda b, h, i: (b, kv_layer, 0, h))]
    args = [q, k, v]
    if cached:
        p = cache[0].shape[2]
        in_specs += [pl.BlockSpec((1, 1, p, wstep), lambda b, h, i: (b, li, 0, h))] * 2
        args += list(cache)
    in_specs += [pl.BlockSpec(a.shape, lambda b, h, i: (0, 0)) for a in (*lam, subln)]
    args += [*lam, subln]
    return pl.pallas_call(
        functools.partial(_diff_attn_kernel, li=li, cached=cached, heads=heads_per_step),
        grid=(bsz, width // wstep, nq),
        in_specs=in_specs,
        out_specs=pl.BlockSpec((tq, wstep), lambda b, h, i: (b * nq + i, h)),
        out_shape=jax.ShapeDtypeStruct((bsz * t, width), BF16),
        compiler_params=_params(3),
        name="diff_attn_latent" if cached else "diff_attn_context",
    )(*args)


def _gqa_kernel(*refs, cached):
    if cached:
        q_ref, k_ref, v_ref, ck_ref, cv_ref, o_ref = refs
    else:
        q_ref, k_ref, v_ref, o_ref = refs
    tq = q_ref.shape[0]
    heads = q_ref.shape[1] // LANES
    lane = lax.broadcasted_iota(jnp.int32, (1, LANES), 1)
    keys = [k_ref[0, 0].astype(BF16)]
    vals = [v_ref[0, 0].astype(BF16)]
    if cached:
        keys.append(ck_ref[0, 0].astype(BF16))
        vals.append(cv_ref[0, 0].astype(BF16))
    groups = range(heads // 2)
    scores = []
    for n in groups:
        q = jnp.concatenate([q_ref[:, (2 * n + g) * LANES:(2 * n + g + 1) * LANES]
                             for g in range(2)], axis=0)
        scores.append([lax.dot_general(q, k, NT, preferred_element_type=F32) for k in keys])
    parts = [_softmax_parts(s) for s in scores]
    outs = []
    for es, r in parts:
        o = None
        for e, v in zip(es, vals):
            part = _dot(e.astype(BF16), v)
            o = part if o is None else o + part
        outs.append(o * r)
    for n in groups:
        first = outs[n][0:tq]
        second = outs[n][tq:2 * tq]
        if n == 0:
            second = pltpu.roll(second, HEAD_DIM, 1)
        else:
            first = pltpu.roll(first, HEAD_DIM, 1)
        o_ref[:, n * LANES:(n + 1) * LANES] = jnp.where(lane < HEAD_DIM, first,
                                                        second).astype(o_ref.dtype)


def _gqa(q, k, v, cache, *, li, kv_layer, bsz, tq):
    t = k.shape[2]
    kvw = k.shape[-1]
    nq = t // tq
    cached = cache is not None
    in_specs = [pl.BlockSpec((tq, q.shape[-1]), lambda b, i: (b * nq + i, 0)),
                pl.BlockSpec((1, 1, t, kvw), lambda b, i: (b, kv_layer, 0, 0)),
                pl.BlockSpec((1, 1, t, kvw), lambda b, i: (b, kv_layer, 0, 0))]
    args = [q, k, v]
    if cached:
        p = cache[0].shape[2]
        in_specs += [pl.BlockSpec((1, 1, p, kvw), lambda b, i: (b, li, 0, 0))] * 2
        args += list(cache)
    return pl.pallas_call(
        functools.partial(_gqa_kernel, cached=cached),
        grid=(bsz, nq),
        in_specs=in_specs,
        out_specs=pl.BlockSpec((tq, 2 * kvw), lambda b, i: (b * nq + i, 0)),
        out_shape=jax.ShapeDtypeStruct((bsz * t, 2 * kvw), BF16),
        compiler_params=_params(2),
        name="gqa_latent" if cached else "gqa_context",
    )(*args)


def _head_masks(width):
    lane_head = lax.broadcasted_iota(jnp.int32, (1, width), 1) // HEAD_DIM
    return [lane_head == h for h in range(width // HEAD_DIM)]


def _stack_heads(x, masks):
    return jnp.concatenate([jnp.where(m, x, jnp.zeros_like(x)) for m in masks], axis=0)


def _block_diag_mask(width):
    r = lax.broadcasted_iota(jnp.int32, (width, width), 0) // HEAD_DIM
    c = lax.broadcasted_iota(jnp.int32, (width, width), 1) // HEAD_DIM
    return r == c


def _ref_rows(b, offsets, span):
    width = b.shape[-1]
    return jnp.concatenate([jnp.broadcast_to(b[o:o + 1], (span, width)) for o in offsets], axis=0)


def _hgrn_chunks(problems):
    n = len(problems)
    c, width = problems[0][0].shape
    qs = [p[0] for p in problems]
    vs = [p[2] for p in problems]
    sts = [p[3] for p in problems]
    rev = [p[4] for p in problems]
    chains = range(n)
    masks = _head_masks(width)
    ti = lax.broadcasted_iota(jnp.int32, (c, c), 0)
    si = lax.broadcasted_iota(jnp.int32, (c, c), 1)
    tri = {False: (si <= ti).astype(BF16), True: (si >= ti).astype(BF16)}
    trow = lax.broadcasted_iota(jnp.int32, (c, 1), 0)
    t_full = lax.broadcasted_iota(jnp.int32, (c, width), 0)
    s_full = lax.broadcasted_iota(jnp.int32, (c, width), 1) % c

    ks = [1.0 - p[1] for p in problems]
    rem = [jnp.log(p[1]) for p in problems]
    b = [None] * n
    for _ in range(3):
        for j in chains:
            piece = rem[j].astype(BF16)
            rem[j] = rem[j] - piece.astype(F32)
            part = _dot(tri[rev[j]], piece)
            b[j] = part if b[j] is None else b[j] + part
    b_end = [b[j][0:1] if rev[j] else b[j][c - 1:c] for j in chains]

    o = [lax.dot_general((qs[j] * jnp.exp(b[j])).astype(BF16), sts[j].astype(BF16), NT,
                         preferred_element_type=F32) for j in chains]

    a = [None] * n
    m = c // 2
    while m >= DIAG_BLOCK:
        blocks = c // (2 * m)
        same = (t_full // (2 * m)) == (s_full // (2 * m))
        for j in chains:
            ref = _ref_rows(b[j], [i * 2 * m + (m if rev[j] else m - 1) for i in range(blocks)],
                            2 * m)
            is_q = ((trow % (2 * m)) < m) if rev[j] else ((trow % (2 * m)) >= m)
            e = jnp.exp(jnp.where(is_q, b[j] - ref, ref - b[j]))
            ql = jnp.where(is_q, qs[j] * e, 0.0).astype(BF16)
            kl = jnp.where(is_q, 0.0, ks[j] * e).astype(BF16)
            al = lax.dot_general(ql, _stack_heads(kl, masks), NT, preferred_element_type=F32)
            if blocks > 1:
                al = jnp.where(same, al, 0.0)
            a[j] = al if a[j] is None else a[j] + al
        m //= 2
    blocks = c // DIAG_BLOCK
    mid = DIAG_BLOCK // 2
    same = (t_full // DIAG_BLOCK) == (s_full // DIAG_BLOCK)
    for j in chains:
        ref = _ref_rows(b[j], [i * DIAG_BLOCK + (mid if rev[j] else mid - 1) for i in range(blocks)],
                        DIAG_BLOCK)
        d = b[j] - ref
        ql = (qs[j] * jnp.exp(d)).astype(BF16)
        kl = (ks[j] * jnp.exp(-d)).astype(BF16)
        al = lax.dot_general(ql, _stack_heads(kl, masks), NT, preferred_element_type=F32)
        causal = (s_full >= t_full) if rev[j] else (s_full <= t_full)
        a[j] = a[j] + jnp.where(same & causal, al, 0.0)

    v_b = [v.astype(BF16) for v in vs]
    o = [o[j] + _dot(a[j].astype(BF16), _stack_heads(v_b[j], masks)) for j in chains]

    bd = _block_diag_mask(width)
    upd = [lax.dot_general(v_b[j], (ks[j] * jnp.exp(b_end[j] - b[j])).astype(BF16), TN,
                           preferred_element_type=F32) for j in chains]
    st_new = [sts[j] * jnp.exp(b_end[j]) + jnp.where(bd, upd[j], 0.0) for j in chains]
    return list(zip(o, st_new))


def _mxu_transpose(x):
    n = x.shape[1]
    r = lax.broadcasted_iota(jnp.int32, (n, n), 0)
    c = lax.broadcasted_iota(jnp.int32, (n, n), 1)
    eye = (r == c).astype(BF16)
    acc = None
    rem = x
    for _ in range(3):
        piece = rem.astype(BF16)
        rem = rem - piece.astype(F32)
        part = lax.dot_general(eye, piece, NT, preferred_element_type=F32)
        acc = part if acc is None else acc + part
    return acc


def _hgrn_kernel(*refs, has_state, want_state, heads):
    refs = list(refs)
    q_ref, ff_ref, fb_ref, v_ref, gate_ref, gn_ref = refs[:6]
    pos = 6
    if has_state:
        s0f_ref, s0b_ref = refs[pos:pos + 2]
        pos += 2
    o_ref = refs[pos]
    pos += 1
    if want_state:
        sf_ref, sb_ref = refs[pos:pos + 2]
        pos += 2
    st_ref, of_ref, ob_ref = refs[pos:pos + 3]
    nb, t, width = q_ref.shape
    nc = t // CHUNK
    bd = _block_diag_mask(width)

    for n in range(nb):
        for d in range(2):
            if has_state:
                x = (s0b_ref if d else s0f_ref)[n, 0].reshape(width, HEAD_DIM)
                xt = _mxu_transpose(x)
                st_ref[2 * n + d] = jnp.where(bd, jnp.concatenate([xt] * heads, axis=0), 0.0)
            else:
                st_ref[2 * n + d] = jnp.zeros((width, width), F32)

    def body(ci, carry):
        rows = (pl.ds(pl.multiple_of(ci * CHUNK, CHUNK), CHUNK),
                pl.ds(pl.multiple_of((nc - 1 - ci) * CHUNK, CHUNK), CHUNK))
        loaded = []
        for n in range(nb):
            for d, f_ref in enumerate((ff_ref, fb_ref)):
                r = rows[d]
                loaded.append((q_ref[n, r, :], f_ref[n, r, :], v_ref[n, r, :], st_ref[2 * n + d],
                               bool(d)))
        for j, (o, st) in enumerate(_hgrn_chunks(loaded)):
            n, d = divmod(j, 2)
            (ob_ref if d else of_ref)[n, rows[d], :] = o
            st_ref[j] = st
        return carry

    lax.fori_loop(0, nc, body, 0)

    for n in range(nb):
        o = of_ref[n] + ob_ref[n]
        o_ref[n] = (_group_rms(o, gn_ref[...], HEAD_DIM) * gate_ref[n]).astype(o_ref.dtype)

    if want_state:
        for n in range(nb):
            for d, dst in enumerate((sf_ref, sb_ref)):
                st = st_ref[2 * n + d]
                rows = st[0:HEAD_DIM]
                for h in range(1, heads):
                    rows = rows + st[h * HEAD_DIM:(h + 1) * HEAD_DIM]
                dst[n] = _mxu_transpose(rows).reshape(heads, HEAD_DIM, HEAD_DIM)


def _hgrn(hq, ff, fb, hv, hg, gn, state, *, li, want_state, nb):
    bsz, t, width = hq.shape
    heads = width // HEAD_DIM
    has_state = state is not None
    seq = pl.BlockSpec((nb, t, width), lambda b: (b, 0, 0))
    in_specs = [seq] * 5 + [pl.BlockSpec(gn.shape, lambda b: (0, 0))]
    args = [hq, ff, fb, hv, hg, gn]
    if has_state:
        in_specs += [pl.BlockSpec((nb, 1, heads, HEAD_DIM, HEAD_DIM), lambda b: (b, li, 0, 0, 0))] * 2
        args += list(state)
    out_specs = [seq]
    out_shape = [jax.ShapeDtypeStruct((bsz, t, width), BF16)]
    if want_state:
        out_specs += [pl.BlockSpec((nb, heads, HEAD_DIM, HEAD_DIM), lambda b: (b, 0, 0, 0))] * 2
        out_shape += [jax.ShapeDtypeStruct((bsz, heads, HEAD_DIM, HEAD_DIM), F32)] * 2
    return pl.pallas_call(
        functools.partial(_hgrn_kernel, has_state=has_state, want_state=want_state, heads=heads),
        grid=(bsz // nb,),
        in_specs=in_specs, out_specs=out_specs, out_shape=out_shape,
        scratch_shapes=[pltpu.VMEM((2 * nb, width, width), F32),
                        pltpu.VMEM((nb, t, width), F32), pltpu.VMEM((nb, t, width), F32)],
        compiler_params=_params(1),
        name="hgrn2_latent" if has_state else "hgrn2_context",
    )(*args)


def _out_mlp_kernel(x_ref, oa_ref, ob_ref, oc_ref, mod_ref, wo_ref, w1_ref, w2_ref,
                    g1_ref, b1_ref, g2_ref, b2_ref, y_ref, *, d, alpha, ff_chunk):
    wa, wb = oa_ref.shape[-1], ob_ref.shape[-1]
    m = (_dot(oa_ref[...], wo_ref[0:wa, :]) + _dot(ob_ref[...], wo_ref[wa:wa + wb, :])
         + _dot(oc_ref[...], wo_ref[wa + wb:, :]))
    gate1 = mod_ref[0, :, 2 * d:3 * d]
    shift2 = mod_ref[0, :, 3 * d:4 * d]
    gain2 = mod_ref[0, :, 4 * d:5 * d]
    gate2 = mod_ref[0, :, 5 * d:6 * d]
    x1 = _layernorm(alpha * x_ref[...] + gate1 * m, g1_ref[...], b1_ref[...])
    h2 = (x1 * (1.0 + gain2) + shift2).astype(BF16)
    acc = None
    for j in range(w1_ref.shape[-1] // ff_chunk):
        cols = slice(j * ff_chunk, (j + 1) * ff_chunk)
        hid = jnp.maximum(_dot(h2, w1_ref[:, cols]), 0.0)
        part = _dot((hid * hid).astype(BF16), w2_ref[cols, :])
        acc = part if acc is None else acc + part
    y_ref[...] = _layernorm(alpha * x1 + gate2 * acc, g2_ref[...], b2_ref[...])


def _out_mlp(x, oa, ob, oc, mod, mod_row, w_out, w_ff1, w_ff2, ln, *, alpha):
    bsz, t, d = x.shape
    tiles = t // ROW_TILE
    row = lambda i: (i, 0)
    const = lambda i: (0, 0)
    resident = lambda a: pl.BlockSpec(a.shape, const, pipeline_mode=pl.Buffered(1))
    in_specs = [pl.BlockSpec((ROW_TILE, d), row),
                pl.BlockSpec((ROW_TILE, oa.shape[-1]), row),
                pl.BlockSpec((ROW_TILE, ob.shape[-1]), row),
                pl.BlockSpec((ROW_TILE, oc.shape[-1]), row),
                pl.BlockSpec((1, 1, mod.shape[-1]), lambda i: (mod_row(i // tiles), 0, 0)),
                resident(w_out), resident(w_ff1), resident(w_ff2)]
    in_specs += [pl.BlockSpec(a.shape, const) for a in ln]
    y = pl.pallas_call(
        functools.partial(_out_mlp_kernel, d=d, alpha=alpha, ff_chunk=1024),
        grid=(bsz * tiles,),
        in_specs=in_specs,
        out_specs=pl.BlockSpec((ROW_TILE, d), row),
        out_shape=jax.ShapeDtypeStruct((bsz * t, d), F32),
        compiler_params=_params(1),
        name="out_mlp",
    )(x.reshape(bsz * t, d), oa, ob.reshape(bsz * t, -1), oc, mod, w_out, w_ff1, w_ff2, *ln)
    return y.reshape(bsz, t, d)


def _rope_tables(n_tokens):
    pairs = HEAD_DIM // 4
    tok = jnp.arange(n_tokens)
    row = (tok // GRID_W).astype(F32)
    col = (tok % GRID_W).astype(F32)
    inv = ROPE_THETA ** (-jnp.arange(pairs, dtype=F32) / pairs)
    ang = jnp.concatenate([row[:, None] * inv, col[:, None] * inv], axis=-1)
    lane = jnp.arange(LANES)
    pair = (lane % HEAD_DIM) // 2
    sign = jnp.where(lane % 2 == 0, -1.0, 1.0).astype(F32)
    return jnp.cos(ang)[:, pair], jnp.sin(ang)[:, pair] * sign


def kernel(x_prompt, x_sample, cache_a_k, cache_a_v, cache_c_k, cache_c_v, state_b_fwd, state_b_bwd, c, c_ctx, w_ada, b_ada, w_in, w_out, lam_q1, lam_k1, lam_q2, lam_k2, subln_g, lb_logits_fwd, lb_logits_bwd, gnorm_g, qnorm_g, knorm_g, ln1_g, ln1_b, ln2_g, ln2_b, w_ff1, w_ff2):
    depth = w_in.shape[0]
    bsz, seq, d = x_prompt.shape
    dec_bsz, dec_seq, _ = x_sample.shape
    past = cache_a_k.shape[2]
    alpha = (2 * depth) ** 0.25
    mix_a, mix_b, mix_c = d // 2, d // 4, d // 4

    cond = jnp.concatenate([c_ctx[None, :], c, jnp.zeros((8 - 1 - dec_bsz, d), F32)], axis=0)
    mod = _modulation(cond, w_ada, b_ada)
    rope = _rope_tables(dec_seq)

    cache = (cache_a_k.reshape(dec_bsz, depth, past, mix_a),
             cache_a_v.reshape(dec_bsz, depth, past, mix_a),
             cache_c_k.reshape(dec_bsz, depth, past, mix_c // 2),
             cache_c_v.reshape(dec_bsz, depth, past, mix_c // 2))
    lam = (lam_q1, lam_k1, lam_q2, lam_k2)

    def stream(x, li, weights, latent):
        w_in_l, w_out_l, w1_l, w2_l = weights
        n, t, _ = x.shape
        mod_l = mod[li].reshape(8, 1, 6 * d)
        mod_row = (lambda b: b + 1) if latent else (lambda b: 0)
        qn = jnp.tile(qnorm_g[li], mix_c // HEAD_DIM)[None, :]
        kn = jnp.tile(knorm_g[li], mix_c // 2 // HEAD_DIM)[None, :]
        gn = jnp.tile(gnorm_g[li], mix_b // HEAD_DIM)[None, :]
        (qa, ka, va, hq, ff, fb, hv, hg, qc, kc, vc) = _in_proj(
            x, mod_l, mod_row, w_in_l, lb_logits_fwd, lb_logits_bwd, qn, kn,
            rope if latent else None, li=li)
        tq = 256
        oa = _diff_attn(qa, ka, va, cache[0:2] if latent else None, lam, subln_g,
                        li=li, kv_layer=0, bsz=n, tq=tq, heads_per_step=2 if latent else 4)
        oc = _gqa(qc, kc, vc, cache[2:4] if latent else None, li=li, kv_layer=0, bsz=n, tq=tq)
        hres = _hgrn(hq, ff, fb, hv, hg, gn, (state_b_fwd, state_b_bwd) if latent else None,
                     li=li, want_state=not latent, nb=2)
        ln = [a[li][None, :] for a in (ln1_g, ln1_b, ln2_g, ln2_b)]
        y = _out_mlp(x, oa, hres[0], oc, mod_l, mod_row, w_out_l, w1_l, w2_l, ln, alpha=alpha)
        own = None if latent else (ka, va, kc, vc, hres[1], hres[2])
        return y, own

    y_prompt, y_sample = x_prompt, x_sample
    owns = []
    for li in range(depth):
        weights = (w_in[li].astype(BF16), w_out[li].astype(BF16),
                   w_ff1[li].astype(BF16), w_ff2[li].astype(BF16))
        y_prompt, own = stream(y_prompt, li, weights, latent=False)
        owns.append(own)
        y_sample, _ = stream(y_sample, li, weights, latent=True)

    stack = lambda j: jnp.concatenate([o[j] for o in owns], axis=1)
    heads_a = mix_a // (2 * HEAD_DIM)
    new_a_k = stack(0).reshape(bsz, depth, seq, heads_a, 2, HEAD_DIM)
    new_a_v = stack(1).reshape(bsz, depth, seq, heads_a, 2 * HEAD_DIM)
    new_c_k = stack(2).reshape(bsz, depth, seq, mix_c // 2 // HEAD_DIM, HEAD_DIM)
    new_c_v = stack(3).reshape(bsz, depth, seq, mix_c // 2 // HEAD_DIM, HEAD_DIM)
    new_sf = jnp.stack([o[4] for o in owns], axis=1)
    new_sb = jnp.stack([o[5] for o in owns], axis=1)
    return (y_prompt, y_sample, new_a_k, new_a_v, new_c_k, new_c_v, new_sf, new_sb)
```

```python
import functools
import math

import jax
import jax.numpy as jnp
from jax import lax
from jax.experimental import pallas as pl
from jax.experimental.pallas import tpu as pltpu

GRID_W = 64
HEAD_DIM = 64
ROPE_THETA = 10000.0
LN_EPS = 1e-6
RMS_EPS = 1e-6
F_MIN = 1e-6
CHUNK = 64
DIAG_BLOCK = 8
LANES = 128
ROW_TILE = 256
VMEM_LIMIT = 56 * 1024 * 1024

F32 = jnp.float32
BF16 = jnp.bfloat16
NT = (((1,), (1,)), ((), ()))
TN = (((0,), (0,)), ((), ()))


def _params(n_grid):
    return pltpu.CompilerParams(dimension_semantics=("arbitrary",) * n_grid,
                                vmem_limit_bytes=VMEM_LIMIT)


def _dot(a, b):
    return jnp.dot(a, b, preferred_element_type=F32)


def _split_dot(a, b_bf16, passes, dims=None):
    acc = None
    rem = a
    for _ in range(passes):
        piece = rem.astype(BF16)
        rem = rem - piece.astype(F32)
        part = (_dot(piece, b_bf16) if dims is None
                else lax.dot_general(piece, b_bf16, dims, preferred_element_type=F32))
        acc = part if acc is None else acc + part
    return acc


def _group_ones(n, group):
    r = lax.broadcasted_iota(jnp.int32, (n, n), 0) // group
    c = lax.broadcasted_iota(jnp.int32, (n, n), 1) // group
    return (r == c).astype(BF16)


def _group_rms(x, g_row, group):
    n = x.shape[-1]
    ms = _split_dot(x * x, _group_ones(n, group), 2) * (1.0 / group)
    return x * lax.rsqrt(ms + RMS_EPS) * g_row


def _pair_swap(x):
    lane = lax.broadcasted_iota(jnp.int32, x.shape, 1)
    return jnp.where(lane % 2 == 0, pltpu.roll(x, LANES - 1, 1), pltpu.roll(x, 1, 1))


def _rope(x, cos, sin):
    blocks = []
    for j in range(x.shape[-1] // LANES):
        blk = x[:, j * LANES:(j + 1) * LANES]
        blocks.append(blk * cos + _pair_swap(blk) * sin)
    return blocks[0] if len(blocks) == 1 else jnp.concatenate(blocks, axis=-1)


def _silu(x):
    return x * jax.nn.sigmoid(x)


def _layernorm(x, g, b):
    mu = jnp.mean(x, axis=-1, keepdims=True)
    xc = x - mu
    var = jnp.mean(xc * xc, axis=-1, keepdims=True)
    return xc * lax.rsqrt(var + LN_EPS) * g + b


def _mod_kernel(c_ref, w_ref, b_ref, o_ref):
    s = _silu(c_ref[...]).astype(BF16)
    o_ref[0] = _dot(s, w_ref[0].astype(BF16)) + b_ref[0]


def _modulation(cond, w_ada, b_ada):
    depth, d, n = w_ada.shape
    tn = 1536
    rows = cond.shape[0]
    return pl.pallas_call(
        _mod_kernel,
        grid=(depth, n // tn),
        in_specs=[pl.BlockSpec((rows, d), lambda l, j: (0, 0)),
                  pl.BlockSpec((1, d, tn), lambda l, j: (l, 0, j)),
                  pl.BlockSpec((1, 1, tn), lambda l, j: (l, 0, j))],
        out_specs=pl.BlockSpec((1, rows, tn), lambda l, j: (l, 0, j)),
        out_shape=jax.ShapeDtypeStruct((depth, rows, n), F32),
        compiler_params=_params(2),
        name="adaln_modulation",
    )(cond, w_ada, b_ada.reshape(depth, 1, n))


def _in_proj_kernel(*refs, li, d, latent, n_alias):
    refs = list(refs)
    x_ref, mod_ref, w_ref, lbf_ref, lbb_ref, qn_ref, kn_ref = refs[:7]
    pos = 7
    if latent:
        cos, sin = refs[pos][...], refs[pos + 1][...]
        pos += 2
    pos += n_alias
    qa_o, ka_o, va_o, hq_o, ff_o, fb_o, hv_o, hg_o, qc_o, kc_o, vc_o = refs[pos:]

    def store_kv(ref, val):
        for slot in range(ref.shape[1]):
            ref[0, slot] = val.astype(ref.dtype)

    mix_a, mix_b, mix_c = d // 2, d // 4, d // 4
    kv_c = mix_c // 2
    scale = HEAD_DIM ** -0.5 * math.log2(math.e)

    shift = mod_ref[0, :, 0:d]
    gain = mod_ref[0, :, d:2 * d]
    h = (x_ref[...] * (1.0 + gain) + shift).astype(BF16)

    def proj(start, width):
        return _dot(h, w_ref[:, start:start + width])

    off = 0
    qa = proj(off, mix_a); off += mix_a
    ka = proj(off, mix_a); off += mix_a
    va = proj(off, mix_a); off += mix_a
    if latent:
        qa = _rope(qa, cos, sin)
        ka = _rope(ka, cos, sin)
    qa_o[...] = (qa * scale).astype(qa_o.dtype)
    store_kv(ka_o, ka)
    store_kv(va_o, va)

    def lower_bound(ref):
        logits = ref[...]
        e = jnp.exp(logits - jnp.max(logits, axis=0, keepdims=True))
        sm = e / jnp.sum(e, axis=0, keepdims=True)
        return jnp.sum(sm[0:li + 1], axis=0, keepdims=True) - sm[0:1]

    def forget(x, lb):
        return jnp.maximum(lb + (1.0 - lb) * jax.nn.sigmoid(x), F_MIN)

    hq_o[0] = _silu(proj(off, mix_b)); off += mix_b
    ff_o[0] = forget(proj(off, mix_b), lower_bound(lbf_ref)); off += mix_b
    fb_o[0] = forget(proj(off, mix_b), lower_bound(lbb_ref)); off += mix_b
    hv_o[0] = proj(off, mix_b); off += mix_b
    hg_o[0] = _silu(proj(off, mix_b)); off += mix_b

    qc = _group_rms(proj(off, mix_c), qn_ref[li:li + 1, :], HEAD_DIM); off += mix_c
    kc = _group_rms(proj(off, kv_c), kn_ref[li:li + 1, :], HEAD_DIM); off += kv_c
    vc = proj(off, kv_c)
    if latent:
        kc = _rope(kc, cos, sin)
        qc = _rope(qc, cos, sin)
    store_kv(kc_o, kc)
    store_kv(vc_o, vc)
    qc = qc * scale
    lane = lax.broadcasted_iota(jnp.int32, (1, LANES), 1)
    for n in range(2):
        blk = qc[:, n * LANES:(n + 1) * LANES]
        in_half = (lane // HEAD_DIM) == n
        for g in range(2):
            src = blk if g == n else pltpu.roll(blk, HEAD_DIM, 1)
            hc = 2 * n + g
            qc_o[:, hc * LANES:(hc + 1) * LANES] = jnp.where(in_half, src, 0.0).astype(qc_o.dtype)


def _in_proj(x, mod, mod_row, w_in, lb_f, lb_b, qn, kn, rope, kv_prev, *, li):
    bsz, t, d = x.shape
    latent = rope is not None
    depth, _, n_in = w_in.shape
    tiles = t // ROW_TILE
    mix_a, mix_b, mix_c = d // 2, d // 4, d // 4
    kv_c = mix_c // 2
    x2 = x.reshape(bsz * t, d)

    row = lambda i: (i, 0)
    brow = lambda i: (i // tiles, i % tiles, 0)
    const = lambda i: (0, 0)
    in_specs = [pl.BlockSpec((ROW_TILE, d), row),
                pl.BlockSpec((1, 1, mod.shape[-1]), lambda i: (mod_row(i // tiles), 0, 0)),
                pl.BlockSpec((None, d, n_in), lambda i: (li, 0, 0)),
                pl.BlockSpec(lb_f.shape, const), pl.BlockSpec(lb_b.shape, const),
                pl.BlockSpec(qn.shape, const), pl.BlockSpec(kn.shape, const)]
    args = [x2, mod, w_in, lb_f, lb_b, qn, kn]
    if latent:
        in_specs += [pl.BlockSpec((ROW_TILE, LANES), lambda i: (i % tiles, 0))] * 2
        args += list(rope)
        kv_shape, kv_slots, kv_slot0, kdt = (bsz, 1, t), 1, 0, BF16
    else:
        first = kv_prev is None
        kv_shape, kv_slots, kv_slot0, kdt = (bsz, depth, t), (depth if first else 1), (0 if first else li), F32
    aliases = {}
    if kv_prev is not None:
        kv_out_index = (1, 2, 9, 10)
        for buf, out_index in zip(kv_prev, kv_out_index):
            aliases[len(args)] = out_index
            in_specs.append(pl.BlockSpec(memory_space=pl.ANY))
            args.append(buf)
    krow = lambda i: (i // tiles, kv_slot0, i % tiles, 0)

    def hspec():
        return pl.BlockSpec((1, ROW_TILE, mix_b), brow)

    def kvspec(width):
        return pl.BlockSpec((1, kv_slots, ROW_TILE, width), krow)

    out_specs = [pl.BlockSpec((ROW_TILE, mix_a), row), kvspec(mix_a), kvspec(mix_a),
                 hspec(), hspec(), hspec(), hspec(), hspec(),
                 pl.BlockSpec((ROW_TILE, 2 * mix_c), row), kvspec(kv_c), kvspec(kv_c)]
    out_shape = [jax.ShapeDtypeStruct((bsz * t, mix_a), BF16),
                 jax.ShapeDtypeStruct(kv_shape + (mix_a,), kdt),
                 jax.ShapeDtypeStruct(kv_shape + (mix_a,), kdt)]
    out_shape += [jax.ShapeDtypeStruct((bsz, t, mix_b), F32)] * 5
    out_shape += [jax.ShapeDtypeStruct((bsz * t, 2 * mix_c), BF16),
                  jax.ShapeDtypeStruct(kv_shape + (kv_c,), kdt),
                  jax.ShapeDtypeStruct(kv_shape + (kv_c,), kdt)]
    return pl.pallas_call(
        functools.partial(_in_proj_kernel, li=li, d=d, latent=latent, n_alias=len(aliases)),
        grid=(bsz * tiles,),
        in_specs=in_specs, out_specs=out_specs, out_shape=out_shape,
        input_output_aliases=aliases,
        compiler_params=_params(1),
        name="in_proj_latent" if latent else "in_proj_context",
    )(*args)


def _softmax_parts(scores):
    m = functools.reduce(jnp.maximum, [jnp.max(s, axis=-1, keepdims=True) for s in scores])
    es = [jnp.exp2(s - m) for s in scores]
    denom = functools.reduce(lambda a, b: a + b, [jnp.sum(e, axis=-1, keepdims=True) for e in es])
    return es, 1.0 / denom


def _diff_attn_kernel(*refs, li, cached, heads):
    if cached:
        (q_ref, k_ref, v_ref, ck_ref, cv_ref, lq1, lk1, lq2, lk2, sub_ref, o_ref) = refs
    else:
        (q_ref, k_ref, v_ref, lq1, lk1, lq2, lk2, sub_ref, o_ref) = refs
    lam_init = 0.8 - 0.6 * math.exp(-0.3 * li)

    def lam_term(a, b):
        return jnp.exp(jnp.sum(a[li:li + 1, :] * b[li:li + 1, :], axis=-1, keepdims=True))

    lam = lam_term(lq1, lk1) - lam_term(lq2, lk2) + lam_init
    tq = q_ref.shape[0]
    lane = lax.broadcasted_iota(jnp.int32, (1, LANES), 1)
    cols = [slice(h * LANES, (h + 1) * LANES) for h in range(heads)]

    scores, vals = [], []
    for c in cols:
        q = q_ref[:, c]
        zero = jnp.zeros_like(q)
        q2 = jnp.concatenate([jnp.where(lane < HEAD_DIM, q, zero),
                              jnp.where(lane >= HEAD_DIM, q, zero)], axis=0)
        keys = [k_ref[0, 0, :, c].astype(BF16)]
        vals.append([v_ref[0, 0, :, c].astype(BF16)])
        if cached:
            keys.append(ck_ref[0, 0, :, c].astype(BF16))
            vals[-1].append(cv_ref[0, 0, :, c].astype(BF16))
        scores.append([lax.dot_general(q2, k, NT, preferred_element_type=F32) for k in keys])
    parts = [_softmax_parts(s) for s in scores]
    outs = []
    for (es, r), vs in zip(parts, vals):
        r0 = r[0:tq]
        r1 = r[tq:2 * tq] * lam
        o = None
        for e, v in zip(es, vs):
            part = _dot((e[0:tq] * r0 - e[tq:2 * tq] * r1).astype(BF16), v)
            o = part if o is None else o + part
        outs.append(o)
    gain = sub_ref[li:li + 1, :] * (1.0 - lam_init)
    for c, o in zip(cols, outs):
        ms = jnp.mean(o * o, axis=-1, keepdims=True)
        o_ref[:, c] = (o * lax.rsqrt(ms + RMS_EPS) * gain).astype(o_ref.dtype)


def _diff_attn(q, k, v, cache, lam, subln, *, li, kv_layer, bsz, tq, heads_per_step):
    t = k.shape[2]
    width = q.shape[-1]
    wstep = heads_per_step * LANES
    nq = t // tq
    cached = cache is not None
    in_specs = [pl.BlockSpec((tq, wstep), lambda b, h, i: (b * nq + i, h)),
                pl.BlockSpec((1, 1, t, wstep), lambda b, h, i: (b, kv_layer, 0, h)),
                pl.BlockSpec((1, 1, t, wstep), lambda b, h, i: (b, kv_layer, 0, h))]
    args = [q, k, v]
    if cached:
        p = cache[0].shape[2]
        in_specs += [pl.BlockSpec((1, 1, p, wstep), lambda b, h, i: (b, li, 0, h))] * 2
        args += list(cache)
    in_specs += [pl.BlockSpec(a.shape, lambda b, h, i: (0, 0)) for a in (*lam, subln)]
    args += [*lam, subln]
    return pl.pallas_call(
        functools.partial(_diff_attn_kernel, li=li, cached=cached, heads=heads_per_step),
        grid=(bsz, width // wstep, nq),
        in_specs=in_specs,
        out_specs=pl.BlockSpec((tq, wstep), lambda b, h, i: (b * nq + i, h)),
        out_shape=jax.ShapeDtypeStruct((bsz * t, width), BF16),
        compiler_params=_params(3),
        name="diff_attn_latent" if cached else "diff_attn_context",
    )(*args)


def _gqa_kernel(*refs, cached):
    if cached:
        q_ref, k_ref, v_ref, ck_ref, cv_ref, o_ref = refs
    else:
        q_ref, k_ref, v_ref, o_ref = refs
    tq = q_ref.shape[0]
    heads = q_ref.shape[1] // LANES
    lane = lax.broadcasted_iota(jnp.int32, (1, LANES), 1)
    keys = [k_ref[0, 0].astype(BF16)]
    vals = [v_ref[0, 0].astype(BF16)]
    if cached:
        keys.append(ck_ref[0, 0].astype(BF16))
        vals.append(cv_ref[0, 0].astype(BF16))
    groups = range(heads // 2)
    scores = []
    for n in groups:
        q = jnp.concatenate([q_ref[:, (2 * n + g) * LANES:(2 * n + g + 1) * LANES]
                             for g in range(2)], axis=0)
        scores.append([lax.dot_general(q, k, NT, preferred_element_type=F32) for k in keys])
    parts = [_softmax_parts(s) for s in scores]
    outs = []
    for es, r in parts:
        o = None
        for e, v in zip(es, vals):
            part = _dot(e.astype(BF16), v)
            o = part if o is None else o + part
        outs.append(o * r)
    for n in groups:
        first = outs[n][0:tq]
        second = outs[n][tq:2 * tq]
        if n == 0:
            second = pltpu.roll(second, HEAD_DIM, 1)
        else:
            first = pltpu.roll(first, HEAD_DIM, 1)
        o_ref[:, n * LANES:(n + 1) * LANES] = jnp.where(lane < HEAD_DIM, first,
                                                        second).astype(o_ref.dtype)


def _gqa(q, k, v, cache, *, li, kv_layer, bsz, tq):
    t = k.shape[2]
    kvw = k.shape[-1]
    nq = t // tq
    cached = cache is not None
    in_specs = [pl.BlockSpec((tq, q.shape[-1]), lambda b, i: (b * nq + i, 0)),
                pl.BlockSpec((1, 1, t, kvw), lambda b, i: (b, kv_layer, 0, 0)),
                pl.BlockSpec((1, 1, t, kvw), lambda b, i: (b, kv_layer, 0, 0))]
    args = [q, k, v]
    if cached:
        p = cache[0].shape[2]
        in_specs += [pl.BlockSpec((1, 1, p, kvw), lambda b, i: (b, li, 0, 0))] * 2
        args += list(cache)
    return pl.pallas_call(
        functools.partial(_gqa_kernel, cached=cached),
        grid=(bsz, nq),
        in_specs=in_specs,
        out_specs=pl.BlockSpec((tq, 2 * kvw), lambda b, i: (b * nq + i, 0)),
        out_shape=jax.ShapeDtypeStruct((bsz * t, 2 * kvw), BF16),
        compiler_params=_params(2),
        name="gqa_latent" if cached else "gqa_context",
    )(*args)


def _head_masks(width):
    lane_head = lax.broadcasted_iota(jnp.int32, (1, width), 1) // HEAD_DIM
    return [lane_head == h for h in range(width // HEAD_DIM)]


def _stack_heads(x, masks):
    return jnp.concatenate([jnp.where(m, x, jnp.zeros_like(x)) for m in masks], axis=0)


def _block_diag_mask(width):
    r = lax.broadcasted_iota(jnp.int32, (width, width), 0) // HEAD_DIM
    c = lax.broadcasted_iota(jnp.int32, (width, width), 1) // HEAD_DIM
    return r == c


def _ref_rows(b, offsets, span):
    width = b.shape[-1]
    return jnp.concatenate([jnp.broadcast_to(b[o:o + 1], (span, width)) for o in offsets], axis=0)


def _hgrn_chunks(problems):
    n = len(problems)
    c, width = problems[0][0].shape
    qs = [p[0] for p in problems]
    vs = [p[2] for p in problems]
    sts = [p[3] for p in problems]
    rev = [p[4] for p in problems]
    chains = range(n)
    masks = _head_masks(width)
    ti = lax.broadcasted_iota(jnp.int32, (c, c), 0)
    si = lax.broadcasted_iota(jnp.int32, (c, c), 1)
    tri = {False: (si <= ti).astype(BF16), True: (si >= ti).astype(BF16)}
    trow = lax.broadcasted_iota(jnp.int32, (c, 1), 0)
    t_full = lax.broadcasted_iota(jnp.int32, (c, width), 0)
    s_full = lax.broadcasted_iota(jnp.int32, (c, width), 1) % c

    ks = [1.0 - p[1] for p in problems]
    rem = [jnp.log(p[1]) for p in problems]
    b = [None] * n
    for _ in range(3):
        for j in chains:
            piece = rem[j].astype(BF16)
            rem[j] = rem[j] - piece.astype(F32)
            part = _dot(tri[rev[j]], piece)
            b[j] = part if b[j] is None else b[j] + part
    b_end = [b[j][0:1] if rev[j] else b[j][c - 1:c] for j in chains]

    o = [lax.dot_general((qs[j] * jnp.exp(b[j])).astype(BF16), sts[j].astype(BF16), NT,
                         preferred_element_type=F32) for j in chains]

    a = [None] * n
    m = c // 2
    while m >= DIAG_BLOCK:
        blocks = c // (2 * m)
        same = (t_full // (2 * m)) == (s_full // (2 * m))
        for j in chains:
            ref = _ref_rows(b[j], [i * 2 * m + (m if rev[j] else m - 1) for i in range(blocks)],
                            2 * m)
            is_q = ((trow % (2 * m)) < m) if rev[j] else ((trow % (2 * m)) >= m)
            e = jnp.exp(jnp.where(is_q, b[j] - ref, ref - b[j]))
            ql = jnp.where(is_q, qs[j] * e, 0.0).astype(BF16)
            kl = jnp.where(is_q, 0.0, ks[j] * e).astype(BF16)
            al = lax.dot_general(ql, _stack_heads(kl, masks), NT, preferred_element_type=F32)
            if blocks > 1:
                al = jnp.where(same, al, 0.0)
            a[j] = al if a[j] is None else a[j] + al
        m //= 2
    blocks = c // DIAG_BLOCK
    mid = DIAG_BLOCK // 2
    same = (t_full // DIAG_BLOCK) == (s_full // DIAG_BLOCK)
    for j in chains:
        ref = _ref_rows(b[j], [i * DIAG_BLOCK + (mid if rev[j] else mid - 1) for i in range(blocks)],
                        DIAG_BLOCK)
        d = b[j] - ref
        ql = (qs[j] * jnp.exp(d)).astype(BF16)
        kl = (ks[j] * jnp.exp(-d)).astype(BF16)
        al = lax.dot_general(ql, _stack_heads(kl, masks), NT, preferred_element_type=F32)
        causal = (s_full >= t_full) if rev[j] else (s_full <= t_full)
        a[j] = a[j] + jnp.where(same & causal, al, 0.0)

    v_b = [v.astype(BF16) for v in vs]
    o = [o[j] + _dot(a[j].astype(BF16), _stack_heads(v_b[j], masks)) for j in chains]

    bd = _block_diag_mask(width)
    upd = [lax.dot_general(v_b[j], (ks[j] * jnp.exp(b_end[j] - b[j])).astype(BF16), TN,
                           preferred_element_type=F32) for j in chains]
    st_new = [sts[j] * jnp.exp(b_end[j]) + jnp.where(bd, upd[j], 0.0) for j in chains]
    return list(zip(o, st_new))


def _mxu_transpose(x):
    n = x.shape[1]
    r = lax.broadcasted_iota(jnp.int32, (n, n), 0)
    c = lax.broadcasted_iota(jnp.int32, (n, n), 1)
    eye = (r == c).astype(BF16)
    acc = None
    rem = x
    for _ in range(3):
        piece = rem.astype(BF16)
        rem = rem - piece.astype(F32)
        part = lax.dot_general(eye, piece, NT, preferred_element_type=F32)
        acc = part if acc is None else acc + part
    return acc


def _hgrn_kernel(*refs, li, has_state, want_state, n_alias, heads):
    refs = list(refs)
    q_ref, ff_ref, fb_ref, v_ref, gate_ref, gn_ref = refs[:6]
    pos = 6
    if has_state:
        s0f_ref, s0b_ref = refs[pos:pos + 2]
        pos += 2
    pos += n_alias
    o_ref = refs[pos]
    pos += 1
    if want_state:
        sf_ref, sb_ref = refs[pos:pos + 2]
        pos += 2
    st_ref, of_ref, ob_ref = refs[pos:pos + 3]
    nb, t, width = q_ref.shape
    nc = t // CHUNK
    bd = _block_diag_mask(width)

    for n in range(nb):
        for d in range(2):
            if has_state:
                x = (s0b_ref if d else s0f_ref)[n, 0].reshape(width, HEAD_DIM)
                xt = _mxu_transpose(x)
                st_ref[2 * n + d] = jnp.where(bd, jnp.concatenate([xt] * heads, axis=0), 0.0)
            else:
                st_ref[2 * n + d] = jnp.zeros((width, width), F32)

    def body(ci, carry):
        rows = (pl.ds(pl.multiple_of(ci * CHUNK, CHUNK), CHUNK),
                pl.ds(pl.multiple_of((nc - 1 - ci) * CHUNK, CHUNK), CHUNK))
        loaded = []
        for n in range(nb):
            for d, f_ref in enumerate((ff_ref, fb_ref)):
                r = rows[d]
                loaded.append((q_ref[n, r, :], f_ref[n, r, :], v_ref[n, r, :], st_ref[2 * n + d],
                               bool(d)))
        for j, (o, st) in enumerate(_hgrn_chunks(loaded)):
            n, d = divmod(j, 2)
            (ob_ref if d else of_ref)[n, rows[d], :] = o
            st_ref[j] = st
        return carry

    lax.fori_loop(0, nc, body, 0)

    for n in range(nb):
        o = of_ref[n] + ob_ref[n]
        o_ref[n] = (_group_rms(o, gn_ref[li:li + 1, :], HEAD_DIM) * gate_ref[n]).astype(o_ref.dtype)

    if want_state:
        for n in range(nb):
            for d, dst in enumerate((sf_ref, sb_ref)):
                st = st_ref[2 * n + d]
                rows = st[0:HEAD_DIM]
                for h in range(1, heads):
                    rows = rows + st[h * HEAD_DIM:(h + 1) * HEAD_DIM]
                final = _mxu_transpose(rows).reshape(heads, HEAD_DIM, HEAD_DIM)
                for slot in range(dst.shape[1]):
                    dst[n, slot] = final


def _hgrn(hq, ff, fb, hv, hg, gn, state, state_prev, *, li, depth, want_state, nb):
    bsz, t, width = hq.shape
    heads = width // HEAD_DIM
    has_state = state is not None
    seq = pl.BlockSpec((nb, t, width), lambda b: (b, 0, 0))
    in_specs = [seq] * 5 + [pl.BlockSpec(gn.shape, lambda b: (0, 0))]
    args = [hq, ff, fb, hv, hg, gn]
    if has_state:
        in_specs += [pl.BlockSpec((nb, 1, heads, HEAD_DIM, HEAD_DIM), lambda b: (b, li, 0, 0, 0))] * 2
        args += list(state)
    aliases = {}
    if state_prev is not None:
        for j, buf in enumerate(state_prev):
            aliases[len(args)] = 1 + j
            in_specs.append(pl.BlockSpec(memory_space=pl.ANY))
            args.append(buf)
    out_specs = [seq]
    out_shape = [jax.ShapeDtypeStruct((bsz, t, width), BF16)]
    if want_state:
        slots, slot0 = (depth, 0) if state_prev is None else (1, li)
        out_specs += [pl.BlockSpec((nb, slots, heads, HEAD_DIM, HEAD_DIM),
                                   lambda b: (b, slot0, 0, 0, 0))] * 2
        out_shape += [jax.ShapeDtypeStruct((bsz, depth, heads, HEAD_DIM, HEAD_DIM), F32)] * 2
    return pl.pallas_call(
        functools.partial(_hgrn_kernel, li=li, has_state=has_state, want_state=want_state,
                          n_alias=len(aliases), heads=heads),
        grid=(bsz // nb,),
        in_specs=in_specs, out_specs=out_specs, out_shape=out_shape,
        input_output_aliases=aliases,
        scratch_shapes=[pltpu.VMEM((2 * nb, width, width), F32),
                        pltpu.VMEM((nb, t, width), F32), pltpu.VMEM((nb, t, width), F32)],
        compiler_params=_params(1),
        name="hgrn2_latent" if has_state else "hgrn2_context",
    )(*args)


def _out_mlp_kernel(x_ref, oa_ref, ob_ref, oc_ref, mod_ref, wo_ref, w1_ref, w2_ref,
                    g1_ref, b1_ref, g2_ref, b2_ref, y_ref, *, li, d, alpha, ff_chunk):
    wa, wb = oa_ref.shape[-1], ob_ref.shape[-1]
    layer = slice(li, li + 1)
    m = (_dot(oa_ref[...], wo_ref[0:wa, :]) + _dot(ob_ref[...], wo_ref[wa:wa + wb, :])
         + _dot(oc_ref[...], wo_ref[wa + wb:, :]))
    gate1 = mod_ref[0, :, 2 * d:3 * d]
    shift2 = mod_ref[0, :, 3 * d:4 * d]
    gain2 = mod_ref[0, :, 4 * d:5 * d]
    gate2 = mod_ref[0, :, 5 * d:6 * d]
    x1 = _layernorm(alpha * x_ref[...] + gate1 * m, g1_ref[layer, :], b1_ref[layer, :])
    h2 = (x1 * (1.0 + gain2) + shift2).astype(BF16)
    acc = None
    for j in range(w1_ref.shape[-1] // ff_chunk):
        cols = slice(j * ff_chunk, (j + 1) * ff_chunk)
        hid = jnp.maximum(_dot(h2, w1_ref[:, cols]), 0.0)
        part = _dot((hid * hid).astype(BF16), w2_ref[cols, :])
        acc = part if acc is None else acc + part
    y_ref[...] = _layernorm(alpha * x1 + gate2 * acc, g2_ref[layer, :], b2_ref[layer, :])


def _out_mlp(x, oa, ob, oc, mod, mod_row, w_out, w_ff1, w_ff2, ln, *, li, alpha):
    bsz, t, d = x.shape
    tiles = t // ROW_TILE
    row = lambda i: (i, 0)
    const = lambda i: (0, 0)
    resident = lambda a: pl.BlockSpec((None,) + a.shape[1:], lambda i: (li, 0, 0),
                                      pipeline_mode=pl.Buffered(1))
    in_specs = [pl.BlockSpec((ROW_TILE, d), row),
                pl.BlockSpec((ROW_TILE, oa.shape[-1]), row),
                pl.BlockSpec((ROW_TILE, ob.shape[-1]), row),
                pl.BlockSpec((ROW_TILE, oc.shape[-1]), row),
                pl.BlockSpec((1, 1, mod.shape[-1]), lambda i: (mod_row(i // tiles), 0, 0)),
                resident(w_out), resident(w_ff1), resident(w_ff2)]
    in_specs += [pl.BlockSpec(a.shape, const) for a in ln]
    y = pl.pallas_call(
        functools.partial(_out_mlp_kernel, li=li, d=d, alpha=alpha, ff_chunk=1024),
        grid=(bsz * tiles,),
        in_specs=in_specs,
        out_specs=pl.BlockSpec((ROW_TILE, d), row),
        out_shape=jax.ShapeDtypeStruct((bsz * t, d), F32),
        compiler_params=_params(1),
        name="out_mlp",
    )(x.reshape(bsz * t, d), oa, ob.reshape(bsz * t, -1), oc, mod, w_out, w_ff1, w_ff2, *ln)
    return y.reshape(bsz, t, d)


def _rope_tables(n_tokens):
    pairs = HEAD_DIM // 4
    tok = jnp.arange(n_tokens)
    row = (tok // GRID_W).astype(F32)
    col = (tok % GRID_W).astype(F32)
    inv = ROPE_THETA ** (-jnp.arange(pairs, dtype=F32) / pairs)
    ang = jnp.concatenate([row[:, None] * inv, col[:, None] * inv], axis=-1)
    lane = jnp.arange(LANES)
    pair = (lane % HEAD_DIM) // 2
    sign = jnp.where(lane % 2 == 0, -1.0, 1.0).astype(F32)
    return jnp.cos(ang)[:, pair], jnp.sin(ang)[:, pair] * sign


def kernel(x_prompt, x_sample, cache_a_k, cache_a_v, cache_c_k, cache_c_v, state_b_fwd, state_b_bwd, c, c_ctx, w_ada, b_ada, w_in, w_out, lam_q1, lam_k1, lam_q2, lam_k2, subln_g, lb_logits_fwd, lb_logits_bwd, gnorm_g, qnorm_g, knorm_g, ln1_g, ln1_b, ln2_g, ln2_b, w_ff1, w_ff2):
    depth = w_in.shape[0]
    bsz, seq, d = x_prompt.shape
    dec_bsz, dec_seq, _ = x_sample.shape
    past = cache_a_k.shape[2]
    alpha = (2 * depth) ** 0.25
    mix_a, mix_b, mix_c = d // 2, d // 4, d // 4

    cond = jnp.concatenate([c_ctx[None, :], c, jnp.zeros((8 - 1 - dec_bsz, d), F32)], axis=0)
    mod = _modulation(cond, w_ada, b_ada)
    rope = _rope_tables(dec_seq)

    cache = (cache_a_k.reshape(dec_bsz, depth, past, mix_a),
             cache_a_v.reshape(dec_bsz, depth, past, mix_a),
             cache_c_k.reshape(dec_bsz, depth, past, mix_c // 2),
             cache_c_v.reshape(dec_bsz, depth, past, mix_c // 2))
    lam = (lam_q1, lam_k1, lam_q2, lam_k2)

    mod = mod.reshape(depth * 8, 1, 6 * d)
    weights = tuple(w.astype(BF16) for w in (w_in, w_out, w_ff1, w_ff2))
    qn = jnp.tile(qnorm_g, (1, mix_c // HEAD_DIM))
    kn = jnp.tile(knorm_g, (1, mix_c // 2 // HEAD_DIM))
    gn = jnp.tile(gnorm_g, (1, mix_b // HEAD_DIM))
    ln = (ln1_g, ln1_b, ln2_g, ln2_b)

    def stream(x, li, latent, own_prev):
        w_in_b, w_out_b, w1_b, w2_b = weights
        n, t, _ = x.shape
        mod_row = (lambda b: li * 8 + b + 1) if latent else (lambda b: li * 8)
        kv_layer = 0 if latent else li
        (qa, ka, va, hq, ff, fb, hv, hg, qc, kc, vc) = _in_proj(
            x, mod, mod_row, w_in_b, lb_logits_fwd, lb_logits_bwd, qn, kn,
            rope if latent else None, None if own_prev is None else own_prev[0:4], li=li)
        tq = 256
        oa = _diff_attn(qa, ka, va, cache[0:2] if latent else None, lam, subln_g,
                        li=li, kv_layer=kv_layer, bsz=n, tq=tq, heads_per_step=2 if latent else 4)
        oc = _gqa(qc, kc, vc, cache[2:4] if latent else None, li=li, kv_layer=kv_layer, bsz=n, tq=tq)
        hres = _hgrn(hq, ff, fb, hv, hg, gn, (state_b_fwd, state_b_bwd) if latent else None,
                     None if own_prev is None else own_prev[4:6],
                     li=li, depth=depth, want_state=not latent, nb=2)
        y = _out_mlp(x, oa, hres[0], oc, mod, mod_row, w_out_b, w1_b, w2_b, ln, li=li, alpha=alpha)
        own = None if latent else (ka, va, kc, vc, hres[1], hres[2])
        return y, own

    y_prompt, y_sample = x_prompt, x_sample
    own = None
    for li in range(depth):
        y_prompt, own = stream(y_prompt, li, False, own)
        y_sample, _ = stream(y_sample, li, True, None)

    heads_a = mix_a // (2 * HEAD_DIM)
    new_a_k = own[0].reshape(bsz, depth, seq, heads_a, 2, HEAD_DIM)
    new_a_v = own[1].reshape(bsz, depth, seq, heads_a, 2 * HEAD_DIM)
    new_c_k = own[2].reshape(bsz, depth, seq, mix_c // 2 // HEAD_DIM, HEAD_DIM)
    new_c_v = own[3].reshape(bsz, depth, seq, mix_c // 2 // HEAD_DIM, HEAD_DIM)
    return (y_prompt, y_sample, new_a_k, new_a_v, new_c_k, new_c_v, own[4], own[5])
```

```python
import functools
import math

import jax
import jax.numpy as jnp
from jax import lax
from jax.experimental import pallas as pl
from jax.experimental.pallas import tpu as pltpu

GRID_W = 64
HEAD_DIM = 64
ROPE_THETA = 10000.0
LN_EPS = 1e-6
RMS_EPS = 1e-6
F_MIN = 1e-6
CHUNK = 64
DIAG_BLOCK = 8
LANES = 128
ROW_TILE = 256
VMEM_LIMIT = 56 * 1024 * 1024

F32 = jnp.float32
BF16 = jnp.bfloat16
NT = (((1,), (1,)), ((), ()))
TN = (((0,), (0,)), ((), ()))


def _params(n_grid):
    return pltpu.CompilerParams(dimension_semantics=("arbitrary",) * n_grid,
                                vmem_limit_bytes=VMEM_LIMIT)


def _dot(a, b):
    return jnp.dot(a, b, preferred_element_type=F32)


def _split_dot(a, b_bf16, passes, dims=None):
    acc = None
    rem = a
    for _ in range(passes):
        piece = rem.astype(BF16)
        rem = rem - piece.astype(F32)
        part = (_dot(piece, b_bf16) if dims is None
                else lax.dot_general(piece, b_bf16, dims, preferred_element_type=F32))
        acc = part if acc is None else acc + part
    return acc


def _group_ones(n, group):
    r = lax.broadcasted_iota(jnp.int32, (n, n), 0) // group
    c = lax.broadcasted_iota(jnp.int32, (n, n), 1) // group
    return (r == c).astype(BF16)


def _group_mean_square(x, group):
    n = x.shape[-1]
    return _split_dot(x * x, _group_ones(n, group), 2) * (1.0 / group)


def _group_rms(x, g_row, group):
    return x * lax.rsqrt(_group_mean_square(x, group) + RMS_EPS) * g_row


def _pair_swap(x):
    lane = lax.broadcasted_iota(jnp.int32, x.shape, 1)
    return jnp.where(lane % 2 == 0, pltpu.roll(x, LANES - 1, 1), pltpu.roll(x, 1, 1))


def _rope(x, cos, sin):
    blocks = []
    for j in range(x.shape[-1] // LANES):
        blk = x[:, j * LANES:(j + 1) * LANES]
        blocks.append(blk * cos + _pair_swap(blk) * sin)
    return blocks[0] if len(blocks) == 1 else jnp.concatenate(blocks, axis=-1)


def _silu(x):
    return x * jax.nn.sigmoid(x)


def _layernorm(x, g, b):
    mu = jnp.mean(x, axis=-1, keepdims=True)
    xc = x - mu
    var = jnp.mean(xc * xc, axis=-1, keepdims=True)
    return xc * lax.rsqrt(var + LN_EPS) * g + b


def _mod_kernel(c_ref, w_ref, b_ref, o_ref):
    s = _silu(c_ref[...]).astype(BF16)
    o_ref[0] = _dot(s, w_ref[0].astype(BF16)) + b_ref[0]


def _modulation(cond, w_ada, b_ada):
    depth, d, n = w_ada.shape
    tn = 1536
    rows = cond.shape[0]
    return pl.pallas_call(
        _mod_kernel,
        grid=(depth, n // tn),
        in_specs=[pl.BlockSpec((rows, d), lambda l, j: (0, 0)),
                  pl.BlockSpec((1, d, tn), lambda l, j: (l, 0, j)),
                  pl.BlockSpec((1, 1, tn), lambda l, j: (l, 0, j))],
        out_specs=pl.BlockSpec((1, rows, tn), lambda l, j: (l, 0, j)),
        out_shape=jax.ShapeDtypeStruct((depth, rows, n), F32),
        compiler_params=_params(2),
        name="adaln_modulation",
    )(cond, w_ada, b_ada.reshape(depth, 1, n))


def _in_proj_kernel(*refs, li, d, latent, n_alias):
    refs = list(refs)
    x_ref, mod_ref, w_ref, lbf_ref, lbb_ref, qn_ref, kn_ref = refs[:7]
    pos = 7
    if latent:
        cos, sin = refs[pos][...], refs[pos + 1][...]
        pos += 2
    pos += n_alias
    qa_o, ka_o, va_o, hq_o, ff_o, fb_o, hv_o, hg_o, qc_o, kc_o, vc_o = refs[pos:]

    def store_kv(ref, val):
        for slot in range(ref.shape[1]):
            ref[0, slot] = val.astype(ref.dtype)

    mix_a, mix_b, mix_c = d // 2, d // 4, d // 4
    kv_c = mix_c // 2
    scale = HEAD_DIM ** -0.5 * math.log2(math.e)

    shift = mod_ref[0, :, 0:d]
    gain = mod_ref[0, :, d:2 * d]
    h = (x_ref[...] * (1.0 + gain) + shift).astype(BF16)

    def proj(start, width):
        return _dot(h, w_ref[:, start:start + width])

    off_b = 3 * mix_a
    off_c = off_b + 5 * mix_b

    zq = proj(off_c, mix_c)
    zk = proj(off_c + mix_c, kv_c)
    vc = proj(off_c + mix_c + kv_c, kv_c)
    qa = proj(0, mix_a)
    msq = _group_mean_square(zq, HEAD_DIM)
    msk = _group_mean_square(zk, HEAD_DIM)
    ka = proj(mix_a, mix_a)
    va = proj(2 * mix_a, mix_a)
    qc = zq * lax.rsqrt(msq + RMS_EPS) * qn_ref[li:li + 1, :]
    kc = zk * lax.rsqrt(msk + RMS_EPS) * kn_ref[li:li + 1, :]

    if latent:
        qa = _rope(qa, cos, sin)
        ka = _rope(ka, cos, sin)
    qa_o[...] = (qa * scale).astype(qa_o.dtype)
    store_kv(ka_o, ka)
    store_kv(va_o, va)

    def lower_bound(ref):
        logits = ref[...]
        e = jnp.exp(logits - jnp.max(logits, axis=0, keepdims=True))
        sm = e / jnp.sum(e, axis=0, keepdims=True)
        return jnp.sum(sm[0:li + 1], axis=0, keepdims=True) - sm[0:1]

    def forget(x, lb):
        return jnp.maximum(lb + (1.0 - lb) * jax.nn.sigmoid(x), F_MIN)

    off = off_b
    zb = [proj(off + j * mix_b, mix_b) for j in range(5)]

    if latent:
        kc = _rope(kc, cos, sin)
        qc = _rope(qc, cos, sin)
    store_kv(kc_o, kc)
    store_kv(vc_o, vc)
    qc = qc * scale
    lane = lax.broadcasted_iota(jnp.int32, (1, LANES), 1)
    for n in range(2):
        blk = qc[:, n * LANES:(n + 1) * LANES]
        in_half = (lane // HEAD_DIM) == n
        for g in range(2):
            src = blk if g == n else pltpu.roll(blk, HEAD_DIM, 1)
            hc = 2 * n + g
            qc_o[:, hc * LANES:(hc + 1) * LANES] = jnp.where(in_half, src, 0.0).astype(qc_o.dtype)

    hq_o[0] = _silu(zb[0])
    ff_o[0] = forget(zb[1], lower_bound(lbf_ref))
    fb_o[0] = forget(zb[2], lower_bound(lbb_ref))
    hv_o[0] = zb[3]
    hg_o[0] = _silu(zb[4])


def _in_proj(x, mod, mod_row, w_in, lb_f, lb_b, qn, kn, rope, kv_prev, *, li):
    bsz, t, d = x.shape
    latent = rope is not None
    depth, _, n_in = w_in.shape
    tiles = t // ROW_TILE
    mix_a, mix_b, mix_c = d // 2, d // 4, d // 4
    kv_c = mix_c // 2
    x2 = x.reshape(bsz * t, d)

    row = lambda i: (i, 0)
    brow = lambda i: (i // tiles, i % tiles, 0)
    const = lambda i: (0, 0)
    in_specs = [pl.BlockSpec((ROW_TILE, d), row),
                pl.BlockSpec((1, 1, mod.shape[-1]), lambda i: (mod_row(i * ROW_TILE), 0, 0)),
                pl.BlockSpec((None, d, n_in), lambda i: (li, 0, 0)),
                pl.BlockSpec(lb_f.shape, const), pl.BlockSpec(lb_b.shape, const),
                pl.BlockSpec(qn.shape, const), pl.BlockSpec(kn.shape, const)]
    args = [x2, mod, w_in, lb_f, lb_b, qn, kn]
    if latent:
        in_specs += [pl.BlockSpec((ROW_TILE, LANES), lambda i: (i % tiles, 0))] * 2
        args += list(rope)
        kv_shape, kv_slots, kv_slot0, kdt = (bsz, 1, t), 1, 0, BF16
    else:
        first = kv_prev is None
        kv_shape, kv_slots, kv_slot0, kdt = (bsz, depth, t), (depth if first else 1), (0 if first else li), F32
    aliases = {}
    if kv_prev is not None:
        kv_out_index = (1, 2, 9, 10)
        for buf, out_index in zip(kv_prev, kv_out_index):
            aliases[len(args)] = out_index
            in_specs.append(pl.BlockSpec(memory_space=pl.ANY))
            args.append(buf)
    krow = lambda i: (i // tiles, kv_slot0, i % tiles, 0)

    def hspec():
        return pl.BlockSpec((1, ROW_TILE, mix_b), brow)

    def kvspec(width):
        return pl.BlockSpec((1, kv_slots, ROW_TILE, width), krow)

    out_specs = [pl.BlockSpec((ROW_TILE, mix_a), row), kvspec(mix_a), kvspec(mix_a),
                 hspec(), hspec(), hspec(), hspec(), hspec(),
                 pl.BlockSpec((ROW_TILE, 2 * mix_c), row), kvspec(kv_c), kvspec(kv_c)]
    out_shape = [jax.ShapeDtypeStruct((bsz * t, mix_a), BF16),
                 jax.ShapeDtypeStruct(kv_shape + (mix_a,), kdt),
                 jax.ShapeDtypeStruct(kv_shape + (mix_a,), kdt)]
    out_shape += [jax.ShapeDtypeStruct((bsz, t, mix_b), F32)] * 5
    out_shape += [jax.ShapeDtypeStruct((bsz * t, 2 * mix_c), BF16),
                  jax.ShapeDtypeStruct(kv_shape + (kv_c,), kdt),
                  jax.ShapeDtypeStruct(kv_shape + (kv_c,), kdt)]
    return pl.pallas_call(
        functools.partial(_in_proj_kernel, li=li, d=d, latent=latent, n_alias=len(aliases)),
        grid=(bsz * tiles,),
        in_specs=in_specs, out_specs=out_specs, out_shape=out_shape,
        input_output_aliases=aliases,
        compiler_params=_params(1),
        name="in_proj_latent" if latent else "in_proj_context",
    )(*args)


def _softmax_parts(scores):
    m = functools.reduce(jnp.maximum, [jnp.max(s, axis=-1, keepdims=True) for s in scores])
    es = [jnp.exp2(s - m) for s in scores]
    denom = functools.reduce(lambda a, b: a + b, [jnp.sum(e, axis=-1, keepdims=True) for e in es])
    return es, 1.0 / denom


def _diff_attn_kernel(*refs, li, cached, heads):
    if cached:
        (q_ref, k_ref, v_ref, ck_ref, cv_ref, lq1, lk1, lq2, lk2, sub_ref, o_ref) = refs
    else:
        (q_ref, k_ref, v_ref, lq1, lk1, lq2, lk2, sub_ref, o_ref) = refs
    lam_init = 0.8 - 0.6 * math.exp(-0.3 * li)

    def lam_term(a, b):
        return jnp.exp(jnp.sum(a[li:li + 1, :] * b[li:li + 1, :], axis=-1, keepdims=True))

    lam = lam_term(lq1, lk1) - lam_term(lq2, lk2) + lam_init
    tq = q_ref.shape[0]
    lane = lax.broadcasted_iota(jnp.int32, (1, LANES), 1)
    cols = [slice(h * LANES, (h + 1) * LANES) for h in range(heads)]

    scores, vals = [], []
    for c in cols:
        q = q_ref[:, c]
        zero = jnp.zeros_like(q)
        q2 = jnp.concatenate([jnp.where(lane < HEAD_DIM, q, zero),
                              jnp.where(lane >= HEAD_DIM, q, zero)], axis=0)
        keys = [k_ref[0, 0, :, c].astype(BF16)]
        vals.append([v_ref[0, 0, :, c].astype(BF16)])
        if cached:
            keys.append(ck_ref[0, 0, :, c].astype(BF16))
            vals[-1].append(cv_ref[0, 0, :, c].astype(BF16))
        scores.append([lax.dot_general(q2, k, NT, preferred_element_type=F32) for k in keys])
    parts = [_softmax_parts(s) for s in scores]
    outs = []
    for (es, r), vs in zip(parts, vals):
        r0 = r[0:tq]
        r1 = r[tq:2 * tq] * lam
        o = None
        for e, v in zip(es, vs):
            part = _dot((e[0:tq] * r0 - e[tq:2 * tq] * r1).astype(BF16), v)
            o = part if o is None else o + part
        outs.append(o)
    gain = sub_ref[li:li + 1, :] * (1.0 - lam_init)
    for c, o in zip(cols, outs):
        ms = jnp.mean(o * o, axis=-1, keepdims=True)
        o_ref[:, c] = (o * lax.rsqrt(ms + RMS_EPS) * gain).astype(o_ref.dtype)


def _diff_attn(q, k, v, cache, lam, subln, *, li, kv_layer, bsz, tq, heads_per_step):
    t = k.shape[2]
    width = q.shape[-1]
    wstep = heads_per_step * LANES
    nq = t // tq
    cached = cache is not None
    in_specs = [pl.BlockSpec((tq, wstep), lambda b, h, i: (b * nq + i, h)),
                pl.BlockSpec((1, 1, t, wstep), lambda b, h, i: (b, kv_layer, 0, h)),
                pl.BlockSpec((1, 1, t, wstep), lambda b, h, i: (b, kv_layer, 0, h))]
    args = [q, k, v]
    if cached:
        p = cache[0].shape[2]
        in_specs += [pl.BlockSpec((1, 1, p, wstep), lambda b, h, i: (b, li, 0, h))] * 2
        args += list(cache)
    in_specs += [pl.BlockSpec(a.shape, lambda b, h, i: (0, 0)) for a in (*lam, subln)]
    args += [*lam, subln]
    return pl.pallas_call(
        functools.partial(_diff_attn_kernel, li=li, cached=cached, heads=heads_per_step),
        grid=(bsz, width // wstep, nq),
        in_specs=in_specs,
        out_specs=pl.BlockSpec((tq, wstep), lambda b, h, i: (b * nq + i, h)),
        out_shape=jax.ShapeDtypeStruct((bsz * t, width), BF16),
        compiler_params=_params(3),
        name="diff_attn_latent" if cached else "diff_attn_context",
    )(*args)


def _gqa_kernel(*refs, cached):
    if cached:
        q_ref, k_ref, v_ref, ck_ref, cv_ref, o_ref = refs
    else:
        q_ref, k_ref, v_ref, o_ref = refs
    tq = q_ref.shape[0]
    heads = q_ref.shape[1] // LANES
    lane = lax.broadcasted_iota(jnp.int32, (1, LANES), 1)
    keys = [k_ref[0, 0].astype(BF16)]
    vals = [v_ref[0, 0].astype(BF16)]
    if cached:
        keys.append(ck_ref[0, 0].astype(BF16))
        vals.append(cv_ref[0, 0].astype(BF16))
    groups = range(heads // 2)
    scores = []
    for n in groups:
        q = jnp.concatenate([q_ref[:, (2 * n + g) * LANES:(2 * n + g + 1) * LANES]
                             for g in range(2)], axis=0)
        scores.append([lax.dot_general(q, k, NT, preferred_element_type=F32) for k in keys])
    parts = [_softmax_parts(s) for s in scores]
    outs = []
    for es, r in parts:
        o = None
        for e, v in zip(es, vals):
            part = _dot(e.astype(BF16), v)
            o = part if o is None else o + part
        outs.append(o * r)
    for n in groups:
        first = outs[n][0:tq]
        second = outs[n][tq:2 * tq]
        if n == 0:
            second = pltpu.roll(second, HEAD_DIM, 1)
        else:
            first = pltpu.roll(first, HEAD_DIM, 1)
        o_ref[:, n * LANES:(n + 1) * LANES] = jnp.where(lane < HEAD_DIM, first,
                                                        second).astype(o_ref.dtype)


def _gqa(q, k, v, cache, *, li, kv_layer, bsz, tq):
    t = k.shape[2]
    kvw = k.shape[-1]
    nq = t // tq
    cached = cache is not None
    in_specs = [pl.BlockSpec((tq, q.shape[-1]), lambda b, i: (b * nq + i, 0)),
                pl.BlockSpec((1, 1, t, kvw), lambda b, i: (b, kv_layer, 0, 0)),
                pl.BlockSpec((1, 1, t, kvw), lambda b, i: (b, kv_layer, 0, 0))]
    args = [q, k, v]
    if cached:
        p = cache[0].shape[2]
        in_specs += [pl.BlockSpec((1, 1, p, kvw), lambda b, i: (b, li, 0, 0))] * 2
        args += list(cache)
    return pl.pallas_call(
        functools.partial(_gqa_kernel, cached=cached),
        grid=(bsz, nq),
        in_specs=in_specs,
        out_specs=pl.BlockSpec((tq, 2 * kvw), lambda b, i: (b * nq + i, 0)),
        out_shape=jax.ShapeDtypeStruct((bsz * t, 2 * kvw), BF16),
        compiler_params=_params(2),
        name="gqa_latent" if cached else "gqa_context",
    )(*args)


def _head_masks(width):
    lane_head = lax.broadcasted_iota(jnp.int32, (1, width), 1) // HEAD_DIM
    return [lane_head == h for h in range(width // HEAD_DIM)]


def _stack_heads(x, masks):
    return jnp.concatenate([jnp.where(m, x, jnp.zeros_like(x)) for m in masks], axis=0)


def _block_diag_mask(width):
    r = lax.broadcasted_iota(jnp.int32, (width, width), 0) // HEAD_DIM
    c = lax.broadcasted_iota(jnp.int32, (width, width), 1) // HEAD_DIM
    return r == c


def _ref_rows(b, offsets, span):
    width = b.shape[-1]
    return jnp.concatenate([jnp.broadcast_to(b[o:o + 1], (span, width)) for o in offsets], axis=0)


def _hgrn_chunks(problems):
    n = len(problems)
    c, width = problems[0][0].shape
    qs = [p[0] for p in problems]
    vs = [p[2] for p in problems]
    sts = [p[3] for p in problems]
    rev = [p[4] for p in problems]
    chains = range(n)
    masks = _head_masks(width)
    ti = lax.broadcasted_iota(jnp.int32, (c, c), 0)
    si = lax.broadcasted_iota(jnp.int32, (c, c), 1)
    tri = {False: (si <= ti).astype(BF16), True: (si >= ti).astype(BF16)}
    trow = lax.broadcasted_iota(jnp.int32, (c, 1), 0)
    t_full = lax.broadcasted_iota(jnp.int32, (c, width), 0)
    s_full = lax.broadcasted_iota(jnp.int32, (c, width), 1) % c

    ks = [1.0 - p[1] for p in problems]
    rem = [jnp.log(p[1]) for p in problems]
    b = [None] * n
    for _ in range(3):
        for j in chains:
            piece = rem[j].astype(BF16)
            rem[j] = rem[j] - piece.astype(F32)
            part = _dot(tri[rev[j]], piece)
            b[j] = part if b[j] is None else b[j] + part
    b_end = [b[j][0:1] if rev[j] else b[j][c - 1:c] for j in chains]

    o = [lax.dot_general((qs[j] * jnp.exp(b[j])).astype(BF16), sts[j].astype(BF16), NT,
                         preferred_element_type=F32) for j in chains]

    a = [None] * n
    m = c // 2
    while m >= DIAG_BLOCK:
        blocks = c // (2 * m)
        same = (t_full // (2 * m)) == (s_full // (2 * m))
        for j in chains:
            ref = _ref_rows(b[j], [i * 2 * m + (m if rev[j] else m - 1) for i in range(blocks)],
                            2 * m)
            is_q = ((trow % (2 * m)) < m) if rev[j] else ((trow % (2 * m)) >= m)
            e = jnp.exp(jnp.where(is_q, b[j] - ref, ref - b[j]))
            ql = jnp.where(is_q, qs[j] * e, 0.0).astype(BF16)
            kl = jnp.where(is_q, 0.0, ks[j] * e).astype(BF16)
            al = lax.dot_general(ql, _stack_heads(kl, masks), NT, preferred_element_type=F32)
            if blocks > 1:
                al = jnp.where(same, al, 0.0)
            a[j] = al if a[j] is None else a[j] + al
        m //= 2
    blocks = c // DIAG_BLOCK
    mid = DIAG_BLOCK // 2
    same = (t_full // DIAG_BLOCK) == (s_full // DIAG_BLOCK)
    for j in chains:
        ref = _ref_rows(b[j], [i * DIAG_BLOCK + (mid if rev[j] else mid - 1) for i in range(blocks)],
                        DIAG_BLOCK)
        d = b[j] - ref
        ql = (qs[j] * jnp.exp(d)).astype(BF16)
        kl = (ks[j] * jnp.exp(-d)).astype(BF16)
        al = lax.dot_general(ql, _stack_heads(kl, masks), NT, preferred_element_type=F32)
        causal = (s_full >= t_full) if rev[j] else (s_full <= t_full)
        a[j] = a[j] + jnp.where(same & causal, al, 0.0)

    v_b = [v.astype(BF16) for v in vs]
    o = [o[j] + _dot(a[j].astype(BF16), _stack_heads(v_b[j], masks)) for j in chains]

    bd = _block_diag_mask(width)
    upd = [lax.dot_general(v_b[j], (ks[j] * jnp.exp(b_end[j] - b[j])).astype(BF16), TN,
                           preferred_element_type=F32) for j in chains]
    st_new = [sts[j] * jnp.exp(b_end[j]) + jnp.where(bd, upd[j], 0.0) for j in chains]
    return list(zip(o, st_new))


def _mxu_transpose(x):
    n = x.shape[1]
    r = lax.broadcasted_iota(jnp.int32, (n, n), 0)
    c = lax.broadcasted_iota(jnp.int32, (n, n), 1)
    eye = (r == c).astype(BF16)
    acc = None
    rem = x
    for _ in range(3):
        piece = rem.astype(BF16)
        rem = rem - piece.astype(F32)
        part = lax.dot_general(eye, piece, NT, preferred_element_type=F32)
        acc = part if acc is None else acc + part
    return acc


def _hgrn_kernel(*refs, li, has_state, want_state, n_alias, heads):
    refs = list(refs)
    q_ref, ff_ref, fb_ref, v_ref, gate_ref, gn_ref = refs[:6]
    pos = 6
    if has_state:
        s0f_ref, s0b_ref = refs[pos:pos + 2]
        pos += 2
    pos += n_alias
    o_ref = refs[pos]
    pos += 1
    if want_state:
        sf_ref, sb_ref = refs[pos:pos + 2]
        pos += 2
    st_ref, of_ref, ob_ref = refs[pos:pos + 3]
    nb, t, width = q_ref.shape
    nc = t // CHUNK
    bd = _block_diag_mask(width)

    for n in range(nb):
        for d in range(2):
            if has_state:
                x = (s0b_ref if d else s0f_ref)[n, 0].reshape(width, HEAD_DIM)
                xt = _mxu_transpose(x)
                st_ref[2 * n + d] = jnp.where(bd, jnp.concatenate([xt] * heads, axis=0), 0.0)
            else:
                st_ref[2 * n + d] = jnp.zeros((width, width), F32)

    def body(ci, carry):
        rows = (pl.ds(pl.multiple_of(ci * CHUNK, CHUNK), CHUNK),
                pl.ds(pl.multiple_of((nc - 1 - ci) * CHUNK, CHUNK), CHUNK))
        loaded = []
        for n in range(nb):
            for d, f_ref in enumerate((ff_ref, fb_ref)):
                r = rows[d]
                loaded.append((q_ref[n, r, :], f_ref[n, r, :], v_ref[n, r, :], st_ref[2 * n + d],
                               bool(d)))
        for j, (o, st) in enumerate(_hgrn_chunks(loaded)):
            n, d = divmod(j, 2)
            (ob_ref if d else of_ref)[n, rows[d], :] = o
            st_ref[j] = st
        return carry

    lax.fori_loop(0, nc, body, 0)

    for n in range(nb):
        o = of_ref[n] + ob_ref[n]
        o_ref[n] = (_group_rms(o, gn_ref[li:li + 1, :], HEAD_DIM) * gate_ref[n]).astype(o_ref.dtype)

    if want_state:
        for n in range(nb):
            for d, dst in enumerate((sf_ref, sb_ref)):
                st = st_ref[2 * n + d]
                rows = st[0:HEAD_DIM]
                for h in range(1, heads):
                    rows = rows + st[h * HEAD_DIM:(h + 1) * HEAD_DIM]
                final = _mxu_transpose(rows).reshape(heads, HEAD_DIM, HEAD_DIM)
                for slot in range(dst.shape[1]):
                    dst[n, slot] = final


def _hgrn(hq, ff, fb, hv, hg, gn, state, state_prev, *, li, depth, want_state, nb):
    bsz, t, width = hq.shape
    heads = width // HEAD_DIM
    has_state = state is not None
    seq = pl.BlockSpec((nb, t, width), lambda b: (b, 0, 0))
    in_specs = [seq] * 5 + [pl.BlockSpec(gn.shape, lambda b: (0, 0))]
    args = [hq, ff, fb, hv, hg, gn]
    if has_state:
        in_specs += [pl.BlockSpec((nb, 1, heads, HEAD_DIM, HEAD_DIM), lambda b: (b, li, 0, 0, 0))] * 2
        args += list(state)
    aliases = {}
    if state_prev is not None:
        for j, buf in enumerate(state_prev):
            aliases[len(args)] = 1 + j
            in_specs.append(pl.BlockSpec(memory_space=pl.ANY))
            args.append(buf)
    out_specs = [seq]
    out_shape = [jax.ShapeDtypeStruct((bsz, t, width), BF16)]
    if want_state:
        slots, slot0 = (depth, 0) if state_prev is None else (1, li)
        out_specs += [pl.BlockSpec((nb, slots, heads, HEAD_DIM, HEAD_DIM),
                                   lambda b: (b, slot0, 0, 0, 0))] * 2
        out_shape += [jax.ShapeDtypeStruct((bsz, depth, heads, HEAD_DIM, HEAD_DIM), F32)] * 2
    return pl.pallas_call(
        functools.partial(_hgrn_kernel, li=li, has_state=has_state, want_state=want_state,
                          n_alias=len(aliases), heads=heads),
        grid=(bsz // nb,),
        in_specs=in_specs, out_specs=out_specs, out_shape=out_shape,
        input_output_aliases=aliases,
        scratch_shapes=[pltpu.VMEM((2 * nb, width, width), F32),
                        pltpu.VMEM((nb, t, width), F32), pltpu.VMEM((nb, t, width), F32)],
        compiler_params=_params(1),
        name="hgrn2_latent" if has_state else "hgrn2_context",
    )(*args)


def _out_mlp_kernel(x_ref, oa_ref, ob_ref, oc_ref, mod_ref, wo_ref, w1_ref, w2_ref,
                    g1_ref, b1_ref, g2_ref, b2_ref, y_ref, *, li, d, alpha, ff_chunk):
    wa, wb = oa_ref.shape[-1], ob_ref.shape[-1]
    layer = slice(li, li + 1)
    gate1 = mod_ref[0, :, 2 * d:3 * d]
    shift2 = mod_ref[0, :, 3 * d:4 * d]
    gain2 = mod_ref[0, :, 4 * d:5 * d]
    gate2 = mod_ref[0, :, 5 * d:6 * d]
    subs = [slice(s * ROW_TILE, (s + 1) * ROW_TILE) for s in range(x_ref.shape[0] // ROW_TILE)]
    m = [_dot(oa_ref[r, :], wo_ref[0:wa, :]) + _dot(ob_ref[r, :], wo_ref[wa:wa + wb, :])
         + _dot(oc_ref[r, :], wo_ref[wa + wb:, :]) for r in subs]
    x1 = [_layernorm(alpha * x_ref[r, :] + gate1 * mi, g1_ref[layer, :], b1_ref[layer, :])
          for r, mi in zip(subs, m)]
    h2 = [(xi * (1.0 + gain2) + shift2).astype(BF16) for xi in x1]
    acc = [None] * len(subs)
    for j in range(w1_ref.shape[-1] // ff_chunk):
        cols = slice(j * ff_chunk, (j + 1) * ff_chunk)
        hid = [jnp.maximum(_dot(hi, w1_ref[:, cols]), 0.0) for hi in h2]
        for s, hd in enumerate(hid):
            part = _dot((hd * hd).astype(BF16), w2_ref[cols, :])
            acc[s] = part if acc[s] is None else acc[s] + part
    for r, xi, ai in zip(subs, x1, acc):
        y_ref[r, :] = _layernorm(alpha * xi + gate2 * ai, g2_ref[layer, :], b2_ref[layer, :])


def _out_mlp(x, oa, ob, oc, mod, mod_row, w_out, w_ff1, w_ff2, ln, *, li, alpha):
    bsz, t, d = x.shape
    rows = 2 * ROW_TILE
    row = lambda i: (i, 0)
    const = lambda i: (0, 0)
    resident = lambda a: pl.BlockSpec((None,) + a.shape[1:], lambda i: (li, 0, 0),
                                      pipeline_mode=pl.Buffered(1))
    in_specs = [pl.BlockSpec((rows, d), row),
                pl.BlockSpec((rows, oa.shape[-1]), row),
                pl.BlockSpec((rows, ob.shape[-1]), row),
                pl.BlockSpec((rows, oc.shape[-1]), row),
                pl.BlockSpec((1, 1, mod.shape[-1]), lambda i: (mod_row(i * rows), 0, 0)),
                resident(w_out), resident(w_ff1), resident(w_ff2)]
    in_specs += [pl.BlockSpec(a.shape, const) for a in ln]
    y = pl.pallas_call(
        functools.partial(_out_mlp_kernel, li=li, d=d, alpha=alpha, ff_chunk=1024),
        grid=(bsz * t // rows,),
        in_specs=in_specs,
        out_specs=pl.BlockSpec((rows, d), row),
        out_shape=jax.ShapeDtypeStruct((bsz * t, d), F32),
        compiler_params=_params(1),
        name="out_mlp",
    )(x.reshape(bsz * t, d), oa, ob.reshape(bsz * t, -1), oc, mod, w_out, w_ff1, w_ff2, *ln)
    return y.reshape(bsz, t, d)


def _rope_tables(n_tokens):
    pairs = HEAD_DIM // 4
    tok = jnp.arange(n_tokens)
    row = (tok // GRID_W).astype(F32)
    col = (tok % GRID_W).astype(F32)
    inv = ROPE_THETA ** (-jnp.arange(pairs, dtype=F32) / pairs)
    ang = jnp.concatenate([row[:, None] * inv, col[:, None] * inv], axis=-1)
    lane = jnp.arange(LANES)
    pair = (lane % HEAD_DIM) // 2
    sign = jnp.where(lane % 2 == 0, -1.0, 1.0).astype(F32)
    return jnp.cos(ang)[:, pair], jnp.sin(ang)[:, pair] * sign


def kernel(x_prompt, x_sample, cache_a_k, cache_a_v, cache_c_k, cache_c_v, state_b_fwd, state_b_bwd, c, c_ctx, w_ada, b_ada, w_in, w_out, lam_q1, lam_k1, lam_q2, lam_k2, subln_g, lb_logits_fwd, lb_logits_bwd, gnorm_g, qnorm_g, knorm_g, ln1_g, ln1_b, ln2_g, ln2_b, w_ff1, w_ff2):
    depth = w_in.shape[0]
    bsz, seq, d = x_prompt.shape
    dec_bsz, dec_seq, _ = x_sample.shape
    past = cache_a_k.shape[2]
    alpha = (2 * depth) ** 0.25
    mix_a, mix_b, mix_c = d // 2, d // 4, d // 4

    cond = jnp.concatenate([c_ctx[None, :], c, jnp.zeros((8 - 1 - dec_bsz, d), F32)], axis=0)
    mod = _modulation(cond, w_ada, b_ada)
    rope = _rope_tables(dec_seq)

    cache = (cache_a_k.reshape(dec_bsz, depth, past, mix_a),
             cache_a_v.reshape(dec_bsz, depth, past, mix_a),
             cache_c_k.reshape(dec_bsz, depth, past, mix_c // 2),
             cache_c_v.reshape(dec_bsz, depth, past, mix_c // 2))
    lam = (lam_q1, lam_k1, lam_q2, lam_k2)

    mod = mod.reshape(depth * 8, 1, 6 * d)
    weights = tuple(w.astype(BF16) for w in (w_in, w_out, w_ff1, w_ff2))
    qn = jnp.tile(qnorm_g, (1, mix_c // HEAD_DIM))
    kn = jnp.tile(knorm_g, (1, mix_c // 2 // HEAD_DIM))
    gn = jnp.tile(gnorm_g, (1, mix_b // HEAD_DIM))
    ln = (ln1_g, ln1_b, ln2_g, ln2_b)

    def stream(x, li, latent, own_prev):
        w_in_b, w_out_b, w1_b, w2_b = weights
        n, t, _ = x.shape
        mod_row = (lambda r0: li * 8 + 1 + r0 // t) if latent else (lambda r0: li * 8)
        kv_layer = 0 if latent else li
        (qa, ka, va, hq, ff, fb, hv, hg, qc, kc, vc) = _in_proj(
            x, mod, mod_row, w_in_b, lb_logits_fwd, lb_logits_bwd, qn, kn,
            rope if latent else None, None if own_prev is None else own_prev[0:4], li=li)
        tq = 256
        oa = _diff_attn(qa, ka, va, cache[0:2] if latent else None, lam, subln_g,
                        li=li, kv_layer=kv_layer, bsz=n, tq=tq, heads_per_step=2 if latent else 4)
        oc = _gqa(qc, kc, vc, cache[2:4] if latent else None, li=li, kv_layer=kv_layer, bsz=n, tq=tq)
        hres = _hgrn(hq, ff, fb, hv, hg, gn, (state_b_fwd, state_b_bwd) if latent else None,
                     None if own_prev is None else own_prev[4:6],
                     li=li, depth=depth, want_state=not latent, nb=2)
        y = _out_mlp(x, oa, hres[0], oc, mod, mod_row, w_out_b, w1_b, w2_b, ln, li=li, alpha=alpha)
        own = None if latent else (ka, va, kc, vc, hres[1], hres[2])
        return y, own

    y_prompt, y_sample = x_prompt, x_sample
    own = None
    for li in range(depth):
        y_prompt, own = stream(y_prompt, li, False, own)
        y_sample, _ = stream(y_sample, li, True, None)

    heads_a = mix_a // (2 * HEAD_DIM)
    new_a_k = own[0].reshape(bsz, depth, seq, heads_a, 2, HEAD_DIM)
    new_a_v = own[1].reshape(bsz, depth, seq, heads_a, 2 * HEAD_DIM)
    new_c_k = own[2].reshape(bsz, depth, seq, mix_c // 2 // HEAD_DIM, HEAD_DIM)
    new_c_v = own[3].reshape(bsz, depth, seq, mix_c // 2 // HEAD_DIM, HEAD_DIM)
    return (y_prompt, y_sample, new_a_k, new_a_v, new_c_k, new_c_v, own[4], own[5])
```

```python
import functools
import math

import jax
import jax.numpy as jnp
from jax import lax
from jax.experimental import pallas as pl
from jax.experimental.pallas import tpu as pltpu

GRID_W = 64
HEAD_DIM = 64
ROPE_THETA = 10000.0
LN_EPS = 1e-6
RMS_EPS = 1e-6
F_MIN = 1e-6
CHUNK = 64
DIAG_BLOCK = 8
LANES = 128
ROW_TILE = 256
VMEM_LIMIT = 56 * 1024 * 1024

F32 = jnp.float32
BF16 = jnp.bfloat16
NT = (((1,), (1,)), ((), ()))
TN = (((0,), (0,)), ((), ()))


def _params(n_grid):
    return pltpu.CompilerParams(dimension_semantics=("arbitrary",) * n_grid,
                                vmem_limit_bytes=VMEM_LIMIT)


def _dot(a, b):
    return jnp.dot(a, b, preferred_element_type=F32)


def _split_dot(a, b_bf16, passes, dims=None):
    acc = None
    rem = a
    for _ in range(passes):
        piece = rem.astype(BF16)
        rem = rem - piece.astype(F32)
        part = (_dot(piece, b_bf16) if dims is None
                else lax.dot_general(piece, b_bf16, dims, preferred_element_type=F32))
        acc = part if acc is None else acc + part
    return acc


def _group_ones(n, group):
    r = lax.broadcasted_iota(jnp.int32, (n, n), 0) // group
    c = lax.broadcasted_iota(jnp.int32, (n, n), 1) // group
    return (r == c).astype(BF16)


def _group_mean_square(x, group):
    n = x.shape[-1]
    return _split_dot(x * x, _group_ones(n, group), 2) * (1.0 / group)


def _group_rms(x, g_row, group):
    return x * lax.rsqrt(_group_mean_square(x, group) + RMS_EPS) * g_row


def _pair_swap(x):
    lane = lax.broadcasted_iota(jnp.int32, x.shape, 1)
    return jnp.where(lane % 2 == 0, pltpu.roll(x, LANES - 1, 1), pltpu.roll(x, 1, 1))


def _rope(x, cos, sin):
    blocks = []
    for j in range(x.shape[-1] // LANES):
        blk = x[:, j * LANES:(j + 1) * LANES]
        blocks.append(blk * cos + _pair_swap(blk) * sin)
    return blocks[0] if len(blocks) == 1 else jnp.concatenate(blocks, axis=-1)


def _silu(x):
    return x * jax.nn.sigmoid(x)


def _layernorm(x, g, b):
    mu = jnp.mean(x, axis=-1, keepdims=True)
    xc = x - mu
    var = jnp.mean(xc * xc, axis=-1, keepdims=True)
    return xc * lax.rsqrt(var + LN_EPS) * g + b


def _mod_kernel(c_ref, w_ref, b_ref, o_ref):
    s = _silu(c_ref[...]).astype(BF16)
    o_ref[0] = _dot(s, w_ref[0].astype(BF16)) + b_ref[0]


def _modulation(cond, w_ada, b_ada):
    depth, d, n = w_ada.shape
    tn = 1536
    rows = cond.shape[0]
    return pl.pallas_call(
        _mod_kernel,
        grid=(depth, n // tn),
        in_specs=[pl.BlockSpec((rows, d), lambda l, j: (0, 0)),
                  pl.BlockSpec((1, d, tn), lambda l, j: (l, 0, j)),
                  pl.BlockSpec((1, 1, tn), lambda l, j: (l, 0, j))],
        out_specs=pl.BlockSpec((1, rows, tn), lambda l, j: (l, 0, j)),
        out_shape=jax.ShapeDtypeStruct((depth, rows, n), F32),
        compiler_params=_params(2),
        name="adaln_modulation",
    )(cond, w_ada, b_ada.reshape(depth, 1, n))


def _in_proj_kernel(*refs, li, d, latent, n_alias):
    refs = list(refs)
    x_ref, mod_ref, w_ref, lbf_ref, lbb_ref, qn_ref, kn_ref = refs[:7]
    pos = 7
    if latent:
        cos, sin = refs[pos][...], refs[pos + 1][...]
        pos += 2
    pos += n_alias
    qa_o, ka_o, va_o, hq_o, ff_o, fb_o, hv_o, hg_o, qc_o, kc_o, vc_o = refs[pos:pos + 11]
    va_rows_o = None if latent else refs[pos + 11]

    def store_kv(ref, val):
        for slot in range(ref.shape[1]):
            ref[0, slot] = val.astype(ref.dtype)

    mix_a, mix_b, mix_c = d // 2, d // 4, d // 4
    kv_c = mix_c // 2
    scale = HEAD_DIM ** -0.5 * math.log2(math.e)

    shift = mod_ref[0, :, 0:d]
    gain = mod_ref[0, :, d:2 * d]
    h = (x_ref[...] * (1.0 + gain) + shift).astype(BF16)

    def proj(start, width):
        return _dot(h, w_ref[:, start:start + width])

    off_b = 3 * mix_a
    off_c = off_b + 5 * mix_b

    zq = proj(off_c, mix_c)
    zk = proj(off_c + mix_c, kv_c)
    vc = proj(off_c + mix_c + kv_c, kv_c)
    qa = proj(0, mix_a)
    msq = _group_mean_square(zq, HEAD_DIM)
    msk = _group_mean_square(zk, HEAD_DIM)
    ka = proj(mix_a, mix_a)
    va = proj(2 * mix_a, mix_a)
    qc = zq * lax.rsqrt(msq + RMS_EPS) * qn_ref[li:li + 1, :]
    kc = zk * lax.rsqrt(msk + RMS_EPS) * kn_ref[li:li + 1, :]

    if latent:
        qa = _rope(qa, cos, sin)
        ka = _rope(ka, cos, sin)
    qa_o[...] = (qa * scale).astype(qa_o.dtype)
    store_kv(ka_o, ka)
    store_kv(va_o, va)
    if va_rows_o is not None:
        heads = mix_a // LANES
        for slot in range(va_rows_o.shape[1]):
            for hd in range(heads):
                va_rows_o[0, slot, pl.ds(hd, ROW_TILE, stride=heads), :] = (
                    va[:, hd * LANES:(hd + 1) * LANES])

    def lower_bound(ref):
        logits = ref[...]
        e = jnp.exp(logits - jnp.max(logits, axis=0, keepdims=True))
        sm = e / jnp.sum(e, axis=0, keepdims=True)
        return jnp.sum(sm[0:li + 1], axis=0, keepdims=True) - sm[0:1]

    def forget(x, lb):
        return jnp.maximum(lb + (1.0 - lb) * jax.nn.sigmoid(x), F_MIN)

    off = off_b
    zb = [proj(off + j * mix_b, mix_b) for j in range(5)]

    if latent:
        kc = _rope(kc, cos, sin)
        qc = _rope(qc, cos, sin)
    store_kv(kc_o, kc)
    store_kv(vc_o, vc)
    qc = qc * scale
    lane = lax.broadcasted_iota(jnp.int32, (1, LANES), 1)
    for n in range(2):
        blk = qc[:, n * LANES:(n + 1) * LANES]
        in_half = (lane // HEAD_DIM) == n
        for g in range(2):
            src = blk if g == n else pltpu.roll(blk, HEAD_DIM, 1)
            hc = 2 * n + g
            qc_o[:, hc * LANES:(hc + 1) * LANES] = jnp.where(in_half, src, 0.0).astype(qc_o.dtype)

    hq_o[0] = _silu(zb[0])
    ff_o[0] = forget(zb[1], lower_bound(lbf_ref))
    fb_o[0] = forget(zb[2], lower_bound(lbb_ref))
    hv_o[0] = zb[3]
    hg_o[0] = _silu(zb[4])


def _in_proj(x, mod, mod_row, w_in, lb_f, lb_b, qn, kn, rope, kv_prev, *, li):
    bsz, t, d = x.shape
    latent = rope is not None
    depth, _, n_in = w_in.shape
    tiles = t // ROW_TILE
    mix_a, mix_b, mix_c = d // 2, d // 4, d // 4
    kv_c = mix_c // 2
    x2 = x.reshape(bsz * t, d)

    row = lambda i: (i, 0)
    brow = lambda i: (i // tiles, i % tiles, 0)
    const = lambda i: (0, 0)
    in_specs = [pl.BlockSpec((ROW_TILE, d), row),
                pl.BlockSpec((1, 1, mod.shape[-1]), lambda i: (mod_row(i * ROW_TILE), 0, 0)),
                pl.BlockSpec((None, d, n_in), lambda i: (li, 0, 0)),
                pl.BlockSpec(lb_f.shape, const), pl.BlockSpec(lb_b.shape, const),
                pl.BlockSpec(qn.shape, const), pl.BlockSpec(kn.shape, const)]
    args = [x2, mod, w_in, lb_f, lb_b, qn, kn]
    if latent:
        in_specs += [pl.BlockSpec((ROW_TILE, LANES), lambda i: (i % tiles, 0))] * 2
        args += list(rope)
        kv_shape, kv_slots, kv_slot0, kdt = (bsz, 1, t), 1, 0, BF16
    else:
        first = kv_prev is None
        kv_shape, kv_slots, kv_slot0, kdt = (bsz, depth, t), (depth if first else 1), (0 if first else li), F32
    aliases = {}
    if kv_prev is not None:
        kv_out_index = (1, 2, 9, 10, 11)
        for buf, out_index in zip(kv_prev, kv_out_index):
            aliases[len(args)] = out_index
            in_specs.append(pl.BlockSpec(memory_space=pl.ANY))
            args.append(buf)
    krow = lambda i: (i // tiles, kv_slot0, i % tiles, 0)

    def hspec():
        return pl.BlockSpec((1, ROW_TILE, mix_b), brow)

    def kvspec(width):
        return pl.BlockSpec((1, kv_slots, ROW_TILE, width), krow)

    out_specs = [pl.BlockSpec((ROW_TILE, mix_a), row), kvspec(mix_a), kvspec(mix_a),
                 hspec(), hspec(), hspec(), hspec(), hspec(),
                 pl.BlockSpec((ROW_TILE, 2 * mix_c), row), kvspec(kv_c), kvspec(kv_c)]
    out_shape = [jax.ShapeDtypeStruct((bsz * t, mix_a), BF16),
                 jax.ShapeDtypeStruct(kv_shape + (mix_a,), kdt),
                 jax.ShapeDtypeStruct(kv_shape + (mix_a,), kdt)]
    out_shape += [jax.ShapeDtypeStruct((bsz, t, mix_b), F32)] * 5
    out_shape += [jax.ShapeDtypeStruct((bsz * t, 2 * mix_c), BF16),
                  jax.ShapeDtypeStruct(kv_shape + (kv_c,), kdt),
                  jax.ShapeDtypeStruct(kv_shape + (kv_c,), kdt)]
    if not latent:
        heads = mix_a // LANES
        out_specs.append(pl.BlockSpec((1, kv_slots, ROW_TILE * heads, LANES), krow))
        out_shape.append(jax.ShapeDtypeStruct((bsz, depth, t * heads, LANES), F32))
    return pl.pallas_call(
        functools.partial(_in_proj_kernel, li=li, d=d, latent=latent, n_alias=len(aliases)),
        grid=(bsz * tiles,),
        in_specs=in_specs, out_specs=out_specs, out_shape=out_shape,
        input_output_aliases=aliases,
        compiler_params=_params(1),
        name="in_proj_latent" if latent else "in_proj_context",
    )(*args)


def _softmax_parts(scores):
    m = functools.reduce(jnp.maximum, [jnp.max(s, axis=-1, keepdims=True) for s in scores])
    es = [jnp.exp2(s - m) for s in scores]
    denom = functools.reduce(lambda a, b: a + b, [jnp.sum(e, axis=-1, keepdims=True) for e in es])
    return es, 1.0 / denom


def _diff_attn_kernel(*refs, li, cached, heads):
    if cached:
        (q_ref, k_ref, v_ref, ck_ref, cv_ref, lq1, lk1, lq2, lk2, sub_ref, o_ref) = refs
    else:
        (q_ref, k_ref, v_ref, lq1, lk1, lq2, lk2, sub_ref, o_ref) = refs
    lam_init = 0.8 - 0.6 * math.exp(-0.3 * li)

    def lam_term(a, b):
        return jnp.exp(jnp.sum(a[li:li + 1, :] * b[li:li + 1, :], axis=-1, keepdims=True))

    lam = lam_term(lq1, lk1) - lam_term(lq2, lk2) + lam_init
    tq = q_ref.shape[0]
    lane = lax.broadcasted_iota(jnp.int32, (1, LANES), 1)
    cols = [slice(h * LANES, (h + 1) * LANES) for h in range(heads)]

    scores, vals = [], []
    for c in cols:
        q = q_ref[:, c]
        zero = jnp.zeros_like(q)
        q2 = jnp.concatenate([jnp.where(lane < HEAD_DIM, q, zero),
                              jnp.where(lane >= HEAD_DIM, q, zero)], axis=0)
        keys = [k_ref[0, 0, :, c].astype(BF16)]
        vals.append([v_ref[0, 0, :, c].astype(BF16)])
        if cached:
            keys.append(ck_ref[0, 0, :, c].astype(BF16))
            vals[-1].append(cv_ref[0, 0, :, c].astype(BF16))
        scores.append([lax.dot_general(q2, k, NT, preferred_element_type=F32) for k in keys])
    parts = [_softmax_parts(s) for s in scores]
    outs = []
    for (es, r), vs in zip(parts, vals):
        r0 = r[0:tq]
        r1 = r[tq:2 * tq] * lam
        o = None
        for e, v in zip(es, vs):
            part = _dot((e[0:tq] * r0 - e[tq:2 * tq] * r1).astype(BF16), v)
            o = part if o is None else o + part
        outs.append(o)
    gain = sub_ref[li:li + 1, :] * (1.0 - lam_init)
    for c, o in zip(cols, outs):
        ms = jnp.mean(o * o, axis=-1, keepdims=True)
        o_ref[:, c] = (o * lax.rsqrt(ms + RMS_EPS) * gain).astype(o_ref.dtype)


def _diff_attn(q, k, v, cache, lam, subln, *, li, kv_layer, bsz, tq, heads_per_step):
    t = k.shape[2]
    width = q.shape[-1]
    wstep = heads_per_step * LANES
    nq = t // tq
    cached = cache is not None
    in_specs = [pl.BlockSpec((tq, wstep), lambda b, h, i: (b * nq + i, h)),
                pl.BlockSpec((1, 1, t, wstep), lambda b, h, i: (b, kv_layer, 0, h)),
                pl.BlockSpec((1, 1, t, wstep), lambda b, h, i: (b, kv_layer, 0, h))]
    args = [q, k, v]
    if cached:
        p = cache[0].shape[2]
        in_specs += [pl.BlockSpec((1, 1, p, wstep), lambda b, h, i: (b, li, 0, h))] * 2
        args += list(cache)
    in_specs += [pl.BlockSpec(a.shape, lambda b, h, i: (0, 0)) for a in (*lam, subln)]
    args += [*lam, subln]
    return pl.pallas_call(
        functools.partial(_diff_attn_kernel, li=li, cached=cached, heads=heads_per_step),
        grid=(bsz, width // wstep, nq),
        in_specs=in_specs,
        out_specs=pl.BlockSpec((tq, wstep), lambda b, h, i: (b * nq + i, h)),
        out_shape=jax.ShapeDtypeStruct((bsz * t, width), BF16),
        compiler_params=_params(3),
        name="diff_attn_latent" if cached else "diff_attn_context",
    )(*args)


def _gqa_kernel(*refs, cached):
    if cached:
        q_ref, k_ref, v_ref, ck_ref, cv_ref, o_ref = refs
    else:
        q_ref, k_ref, v_ref, o_ref = refs
    tq = q_ref.shape[0]
    heads = q_ref.shape[1] // LANES
    lane = lax.broadcasted_iota(jnp.int32, (1, LANES), 1)
    keys = [k_ref[0, 0].astype(BF16)]
    vals = [v_ref[0, 0].astype(BF16)]
    if cached:
        keys.append(ck_ref[0, 0].astype(BF16))
        vals.append(cv_ref[0, 0].astype(BF16))
    groups = range(heads // 2)
    scores = []
    for n in groups:
        q = jnp.concatenate([q_ref[:, (2 * n + g) * LANES:(2 * n + g + 1) * LANES]
                             for g in range(2)], axis=0)
        scores.append([lax.dot_general(q, k, NT, preferred_element_type=F32) for k in keys])
    parts = [_softmax_parts(s) for s in scores]
    outs = []
    for es, r in parts:
        o = None
        for e, v in zip(es, vals):
            part = _dot(e.astype(BF16), v)
            o = part if o is None else o + part
        outs.append(o * r)
    for n in groups:
        first = outs[n][0:tq]
        second = outs[n][tq:2 * tq]
        if n == 0:
            second = pltpu.roll(second, HEAD_DIM, 1)
        else:
            first = pltpu.roll(first, HEAD_DIM, 1)
        o_ref[:, n * LANES:(n + 1) * LANES] = jnp.where(lane < HEAD_DIM, first,
                                                        second).astype(o_ref.dtype)


def _gqa(q, k, v, cache, *, li, kv_layer, bsz, tq):
    t = k.shape[2]
    kvw = k.shape[-1]
    nq = t // tq
    cached = cache is not None
    in_specs = [pl.BlockSpec((tq, q.shape[-1]), lambda b, i: (b * nq + i, 0)),
                pl.BlockSpec((1, 1, t, kvw), lambda b, i: (b, kv_layer, 0, 0)),
                pl.BlockSpec((1, 1, t, kvw), lambda b, i: (b, kv_layer, 0, 0))]
    args = [q, k, v]
    if cached:
        p = cache[0].shape[2]
        in_specs += [pl.BlockSpec((1, 1, p, kvw), lambda b, i: (b, li, 0, 0))] * 2
        args += list(cache)
    return pl.pallas_call(
        functools.partial(_gqa_kernel, cached=cached),
        grid=(bsz, nq),
        in_specs=in_specs,
        out_specs=pl.BlockSpec((tq, 2 * kvw), lambda b, i: (b * nq + i, 0)),
        out_shape=jax.ShapeDtypeStruct((bsz * t, 2 * kvw), BF16),
        compiler_params=_params(2),
        name="gqa_latent" if cached else "gqa_context",
    )(*args)


def _head_masks(width):
    lane_head = lax.broadcasted_iota(jnp.int32, (1, width), 1) // HEAD_DIM
    return [lane_head == h for h in range(width // HEAD_DIM)]


def _stack_heads(x, masks):
    return jnp.concatenate([jnp.where(m, x, jnp.zeros_like(x)) for m in masks], axis=0)


def _block_diag_mask(width):
    r = lax.broadcasted_iota(jnp.int32, (width, width), 0) // HEAD_DIM
    c = lax.broadcasted_iota(jnp.int32, (width, width), 1) // HEAD_DIM
    return r == c


def _ref_rows(b, offsets, span):
    width = b.shape[-1]
    return jnp.concatenate([jnp.broadcast_to(b[o:o + 1], (span, width)) for o in offsets], axis=0)


def _hgrn_chunks(problems):
    n = len(problems)
    c, width = problems[0][0].shape
    qs = [p[0] for p in problems]
    vs = [p[2] for p in problems]
    sts = [p[3] for p in problems]
    rev = [p[4] for p in problems]
    chains = range(n)
    masks = _head_masks(width)
    trow =lax.broadcasted_iota(jnp.int32, (c, 1), 0)
    t_full = lax.broadcasted_iota(jnp.int32, (c, width), 0)
    s_full = lax.broadcasted_iota(jnp.int32, (c, width), 1) % c

    ks = [1.0 - p[1] for p in problems]
    b = [jnp.log(p[1]) for p in problems]
    step = 1
    while step < c:
        for j in chains:
            if rev[j]:
                b[j] = b[j] + jnp.where(trow < c - step, pltpu.roll(b[j], c - step, 0), 0.0)
            else:
                b[j] = b[j] + jnp.where(trow >= step, pltpu.roll(b[j], step, 0), 0.0)
        step *= 2
    b_end = [b[j][0:1] if rev[j] else b[j][c - 1:c] for j in chains]

    o = [lax.dot_general((qs[j] * jnp.exp(b[j])).astype(BF16), sts[j].astype(BF16), NT,
                         preferred_element_type=F32) for j in chains]

    a = [None] * n
    m = c // 2
    while m >= DIAG_BLOCK:
        blocks = c // (2 * m)
        same = (t_full // (2 * m)) == (s_full // (2 * m))
        for j in chains:
            ref = _ref_rows(b[j], [i * 2 * m + (m if rev[j] else m - 1) for i in range(blocks)],
                            2 * m)
            is_q = ((trow % (2 * m)) < m) if rev[j] else ((trow % (2 * m)) >= m)
            e = jnp.exp(jnp.where(is_q, b[j] - ref, ref - b[j]))
            ql = jnp.where(is_q, qs[j] * e, 0.0).astype(BF16)
            kl = jnp.where(is_q, 0.0, ks[j] * e).astype(BF16)
            al = lax.dot_general(ql, _stack_heads(kl, masks), NT, preferred_element_type=F32)
            if blocks > 1:
                al = jnp.where(same, al, 0.0)
            a[j] = al if a[j] is None else a[j] + al
        m //= 2
    blocks = c // DIAG_BLOCK
    mid = DIAG_BLOCK // 2
    same = (t_full // DIAG_BLOCK) == (s_full // DIAG_BLOCK)
    for j in chains:
        ref = _ref_rows(b[j], [i * DIAG_BLOCK + (mid if rev[j] else mid - 1) for i in range(blocks)],
                        DIAG_BLOCK)
        d = b[j] - ref
        ql = (qs[j] * jnp.exp(d)).astype(BF16)
        kl = (ks[j] * jnp.exp(-d)).astype(BF16)
        al = lax.dot_general(ql, _stack_heads(kl, masks), NT, preferred_element_type=F32)
        causal = (s_full >= t_full) if rev[j] else (s_full <= t_full)
        a[j] = a[j] + jnp.where(same & causal, al, 0.0)

    v_b = [v.astype(BF16) for v in vs]
    o = [o[j] + _dot(a[j].astype(BF16), _stack_heads(v_b[j], masks)) for j in chains]

    bd = _block_diag_mask(width)
    upd = [lax.dot_general(v_b[j], (ks[j] * jnp.exp(b_end[j] - b[j])).astype(BF16), TN,
                           preferred_element_type=F32) for j in chains]
    st_new = [sts[j] * jnp.exp(b_end[j]) + jnp.where(bd, upd[j], 0.0) for j in chains]
    return list(zip(o, st_new))


def _mxu_transpose(x):
    n = x.shape[1]
    r = lax.broadcasted_iota(jnp.int32, (n, n), 0)
    c = lax.broadcasted_iota(jnp.int32, (n, n), 1)
    eye = (r == c).astype(BF16)
    acc = None
    rem = x
    for _ in range(3):
        piece = rem.astype(BF16)
        rem = rem - piece.astype(F32)
        part = lax.dot_general(eye, piece, NT, preferred_element_type=F32)
        acc = part if acc is None else acc + part
    return acc


def _hgrn_kernel(*refs, li, has_state, want_state, n_alias, heads):
    refs = list(refs)
    q_ref, ff_ref, fb_ref, v_ref, gate_ref, gn_ref = refs[:6]
    pos = 6
    if has_state:
        s0f_ref, s0b_ref = refs[pos:pos + 2]
        pos += 2
    pos += n_alias
    o_ref = refs[pos]
    pos += 1
    if want_state:
        sf_ref, sb_ref = refs[pos:pos + 2]
        pos += 2
    st_ref, of_ref, ob_ref = refs[pos:pos + 3]
    nb, t, width = q_ref.shape
    nc = t // CHUNK
    bd = _block_diag_mask(width)

    for n in range(nb):
        for d in range(2):
            if has_state:
                x = (s0b_ref if d else s0f_ref)[n, 0].reshape(width, HEAD_DIM)
                xt = _mxu_transpose(x)
                st_ref[2 * n + d] = jnp.where(bd, jnp.concatenate([xt] * heads, axis=0), 0.0)
            else:
                st_ref[2 * n + d] = jnp.zeros((width, width), F32)

    def body(ci, carry):
        rows = (pl.ds(pl.multiple_of(ci * CHUNK, CHUNK), CHUNK),
                pl.ds(pl.multiple_of((nc - 1 - ci) * CHUNK, CHUNK), CHUNK))
        loaded = []
        for n in range(nb):
            for d, f_ref in enumerate((ff_ref, fb_ref)):
                r = rows[d]
                loaded.append((q_ref[n, r, :], f_ref[n, r, :], v_ref[n, r, :], st_ref[2 * n + d],
                               bool(d)))
        for j, (o, st) in enumerate(_hgrn_chunks(loaded)):
            n, d = divmod(j, 2)
            (ob_ref if d else of_ref)[n, rows[d], :] = o
            st_ref[j] = st
        return carry

    lax.fori_loop(0, nc, body, 0)

    for n in range(nb):
        o = of_ref[n] + ob_ref[n]
        o_ref[n] = (_group_rms(o, gn_ref[li:li + 1, :], HEAD_DIM) * gate_ref[n]).astype(o_ref.dtype)

    if want_state:
        for n in range(nb):
            for d, dst in enumerate((sf_ref, sb_ref)):
                st = st_ref[2 * n + d]
                rows = st[0:HEAD_DIM]
                for h in range(1, heads):
                    rows = rows + st[h * HEAD_DIM:(h + 1) * HEAD_DIM]
                final = _mxu_transpose(rows).reshape(heads, HEAD_DIM, HEAD_DIM)
                for slot in range(dst.shape[1]):
                    dst[n, slot] = final


def _hgrn(hq, ff, fb, hv, hg, gn, state, state_prev, *, li, depth, want_state, nb):
    bsz, t, width = hq.shape
    heads = width // HEAD_DIM
    has_state = state is not None
    seq = pl.BlockSpec((nb, t, width), lambda b: (b, 0, 0))
    in_specs = [seq] * 5 + [pl.BlockSpec(gn.shape, lambda b: (0, 0))]
    args = [hq, ff, fb, hv, hg, gn]
    if has_state:
        in_specs += [pl.BlockSpec((nb, 1, heads, HEAD_DIM, HEAD_DIM), lambda b: (b, li, 0, 0, 0))] * 2
        args += list(state)
    aliases = {}
    if state_prev is not None:
        for j, buf in enumerate(state_prev):
            aliases[len(args)] = 1 + j
            in_specs.append(pl.BlockSpec(memory_space=pl.ANY))
            args.append(buf)
    out_specs = [seq]
    out_shape = [jax.ShapeDtypeStruct((bsz, t, width), BF16)]
    if want_state:
        slots, slot0 = (depth, 0) if state_prev is None else (1, li)
        out_specs += [pl.BlockSpec((nb, slots, heads, HEAD_DIM, HEAD_DIM),
                                   lambda b: (b, slot0, 0, 0, 0))] * 2
        out_shape += [jax.ShapeDtypeStruct((bsz, depth, heads, HEAD_DIM, HEAD_DIM), F32)] * 2
    return pl.pallas_call(
        functools.partial(_hgrn_kernel, li=li, has_state=has_state, want_state=want_state,
                          n_alias=len(aliases), heads=heads),
        grid=(bsz // nb,),
        in_specs=in_specs, out_specs=out_specs, out_shape=out_shape,
        input_output_aliases=aliases,
        scratch_shapes=[pltpu.VMEM((2 * nb, width, width), F32),
                        pltpu.VMEM((nb, t, width), F32), pltpu.VMEM((nb, t, width), F32)],
        compiler_params=_params(1),
        name="hgrn2_latent" if has_state else "hgrn2_context",
    )(*args)


def _out_mlp_kernel(x_ref, oa_ref, ob_ref, oc_ref, mod_ref, wo_ref, w1_ref, w2_ref,
                    g1_ref, b1_ref, g2_ref, b2_ref, y_ref, *, li, d, alpha, ff_chunk):
    wa, wb = oa_ref.shape[-1], ob_ref.shape[-1]
    layer = slice(li, li + 1)
    gate1 = mod_ref[0, :, 2 * d:3 * d]
    shift2 = mod_ref[0, :, 3 * d:4 * d]
    gain2 = mod_ref[0, :, 4 * d:5 * d]
    gate2 = mod_ref[0, :, 5 * d:6 * d]
    subs = [slice(s * ROW_TILE, (s + 1) * ROW_TILE) for s in range(x_ref.shape[0] // ROW_TILE)]
    m = [_dot(oa_ref[r, :], wo_ref[0:wa, :]) + _dot(ob_ref[r, :], wo_ref[wa:wa + wb, :])
         + _dot(oc_ref[r, :], wo_ref[wa + wb:, :]) for r in subs]
    x1 = [_layernorm(alpha * x_ref[r, :] + gate1 * mi, g1_ref[layer, :], b1_ref[layer, :])
          for r, mi in zip(subs, m)]
    h2 = [(xi * (1.0 + gain2) + shift2).astype(BF16) for xi in x1]
    acc = [None] * len(subs)
    for j in range(w1_ref.shape[-1] // ff_chunk):
        cols = slice(j * ff_chunk, (j + 1) * ff_chunk)
        hid = [jnp.maximum(_dot(hi, w1_ref[:, cols]), 0.0) for hi in h2]
        for s, hd in enumerate(hid):
            part = _dot((hd * hd).astype(BF16), w2_ref[cols, :])
            acc[s] = part if acc[s] is None else acc[s] + part
    for r, xi, ai in zip(subs, x1, acc):
        y_ref[r, :] = _layernorm(alpha * xi + gate2 * ai, g2_ref[layer, :], b2_ref[layer, :])


def _out_mlp(x, oa, ob, oc, mod, mod_row, w_out, w_ff1, w_ff2, ln, *, li, alpha):
    bsz, t, d = x.shape
    rows = 2 * ROW_TILE
    row = lambda i: (i, 0)
    const = lambda i: (0, 0)
    resident = lambda a: pl.BlockSpec((None,) + a.shape[1:], lambda i: (li, 0, 0),
                                      pipeline_mode=pl.Buffered(1))
    in_specs = [pl.BlockSpec((rows, d), row),
                pl.BlockSpec((rows, oa.shape[-1]), row),
                pl.BlockSpec((rows, ob.shape[-1]), row),
                pl.BlockSpec((rows, oc.shape[-1]), row),
                pl.BlockSpec((1, 1, mod.shape[-1]), lambda i: (mod_row(i * rows), 0, 0)),
                resident(w_out), resident(w_ff1), resident(w_ff2)]
    in_specs += [pl.BlockSpec(a.shape, const) for a in ln]
    y = pl.pallas_call(
        functools.partial(_out_mlp_kernel, li=li, d=d, alpha=alpha, ff_chunk=1024),
        grid=(bsz * t // rows,),
        in_specs=in_specs,
        out_specs=pl.BlockSpec((rows, d), row),
        out_shape=jax.ShapeDtypeStruct((bsz * t, d), F32),
        compiler_params=_params(1),
        name="out_mlp",
    )(x.reshape(bsz * t, d), oa, ob.reshape(bsz * t, -1), oc, mod, w_out, w_ff1, w_ff2, *ln)
    return y.reshape(bsz, t, d)


def _rope_tables(n_tokens):
    pairs = HEAD_DIM // 4
    tok = jnp.arange(n_tokens)
    row = (tok // GRID_W).astype(F32)
    col = (tok % GRID_W).astype(F32)
    inv = ROPE_THETA ** (-jnp.arange(pairs, dtype=F32) / pairs)
    ang = jnp.concatenate([row[:, None] * inv, col[:, None] * inv], axis=-1)
    lane = jnp.arange(LANES)
    pair = (lane % HEAD_DIM) // 2
    sign = jnp.where(lane % 2 == 0, -1.0, 1.0).astype(F32)
    return jnp.cos(ang)[:, pair], jnp.sin(ang)[:, pair] * sign


def kernel(x_prompt, x_sample, cache_a_k, cache_a_v, cache_c_k, cache_c_v, state_b_fwd, state_b_bwd, c, c_ctx, w_ada, b_ada, w_in, w_out, lam_q1, lam_k1, lam_q2, lam_k2, subln_g, lb_logits_fwd, lb_logits_bwd, gnorm_g, qnorm_g, knorm_g, ln1_g, ln1_b, ln2_g, ln2_b, w_ff1, w_ff2):
    depth = w_in.shape[0]
    bsz, seq, d = x_prompt.shape
    dec_bsz, dec_seq, _ = x_sample.shape
    past = cache_a_k.shape[2]
    alpha = (2 * depth) ** 0.25
    mix_a, mix_b, mix_c = d // 2, d // 4, d // 4

    cond = jnp.concatenate([c_ctx[None, :], c, jnp.zeros((8 - 1 - dec_bsz, d), F32)], axis=0)
    mod = _modulation(cond, w_ada, b_ada)
    rope = _rope_tables(dec_seq)

    cache = (cache_a_k.reshape(dec_bsz, depth, past, mix_a),
             cache_a_v.reshape(dec_bsz, depth, past, mix_a),
             cache_c_k.reshape(dec_bsz, depth, past, mix_c // 2),
             cache_c_v.reshape(dec_bsz, depth, past, mix_c // 2))
    lam = (lam_q1, lam_k1, lam_q2, lam_k2)

    mod = mod.reshape(depth * 8, 1, 6 * d)
    weights = tuple(w.astype(BF16) for w in (w_in, w_out, w_ff1, w_ff2))
    qn = jnp.tile(qnorm_g, (1, mix_c // HEAD_DIM))
    kn = jnp.tile(knorm_g, (1, mix_c // 2 // HEAD_DIM))
    gn = jnp.tile(gnorm_g, (1, mix_b // HEAD_DIM))
    ln = (ln1_g, ln1_b, ln2_g, ln2_b)

    def stream(x, li, latent, own_prev):
        w_in_b, w_out_b, w1_b, w2_b = weights
        n, t, _ = x.shape
        mod_row = (lambda r0: li * 8 + 1 + r0 // t) if latent else (lambda r0: li * 8)
        kv_layer = 0 if latent else li
        (qa, ka, va, hq, ff, fb, hv, hg, qc, kc, vc, *va_rows) = _in_proj(
            x, mod, mod_row, w_in_b, lb_logits_fwd, lb_logits_bwd, qn, kn,
            rope if latent else None, None if own_prev is None else own_prev[0:5], li=li)
        tq = 256
        oa = _diff_attn(qa, ka, va, cache[0:2] if latent else None, lam, subln_g,
                        li=li, kv_layer=kv_layer, bsz=n, tq=tq, heads_per_step=2 if latent else 4)
        oc = _gqa(qc, kc, vc, cache[2:4] if latent else None, li=li, kv_layer=kv_layer, bsz=n, tq=tq)
        hres = _hgrn(hq, ff, fb, hv, hg, gn, (state_b_fwd, state_b_bwd) if latent else None,
                     None if own_prev is None else own_prev[5:7],
                     li=li, depth=depth, want_state=not latent, nb=2)
        y = _out_mlp(x, oa, hres[0], oc, mod, mod_row, w_out_b, w1_b, w2_b, ln, li=li, alpha=alpha)
        own = None if latent else (ka, va, kc, vc, va_rows[0], hres[1], hres[2])
        return y, own

    y_prompt, y_sample = x_prompt, x_sample
    own = None
    for li in range(depth):
        y_prompt, own = stream(y_prompt, li, False, own)
        y_sample, _ = stream(y_sample, li, True, None)

    heads_a = mix_a // (2 * HEAD_DIM)
    new_a_k = own[0].reshape(bsz, depth, seq, heads_a, 2, HEAD_DIM)
    new_a_v = own[4].reshape(bsz, depth, seq, heads_a, 2 * HEAD_DIM)
    new_c_k = own[2].reshape(bsz, depth, seq, mix_c // 2 // HEAD_DIM, HEAD_DIM)
    new_c_v = own[3].reshape(bsz, depth, seq, mix_c // 2 // HEAD_DIM, HEAD_DIM)
    return (y_prompt, y_sample, new_a_k, new_a_v, new_c_k, new_c_v, own[5], own[6])
```

```python
import functools
import math

import jax
import jax.numpy as jnp
import numpy as np
from jax import lax
from jax.experimental import pallas as pl
from jax.experimental.pallas import tpu as pltpu

GRID_W = 64
HEAD_DIM = 64
ROPE_THETA = 10000.0
LN_EPS = 1e-6
RMS_EPS = 1e-6
F_MIN = 1e-6
CHUNK = 64
DIAG_BLOCK = 8
LANES = 128
ROW_TILE = 256
VMEM_LIMIT = 56 * 1024 * 1024

F32 = jnp.float32
BF16 = jnp.bfloat16
NT = (((1,), (1,)), ((), ()))
TN = (((0,), (0,)), ((), ()))


def _params(n_grid):
    return pltpu.CompilerParams(dimension_semantics=("arbitrary",) * n_grid,
                                vmem_limit_bytes=VMEM_LIMIT)


def _dot(a, b):
    return jnp.dot(a, b, preferred_element_type=F32)


def _split_dot(a, b_bf16, passes, dims=None):
    acc = None
    rem = a
    for _ in range(passes):
        piece = rem.astype(BF16)
        rem = rem - piece.astype(F32)
        part = (_dot(piece, b_bf16) if dims is None
                else lax.dot_general(piece, b_bf16, dims, preferred_element_type=F32))
        acc = part if acc is None else acc + part
    return acc


def _group_ones(n, group):
    r = lax.broadcasted_iota(jnp.int32, (n, n), 0) // group
    c = lax.broadcasted_iota(jnp.int32, (n, n), 1) // group
    return (r == c).astype(BF16)


def _group_mean_square(x, group):
    n = x.shape[-1]
    return _split_dot(x * x, _group_ones(n, group), 2) * (1.0 / group)


def _group_rms(x, g_row, group):
    return x * lax.rsqrt(_group_mean_square(x, group) + RMS_EPS) * g_row


def _pair_swap(x):
    lane = lax.broadcasted_iota(jnp.int32, x.shape, 1)
    return jnp.where(lane % 2 == 0, pltpu.roll(x, LANES - 1, 1), pltpu.roll(x, 1, 1))


def _rope(x, cos, sin):
    blocks = []
    for j in range(x.shape[-1] // LANES):
        blk = x[:, j * LANES:(j + 1) * LANES]
        blocks.append(blk * cos + _pair_swap(blk) * sin)
    return blocks[0] if len(blocks) == 1 else jnp.concatenate(blocks, axis=-1)


def _silu(x):
    return x * jax.nn.sigmoid(x)


def _layernorm(x, g, b):
    mu = jnp.mean(x, axis=-1, keepdims=True)
    xc = x - mu
    var = jnp.mean(xc * xc, axis=-1, keepdims=True)
    return xc * lax.rsqrt(var + LN_EPS) * g + b


def _mod_kernel(c_ref, w_ref, b_ref, o_ref):
    s = _silu(c_ref[...]).astype(BF16)
    o_ref[0] = _dot(s, w_ref[0].astype(BF16)) + b_ref[0]


def _modulation(cond, w_ada, b_ada):
    depth, d, n = w_ada.shape
    tn = 1536
    rows = cond.shape[0]
    return pl.pallas_call(
        _mod_kernel,
        grid=(depth, n // tn),
        in_specs=[pl.BlockSpec((rows, d), lambda l, j: (0, 0)),
                  pl.BlockSpec((1, d, tn), lambda l, j: (l, 0, j)),
                  pl.BlockSpec((1, 1, tn), lambda l, j: (l, 0, j))],
        out_specs=pl.BlockSpec((1, rows, tn), lambda l, j: (l, 0, j)),
        out_shape=jax.ShapeDtypeStruct((depth, rows, n), F32),
        compiler_params=_params(2),
        name="adaln_modulation",
    )(cond, w_ada, b_ada.reshape(depth, 1, n))


def _in_proj_kernel(*refs, li, d, latent, n_alias):
    refs = list(refs)
    x_ref, mod_ref, w_ref, lbf_ref, lbb_ref, qn_ref, kn_ref = refs[:7]
    pos = 7
    if latent:
        cos, sin = refs[pos][...], refs[pos + 1][...]
        pos += 2
    pos += n_alias
    qa_o, ka_o, va_o, hq_o, ff_o, fb_o, hv_o, hg_o, qc_o, kc_o, vc_o = refs[pos:pos + 11]
    va_rows_o, kct_o, vct_o = (None, None, None) if latent else refs[pos + 11:pos + 14]

    def store_kv(ref, val):
        for slot in range(ref.shape[1]):
            ref[0, slot] = val.astype(ref.dtype)

    mix_a, mix_b, mix_c = d // 2, d // 4, d // 4
    kv_c = mix_c // 2
    scale = HEAD_DIM ** -0.5 * math.log2(math.e)

    shift = mod_ref[0, :, 0:d]
    gain = mod_ref[0, :, d:2 * d]
    h = (x_ref[...] * (1.0 + gain) + shift).astype(BF16)

    def proj(start, width):
        return _dot(h, w_ref[:, start:start + width])

    off_b = 3 * mix_a
    off_c = off_b + 5 * mix_b

    zq = proj(off_c, mix_c)
    zk = proj(off_c + mix_c, kv_c)
    vc = proj(off_c + mix_c + kv_c, kv_c)
    qa = proj(0, mix_a)
    msq = _group_mean_square(zq, HEAD_DIM)
    msk = _group_mean_square(zk, HEAD_DIM)
    ka = proj(mix_a, mix_a)
    va = proj(2 * mix_a, mix_a)
    qc = zq * lax.rsqrt(msq + RMS_EPS) * qn_ref[li:li + 1, :]
    kc = zk * lax.rsqrt(msk + RMS_EPS) * kn_ref[li:li + 1, :]

    if latent:
        qa = _rope(qa, cos, sin)
        ka = _rope(ka, cos, sin)
    qa_o[...] = (qa * scale).astype(qa_o.dtype)
    store_kv(ka_o, ka)
    store_kv(va_o, va)
    if va_rows_o is not None:
        heads = mix_a // LANES
        for slot in range(va_rows_o.shape[1]):
            for hd in range(heads):
                va_rows_o[0, slot, pl.ds(hd, ROW_TILE, stride=heads), :] = (
                    va[:, hd * LANES:(hd + 1) * LANES])

    def lower_bound(ref):
        logits = ref[...]
        e = jnp.exp(logits - jnp.max(logits, axis=0, keepdims=True))
        sm = e / jnp.sum(e, axis=0, keepdims=True)
        return jnp.sum(sm[0:li + 1], axis=0, keepdims=True) - sm[0:1]

    def forget(x, lb):
        return jnp.maximum(lb + (1.0 - lb) * jax.nn.sigmoid(x), F_MIN)

    off = off_b
    zb = [proj(off + j * mix_b, mix_b) for j in range(5)]

    if latent:
        kc = _rope(kc, cos, sin)
        qc = _rope(qc, cos, sin)
    store_kv(kc_o, kc)
    store_kv(vc_o, vc)
    if kct_o is not None:
        store_kv(kct_o, kc.T)
        store_kv(vct_o, vc.T)
    qc = qc * scale
    lane = lax.broadcasted_iota(jnp.int32, (1, LANES), 1)
    for n in range(2):
        blk = qc[:, n * LANES:(n + 1) * LANES]
        in_half = (lane // HEAD_DIM) == n
        for g in range(2):
            src = blk if g == n else pltpu.roll(blk, HEAD_DIM, 1)
            hc = 2 * n + g
            qc_o[:, hc * LANES:(hc + 1) * LANES] = jnp.where(in_half, src, 0.0).astype(qc_o.dtype)

    hq_o[0] = _silu(zb[0])
    ff_o[0] = forget(zb[1], lower_bound(lbf_ref))
    fb_o[0] = forget(zb[2], lower_bound(lbb_ref))
    hv_o[0] = zb[3]
    hg_o[0] = _silu(zb[4])


def _in_proj(x, mod, mod_row, w_in, lb_f, lb_b, qn, kn, rope, kv_prev, *, li):
    bsz, t, d = x.shape
    latent = rope is not None
    depth, _, n_in = w_in.shape
    tiles = t // ROW_TILE
    mix_a, mix_b, mix_c = d // 2, d // 4, d // 4
    kv_c = mix_c // 2
    x2 = x.reshape(bsz * t, d)

    row = lambda i: (i, 0)
    brow = lambda i: (i // tiles, i % tiles, 0)
    const = lambda i: (0, 0)
    in_specs = [pl.BlockSpec((ROW_TILE, d), row),
                pl.BlockSpec((1, 1, mod.shape[-1]), lambda i: (mod_row(i * ROW_TILE), 0, 0)),
                pl.BlockSpec((None, d, n_in), lambda i: (li, 0, 0)),
                pl.BlockSpec(lb_f.shape, const), pl.BlockSpec(lb_b.shape, const),
                pl.BlockSpec(qn.shape, const), pl.BlockSpec(kn.shape, const)]
    args = [x2, mod, w_in, lb_f, lb_b, qn, kn]
    if latent:
        in_specs += [pl.BlockSpec((ROW_TILE, LANES), lambda i: (i % tiles, 0))] * 2
        args += list(rope)
        kv_shape, kv_slots, kv_slot0, kdt = (bsz, 1, t), 1, 0, BF16
    else:
        first = kv_prev is None
        kv_shape, kv_slots, kv_slot0, kdt = (bsz, depth, t), (depth if first else 1), (0 if first else li), F32
    aliases = {}
    if kv_prev is not None:
        kv_out_index = (1, 2, 9, 10, 11, 12, 13)
        for buf, out_index in zip(kv_prev, kv_out_index):
            aliases[len(args)] = out_index
            in_specs.append(pl.BlockSpec(memory_space=pl.ANY))
            args.append(buf)
    krow = lambda i: (i // tiles, kv_slot0, i % tiles, 0)

    def hspec():
        return pl.BlockSpec((1, ROW_TILE, mix_b), brow)

    def kvspec(width):
        return pl.BlockSpec((1, kv_slots, ROW_TILE, width), krow)

    out_specs = [pl.BlockSpec((ROW_TILE, mix_a), row), kvspec(mix_a), kvspec(mix_a),
                 hspec(), hspec(), hspec(), hspec(), hspec(),
                 pl.BlockSpec((ROW_TILE, 2 * mix_c), row), kvspec(kv_c), kvspec(kv_c)]
    out_shape = [jax.ShapeDtypeStruct((bsz * t, mix_a), BF16),
                 jax.ShapeDtypeStruct(kv_shape + (mix_a,), kdt),
                 jax.ShapeDtypeStruct(kv_shape + (mix_a,), kdt)]
    out_shape += [jax.ShapeDtypeStruct((bsz, t, mix_b), F32)] * 5
    out_shape += [jax.ShapeDtypeStruct((bsz * t, 2 * mix_c), BF16),
                  jax.ShapeDtypeStruct(kv_shape + (kv_c,), kdt),
                  jax.ShapeDtypeStruct(kv_shape + (kv_c,), kdt)]
    if not latent:
        heads = mix_a // LANES
        out_specs.append(pl.BlockSpec((1, kv_slots, ROW_TILE * heads, LANES), krow))
        out_shape.append(jax.ShapeDtypeStruct((bsz, depth, t * heads, LANES), F32))
        tcol = lambda i: (i // tiles, kv_slot0, 0, i % tiles)
        out_specs += [pl.BlockSpec((1, kv_slots, kv_c, ROW_TILE), tcol)] * 2
        out_shape += [jax.ShapeDtypeStruct((bsz, depth, kv_c, t), F32)] * 2
    return pl.pallas_call(
        functools.partial(_in_proj_kernel, li=li, d=d, latent=latent, n_alias=len(aliases)),
        grid=(bsz * tiles,),
        in_specs=in_specs, out_specs=out_specs, out_shape=out_shape,
        input_output_aliases=aliases,
        compiler_params=_params(1),
        name="in_proj_latent" if latent else "in_proj_context",
    )(*args)


def _softmax_parts(scores):
    m = functools.reduce(jnp.maximum, [jnp.max(s, axis=-1, keepdims=True) for s in scores])
    es = [jnp.exp2(s - m) for s in scores]
    denom = functools.reduce(lambda a, b: a + b, [jnp.sum(e, axis=-1, keepdims=True) for e in es])
    return es, 1.0 / denom


def _diff_attn_kernel(*refs, li, cached, heads):
    if cached:
        (q_ref, k_ref, v_ref, ck_ref, cv_ref, lq1, lk1, lq2, lk2, sub_ref, o_ref) = refs
    else:
        (q_ref, k_ref, v_ref, lq1, lk1, lq2, lk2, sub_ref, o_ref) = refs
    lam_init = 0.8 - 0.6 * math.exp(-0.3 * li)

    def lam_term(a, b):
        return jnp.exp(jnp.sum(a[li:li + 1, :] * b[li:li + 1, :], axis=-1, keepdims=True))

    lam = lam_term(lq1, lk1) - lam_term(lq2, lk2) + lam_init
    tq = q_ref.shape[0]
    lane = lax.broadcasted_iota(jnp.int32, (1, LANES), 1)
    cols = [slice(h * LANES, (h + 1) * LANES) for h in range(heads)]

    scores, vals = [], []
    for c in cols:
        q = q_ref[:, c]
        zero = jnp.zeros_like(q)
        q2 = jnp.concatenate([jnp.where(lane < HEAD_DIM, q, zero),
                              jnp.where(lane >= HEAD_DIM, q, zero)], axis=0)
        head_scores = [lax.dot_general(q2, k_ref[0, 0, :, c].astype(BF16), NT,
                                       preferred_element_type=F32)]
        vals.append([v_ref[0, 0, :, c].astype(BF16)])
        if cached:
            past = ck_ref.shape[-1]
            all_heads = cv_ref.shape[2] // past
            head = pl.program_id(1) * heads + c.start // LANES
            head_scores.append(_dot(q2, ck_ref[0, 0, c, :].astype(BF16)))
            vals[-1].append(cv_ref[0, 0, pl.ds(head, past, stride=all_heads), :].astype(BF16))
        scores.append(head_scores)
    parts = [_softmax_parts(s) for s in scores]
    outs = []
    for (es, r), vs in zip(parts, vals):
        r0 = r[0:tq]
        r1 = r[tq:2 * tq] * lam
        o = None
        for e, v in zip(es, vs):
            part = _dot((e[0:tq] * r0 - e[tq:2 * tq] * r1).astype(BF16), v)
            o = part if o is None else o + part
        outs.append(o)
    gain = sub_ref[li:li + 1, :] * (1.0 - lam_init)
    for c, o in zip(cols, outs):
        ms = jnp.mean(o * o, axis=-1, keepdims=True)
        o_ref[:, c] = (o * lax.rsqrt(ms + RMS_EPS) * gain).astype(o_ref.dtype)


def _diff_attn(q, k, v, cache, lam, subln, *, li, kv_layer, bsz, tq, heads_per_step):
    t = k.shape[2]
    width = q.shape[-1]
    wstep = heads_per_step * LANES
    nq = t // tq
    cached = cache is not None
    in_specs = [pl.BlockSpec((tq, wstep), lambda b, h, i: (b * nq + i, h)),
                pl.BlockSpec((1, 1, t, wstep), lambda b, h, i: (b, kv_layer, 0, h)),
                pl.BlockSpec((1, 1, t, wstep), lambda b, h, i: (b, kv_layer, 0, h))]
    args = [q, k, v]
    if cached:
        ck, cv = cache
        in_specs += [pl.BlockSpec((1, 1, wstep, ck.shape[-1]), lambda b, h, i: (b, li, h, 0)),
                     pl.BlockSpec((1, 1) + cv.shape[2:], lambda b, h, i: (b, li, 0, 0))]
        args += [ck, cv]
    in_specs += [pl.BlockSpec(a.shape, lambda b, h, i: (0, 0)) for a in (*lam, subln)]
    args += [*lam, subln]
    return pl.pallas_call(
        functools.partial(_diff_attn_kernel, li=li, cached=cached, heads=heads_per_step),
        grid=(bsz, width // wstep, nq),
        in_specs=in_specs,
        out_specs=pl.BlockSpec((tq, wstep), lambda b, h, i: (b * nq + i, h)),
        out_shape=jax.ShapeDtypeStruct((bsz * t, width), BF16),
        compiler_params=_params(3),
        name="diff_attn_latent" if cached else "diff_attn_context",
    )(*args)


def _gqa_kernel(*refs, cached):
    if cached:
        q_ref, k_ref, v_ref, ck_ref, cv_ref, o_ref = refs
    else:
        q_ref, k_ref, v_ref, o_ref = refs
    tq = q_ref.shape[0]
    heads = q_ref.shape[1] // LANES
    lane = lax.broadcasted_iota(jnp.int32, (1, LANES), 1)
    key = k_ref[0, 0].astype(BF16)
    val = v_ref[0, 0].astype(BF16)
    if cached:
        key_t = ck_ref[0, 0].astype(BF16)
        val_t = cv_ref[0, 0].astype(BF16)
    groups = range(heads // 2)
    scores = []
    for n in groups:
        q = jnp.concatenate([q_ref[:, (2 * n + g) * LANES:(2 * n + g + 1) * LANES]
                             for g in range(2)], axis=0)
        group_scores = [lax.dot_general(q, key, NT, preferred_element_type=F32)]
        if cached:
            group_scores.append(_dot(q, key_t))
        scores.append(group_scores)
    parts = [_softmax_parts(s) for s in scores]
    outs = []
    for es, r in parts:
        o = _dot(es[0].astype(BF16), val)
        if cached:
            o = o + lax.dot_general(es[1].astype(BF16), val_t, NT, preferred_element_type=F32)
        outs.append(o * r)
    for n in groups:
        first = outs[n][0:tq]
        second = outs[n][tq:2 * tq]
        if n == 0:
            second = pltpu.roll(second, HEAD_DIM, 1)
        else:
            first = pltpu.roll(first, HEAD_DIM, 1)
        o_ref[:, n * LANES:(n + 1) * LANES] = jnp.where(lane < HEAD_DIM, first,
                                                        second).astype(o_ref.dtype)


def _gqa(q, k, v, cache, *, li, kv_layer, bsz, tq):
    t = k.shape[2]
    kvw = k.shape[-1]
    nq = t // tq
    cached = cache is not None
    in_specs = [pl.BlockSpec((tq, q.shape[-1]), lambda b, i: (b * nq + i, 0)),
                pl.BlockSpec((1, 1, t, kvw), lambda b, i: (b, kv_layer, 0, 0)),
                pl.BlockSpec((1, 1, t, kvw), lambda b, i: (b, kv_layer, 0, 0))]
    args = [q, k, v]
    if cached:
        in_specs += [pl.BlockSpec((1, 1) + cache[0].shape[2:], lambda b, i: (b, li, 0, 0))] * 2
        args += list(cache)
    return pl.pallas_call(
        functools.partial(_gqa_kernel, cached=cached),
        grid=(bsz, nq),
        in_specs=in_specs,
        out_specs=pl.BlockSpec((tq, 2 * kvw), lambda b, i: (b * nq + i, 0)),
        out_shape=jax.ShapeDtypeStruct((bsz * t, 2 * kvw), BF16),
        compiler_params=_params(2),
        name="gqa_latent" if cached else "gqa_context",
    )(*args)


def _head_masks(width):
    lane_head = lax.broadcasted_iota(jnp.int32, (1, width), 1) // HEAD_DIM
    return [lane_head == h for h in range(width // HEAD_DIM)]


def _stack_heads(x, masks):
    return jnp.concatenate([jnp.where(m, x, jnp.zeros_like(x)) for m in masks], axis=0)


def _block_diag_mask(width):
    r = lax.broadcasted_iota(jnp.int32, (width, width), 0) // HEAD_DIM
    c = lax.broadcasted_iota(jnp.int32, (width, width), 1) // HEAD_DIM
    return r == c


def _ref_rows(b, offsets, span):
    width = b.shape[-1]
    return jnp.concatenate([jnp.broadcast_to(b[o:o + 1], (span, width)) for o in offsets], axis=0)


def _hgrn_chunks(problems):
    n = len(problems)
    c, width = problems[0][0].shape
    qs = [p[0] for p in problems]
    vs = [p[2] for p in problems]
    sts = [p[3] for p in problems]
    rev = [p[4] for p in problems]
    chains = range(n)
    masks = _head_masks(width)
    trow =lax.broadcasted_iota(jnp.int32, (c, 1), 0)
    t_full = lax.broadcasted_iota(jnp.int32, (c, width), 0)
    s_full = lax.broadcasted_iota(jnp.int32, (c, width), 1) % c

    ks = [1.0 - p[1] for p in problems]
    b = [jnp.log(p[1]) for p in problems]
    step = 1
    while step < c:
        for j in chains:
            if rev[j]:
                b[j] = b[j] + jnp.where(trow < c - step, pltpu.roll(b[j], c - step, 0), 0.0)
            else:
                b[j] = b[j] + jnp.where(trow >= step, pltpu.roll(b[j], step, 0), 0.0)
        step *= 2
    b_end = [b[j][0:1] if rev[j] else b[j][c - 1:c] for j in chains]

    o = [lax.dot_general((qs[j] * jnp.exp(b[j])).astype(BF16), sts[j].astype(BF16), NT,
                         preferred_element_type=F32) for j in chains]

    a = [None] * n
    m = c // 2
    while m >= DIAG_BLOCK:
        blocks = c // (2 * m)
        same = (t_full // (2 * m)) == (s_full // (2 * m))
        for j in chains:
            ref = _ref_rows(b[j], [i * 2 * m + (m if rev[j] else m - 1) for i in range(blocks)],
                            2 * m)
            is_q = ((trow % (2 * m)) < m) if rev[j] else ((trow % (2 * m)) >= m)
            e = jnp.exp(jnp.where(is_q, b[j] - ref, ref - b[j]))
            ql = jnp.where(is_q, qs[j] * e, 0.0).astype(BF16)
            kl = jnp.where(is_q, 0.0, ks[j] * e).astype(BF16)
            al = lax.dot_general(ql, _stack_heads(kl, masks), NT, preferred_element_type=F32)
            if blocks > 1:
                al = jnp.where(same, al, 0.0)
            a[j] = al if a[j] is None else a[j] + al
        m //= 2
    blocks = c // DIAG_BLOCK
    mid = DIAG_BLOCK // 2
    same = (t_full // DIAG_BLOCK) == (s_full // DIAG_BLOCK)
    for j in chains:
        ref = _ref_rows(b[j], [i * DIAG_BLOCK + (mid if rev[j] else mid - 1) for i in range(blocks)],
                        DIAG_BLOCK)
        d = b[j] - ref
        ql = (qs[j] * jnp.exp(d)).astype(BF16)
        kl = (ks[j] * jnp.exp(-d)).astype(BF16)
        al = lax.dot_general(ql, _stack_heads(kl, masks), NT, preferred_element_type=F32)
        causal = (s_full >= t_full) if rev[j] else (s_full <= t_full)
        a[j] = a[j] + jnp.where(same & causal, al, 0.0)

    v_b = [v.astype(BF16) for v in vs]
    o = [o[j] + _dot(a[j].astype(BF16), _stack_heads(v_b[j], masks)) for j in chains]

    bd = _block_diag_mask(width)
    upd = [lax.dot_general(v_b[j], (ks[j] * jnp.exp(b_end[j] - b[j])).astype(BF16), TN,
                           preferred_element_type=F32) for j in chains]
    st_new = [sts[j] * jnp.exp(b_end[j]) + jnp.where(bd, upd[j], 0.0) for j in chains]
    return list(zip(o, st_new))


def _mxu_transpose(x):
    n = x.shape[1]
    r = lax.broadcasted_iota(jnp.int32, (n, n), 0)
    c = lax.broadcasted_iota(jnp.int32, (n, n), 1)
    eye = (r == c).astype(BF16)
    acc = None
    rem = x
    for _ in range(3):
        piece = rem.astype(BF16)
        rem = rem - piece.astype(F32)
        part = lax.dot_general(eye, piece, NT, preferred_element_type=F32)
        acc = part if acc is None else acc + part
    return acc


def _hgrn_kernel(*refs, li, has_state, want_state, n_alias, heads):
    refs = list(refs)
    q_ref, ff_ref, fb_ref, v_ref, gate_ref, gn_ref = refs[:6]
    pos = 6
    if has_state:
        s0f_ref, s0b_ref = refs[pos:pos + 2]
        pos += 2
    pos += n_alias
    o_ref = refs[pos]
    pos += 1
    if want_state:
        sf_ref, sb_ref = refs[pos:pos + 2]
        pos += 2
    st_ref, of_ref, ob_ref = refs[pos:pos + 3]
    nb, t, width = q_ref.shape
    nc = t // CHUNK
    bd = _block_diag_mask(width)

    for n in range(nb):
        for d in range(2):
            if has_state:
                x = (s0b_ref if d else s0f_ref)[n, 0].reshape(width, HEAD_DIM)
                xt = _mxu_transpose(x)
                st_ref[2 * n + d] = jnp.where(bd, jnp.concatenate([xt] * heads, axis=0), 0.0)
            else:
                st_ref[2 * n + d] = jnp.zeros((width, width), F32)

    def body(ci, carry):
        rows = (pl.ds(pl.multiple_of(ci * CHUNK, CHUNK), CHUNK),
                pl.ds(pl.multiple_of((nc - 1 - ci) * CHUNK, CHUNK), CHUNK))
        loaded = []
        for n in range(nb):
            for d, f_ref in enumerate((ff_ref, fb_ref)):
                r = rows[d]
                loaded.append((q_ref[n, r, :], f_ref[n, r, :], v_ref[n, r, :], st_ref[2 * n + d],
                               bool(d)))
        for j, (o, st) in enumerate(_hgrn_chunks(loaded)):
            n, d = divmod(j, 2)
            (ob_ref if d else of_ref)[n, rows[d], :] = o
            st_ref[j] = st
        return carry

    lax.fori_loop(0, nc, body, 0)

    for n in range(nb):
        o = of_ref[n] + ob_ref[n]
        o_ref[n] = (_group_rms(o, gn_ref[li:li + 1, :], HEAD_DIM) * gate_ref[n]).astype(o_ref.dtype)

    if want_state:
        for n in range(nb):
            for d, dst in enumerate((sf_ref, sb_ref)):
                st = st_ref[2 * n + d]
                rows = st[0:HEAD_DIM]
                for h in range(1, heads):
                    rows = rows + st[h * HEAD_DIM:(h + 1) * HEAD_DIM]
                final = _mxu_transpose(rows).reshape(heads, HEAD_DIM, HEAD_DIM)
                for slot in range(dst.shape[1]):
                    dst[n, slot] = final


def _hgrn(hq, ff, fb, hv, hg, gn, state, state_prev, *, li, depth, want_state, nb):
    bsz, t, width = hq.shape
    heads = width // HEAD_DIM
    has_state = state is not None
    seq = pl.BlockSpec((nb, t, width), lambda b: (b, 0, 0))
    in_specs = [seq] * 5 + [pl.BlockSpec(gn.shape, lambda b: (0, 0))]
    args = [hq, ff, fb, hv, hg, gn]
    if has_state:
        in_specs += [pl.BlockSpec((nb, 1, heads, HEAD_DIM, HEAD_DIM), lambda b: (b, li, 0, 0, 0))] * 2
        args += list(state)
    aliases = {}
    if state_prev is not None:
        for j, buf in enumerate(state_prev):
            aliases[len(args)] = 1 + j
            in_specs.append(pl.BlockSpec(memory_space=pl.ANY))
            args.append(buf)
    out_specs = [seq]
    out_shape = [jax.ShapeDtypeStruct((bsz, t, width), BF16)]
    if want_state:
        slots, slot0 = (depth, 0) if state_prev is None else (1, li)
        out_specs += [pl.BlockSpec((nb, slots, heads, HEAD_DIM, HEAD_DIM),
                                   lambda b: (b, slot0, 0, 0, 0))] * 2
        out_shape += [jax.ShapeDtypeStruct((bsz, depth, heads, HEAD_DIM, HEAD_DIM), F32)] * 2
    return pl.pallas_call(
        functools.partial(_hgrn_kernel, li=li, has_state=has_state, want_state=want_state,
                          n_alias=len(aliases), heads=heads),
        grid=(bsz // nb,),
        in_specs=in_specs, out_specs=out_specs, out_shape=out_shape,
        input_output_aliases=aliases,
        scratch_shapes=[pltpu.VMEM((2 * nb, width, width), F32),
                        pltpu.VMEM((nb, t, width), F32), pltpu.VMEM((nb, t, width), F32)],
        compiler_params=_params(1),
        name="hgrn2_latent" if has_state else "hgrn2_context",
    )(*args)


def _out_mlp_kernel(x_ref, oa_ref, ob_ref, oc_ref, mod_ref, wo_ref, w1_ref, w2_ref,
                    g1_ref, b1_ref, g2_ref, b2_ref, y_ref, *, li, d, alpha, ff_chunk):
    wa, wb = oa_ref.shape[-1], ob_ref.shape[-1]
    layer = slice(li, li + 1)
    gate1 = mod_ref[0, :, 2 * d:3 * d]
    shift2 = mod_ref[0, :, 3 * d:4 * d]
    gain2 = mod_ref[0, :, 4 * d:5 * d]
    gate2 = mod_ref[0, :, 5 * d:6 * d]
    subs = [slice(s * ROW_TILE, (s + 1) * ROW_TILE) for s in range(x_ref.shape[0] // ROW_TILE)]
    m = [_dot(oa_ref[r, :], wo_ref[0:wa, :]) + _dot(ob_ref[r, :], wo_ref[wa:wa + wb, :])
         + _dot(oc_ref[r, :], wo_ref[wa + wb:, :]) for r in subs]
    x1 = [_layernorm(alpha * x_ref[r, :] + gate1 * mi, g1_ref[layer, :], b1_ref[layer, :])
          for r, mi in zip(subs, m)]
    h2 = [(xi * (1.0 + gain2) + shift2).astype(BF16) for xi in x1]
    acc = [None] * len(subs)
    for j in range(w1_ref.shape[-1] // ff_chunk):
        cols = slice(j * ff_chunk, (j + 1) * ff_chunk)
        hid = [jnp.maximum(_dot(hi, w1_ref[:, cols]), 0.0) for hi in h2]
        for s, hd in enumerate(hid):
            part = _dot((hd * hd).astype(BF16), w2_ref[cols, :])
            acc[s] = part if acc[s] is None else acc[s] + part
    for r, xi, ai in zip(subs, x1, acc):
        y_ref[r, :] = _layernorm(alpha * xi + gate2 * ai, g2_ref[layer, :], b2_ref[layer, :])


def _out_mlp(x, oa, ob, oc, mod, mod_row, w_out, w_ff1, w_ff2, ln, *, li, alpha):
    bsz, t, d = x.shape
    rows = 2 * ROW_TILE
    row = lambda i: (i, 0)
    const = lambda i: (0, 0)
    resident = lambda a: pl.BlockSpec((None,) + a.shape[1:], lambda i: (li, 0, 0),
                                      pipeline_mode=pl.Buffered(1))
    in_specs = [pl.BlockSpec((rows, d), row),
                pl.BlockSpec((rows, oa.shape[-1]), row),
                pl.BlockSpec((rows, ob.shape[-1]), row),
                pl.BlockSpec((rows, oc.shape[-1]), row),
                pl.BlockSpec((1, 1, mod.shape[-1]), lambda i: (mod_row(i * rows), 0, 0)),
                resident(w_out), resident(w_ff1), resident(w_ff2)]
    in_specs += [pl.BlockSpec(a.shape, const) for a in ln]
    y = pl.pallas_call(
        functools.partial(_out_mlp_kernel, li=li, d=d, alpha=alpha, ff_chunk=1024),
        grid=(bsz * t // rows,),
        in_specs=in_specs,
        out_specs=pl.BlockSpec((rows, d), row),
        out_shape=jax.ShapeDtypeStruct((bsz * t, d), F32),
        compiler_params=_params(1),
        name="out_mlp",
    )(x.reshape(bsz * t, d), oa, ob.reshape(bsz * t, -1), oc, mod, w_out, w_ff1, w_ff2, *ln)
    return y.reshape(bsz, t, d)


def _rope_tables(n_tokens):
    pairs = HEAD_DIM // 4
    tok = np.arange(n_tokens)
    row = (tok // GRID_W).astype(np.float64)
    col = (tok % GRID_W).astype(np.float64)
    inv = ROPE_THETA ** (-np.arange(pairs, dtype=np.float64) / pairs)
    ang = np.concatenate([row[:, None] * inv, col[:, None] * inv], axis=-1)
    lane = np.arange(LANES)
    pair = (lane % HEAD_DIM) // 2
    sign = np.where(lane % 2 == 0, -1.0, 1.0)
    return (jnp.asarray(np.cos(ang)[:, pair], F32), jnp.asarray(np.sin(ang)[:, pair] * sign, F32))


def kernel(x_prompt, x_sample, cache_a_k, cache_a_v, cache_c_k, cache_c_v, state_b_fwd, state_b_bwd, c, c_ctx, w_ada, b_ada, w_in, w_out, lam_q1, lam_k1, lam_q2, lam_k2, subln_g, lb_logits_fwd, lb_logits_bwd, gnorm_g, qnorm_g, knorm_g, ln1_g, ln1_b, ln2_g, ln2_b, w_ff1, w_ff2):
    depth = w_in.shape[0]
    bsz, seq, d = x_prompt.shape
    dec_bsz, dec_seq, _ = x_sample.shape
    past = cache_a_k.shape[2]
    alpha = (2 * depth) ** 0.25
    mix_a, mix_b, mix_c = d // 2, d // 4, d // 4

    cond = jnp.concatenate([c_ctx[None, :], c, jnp.zeros((8 - 1 - dec_bsz, d), F32)], axis=0)
    mod = _modulation(cond, w_ada, b_ada)
    rope = _rope_tables(dec_seq)

    cache = (cache_a_k.transpose(0, 1, 3, 4, 5, 2).reshape(dec_bsz, depth, mix_a, past),
             cache_a_v.reshape(dec_bsz, depth, past * (mix_a // LANES), LANES),
             cache_c_k.transpose(0, 1, 3, 4, 2).reshape(dec_bsz, depth, mix_c // 2, past),
             cache_c_v.transpose(0, 1, 3, 4, 2).reshape(dec_bsz, depth, mix_c // 2, past))
    lam = (lam_q1, lam_k1, lam_q2, lam_k2)

    mod = mod.reshape(depth * 8, 1, 6 * d)
    weights = tuple(w.astype(BF16) for w in (w_in, w_out, w_ff1, w_ff2))
    qn = jnp.tile(qnorm_g, (1, mix_c // HEAD_DIM))
    kn = jnp.tile(knorm_g, (1, mix_c // 2 // HEAD_DIM))
    gn = jnp.tile(gnorm_g, (1, mix_b // HEAD_DIM))
    ln = (ln1_g, ln1_b, ln2_g, ln2_b)

    def stream(x, li, latent, own_prev):
        w_in_b, w_out_b, w1_b, w2_b = weights
        n, t, _ = x.shape
        mod_row = (lambda r0: li * 8 + 1 + r0 // t) if latent else (lambda r0: li * 8)
        kv_layer = 0 if latent else li
        (qa, ka, va, hq, ff, fb, hv, hg, qc, kc, vc, *extra) = _in_proj(
            x, mod, mod_row, w_in_b, lb_logits_fwd, lb_logits_bwd, qn, kn,
            rope if latent else None, None if own_prev is None else own_prev[0:7], li=li)
        tq = 256
        oa = _diff_attn(qa, ka, va, cache[0:2] if latent else None, lam, subln_g,
                        li=li, kv_layer=kv_layer, bsz=n, tq=tq, heads_per_step=2 if latent else 4)
        oc = _gqa(qc, kc, vc, cache[2:4] if latent else None, li=li, kv_layer=kv_layer, bsz=n, tq=tq)
        hres = _hgrn(hq, ff, fb, hv, hg, gn, (state_b_fwd, state_b_bwd) if latent else None,
                     None if own_prev is None else own_prev[7:9],
                     li=li, depth=depth, want_state=not latent, nb=2)
        y = _out_mlp(x, oa, hres[0], oc, mod, mod_row, w_out_b, w1_b, w2_b, ln, li=li, alpha=alpha)
        own = None if latent else (ka, va, kc, vc, *extra, hres[1], hres[2])
        return y, own

    y_prompt, y_sample = x_prompt, x_sample
    own = None
    for li in range(depth):
        y_prompt, own = stream(y_prompt, li, False, own)
        y_sample, _ = stream(y_sample, li, True, None)

    heads_a = mix_a // (2 * HEAD_DIM)
    new_a_k = own[0].reshape(bsz, depth, seq, heads_a, 2, HEAD_DIM)
    new_a_v = own[4].reshape(bsz, depth, seq, heads_a, 2 * HEAD_DIM)
    kv_heads = mix_c // 2 // HEAD_DIM
    new_c_k = own[5].reshape(bsz, depth, kv_heads, HEAD_DIM, seq).transpose(0, 1, 4, 2, 3)
    new_c_v = own[6].reshape(bsz, depth, kv_heads, HEAD_DIM, seq).transpose(0, 1, 4, 2, 3)
    return (y_prompt, y_sample, new_a_k, new_a_v, new_c_k, new_c_v, own[7], own[8])
```

```python
import functools
import math

import jax
import jax.numpy as jnp
import numpy as np
from jax import lax
from jax.experimental import pallas as pl
from jax.experimental.pallas import tpu as pltpu

GRID_W = 64
HEAD_DIM = 64
ROPE_THETA = 10000.0
LN_EPS = 1e-6
RMS_EPS = 1e-6
F_MIN = 1e-6
CHUNK = 64
DIAG_BLOCK = 8
LANES = 128
ROW_TILE = 256
VMEM_LIMIT = 56 * 1024 * 1024

F32 = jnp.float32
BF16 = jnp.bfloat16
NT = (((1,), (1,)), ((), ()))
TN = (((0,), (0,)), ((), ()))


def _params(n_grid):
    return pltpu.CompilerParams(dimension_semantics=("arbitrary",) * n_grid,
                                vmem_limit_bytes=VMEM_LIMIT)


def _dot(a, b):
    return jnp.dot(a, b, preferred_element_type=F32)


def _split_dot(a, b_bf16, passes, dims=None):
    acc = None
    rem = a
    for _ in range(passes):
        piece = rem.astype(BF16)
        rem = rem - piece.astype(F32)
        part = (_dot(piece, b_bf16) if dims is None
                else lax.dot_general(piece, b_bf16, dims, preferred_element_type=F32))
        acc = part if acc is None else acc + part
    return acc


def _group_ones(n, group):
    r = lax.broadcasted_iota(jnp.int32, (n, n), 0) // group
    c = lax.broadcasted_iota(jnp.int32, (n, n), 1) // group
    return (r == c).astype(BF16)


def _group_mean_square(x, group):
    n = x.shape[-1]
    return _split_dot(x * x, _group_ones(n, group), 2) * (1.0 / group)


def _group_rms(x, g_row, group):
    return x * lax.rsqrt(_group_mean_square(x, group) + RMS_EPS) * g_row


def _pair_swap(x):
    lane = lax.broadcasted_iota(jnp.int32, x.shape, 1)
    return jnp.where(lane % 2 == 0, pltpu.roll(x, LANES - 1, 1), pltpu.roll(x, 1, 1))


def _rope(x, cos, sin):
    blocks = []
    for j in range(x.shape[-1] // LANES):
        blk = x[:, j * LANES:(j + 1) * LANES]
        blocks.append(blk * cos + _pair_swap(blk) * sin)
    return blocks[0] if len(blocks) == 1 else jnp.concatenate(blocks, axis=-1)


def _silu(x):
    return x * jax.nn.sigmoid(x)


def _layernorm(x, g, b):
    mu = jnp.mean(x, axis=-1, keepdims=True)
    xc = x - mu
    var = jnp.mean(xc * xc, axis=-1, keepdims=True)
    return xc * lax.rsqrt(var + LN_EPS) * g + b


MOD_ROWS = 8


def _mod_kernel(cctx_ref, c_ref, w_ref, b_ref, o_ref, s_ref):
    n_req = c_ref.shape[0]
    s_ref[...] = jnp.zeros_like(s_ref)
    s_ref[0:1, :] = _silu(cctx_ref[...])
    s_ref[1:1 + n_req, :] = _silu(c_ref[...])
    layer = pl.program_id(0)
    res = _dot(s_ref[...].astype(BF16), w_ref[0].astype(BF16)) + b_ref[pl.ds(layer, 1), :]
    for r in range(MOD_ROWS):
        o_ref[r] = res[r:r + 1, :]


def _modulation(c_ctx, c, w_ada, b_ada):
    depth, d, n = w_ada.shape
    tn = 1536
    return pl.pallas_call(
        _mod_kernel,
        grid=(depth, n // tn),
        in_specs=[pl.BlockSpec((1, d), lambda l, j: (0, 0)),
                  pl.BlockSpec(c.shape, lambda l, j: (0, 0)),
                  pl.BlockSpec((1, d, tn), lambda l, j: (l, 0, j)),
                  pl.BlockSpec((depth, tn), lambda l, j: (0, j))],
        out_specs=pl.BlockSpec((MOD_ROWS, 1, tn), lambda l, j: (l, 0, j)),
        out_shape=jax.ShapeDtypeStruct((depth * MOD_ROWS, 1, n), F32),
        scratch_shapes=[pltpu.VMEM((MOD_ROWS, d), F32)],
        compiler_params=_params(2),
        name="adaln_modulation",
    )(c_ctx.reshape(1, d), c, w_ada, b_ada)


def _in_proj_kernel(*refs, li, d, latent, n_alias):
    refs = list(refs)
    x_ref, mod_ref, w_ref, lbf_ref, lbb_ref, qn_ref, kn_ref = refs[:7]
    pos = 7
    if latent:
        cos, sin = refs[pos][...], refs[pos + 1][...]
        pos += 2
    pos += n_alias
    qa_o, ka_o, va_o, hq_o, ff_o, fb_o, hv_o, hg_o, qc_o, kc_o, vc_o = refs[pos:pos + 11]
    va_rows_o, kct_o, vct_o = (None, None, None) if latent else refs[pos + 11:pos + 14]

    def store_kv(ref, val):
        for slot in range(ref.shape[1]):
            ref[0, slot] = val.astype(ref.dtype)

    mix_a, mix_b, mix_c = d // 2, d // 4, d // 4
    kv_c = mix_c // 2
    scale = HEAD_DIM ** -0.5 * math.log2(math.e)

    shift = mod_ref[0, :, 0:d]
    gain = mod_ref[0, :, d:2 * d]
    h = x_ref[...] * (1.0 + gain) + shift

    def proj(start, width):
        return _dot(h, w_ref[:, start:start + width])

    off_b = 3 * mix_a
    off_c = off_b + 5 * mix_b

    zq = proj(off_c, mix_c)
    zk = proj(off_c + mix_c, kv_c)
    vc = proj(off_c + mix_c + kv_c, kv_c)
    qa = proj(0, mix_a)
    msq = _group_mean_square(zq, HEAD_DIM)
    msk = _group_mean_square(zk, HEAD_DIM)
    ka = proj(mix_a, mix_a)
    va = proj(2 * mix_a, mix_a)
    qc = zq * lax.rsqrt(msq + RMS_EPS) * qn_ref[li:li + 1, :]
    kc = zk * lax.rsqrt(msk + RMS_EPS) * kn_ref[li:li + 1, :]

    if latent:
        qa = _rope(qa, cos, sin)
        ka = _rope(ka, cos, sin)
    qa_o[...] = (qa * scale).astype(qa_o.dtype)
    store_kv(ka_o, ka)
    store_kv(va_o, va)
    if va_rows_o is not None:
        heads = mix_a // LANES
        for slot in range(va_rows_o.shape[1]):
            for hd in range(heads):
                va_rows_o[0, slot, pl.ds(hd, ROW_TILE, stride=heads), :] = (
                    va[:, hd * LANES:(hd + 1) * LANES])

    def lower_bound(ref):
        logits = ref[...]
        e = jnp.exp(logits - jnp.max(logits, axis=0, keepdims=True))
        sm = e / jnp.sum(e, axis=0, keepdims=True)
        return jnp.sum(sm[0:li + 1], axis=0, keepdims=True) - sm[0:1]

    def forget(x, lb):
        return jnp.maximum(lb + (1.0 - lb) * jax.nn.sigmoid(x), F_MIN)

    off = off_b
    zb = [proj(off + j * mix_b, mix_b) for j in range(5)]

    if latent:
        kc = _rope(kc, cos, sin)
        qc = _rope(qc, cos, sin)
    store_kv(kc_o, kc)
    store_kv(vc_o, vc)
    if kct_o is not None:
        store_kv(kct_o, kc.T)
        store_kv(vct_o, vc.T)
    qc = qc * scale
    lane = lax.broadcasted_iota(jnp.int32, (1, LANES), 1)
    for n in range(2):
        blk = qc[:, n * LANES:(n + 1) * LANES]
        in_half = (lane // HEAD_DIM) == n
        for g in range(2):
            src = blk if g == n else pltpu.roll(blk, HEAD_DIM, 1)
            hc = 2 * n + g
            qc_o[:, hc * LANES:(hc + 1) * LANES] = jnp.where(in_half, src, 0.0).astype(qc_o.dtype)

    hq_o[0] = _silu(zb[0])
    ff_o[0] = forget(zb[1], lower_bound(lbf_ref))
    fb_o[0] = forget(zb[2], lower_bound(lbb_ref))
    hv_o[0] = zb[3]
    hg_o[0] = _silu(zb[4])


def _in_proj(x, mod, mod_row, w_in, lb_f, lb_b, qn, kn, rope, kv_prev, *, li):
    bsz, t, d = x.shape
    latent = rope is not None
    depth, _, n_in = w_in.shape
    tiles = t // ROW_TILE
    mix_a, mix_b, mix_c = d // 2, d // 4, d // 4
    kv_c = mix_c // 2
    x2 = x.reshape(bsz * t, d)

    row = lambda i: (i, 0)
    brow = lambda i: (i // tiles, i % tiles, 0)
    const = lambda i: (0, 0)
    in_specs = [pl.BlockSpec((ROW_TILE, d), row),
                pl.BlockSpec((1, 1, mod.shape[-1]), lambda i: (mod_row(i * ROW_TILE), 0, 0)),
                pl.BlockSpec((None, d, n_in), lambda i: (li, 0, 0)),
                pl.BlockSpec(lb_f.shape, const), pl.BlockSpec(lb_b.shape, const),
                pl.BlockSpec(qn.shape, const), pl.BlockSpec(kn.shape, const)]
    args = [x2, mod, w_in, lb_f, lb_b, qn, kn]
    if latent:
        in_specs += [pl.BlockSpec((ROW_TILE, LANES), lambda i: (i % tiles, 0))] * 2
        args += list(rope)
        kv_shape, kv_slots, kv_slot0, kdt = (bsz, 1, t), 1, 0, BF16
    else:
        first = kv_prev is None
        kv_shape, kv_slots, kv_slot0, kdt = (bsz, depth, t), (depth if first else 1), (0 if first else li), F32
    aliases = {}
    if kv_prev is not None:
        kv_out_index = (1, 2, 9, 10, 11, 12, 13)
        for buf, out_index in zip(kv_prev, kv_out_index):
            aliases[len(args)] = out_index
            in_specs.append(pl.BlockSpec(memory_space=pl.ANY))
            args.append(buf)
    krow = lambda i: (i // tiles, kv_slot0, i % tiles, 0)

    def hspec():
        return pl.BlockSpec((1, ROW_TILE, mix_b), brow)

    def kvspec(width):
        return pl.BlockSpec((1, kv_slots, ROW_TILE, width), krow)

    out_specs = [pl.BlockSpec((ROW_TILE, mix_a), row), kvspec(mix_a), kvspec(mix_a),
                 hspec(), hspec(), hspec(), hspec(), hspec(),
                 pl.BlockSpec((ROW_TILE, 2 * mix_c), row), kvspec(kv_c), kvspec(kv_c)]
    out_shape = [jax.ShapeDtypeStruct((bsz * t, mix_a), BF16),
                 jax.ShapeDtypeStruct(kv_shape + (mix_a,), kdt),
                 jax.ShapeDtypeStruct(kv_shape + (mix_a,), kdt)]
    out_shape += [jax.ShapeDtypeStruct((bsz, t, mix_b), F32)] * 5
    out_shape += [jax.ShapeDtypeStruct((bsz * t, 2 * mix_c), BF16),
                  jax.ShapeDtypeStruct(kv_shape + (kv_c,), kdt),
                  jax.ShapeDtypeStruct(kv_shape + (kv_c,), kdt)]
    if not latent:
        heads = mix_a // LANES
        out_specs.append(pl.BlockSpec((1, kv_slots, ROW_TILE * heads, LANES), krow))
        out_shape.append(jax.ShapeDtypeStruct((bsz, depth, t * heads, LANES), F32))
        tcol = lambda i: (i // tiles, kv_slot0, 0, i % tiles)
        out_specs += [pl.BlockSpec((1, kv_slots, kv_c, ROW_TILE), tcol)] * 2
        out_shape += [jax.ShapeDtypeStruct((bsz, depth, kv_c, t), F32)] * 2
    return pl.pallas_call(
        functools.partial(_in_proj_kernel, li=li, d=d, latent=latent, n_alias=len(aliases)),
        grid=(bsz * tiles,),
        in_specs=in_specs, out_specs=out_specs, out_shape=out_shape,
        input_output_aliases=aliases,
        compiler_params=_params(1),
        name="in_proj_latent" if latent else "in_proj_context",
    )(*args)


def _softmax_parts(scores):
    m = functools.reduce(jnp.maximum, [jnp.max(s, axis=-1, keepdims=True) for s in scores])
    es = [jnp.exp2(s - m) for s in scores]
    denom = functools.reduce(lambda a, b: a + b, [jnp.sum(e, axis=-1, keepdims=True) for e in es])
    return es, 1.0 / denom


def _diff_attn_kernel(*refs, li, cached, heads):
    if cached:
        (q_ref, k_ref, v_ref, ck_ref, cv_ref, lq1, lk1, lq2, lk2, sub_ref, o_ref) = refs
    else:
        (q_ref, k_ref, v_ref, lq1, lk1, lq2, lk2, sub_ref, o_ref) = refs
    lam_init = 0.8 - 0.6 * math.exp(-0.3 * li)

    def lam_term(a, b):
        return jnp.exp(jnp.sum(a[li:li + 1, :] * b[li:li + 1, :], axis=-1, keepdims=True))

    lam = lam_term(lq1, lk1) - lam_term(lq2, lk2) + lam_init
    tq = q_ref.shape[0]
    lane = lax.broadcasted_iota(jnp.int32, (1, LANES), 1)
    cols = [slice(h * LANES, (h + 1) * LANES) for h in range(heads)]

    scores, vals = [], []
    for c in cols:
        q = q_ref[:, c]
        zero = jnp.zeros_like(q)
        q2 = jnp.concatenate([jnp.where(lane < HEAD_DIM, q, zero),
                              jnp.where(lane >= HEAD_DIM, q, zero)], axis=0)
        head_scores = [lax.dot_general(q2, k_ref[0, 0, :, c].astype(BF16), NT,
                                       preferred_element_type=F32)]
        vals.append([v_ref[0, 0, :, c].astype(BF16)])
        if cached:
            past = ck_ref.shape[-1]
            all_heads = cv_ref.shape[2] // past
            head = pl.program_id(1) * heads + c.start // LANES
            head_scores.append(_dot(q2, ck_ref[0, 0, c, :].astype(BF16)))
            vals[-1].append(cv_ref[0, 0, pl.ds(head, past, stride=all_heads), :].astype(BF16))
        scores.append(head_scores)
    parts = [_softmax_parts(s) for s in scores]
    outs = []
    for (es, r), vs in zip(parts, vals):
        r0 = r[0:tq]
        r1 = r[tq:2 * tq] * lam
        o = None
        for e, v in zip(es, vs):
            part = _dot((e[0:tq] * r0 - e[tq:2 * tq] * r1).astype(BF16), v)
            o = part if o is None else o + part
        outs.append(o)
    gain = sub_ref[li:li + 1, :] * (1.0 - lam_init)
    for c, o in zip(cols, outs):
        ms = jnp.mean(o * o, axis=-1, keepdims=True)
        o_ref[:, c] = (o * lax.rsqrt(ms + RMS_EPS) * gain).astype(o_ref.dtype)


def _diff_attn(q, k, v, cache, lam, subln, *, li, kv_layer, bsz, tq, heads_per_step):
    t = k.shape[2]
    width = q.shape[-1]
    wstep = heads_per_step * LANES
    nq = t // tq
    cached = cache is not None
    in_specs = [pl.BlockSpec((tq, wstep), lambda b, h, i: (b * nq + i, h)),
                pl.BlockSpec((1, 1, t, wstep), lambda b, h, i: (b, kv_layer, 0, h)),
                pl.BlockSpec((1, 1, t, wstep), lambda b, h, i: (b, kv_layer, 0, h))]
    args = [q, k, v]
    if cached:
        ck, cv = cache
        in_specs += [pl.BlockSpec((1, 1, wstep, ck.shape[-1]), lambda b, h, i: (b, li, h, 0)),
                     pl.BlockSpec((1, 1) + cv.shape[2:], lambda b, h, i: (b, li, 0, 0))]
        args += [ck, cv]
    in_specs += [pl.BlockSpec(a.shape, lambda b, h, i: (0, 0)) for a in (*lam, subln)]
    args += [*lam, subln]
    return pl.pallas_call(
        functools.partial(_diff_attn_kernel, li=li, cached=cached, heads=heads_per_step),
        grid=(bsz, width // wstep, nq),
        in_specs=in_specs,
        out_specs=pl.BlockSpec((tq, wstep), lambda b, h, i: (b * nq + i, h)),
        out_shape=jax.ShapeDtypeStruct((bsz * t, width), BF16),
        compiler_params=_params(3),
        name="diff_attn_latent" if cached else "diff_attn_context",
    )(*args)


def _gqa_kernel(*refs, cached):
    if cached:
        q_ref, k_ref, v_ref, ck_ref, cv_ref, o_ref = refs
    else:
        q_ref, k_ref, v_ref, o_ref = refs
    tq = q_ref.shape[0]
    heads = q_ref.shape[1] // LANES
    lane = lax.broadcasted_iota(jnp.int32, (1, LANES), 1)
    key = k_ref[0, 0].astype(BF16)
    val = v_ref[0, 0].astype(BF16)
    if cached:
        key_t = ck_ref[0, 0].astype(BF16)
        val_t = cv_ref[0, 0].astype(BF16)
    groups = range(heads // 2)
    scores = []
    for n in groups:
        q = jnp.concatenate([q_ref[:, (2 * n + g) * LANES:(2 * n + g + 1) * LANES]
                             for g in range(2)], axis=0)
        group_scores = [lax.dot_general(q, key, NT, preferred_element_type=F32)]
        if cached:
            group_scores.append(_dot(q, key_t))
        scores.append(group_scores)
    parts = [_softmax_parts(s) for s in scores]
    outs = []
    for es, r in parts:
        o = _dot(es[0].astype(BF16), val)
        if cached:
            o = o + lax.dot_general(es[1].astype(BF16), val_t, NT, preferred_element_type=F32)
        outs.append(o * r)
    for n in groups:
        first = outs[n][0:tq]
        second = outs[n][tq:2 * tq]
        if n == 0:
            second = pltpu.roll(second, HEAD_DIM, 1)
        else:
            first = pltpu.roll(first, HEAD_DIM, 1)
        o_ref[:, n * LANES:(n + 1) * LANES] = jnp.where(lane < HEAD_DIM, first,
                                                        second).astype(o_ref.dtype)


def _gqa(q, k, v, cache, *, li, kv_layer, bsz, tq):
    t = k.shape[2]
    kvw = k.shape[-1]
    nq = t // tq
    cached = cache is not None
    in_specs = [pl.BlockSpec((tq, q.shape[-1]), lambda b, i: (b * nq + i, 0)),
                pl.BlockSpec((1, 1, t, kvw), lambda b, i: (b, kv_layer, 0, 0)),
                pl.BlockSpec((1, 1, t, kvw), lambda b, i: (b, kv_layer, 0, 0))]
    args = [q, k, v]
    if cached:
        in_specs += [pl.BlockSpec((1, 1) + cache[0].shape[2:], lambda b, i: (b, li, 0, 0))] * 2
        args += list(cache)
    return pl.pallas_call(
        functools.partial(_gqa_kernel, cached=cached),
        grid=(bsz, nq),
        in_specs=in_specs,
        out_specs=pl.BlockSpec((tq, 2 * kvw), lambda b, i: (b * nq + i, 0)),
        out_shape=jax.ShapeDtypeStruct((bsz * t, 2 * kvw), BF16),
        compiler_params=_params(2),
        name="gqa_latent" if cached else "gqa_context",
    )(*args)


def _head_masks(width):
    lane_head = lax.broadcasted_iota(jnp.int32, (1, width), 1) // HEAD_DIM
    return [lane_head == h for h in range(width // HEAD_DIM)]


def _stack_heads(x, masks):
    return jnp.concatenate([jnp.where(m, x, jnp.zeros_like(x)) for m in masks], axis=0)


def _block_diag_mask(width):
    r = lax.broadcasted_iota(jnp.int32, (width, width), 0) // HEAD_DIM
    c = lax.broadcasted_iota(jnp.int32, (width, width), 1) // HEAD_DIM
    return r == c


def _ref_rows(b, offsets, span):
    width = b.shape[-1]
    return jnp.concatenate([jnp.broadcast_to(b[o:o + 1], (span, width)) for o in offsets], axis=0)


def _hgrn_chunks(problems):
    n = len(problems)
    c, width = problems[0][0].shape
    qs = [p[0] for p in problems]
    vs = [p[2] for p in problems]
    sts = [p[3] for p in problems]
    rev = [p[4] for p in problems]
    chains = range(n)
    masks = _head_masks(width)
    trow =lax.broadcasted_iota(jnp.int32, (c, 1), 0)
    t_full = lax.broadcasted_iota(jnp.int32, (c, width), 0)
    s_full = lax.broadcasted_iota(jnp.int32, (c, width), 1) % c

    ks = [1.0 - p[1] for p in problems]
    b = [jnp.log(p[1]) for p in problems]
    step = 1
    while step < c:
        for j in chains:
            if rev[j]:
                b[j] = b[j] + jnp.where(trow < c - step, pltpu.roll(b[j], c - step, 0), 0.0)
            else:
                b[j] = b[j] + jnp.where(trow >= step, pltpu.roll(b[j], step, 0), 0.0)
        step *= 2
    b_end = [b[j][0:1] if rev[j] else b[j][c - 1:c] for j in chains]

    o = [lax.dot_general((qs[j] * jnp.exp(b[j])).astype(BF16), sts[j].astype(BF16), NT,
                         preferred_element_type=F32) for j in chains]

    a = [None] * n
    m = c // 2
    while m >= DIAG_BLOCK:
        blocks = c // (2 * m)
        same = (t_full // (2 * m)) == (s_full // (2 * m))
        for j in chains:
            ref = _ref_rows(b[j], [i * 2 * m + (m if rev[j] else m - 1) for i in range(blocks)],
                            2 * m)
            is_q = ((trow % (2 * m)) < m) if rev[j] else ((trow % (2 * m)) >= m)
            e = jnp.exp(jnp.where(is_q, b[j] - ref, ref - b[j]))
            ql = jnp.where(is_q, qs[j] * e, 0.0).astype(BF16)
            kl = jnp.where(is_q, 0.0, ks[j] * e).astype(BF16)
            al = lax.dot_general(ql, _stack_heads(kl, masks), NT, preferred_element_type=F32)
            if blocks > 1:
                al = jnp.where(same, al, 0.0)
            a[j] = al if a[j] is None else a[j] + al
        m //= 2
    blocks = c // DIAG_BLOCK
    mid = DIAG_BLOCK // 2
    same = (t_full // DIAG_BLOCK) == (s_full // DIAG_BLOCK)
    for j in chains:
        ref = _ref_rows(b[j], [i * DIAG_BLOCK + (mid if rev[j] else mid - 1) for i in range(blocks)],
                        DIAG_BLOCK)
        d = b[j] - ref
        ql = (qs[j] * jnp.exp(d)).astype(BF16)
        kl = (ks[j] * jnp.exp(-d)).astype(BF16)
        al = lax.dot_general(ql, _stack_heads(kl, masks), NT, preferred_element_type=F32)
        causal = (s_full >= t_full) if rev[j] else (s_full <= t_full)
        a[j] = a[j] + jnp.where(same & causal, al, 0.0)

    v_b = [v.astype(BF16) for v in vs]
    o = [o[j] + _dot(a[j].astype(BF16), _stack_heads(v_b[j], masks)) for j in chains]

    bd = _block_diag_mask(width)
    upd = [lax.dot_general(v_b[j], (ks[j] * jnp.exp(b_end[j] - b[j])).astype(BF16), TN,
                           preferred_element_type=F32) for j in chains]
    st_new = [sts[j] * jnp.exp(b_end[j]) + jnp.where(bd, upd[j], 0.0) for j in chains]
    return list(zip(o, st_new))


def _mxu_transpose(x):
    n = x.shape[1]
    r = lax.broadcasted_iota(jnp.int32, (n, n), 0)
    c = lax.broadcasted_iota(jnp.int32, (n, n), 1)
    eye = (r == c).astype(BF16)
    acc = None
    rem = x
    for _ in range(3):
        piece = rem.astype(BF16)
        rem = rem - piece.astype(F32)
        part = lax.dot_general(eye, piece, NT, preferred_element_type=F32)
        acc = part if acc is None else acc + part
    return acc


def _hgrn_kernel(*refs, li, has_state, want_state, n_alias, heads):
    refs = list(refs)
    q_ref, ff_ref, fb_ref, v_ref, gate_ref, gn_ref = refs[:6]
    pos = 6
    if has_state:
        s0f_ref, s0b_ref = refs[pos:pos + 2]
        pos += 2
    pos += n_alias
    o_ref = refs[pos]
    pos += 1
    if want_state:
        sf_ref, sb_ref = refs[pos:pos + 2]
        pos += 2
    st_ref, of_ref, ob_ref = refs[pos:pos + 3]
    nb, t, width = q_ref.shape
    nc = t // CHUNK
    bd = _block_diag_mask(width)

    for n in range(nb):
        for d in range(2):
            if has_state:
                x = (s0b_ref if d else s0f_ref)[n, 0].reshape(width, HEAD_DIM)
                xt = _mxu_transpose(x)
                st_ref[2 * n + d] = jnp.where(bd, jnp.concatenate([xt] * heads, axis=0), 0.0)
            else:
                st_ref[2 * n + d] = jnp.zeros((width, width), F32)

    def body(ci, carry):
        rows = (pl.ds(pl.multiple_of(ci * CHUNK, CHUNK), CHUNK),
                pl.ds(pl.multiple_of((nc - 1 - ci) * CHUNK, CHUNK), CHUNK))
        loaded = []
        for n in range(nb):
            for d, f_ref in enumerate((ff_ref, fb_ref)):
                r = rows[d]
                loaded.append((q_ref[n, r, :], f_ref[n, r, :], v_ref[n, r, :], st_ref[2 * n + d],
                               bool(d)))
        for j, (o, st) in enumerate(_hgrn_chunks(loaded)):
            n, d = divmod(j, 2)
            (ob_ref if d else of_ref)[n, rows[d], :] = o
            st_ref[j] = st
        return carry

    lax.fori_loop(0, nc, body, 0)

    for n in range(nb):
        o = of_ref[n] + ob_ref[n]
        o_ref[n] = (_group_rms(o, gn_ref[li:li + 1, :], HEAD_DIM) * gate_ref[n]).astype(o_ref.dtype)

    if want_state:
        for n in range(nb):
            for d, dst in enumerate((sf_ref, sb_ref)):
                st = st_ref[2 * n + d]
                rows = st[0:HEAD_DIM]
                for h in range(1, heads):
                    rows = rows + st[h * HEAD_DIM:(h + 1) * HEAD_DIM]
                final = _mxu_transpose(rows).reshape(heads, HEAD_DIM, HEAD_DIM)
                for slot in range(dst.shape[1]):
                    dst[n, slot] = final


def _hgrn(hq, ff, fb, hv, hg, gn, state, state_prev, *, li, depth, want_state, nb):
    bsz, t, width = hq.shape
    heads = width // HEAD_DIM
    has_state = state is not None
    seq = pl.BlockSpec((nb, t, width), lambda b: (b, 0, 0))
    in_specs = [seq] * 5 + [pl.BlockSpec(gn.shape, lambda b: (0, 0))]
    args = [hq, ff, fb, hv, hg, gn]
    if has_state:
        in_specs += [pl.BlockSpec((nb, 1, heads, HEAD_DIM, HEAD_DIM), lambda b: (b, li, 0, 0, 0))] * 2
        args += list(state)
    aliases = {}
    if state_prev is not None:
        for j, buf in enumerate(state_prev):
            aliases[len(args)] = 1 + j
            in_specs.append(pl.BlockSpec(memory_space=pl.ANY))
            args.append(buf)
    out_specs = [seq]
    out_shape = [jax.ShapeDtypeStruct((bsz, t, width), BF16)]
    if want_state:
        slots, slot0 = (depth, 0) if state_prev is None else (1, li)
        out_specs += [pl.BlockSpec((nb, slots, heads, HEAD_DIM, HEAD_DIM),
                                   lambda b: (b, slot0, 0, 0, 0))] * 2
        out_shape += [jax.ShapeDtypeStruct((bsz, depth, heads, HEAD_DIM, HEAD_DIM), F32)] * 2
    return pl.pallas_call(
        functools.partial(_hgrn_kernel, li=li, has_state=has_state, want_state=want_state,
                          n_alias=len(aliases), heads=heads),
        grid=(bsz // nb,),
        in_specs=in_specs, out_specs=out_specs, out_shape=out_shape,
        input_output_aliases=aliases,
        scratch_shapes=[pltpu.VMEM((2 * nb, width, width), F32),
                        pltpu.VMEM((nb, t, width), F32), pltpu.VMEM((nb, t, width), F32)],
        compiler_params=_params(1),
        name="hgrn2_latent" if has_state else "hgrn2_context",
    )(*args)


def _out_mlp_kernel(x_ref, oa_ref, ob_ref, oc_ref, mod_ref, wo_ref, w1_ref, w2_ref,
                    g1_ref, b1_ref, g2_ref, b2_ref, y_ref, *, li, d, alpha, ff_chunk):
    wa, wb = oa_ref.shape[-1], ob_ref.shape[-1]
    layer = slice(li, li + 1)
    gate1 = mod_ref[0, :, 2 * d:3 * d]
    shift2 = mod_ref[0, :, 3 * d:4 * d]
    gain2 = mod_ref[0, :, 4 * d:5 * d]
    gate2 = mod_ref[0, :, 5 * d:6 * d]
    subs = [slice(s * ROW_TILE, (s + 1) * ROW_TILE) for s in range(x_ref.shape[0] // ROW_TILE)]
    wo = wo_ref[...].astype(BF16)
    m = [_dot(oa_ref[r, :], wo[0:wa, :]) + _dot(ob_ref[r, :], wo[wa:wa + wb, :])
         + _dot(oc_ref[r, :], wo[wa + wb:, :]) for r in subs]
    x1 = [_layernorm(alpha * x_ref[r, :] + gate1 * mi, g1_ref[layer, :], b1_ref[layer, :])
          for r, mi in zip(subs, m)]
    h2 = [(xi * (1.0 + gain2) + shift2).astype(BF16) for xi in x1]
    acc = [None] * len(subs)
    for j in range(w1_ref.shape[-1] // ff_chunk):
        cols = slice(j * ff_chunk, (j + 1) * ff_chunk)
        hid = [jnp.maximum(_dot(hi, w1_ref[:, cols]), 0.0) for hi in h2]
        for s, hd in enumerate(hid):
            part = _dot((hd * hd).astype(BF16), w2_ref[cols, :])
            acc[s] = part if acc[s] is None else acc[s] + part
    for r, xi, ai in zip(subs, x1, acc):
        y_ref[r, :] = _layernorm(alpha * xi + gate2 * ai, g2_ref[layer, :], b2_ref[layer, :])


def _out_mlp(x, oa, ob, oc, mod, mod_row, w_out, w_ff1, w_ff2, ln, *, li, alpha):
    bsz, t, d = x.shape
    rows = 2 * ROW_TILE
    row = lambda i: (i, 0)
    const = lambda i: (0, 0)
    resident = lambda a: pl.BlockSpec((None,) + a.shape[1:], lambda i: (li, 0, 0),
                                      pipeline_mode=pl.Buffered(1))
    in_specs = [pl.BlockSpec((rows, d), row),
                pl.BlockSpec((rows, oa.shape[-1]), row),
                pl.BlockSpec((rows, ob.shape[-1]), row),
                pl.BlockSpec((rows, oc.shape[-1]), row),
                pl.BlockSpec((1, 1, mod.shape[-1]), lambda i: (mod_row(i * rows), 0, 0)),
                resident(w_out), resident(w_ff1), resident(w_ff2)]
    in_specs += [pl.BlockSpec(a.shape, const) for a in ln]
    y = pl.pallas_call(
        functools.partial(_out_mlp_kernel, li=li, d=d, alpha=alpha, ff_chunk=1024),
        grid=(bsz * t // rows,),
        in_specs=in_specs,
        out_specs=pl.BlockSpec((rows, d), row),
        out_shape=jax.ShapeDtypeStruct((bsz * t, d), F32),
        compiler_params=_params(1),
        name="out_mlp",
    )(x.reshape(bsz * t, d), oa, ob.reshape(bsz * t, -1), oc, mod, w_out, w_ff1, w_ff2, *ln)
    return y.reshape(bsz, t, d)


def _rope_tables(n_tokens):
    pairs = HEAD_DIM // 4
    tok = np.arange(n_tokens)
    row = (tok // GRID_W).astype(np.float64)
    col = (tok % GRID_W).astype(np.float64)
    inv = ROPE_THETA ** (-np.arange(pairs, dtype=np.float64) / pairs)
    ang = np.concatenate([row[:, None] * inv, col[:, None] * inv], axis=-1)
    lane = np.arange(LANES)
    pair = (lane % HEAD_DIM) // 2
    sign = np.where(lane % 2 == 0, -1.0, 1.0)
    return (jnp.asarray(np.cos(ang)[:, pair], F32), jnp.asarray(np.sin(ang)[:, pair] * sign, F32))


def kernel(x_prompt, x_sample, cache_a_k, cache_a_v, cache_c_k, cache_c_v, state_b_fwd, state_b_bwd, c, c_ctx, w_ada, b_ada, w_in, w_out, lam_q1, lam_k1, lam_q2, lam_k2, subln_g, lb_logits_fwd, lb_logits_bwd, gnorm_g, qnorm_g, knorm_g, ln1_g, ln1_b, ln2_g, ln2_b, w_ff1, w_ff2):
    depth = w_in.shape[0]
    bsz, seq, d = x_prompt.shape
    dec_bsz, dec_seq, _ = x_sample.shape
    past = cache_a_k.shape[2]
    alpha = (2 * depth) ** 0.25
    mix_a, mix_b, mix_c = d // 2, d // 4, d // 4

    mod = _modulation(c_ctx, c, w_ada, b_ada)
    rope = _rope_tables(dec_seq)

    cache = (cache_a_k.transpose(0, 1, 3, 4, 5, 2).reshape(dec_bsz, depth, mix_a, past),
             cache_a_v.reshape(dec_bsz, depth, past * (mix_a // LANES), LANES),
             cache_c_k.transpose(0, 1, 3, 4, 2).reshape(dec_bsz, depth, mix_c // 2, past),
             cache_c_v.transpose(0, 1, 3, 4, 2).reshape(dec_bsz, depth, mix_c // 2, past))
    lam = (lam_q1, lam_k1, lam_q2, lam_k2)

    weights = (w_in, w_out, w_ff1.astype(BF16), w_ff2.astype(BF16))
    qn = jnp.tile(qnorm_g, (1, mix_c // HEAD_DIM))
    kn = jnp.tile(knorm_g, (1, mix_c // 2 // HEAD_DIM))
    gn = jnp.tile(gnorm_g, (1, mix_b // HEAD_DIM))
    ln = (ln1_g, ln1_b, ln2_g, ln2_b)

    def stream(x, li, latent, own_prev):
        w_in_b, w_out_b, w1_b, w2_b = weights
        n, t, _ = x.shape
        mod_row = ((lambda r0: li * MOD_ROWS + 1 + r0 // t) if latent
                   else (lambda r0: li * MOD_ROWS))
        kv_layer = 0 if latent else li
        (qa, ka, va, hq, ff, fb, hv, hg, qc, kc, vc, *extra) = _in_proj(
            x, mod, mod_row, w_in_b, lb_logits_fwd, lb_logits_bwd, qn, kn,
            rope if latent else None, None if own_prev is None else own_prev[0:7], li=li)
        tq = 256
        oa = _diff_attn(qa, ka, va, cache[0:2] if latent else None, lam, subln_g,
                        li=li, kv_layer=kv_layer, bsz=n, tq=tq, heads_per_step=2 if latent else 4)
        oc = _gqa(qc, kc, vc, cache[2:4] if latent else None, li=li, kv_layer=kv_layer, bsz=n, tq=tq)
        hres = _hgrn(hq, ff, fb, hv, hg, gn, (state_b_fwd, state_b_bwd) if latent else None,
                     None if own_prev is None else own_prev[7:9],
                     li=li, depth=depth, want_state=not latent, nb=2)
        y = _out_mlp(x, oa, hres[0], oc, mod, mod_row, w_out_b, w1_b, w2_b, ln, li=li, alpha=alpha)
        own = None if latent else (ka, va, kc, vc, *extra, hres[1], hres[2])
        return y, own

    y_prompt, y_sample = x_prompt, x_sample
    own = None
    for li in range(depth):
        y_prompt, own = stream(y_prompt, li, False, own)
        y_sample, _ = stream(y_sample, li, True, None)

    heads_a = mix_a // (2 * HEAD_DIM)
    new_a_k = own[0].reshape(bsz, depth, seq, heads_a, 2, HEAD_DIM)
    new_a_v = own[4].reshape(bsz, depth, seq, heads_a, 2 * HEAD_DIM)
    kv_heads = mix_c // 2 // HEAD_DIM
    new_c_k = own[5].reshape(bsz, depth, kv_heads, HEAD_DIM, seq).transpose(0, 1, 4, 2, 3)
    new_c_v = own[6].reshape(bsz, depth, kv_heads, HEAD_DIM, seq).transpose(0, 1, 4, 2, 3)
    return (y_prompt, y_sample, new_a_k, new_a_v, new_c_k, new_c_v, own[7], own[8])
```

```python
import functools
import math

import jax
import jax.numpy as jnp
import numpy as np
from jax import lax
from jax.experimental import pallas as pl
from jax.experimental.pallas import tpu as pltpu

GRID_W = 64
HEAD_DIM = 64
ROPE_THETA = 10000.0
LN_EPS = 1e-6
RMS_EPS = 1e-6
F_MIN = 1e-6
CHUNK = 64
DIAG_BLOCK = 8
LANES = 128
ROW_TILE = 256
VMEM_LIMIT = 56 * 1024 * 1024

F32 = jnp.float32
BF16 = jnp.bfloat16
NT = (((1,), (1,)), ((), ()))
TN = (((0,), (0,)), ((), ()))


def _params(n_grid):
    return pltpu.CompilerParams(dimension_semantics=("arbitrary",) * n_grid,
                                vmem_limit_bytes=VMEM_LIMIT)


def _dot(a, b):
    return jnp.dot(a, b, preferred_element_type=F32)


def _split_dot(a, b_bf16, passes, dims=None):
    acc = None
    rem = a
    for _ in range(passes):
        piece = rem.astype(BF16)
        rem = rem - piece.astype(F32)
        part = (_dot(piece, b_bf16) if dims is None
                else lax.dot_general(piece, b_bf16, dims, preferred_element_type=F32))
        acc = part if acc is None else acc + part
    return acc


def _group_ones(n, group):
    r = lax.broadcasted_iota(jnp.int32, (n, n), 0) // group
    c = lax.broadcasted_iota(jnp.int32, (n, n), 1) // group
    return (r == c).astype(BF16)


def _group_mean_square(x, group):
    n = x.shape[-1]
    return _split_dot(x * x, _group_ones(n, group), 2) * (1.0 / group)


def _group_rms(x, g_row, group):
    return x * lax.rsqrt(_group_mean_square(x, group) + RMS_EPS) * g_row


def _pair_swap(x):
    lane = lax.broadcasted_iota(jnp.int32, x.shape, 1)
    return jnp.where(lane % 2 == 0, pltpu.roll(x, LANES - 1, 1), pltpu.roll(x, 1, 1))


def _rope(x, cos, sin):
    blocks = []
    for j in range(x.shape[-1] // LANES):
        blk = x[:, j * LANES:(j + 1) * LANES]
        blocks.append(blk * cos + _pair_swap(blk) * sin)
    return blocks[0] if len(blocks) == 1 else jnp.concatenate(blocks, axis=-1)


def _silu(x):
    return x * jax.nn.sigmoid(x)


def _layernorm(x, g, b):
    mu = jnp.mean(x, axis=-1, keepdims=True)
    xc = x - mu
    var = jnp.mean(xc * xc, axis=-1, keepdims=True)
    return xc * lax.rsqrt(var + LN_EPS) * g + b


MOD_ROWS = 8


def _mod_kernel(cctx_ref, c_ref, w_ref, b_ref, o_ref, s_ref):
    n_req = c_ref.shape[0]
    s_ref[...] = jnp.zeros_like(s_ref)
    s_ref[0:1, :] = _silu(cctx_ref[...])
    s_ref[1:1 + n_req, :] = _silu(c_ref[...])
    layer = pl.program_id(0)
    res = _dot(s_ref[...].astype(BF16), w_ref[0].astype(BF16)) + b_ref[pl.ds(layer, 1), :]
    for r in range(MOD_ROWS):
        o_ref[r] = res[r:r + 1, :]


def _modulation(c_ctx, c, w_ada, b_ada):
    depth, d, n = w_ada.shape
    tn = 1536
    return pl.pallas_call(
        _mod_kernel,
        grid=(depth, n // tn),
        in_specs=[pl.BlockSpec((1, d), lambda l, j: (0, 0)),
                  pl.BlockSpec(c.shape, lambda l, j: (0, 0)),
                  pl.BlockSpec((1, d, tn), lambda l, j: (l, 0, j)),
                  pl.BlockSpec((depth, tn), lambda l, j: (0, j))],
        out_specs=pl.BlockSpec((MOD_ROWS, 1, tn), lambda l, j: (l, 0, j)),
        out_shape=jax.ShapeDtypeStruct((depth * MOD_ROWS, 1, n), F32),
        scratch_shapes=[pltpu.VMEM((MOD_ROWS, d), F32)],
        compiler_params=_params(2),
        name="adaln_modulation",
    )(c_ctx.reshape(1, d), c, w_ada, b_ada)


def _in_proj_kernel(*refs, li, d, latent, n_alias):
    refs = list(refs)
    x_ref, mod_ref, w_ref, lbf_ref, lbb_ref, qn_ref, kn_ref = refs[:7]
    pos = 7
    if latent:
        cos, sin = refs[pos][...], refs[pos + 1][...]
        pos += 2
    pos += n_alias
    qa_o, hq_o, ff_o, fb_o, hv_o, hg_o, qc_o = refs[pos:pos + 7]
    if latent:
        ka_o, va_o, kc_o, vc_o = refs[pos + 7:pos + 11]
    else:
        ka_o, va_rows_o, kct_o, vct_o = refs[pos + 7:pos + 11]

    def store_kv(ref, val):
        for slot in range(ref.shape[1]):
            ref[0, slot] = val.astype(ref.dtype)

    mix_a, mix_b, mix_c = d // 2, d // 4, d // 4
    kv_c = mix_c // 2
    scale = HEAD_DIM ** -0.5 * math.log2(math.e)

    shift = mod_ref[0, :, 0:d]
    gain = mod_ref[0, :, d:2 * d]
    h = x_ref[...] * (1.0 + gain) + shift

    def proj(start, width):
        return _dot(h, w_ref[:, start:start + width])

    off_b = 3 * mix_a
    off_c = off_b + 5 * mix_b

    zq = proj(off_c, mix_c)
    zk = proj(off_c + mix_c, kv_c)
    vc = proj(off_c + mix_c + kv_c, kv_c)
    qa = proj(0, mix_a)
    msq = _group_mean_square(zq, HEAD_DIM)
    msk = _group_mean_square(zk, HEAD_DIM)
    ka = proj(mix_a, mix_a)
    va = proj(2 * mix_a, mix_a)
    qc = zq * lax.rsqrt(msq + RMS_EPS) * qn_ref[li:li + 1, :]
    kc = zk * lax.rsqrt(msk + RMS_EPS) * kn_ref[li:li + 1, :]

    if latent:
        qa = _rope(qa, cos, sin)
        ka = _rope(ka, cos, sin)
    qa_o[...] = (qa * scale).astype(qa_o.dtype)
    store_kv(ka_o, ka)
    if latent:
        store_kv(va_o, va)
    else:
        heads = mix_a // LANES
        for slot in range(va_rows_o.shape[1]):
            for hd in range(heads):
                va_rows_o[0, slot, pl.ds(hd, ROW_TILE, stride=heads), :] = (
                    va[:, hd * LANES:(hd + 1) * LANES])

    def lower_bound(ref):
        logits = ref[...]
        e = jnp.exp(logits - jnp.max(logits, axis=0, keepdims=True))
        sm = e / jnp.sum(e, axis=0, keepdims=True)
        return jnp.sum(sm[0:li + 1], axis=0, keepdims=True) - sm[0:1]

    def forget(x, lb):
        return jnp.maximum(lb + (1.0 - lb) * jax.nn.sigmoid(x), F_MIN)

    off = off_b
    zb = [proj(off + j * mix_b, mix_b) for j in range(5)]

    if latent:
        kc = _rope(kc, cos, sin)
        qc = _rope(qc, cos, sin)
        store_kv(kc_o, kc)
        store_kv(vc_o, vc)
    else:
        store_kv(kct_o, kc.T)
        store_kv(vct_o, vc.T)
    qc = qc * scale
    lane = lax.broadcasted_iota(jnp.int32, (1, LANES), 1)
    for n in range(2):
        blk = qc[:, n * LANES:(n + 1) * LANES]
        in_half = (lane // HEAD_DIM) == n
        for g in range(2):
            src = blk if g == n else pltpu.roll(blk, HEAD_DIM, 1)
            hc = 2 * n + g
            qc_o[:, hc * LANES:(hc + 1) * LANES] = jnp.where(in_half, src, 0.0).astype(qc_o.dtype)

    hq_o[0] = _silu(zb[0])
    ff_o[0] = forget(zb[1], lower_bound(lbf_ref))
    fb_o[0] = forget(zb[2], lower_bound(lbb_ref))
    hv_o[0] = zb[3]
    hg_o[0] = _silu(zb[4])


def _in_proj(x, mod, mod_row, w_in, lb_f, lb_b, qn, kn, rope, kv_prev, *, li):
    bsz, t, d = x.shape
    latent = rope is not None
    depth, _, n_in = w_in.shape
    tiles = t // ROW_TILE
    mix_a, mix_b, mix_c = d // 2, d // 4, d // 4
    kv_c = mix_c // 2
    x2 = x.reshape(bsz * t, d)

    row = lambda i: (i, 0)
    brow = lambda i: (i // tiles, i % tiles, 0)
    const = lambda i: (0, 0)
    in_specs = [pl.BlockSpec((ROW_TILE, d), row),
                pl.BlockSpec((1, 1, mod.shape[-1]), lambda i: (mod_row(i * ROW_TILE), 0, 0)),
                pl.BlockSpec((None, d, n_in), lambda i: (li, 0, 0)),
                pl.BlockSpec(lb_f.shape, const), pl.BlockSpec(lb_b.shape, const),
                pl.BlockSpec(qn.shape, const), pl.BlockSpec(kn.shape, const)]
    args = [x2, mod, w_in, lb_f, lb_b, qn, kn]
    if latent:
        in_specs += [pl.BlockSpec((ROW_TILE, LANES), lambda i: (i % tiles, 0))] * 2
        args += list(rope)
        slots, slot0 = 1, 0
    else:
        slots, slot0 = (depth, 0) if kv_prev is None else (1, li)
    aliases = {}
    if kv_prev is not None:
        for j, buf in enumerate(kv_prev):
            aliases[len(args)] = 7 + j
            in_specs.append(pl.BlockSpec(memory_space=pl.ANY))
            args.append(buf)
    krow = lambda i: (i // tiles, slot0, i % tiles, 0)
    kcol = lambda i: (i // tiles, slot0, 0, i % tiles)
    heads = mix_a // LANES

    out_specs = [pl.BlockSpec((ROW_TILE, mix_a), row)]
    out_shape = [jax.ShapeDtypeStruct((bsz * t, mix_a), BF16)]
    out_specs += [pl.BlockSpec((1, ROW_TILE, mix_b), brow)] * 5
    out_shape += [jax.ShapeDtypeStruct((bsz, t, mix_b), F32)] * 5
    out_specs.append(pl.BlockSpec((ROW_TILE, 2 * mix_c), row))
    out_shape.append(jax.ShapeDtypeStruct((bsz * t, 2 * mix_c), BF16))
    if latent:
        for width in (mix_a, mix_a, kv_c, kv_c):
            out_specs.append(pl.BlockSpec((1, 1, ROW_TILE, width), krow))
            out_shape.append(jax.ShapeDtypeStruct((bsz, 1, t, width), BF16))
    else:
        out_specs += [pl.BlockSpec((1, slots, ROW_TILE, mix_a), krow),
                      pl.BlockSpec((1, slots, ROW_TILE * heads, LANES), krow),
                      pl.BlockSpec((1, slots, kv_c, ROW_TILE), kcol),
                      pl.BlockSpec((1, slots, kv_c, ROW_TILE), kcol)]
        out_shape += [jax.ShapeDtypeStruct((bsz, depth, t, mix_a), F32),
                      jax.ShapeDtypeStruct((bsz, depth, t * heads, LANES), F32),
                      jax.ShapeDtypeStruct((bsz, depth, kv_c, t), F32),
                      jax.ShapeDtypeStruct((bsz, depth, kv_c, t), F32)]
    return pl.pallas_call(
        functools.partial(_in_proj_kernel, li=li, d=d, latent=latent, n_alias=len(aliases)),
        grid=(bsz * tiles,),
        in_specs=in_specs, out_specs=out_specs, out_shape=out_shape,
        input_output_aliases=aliases,
        compiler_params=_params(1),
        name="in_proj_latent" if latent else "in_proj_context",
    )(*args)


def _softmax_parts(scores):
    m = functools.reduce(jnp.maximum, [jnp.max(s, axis=-1, keepdims=True) for s in scores])
    es = [jnp.exp2(s - m) for s in scores]
    denom = functools.reduce(lambda a, b: a + b, [jnp.sum(e, axis=-1, keepdims=True) for e in es])
    return es, 1.0 / denom


def _diff_attn_kernel(*refs, li, cached, heads, v_rows):
    if cached:
        (q_ref, k_ref, v_ref, ck_ref, cv_ref, lq1, lk1, lq2, lk2, sub_ref, o_ref) = refs
    else:
        (q_ref, k_ref, v_ref, lq1, lk1, lq2, lk2, sub_ref, o_ref) = refs
    lam_init = 0.8 - 0.6 * math.exp(-0.3 * li)

    def lam_term(a, b):
        return jnp.exp(jnp.sum(a[li:li + 1, :] * b[li:li + 1, :], axis=-1, keepdims=True))

    lam = lam_term(lq1, lk1) - lam_term(lq2, lk2) + lam_init
    tq = q_ref.shape[0]
    lane = lax.broadcasted_iota(jnp.int32, (1, LANES), 1)
    cols = [slice(h * LANES, (h + 1) * LANES) for h in range(heads)]

    scores, vals = [], []
    for c in cols:
        q = q_ref[:, c]
        zero = jnp.zeros_like(q)
        q2 = jnp.concatenate([jnp.where(lane < HEAD_DIM, q, zero),
                              jnp.where(lane >= HEAD_DIM, q, zero)], axis=0)
        head_scores = [lax.dot_general(q2, k_ref[0, 0, :, c].astype(BF16), NT,
                                       preferred_element_type=F32)]
        head = pl.program_id(1) * heads + c.start // LANES
        if v_rows:
            t = k_ref.shape[2]
            own_v = v_ref[0, 0, pl.ds(head, t, stride=v_ref.shape[2] // t), :]
        else:
            own_v = v_ref[0, 0, :, c]
        vals.append([own_v.astype(BF16)])
        if cached:
            past = ck_ref.shape[-1]
            all_heads = cv_ref.shape[2] // past
            head_scores.append(_dot(q2, ck_ref[0, 0, c, :].astype(BF16)))
            vals[-1].append(cv_ref[0, 0, pl.ds(head, past, stride=all_heads), :].astype(BF16))
        scores.append(head_scores)
    parts = [_softmax_parts(s) for s in scores]
    outs = []
    for (es, r), vs in zip(parts, vals):
        r0 = r[0:tq]
        r1 = r[tq:2 * tq] * lam
        o = None
        for e, v in zip(es, vs):
            part = _dot((e[0:tq] * r0 - e[tq:2 * tq] * r1).astype(BF16), v)
            o = part if o is None else o + part
        outs.append(o)
    gain = sub_ref[li:li + 1, :] * (1.0 - lam_init)
    for c, o in zip(cols, outs):
        ms = jnp.mean(o * o, axis=-1, keepdims=True)
        o_ref[:, c] = (o * lax.rsqrt(ms + RMS_EPS) * gain).astype(o_ref.dtype)


def _diff_attn(q, k, v, cache, lam, subln, *, li, kv_layer, bsz, tq, heads_per_step):
    t = k.shape[2]
    width = q.shape[-1]
    wstep = heads_per_step * LANES
    nq = t // tq
    cached = cache is not None
    v_rows = v.shape[2] != t
    v_spec = (pl.BlockSpec((1, 1) + v.shape[2:], lambda b, h, i: (b, kv_layer, 0, 0)) if v_rows
              else pl.BlockSpec((1, 1, t, wstep), lambda b, h, i: (b, kv_layer, 0, h)))
    in_specs = [pl.BlockSpec((tq, wstep), lambda b, h, i: (b * nq + i, h)),
                pl.BlockSpec((1, 1, t, wstep), lambda b, h, i: (b, kv_layer, 0, h)),
                v_spec]
    args = [q, k, v]
    if cached:
        ck, cv = cache
        in_specs += [pl.BlockSpec((1, 1, wstep, ck.shape[-1]), lambda b, h, i: (b, li, h, 0)),
                     pl.BlockSpec((1, 1) + cv.shape[2:], lambda b, h, i: (b, li, 0, 0))]
        args += [ck, cv]
    in_specs += [pl.BlockSpec(a.shape, lambda b, h, i: (0, 0)) for a in (*lam, subln)]
    args += [*lam, subln]
    return pl.pallas_call(
        functools.partial(_diff_attn_kernel, li=li, cached=cached, heads=heads_per_step,
                          v_rows=v_rows),
        grid=(bsz, width // wstep, nq),
        in_specs=in_specs,
        out_specs=pl.BlockSpec((tq, wstep), lambda b, h, i: (b * nq + i, h)),
        out_shape=jax.ShapeDtypeStruct((bsz * t, width), BF16),
        compiler_params=_params(3),
        name="diff_attn_latent" if cached else "diff_attn_context",
    )(*args)


def _gqa_kernel(*refs, cached, own_t):
    if cached:
        q_ref, k_ref, v_ref, ck_ref, cv_ref, o_ref = refs
    else:
        q_ref, k_ref, v_ref, o_ref = refs
    tq = q_ref.shape[0]
    heads = q_ref.shape[1] // LANES
    lane = lax.broadcasted_iota(jnp.int32, (1, LANES), 1)
    keys = [(k_ref[0, 0].astype(BF16), own_t)]
    vals = [(v_ref[0, 0].astype(BF16), own_t)]
    if cached:
        keys.append((ck_ref[0, 0].astype(BF16), True))
        vals.append((cv_ref[0, 0].astype(BF16), True))
    groups = range(heads // 2)
    scores = []
    for n in groups:
        q = jnp.concatenate([q_ref[:, (2 * n + g) * LANES:(2 * n + g + 1) * LANES]
                             for g in range(2)], axis=0)
        scores.append([_dot(q, k) if k_t else lax.dot_general(q, k, NT, preferred_element_type=F32)
                       for k, k_t in keys])
    parts = [_softmax_parts(s) for s in scores]
    outs = []
    for es, r in parts:
        o = None
        for e, (v, v_t) in zip(es, vals):
            p = e.astype(BF16)
            part = lax.dot_general(p, v, NT, preferred_element_type=F32) if v_t else _dot(p, v)
            o = part if o is None else o + part
        outs.append(o * r)
    for n in groups:
        first = outs[n][0:tq]
        second = outs[n][tq:2 * tq]
        if n == 0:
            second = pltpu.roll(second, HEAD_DIM, 1)
        else:
            first = pltpu.roll(first, HEAD_DIM, 1)
        o_ref[:, n * LANES:(n + 1) * LANES] = jnp.where(lane < HEAD_DIM, first,
                                                        second).astype(o_ref.dtype)


def _gqa(q, k, v, cache, *, li, kv_layer, bsz, tq, own_t):
    t, kvw = (k.shape[3], k.shape[2]) if own_t else (k.shape[2], k.shape[3])
    nq = t // tq
    cached = cache is not None
    in_specs = [pl.BlockSpec((tq, q.shape[-1]), lambda b, i: (b * nq + i, 0)),
                pl.BlockSpec((1, 1) + k.shape[2:], lambda b, i: (b, kv_layer, 0, 0)),
                pl.BlockSpec((1, 1) + k.shape[2:], lambda b, i: (b, kv_layer, 0, 0))]
    args = [q, k, v]
    if cached:
        in_specs += [pl.BlockSpec((1, 1) + cache[0].shape[2:], lambda b, i: (b, li, 0, 0))] * 2
        args += list(cache)
    return pl.pallas_call(
        functools.partial(_gqa_kernel, cached=cached, own_t=own_t),
        grid=(bsz, nq),
        in_specs=in_specs,
        out_specs=pl.BlockSpec((tq, 2 * kvw), lambda b, i: (b * nq + i, 0)),
        out_shape=jax.ShapeDtypeStruct((bsz * t, 2 * kvw), BF16),
        compiler_params=_params(2),
        name="gqa_latent" if cached else "gqa_context",
    )(*args)


def _head_masks(width):
    lane_head = lax.broadcasted_iota(jnp.int32, (1, width), 1) // HEAD_DIM
    return [lane_head == h for h in range(width // HEAD_DIM)]


def _stack_heads(x, masks):
    return jnp.concatenate([jnp.where(m, x, jnp.zeros_like(x)) for m in masks], axis=0)


def _block_diag_mask(width):
    r = lax.broadcasted_iota(jnp.int32, (width, width), 0) // HEAD_DIM
    c = lax.broadcasted_iota(jnp.int32, (width, width), 1) // HEAD_DIM
    return r == c


def _ref_rows(b, offsets, span):
    width = b.shape[-1]
    return jnp.concatenate([jnp.broadcast_to(b[o:o + 1], (span, width)) for o in offsets], axis=0)


def _hgrn_chunks(problems):
    n = len(problems)
    c, width = problems[0][0].shape
    qs = [p[0] for p in problems]
    vs = [p[2] for p in problems]
    sts = [p[3] for p in problems]
    rev = [p[4] for p in problems]
    chains = range(n)
    masks = _head_masks(width)
    trow =lax.broadcasted_iota(jnp.int32, (c, 1), 0)
    t_full = lax.broadcasted_iota(jnp.int32, (c, width), 0)
    s_full = lax.broadcasted_iota(jnp.int32, (c, width), 1) % c

    ks = [1.0 - p[1] for p in problems]
    b = [jnp.log(p[1]) for p in problems]
    step = 1
    while step < c:
        for j in chains:
            if rev[j]:
                b[j] = b[j] + jnp.where(trow < c - step, pltpu.roll(b[j], c - step, 0), 0.0)
            else:
                b[j] = b[j] + jnp.where(trow >= step, pltpu.roll(b[j], step, 0), 0.0)
        step *= 2
    b_end = [b[j][0:1] if rev[j] else b[j][c - 1:c] for j in chains]

    o = [lax.dot_general((qs[j] * jnp.exp(b[j])).astype(BF16), sts[j].astype(BF16), NT,
                         preferred_element_type=F32) for j in chains]

    a = [None] * n
    m = c // 2
    while m >= DIAG_BLOCK:
        blocks = c // (2 * m)
        same = (t_full // (2 * m)) == (s_full // (2 * m))
        for j in chains:
            ref = _ref_rows(b[j], [i * 2 * m + (m if rev[j] else m - 1) for i in range(blocks)],
                            2 * m)
            is_q = ((trow % (2 * m)) < m) if rev[j] else ((trow % (2 * m)) >= m)
            e = jnp.exp(jnp.where(is_q, b[j] - ref, ref - b[j]))
            ql = jnp.where(is_q, qs[j] * e, 0.0).astype(BF16)
            kl = jnp.where(is_q, 0.0, ks[j] * e).astype(BF16)
            al = lax.dot_general(ql, _stack_heads(kl, masks), NT, preferred_element_type=F32)
            if blocks > 1:
                al = jnp.where(same, al, 0.0)
            a[j] = al if a[j] is None else a[j] + al
        m //= 2
    blocks = c // DIAG_BLOCK
    mid = DIAG_BLOCK // 2
    same = (t_full // DIAG_BLOCK) == (s_full // DIAG_BLOCK)
    for j in chains:
        ref = _ref_rows(b[j], [i * DIAG_BLOCK + (mid if rev[j] else mid - 1) for i in range(blocks)],
                        DIAG_BLOCK)
        d = b[j] - ref
        ql = (qs[j] * jnp.exp(d)).astype(BF16)
        kl = (ks[j] * jnp.exp(-d)).astype(BF16)
        al = lax.dot_general(ql, _stack_heads(kl, masks), NT, preferred_element_type=F32)
        causal = (s_full >= t_full) if rev[j] else (s_full <= t_full)
        a[j] = a[j] + jnp.where(same & causal, al, 0.0)

    v_b = [v.astype(BF16) for v in vs]
    o = [o[j] + _dot(a[j].astype(BF16), _stack_heads(v_b[j], masks)) for j in chains]

    bd = _block_diag_mask(width)
    upd = [lax.dot_general(v_b[j], (ks[j] * jnp.exp(b_end[j] - b[j])).astype(BF16), TN,
                           preferred_element_type=F32) for j in chains]
    st_new = [sts[j] * jnp.exp(b_end[j]) + jnp.where(bd, upd[j], 0.0) for j in chains]
    return list(zip(o, st_new))


def _mxu_transpose(x):
    n = x.shape[1]
    r = lax.broadcasted_iota(jnp.int32, (n, n), 0)
    c = lax.broadcasted_iota(jnp.int32, (n, n), 1)
    eye = (r == c).astype(BF16)
    acc = None
    rem = x
    for _ in range(3):
        piece = rem.astype(BF16)
        rem = rem - piece.astype(F32)
        part = lax.dot_general(eye, piece, NT, preferred_element_type=F32)
        acc = part if acc is None else acc + part
    return acc


def _hgrn_kernel(*refs, li, has_state, want_state, n_alias, heads):
    refs = list(refs)
    q_ref, ff_ref, fb_ref, v_ref, gate_ref, gn_ref = refs[:6]
    pos = 6
    if has_state:
        s0f_ref, s0b_ref = refs[pos:pos + 2]
        pos += 2
    pos += n_alias
    o_ref = refs[pos]
    pos += 1
    if want_state:
        sf_ref, sb_ref = refs[pos:pos + 2]
        pos += 2
    st_ref, of_ref, ob_ref = refs[pos:pos + 3]
    nb, t, width = q_ref.shape
    nc = t // CHUNK
    bd = _block_diag_mask(width)

    for n in range(nb):
        for d in range(2):
            if has_state:
                x = (s0b_ref if d else s0f_ref)[n, 0].reshape(width, HEAD_DIM)
                xt = _mxu_transpose(x)
                st_ref[2 * n + d] = jnp.where(bd, jnp.concatenate([xt] * heads, axis=0), 0.0)
            else:
                st_ref[2 * n + d] = jnp.zeros((width, width), F32)

    def body(ci, carry):
        rows = (pl.ds(pl.multiple_of(ci * CHUNK, CHUNK), CHUNK),
                pl.ds(pl.multiple_of((nc - 1 - ci) * CHUNK, CHUNK), CHUNK))
        loaded = []
        for n in range(nb):
            for d, f_ref in enumerate((ff_ref, fb_ref)):
                r = rows[d]
                loaded.append((q_ref[n, r, :], f_ref[n, r, :], v_ref[n, r, :], st_ref[2 * n + d],
                               bool(d)))
        for j, (o, st) in enumerate(_hgrn_chunks(loaded)):
            n, d = divmod(j, 2)
            (ob_ref if d else of_ref)[n, rows[d], :] = o
            st_ref[j] = st
        return carry

    lax.fori_loop(0, nc, body, 0)

    for n in range(nb):
        o = of_ref[n] + ob_ref[n]
        o_ref[n] = (_group_rms(o, gn_ref[li:li + 1, :], HEAD_DIM) * gate_ref[n]).astype(o_ref.dtype)

    if want_state:
        for n in range(nb):
            for d, dst in enumerate((sf_ref, sb_ref)):
                st = st_ref[2 * n + d]
                rows = st[0:HEAD_DIM]
                for h in range(1, heads):
                    rows = rows + st[h * HEAD_DIM:(h + 1) * HEAD_DIM]
                final = _mxu_transpose(rows).reshape(heads, HEAD_DIM, HEAD_DIM)
                for slot in range(dst.shape[1]):
                    dst[n, slot] = final


def _hgrn(hq, ff, fb, hv, hg, gn, state, state_prev, *, li, depth, want_state, nb):
    bsz, t, width = hq.shape
    heads = width // HEAD_DIM
    has_state = state is not None
    seq = pl.BlockSpec((nb, t, width), lambda b: (b, 0, 0))
    in_specs = [seq] * 5 + [pl.BlockSpec(gn.shape, lambda b: (0, 0))]
    args = [hq, ff, fb, hv, hg, gn]
    if has_state:
        in_specs += [pl.BlockSpec((nb, 1, heads, HEAD_DIM, HEAD_DIM), lambda b: (b, li, 0, 0, 0))] * 2
        args += list(state)
    aliases = {}
    if state_prev is not None:
        for j, buf in enumerate(state_prev):
            aliases[len(args)] = 1 + j
            in_specs.append(pl.BlockSpec(memory_space=pl.ANY))
            args.append(buf)
    out_specs = [seq]
    out_shape = [jax.ShapeDtypeStruct((bsz, t, width), BF16)]
    if want_state:
        slots, slot0 = (depth, 0) if state_prev is None else (1, li)
        out_specs += [pl.BlockSpec((nb, slots, heads, HEAD_DIM, HEAD_DIM),
                                   lambda b: (b, slot0, 0, 0, 0))] * 2
        out_shape += [jax.ShapeDtypeStruct((bsz, depth, heads, HEAD_DIM, HEAD_DIM), F32)] * 2
    return pl.pallas_call(
        functools.partial(_hgrn_kernel, li=li, has_state=has_state, want_state=want_state,
                          n_alias=len(aliases), heads=heads),
        grid=(bsz // nb,),
        in_specs=in_specs, out_specs=out_specs, out_shape=out_shape,
        input_output_aliases=aliases,
        scratch_shapes=[pltpu.VMEM((2 * nb, width, width), F32),
                        pltpu.VMEM((nb, t, width), F32), pltpu.VMEM((nb, t, width), F32)],
        compiler_params=_params(1),
        name="hgrn2_latent" if has_state else "hgrn2_context",
    )(*args)


def _out_mlp_kernel(x_ref, oa_ref, ob_ref, oc_ref, mod_ref, wo_ref, w1_ref, w2_ref,
                    g1_ref, b1_ref, g2_ref, b2_ref, y_ref, *, li, d, alpha, ff_chunk):
    wa, wb = oa_ref.shape[-1], ob_ref.shape[-1]
    layer = slice(li, li + 1)
    gate1 = mod_ref[0, :, 2 * d:3 * d]
    shift2 = mod_ref[0, :, 3 * d:4 * d]
    gain2 = mod_ref[0, :, 4 * d:5 * d]
    gate2 = mod_ref[0, :, 5 * d:6 * d]
    subs = [slice(s * ROW_TILE, (s + 1) * ROW_TILE) for s in range(x_ref.shape[0] // ROW_TILE)]
    wo = wo_ref[...].astype(BF16)
    m = [_dot(oa_ref[r, :], wo[0:wa, :]) + _dot(ob_ref[r, :], wo[wa:wa + wb, :])
         + _dot(oc_ref[r, :], wo[wa + wb:, :]) for r in subs]
    x1 = [_layernorm(alpha * x_ref[r, :] + gate1 * mi, g1_ref[layer, :], b1_ref[layer, :])
          for r, mi in zip(subs, m)]
    h2 = [(xi * (1.0 + gain2) + shift2).astype(BF16) for xi in x1]
    acc = [None] * len(subs)
    for j in range(w1_ref.shape[-1] // ff_chunk):
        cols = slice(j * ff_chunk, (j + 1) * ff_chunk)
        hid = [jnp.maximum(_dot(hi, w1_ref[:, cols]), 0.0) for hi in h2]
        for s, hd in enumerate(hid):
            part = _dot((hd * hd).astype(BF16), w2_ref[cols, :])
            acc[s] = part if acc[s] is None else acc[s] + part
    for r, xi, ai in zip(subs, x1, acc):
        y_ref[r, :] = _layernorm(alpha * xi + gate2 * ai, g2_ref[layer, :], b2_ref[layer, :])


def _out_mlp(x, oa, ob, oc, mod, mod_row, w_out, w_ff1, w_ff2, ln, *, li, alpha):
    bsz, t, d = x.shape
    rows = 2 * ROW_TILE
    row = lambda i: (i, 0)
    const = lambda i: (0, 0)
    resident = lambda a: pl.BlockSpec((None,) + a.shape[1:], lambda i: (li, 0, 0),
                                      pipeline_mode=pl.Buffered(1))
    in_specs = [pl.BlockSpec((rows, d), row),
                pl.BlockSpec((rows, oa.shape[-1]), row),
                pl.BlockSpec((rows, ob.shape[-1]), row),
                pl.BlockSpec((rows, oc.shape[-1]), row),
                pl.BlockSpec((1, 1, mod.shape[-1]), lambda i: (mod_row(i * rows), 0, 0)),
                resident(w_out), resident(w_ff1), resident(w_ff2)]
    in_specs += [pl.BlockSpec(a.shape, const) for a in ln]
    y = pl.pallas_call(
        functools.partial(_out_mlp_kernel, li=li, d=d, alpha=alpha, ff_chunk=1024),
        grid=(bsz * t // rows,),
        in_specs=in_specs,
        out_specs=pl.BlockSpec((rows, d), row),
        out_shape=jax.ShapeDtypeStruct((bsz * t, d), F32),
        compiler_params=_params(1),
        name="out_mlp",
    )(x.reshape(bsz * t, d), oa, ob.reshape(bsz * t, -1), oc, mod, w_out, w_ff1, w_ff2, *ln)
    return y.reshape(bsz, t, d)


def _rope_tables(n_tokens):
    pairs = HEAD_DIM // 4
    tok = np.arange(n_tokens)
    row = (tok // GRID_W).astype(np.float64)
    col = (tok % GRID_W).astype(np.float64)
    inv = ROPE_THETA ** (-np.arange(pairs, dtype=np.float64) / pairs)
    ang = np.concatenate([row[:, None] * inv, col[:, None] * inv], axis=-1)
    lane = np.arange(LANES)
    pair = (lane % HEAD_DIM) // 2
    sign = np.where(lane % 2 == 0, -1.0, 1.0)
    return (jnp.asarray(np.cos(ang)[:, pair], F32), jnp.asarray(np.sin(ang)[:, pair] * sign, F32))


def kernel(x_prompt, x_sample, cache_a_k, cache_a_v, cache_c_k, cache_c_v, state_b_fwd, state_b_bwd, c, c_ctx, w_ada, b_ada, w_in, w_out, lam_q1, lam_k1, lam_q2, lam_k2, subln_g, lb_logits_fwd, lb_logits_bwd, gnorm_g, qnorm_g, knorm_g, ln1_g, ln1_b, ln2_g, ln2_b, w_ff1, w_ff2):
    depth = w_in.shape[0]
    bsz, seq, d = x_prompt.shape
    dec_bsz, dec_seq, _ = x_sample.shape
    past = cache_a_k.shape[2]
    alpha = (2 * depth) ** 0.25
    mix_a, mix_b, mix_c = d // 2, d // 4, d // 4

    mod = _modulation(c_ctx, c, w_ada, b_ada)
    rope = _rope_tables(dec_seq)

    cache = (cache_a_k.transpose(0, 1, 3, 4, 5, 2).reshape(dec_bsz, depth, mix_a, past),
             cache_a_v.reshape(dec_bsz, depth, past * (mix_a // LANES), LANES),
             cache_c_k.transpose(0, 1, 3, 4, 2).reshape(dec_bsz, depth, mix_c // 2, past),
             cache_c_v.transpose(0, 1, 3, 4, 2).reshape(dec_bsz, depth, mix_c // 2, past))
    lam = (lam_q1, lam_k1, lam_q2, lam_k2)

    weights = (w_in, w_out, w_ff1.astype(BF16), w_ff2.astype(BF16))
    qn = jnp.tile(qnorm_g, (1, mix_c // HEAD_DIM))
    kn = jnp.tile(knorm_g, (1, mix_c // 2 // HEAD_DIM))
    gn = jnp.tile(gnorm_g, (1, mix_b // HEAD_DIM))
    ln = (ln1_g, ln1_b, ln2_g, ln2_b)

    def stream(x, li, latent, own_prev):
        w_in_b, w_out_b, w1_b, w2_b = weights
        n, t, _ = x.shape
        mod_row = ((lambda r0: li * MOD_ROWS + 1 + r0 // t) if latent
                   else (lambda r0: li * MOD_ROWS))
        kv_layer = 0 if latent else li
        (qa, hq, ff, fb, hv, hg, qc, ka, va, kc, vc) = _in_proj(
            x, mod, mod_row, w_in_b, lb_logits_fwd, lb_logits_bwd, qn, kn,
            rope if latent else None, None if own_prev is None else own_prev[0:4], li=li)
        tq = 256
        oa = _diff_attn(qa, ka, va, cache[0:2] if latent else None, lam, subln_g,
                        li=li, kv_layer=kv_layer, bsz=n, tq=tq, heads_per_step=2 if latent else 4)
        oc = _gqa(qc, kc, vc, cache[2:4] if latent else None, li=li, kv_layer=kv_layer, bsz=n, tq=tq,
                  own_t=not latent)
        hres = _hgrn(hq, ff, fb, hv, hg, gn, (state_b_fwd, state_b_bwd) if latent else None,
                     None if own_prev is None else own_prev[4:6],
                     li=li, depth=depth, want_state=not latent, nb=2)
        y = _out_mlp(x, oa, hres[0], oc, mod, mod_row, w_out_b, w1_b, w2_b, ln, li=li, alpha=alpha)
        own = None if latent else (ka, va, kc, vc, hres[1], hres[2])
        return y, own

    y_prompt, y_sample = x_prompt, x_sample
    own = None
    for li in range(depth):
        y_prompt, own = stream(y_prompt, li, False, own)
        y_sample, _ = stream(y_sample, li, True, None)

    heads_a = mix_a // (2 * HEAD_DIM)
    new_a_k = own[0].reshape(bsz, depth, seq, heads_a, 2, HEAD_DIM)
    new_a_v = own[1].reshape(bsz, depth, seq, heads_a, 2 * HEAD_DIM)
    kv_heads = mix_c // 2 // HEAD_DIM
    new_c_k = own[2].reshape(bsz, depth, kv_heads, HEAD_DIM, seq).transpose(0, 1, 4, 2, 3)
    new_c_v = own[3].reshape(bsz, depth, kv_heads, HEAD_DIM, seq).transpose(0, 1, 4, 2, 3)
    return (y_prompt, y_sample, new_a_k, new_a_v, new_c_k, new_c_v, own[4], own[5])
```

```python
import functools
import math

import jax
import jax.numpy as jnp
import numpy as np
from jax import lax
from jax.experimental import pallas as pl
from jax.experimental.pallas import tpu as pltpu

GRID_W = 64
HEAD_DIM = 64
ROPE_THETA = 10000.0
LN_EPS = 1e-6
RMS_EPS = 1e-6
F_MIN = 1e-6
CHUNK = 64
DIAG_BLOCK = 8
LANES = 128
ROW_TILE = 256
VMEM_LIMIT = 56 * 1024 * 1024

F32 = jnp.float32
BF16 = jnp.bfloat16
NT = (((1,), (1,)), ((), ()))
TN = (((0,), (0,)), ((), ()))


def _params(n_grid):
    return pltpu.CompilerParams(dimension_semantics=("arbitrary",) * n_grid,
                                vmem_limit_bytes=VMEM_LIMIT)


def _dot(a, b):
    return jnp.dot(a, b, preferred_element_type=F32)


def _split_dot(a, b_bf16, passes, dims=None):
    acc = None
    rem = a
    for _ in range(passes):
        piece = rem.astype(BF16)
        rem = rem - piece.astype(F32)
        part = (_dot(piece, b_bf16) if dims is None
                else lax.dot_general(piece, b_bf16, dims, preferred_element_type=F32))
        acc = part if acc is None else acc + part
    return acc


def _group_ones(n, group):
    r = lax.broadcasted_iota(jnp.int32, (n, n), 0) // group
    c = lax.broadcasted_iota(jnp.int32, (n, n), 1) // group
    return (r == c).astype(BF16)


def _group_mean_square(x, group):
    n = x.shape[-1]
    return _split_dot(x * x, _group_ones(n, group), 2) * (1.0 / group)


def _group_rms(x, g_row, group):
    return x * lax.rsqrt(_group_mean_square(x, group) + RMS_EPS) * g_row


def _pair_swap(x):
    lane = lax.broadcasted_iota(jnp.int32, x.shape, 1)
    return jnp.where(lane % 2 == 0, pltpu.roll(x, LANES - 1, 1), pltpu.roll(x, 1, 1))


def _rope(x, cos, sin):
    blocks = []
    for j in range(x.shape[-1] // LANES):
        blk = x[:, j * LANES:(j + 1) * LANES]
        blocks.append(blk * cos + _pair_swap(blk) * sin)
    return blocks[0] if len(blocks) == 1 else jnp.concatenate(blocks, axis=-1)


def _silu(x):
    return x * jax.nn.sigmoid(x)


def _layernorm(x, g, b):
    mu = jnp.mean(x, axis=-1, keepdims=True)
    xc = x - mu
    var = jnp.mean(xc * xc, axis=-1, keepdims=True)
    return xc * lax.rsqrt(var + LN_EPS) * g + b


MOD_ROWS = 8


def _mod_kernel(cctx_ref, c_ref, w_ref, b_ref, o_ref, s_ref):
    n_req = c_ref.shape[0]
    s_ref[...] = jnp.zeros_like(s_ref)
    s_ref[0:1, :] = _silu(cctx_ref[...])
    s_ref[1:1 + n_req, :] = _silu(c_ref[...])
    layer = pl.program_id(0)
    res = _dot(s_ref[...].astype(BF16), w_ref[0].astype(BF16)) + b_ref[pl.ds(layer, 1), :]
    for r in range(MOD_ROWS):
        o_ref[r] = res[r:r + 1, :]


def _modulation(c_ctx, c, w_ada, b_ada):
    depth, d, n = w_ada.shape
    tn = 1536
    return pl.pallas_call(
        _mod_kernel,
        grid=(depth, n // tn),
        in_specs=[pl.BlockSpec((1, d), lambda l, j: (0, 0)),
                  pl.BlockSpec(c.shape, lambda l, j: (0, 0)),
                  pl.BlockSpec((1, d, tn), lambda l, j: (l, 0, j)),
                  pl.BlockSpec((depth, tn), lambda l, j: (0, j))],
        out_specs=pl.BlockSpec((MOD_ROWS, 1, tn), lambda l, j: (l, 0, j)),
        out_shape=jax.ShapeDtypeStruct((depth * MOD_ROWS, 1, n), F32),
        scratch_shapes=[pltpu.VMEM((MOD_ROWS, d), F32)],
        compiler_params=_params(2),
        name="adaln_modulation",
    )(c_ctx.reshape(1, d), c, w_ada, b_ada)


def _in_proj_kernel(*refs, li, d, latent, n_alias):
    refs = list(refs)
    x_ref, mod_ref, w_ref, lbf_ref, lbb_ref, qn_ref, kn_ref = refs[:7]
    pos = 7
    if latent:
        cos, sin = refs[pos][...], refs[pos + 1][...]
        pos += 2
    pos += n_alias
    qa_o, hq_o, ff_o, fb_o, hv_o, hg_o, qc_o = refs[pos:pos + 7]
    if latent:
        ka_o, va_o, kc_o, vc_o = refs[pos + 7:pos + 11]
    else:
        ka_o, va_rows_o, kct_o, vct_o = refs[pos + 7:pos + 11]

    def store_kv(ref, val):
        for slot in range(ref.shape[1]):
            ref[0, slot] = val.astype(ref.dtype)

    mix_a, mix_b, mix_c = d // 2, d // 4, d // 4
    kv_c = mix_c // 2
    scale = HEAD_DIM ** -0.5 * math.log2(math.e)

    shift = mod_ref[0, :, 0:d]
    gain = mod_ref[0, :, d:2 * d]
    h = x_ref[...] * (1.0 + gain) + shift

    def proj(start, width):
        return _dot(h, w_ref[:, start:start + width])

    off_b = 3 * mix_a
    off_c = off_b + 5 * mix_b

    zq = proj(off_c, mix_c)
    zk = proj(off_c + mix_c, kv_c)
    vc = proj(off_c + mix_c + kv_c, kv_c)
    qa = proj(0, mix_a)
    msq = _group_mean_square(zq, HEAD_DIM)
    msk = _group_mean_square(zk, HEAD_DIM)
    ka = proj(mix_a, mix_a)
    va = proj(2 * mix_a, mix_a)
    qc = zq * lax.rsqrt(msq + RMS_EPS) * qn_ref[li:li + 1, :]
    kc = zk * lax.rsqrt(msk + RMS_EPS) * kn_ref[li:li + 1, :]

    if latent:
        qa = _rope(qa, cos, sin)
        ka = _rope(ka, cos, sin)
    qa_o[...] = (qa * scale).astype(qa_o.dtype)
    if latent:
        store_kv(ka_o, ka)
        store_kv(va_o, va)
    else:
        store_kv(ka_o, ka.T)
        heads = mix_a // LANES
        for slot in range(va_rows_o.shape[1]):
            for hd in range(heads):
                va_rows_o[0, slot, pl.ds(hd, ROW_TILE, stride=heads), :] = (
                    va[:, hd * LANES:(hd + 1) * LANES])

    def lower_bound(ref):
        logits = ref[...]
        e = jnp.exp(logits - jnp.max(logits, axis=0, keepdims=True))
        sm = e / jnp.sum(e, axis=0, keepdims=True)
        return jnp.sum(sm[0:li + 1], axis=0, keepdims=True) - sm[0:1]

    def forget(x, lb):
        return jnp.maximum(lb + (1.0 - lb) * jax.nn.sigmoid(x), F_MIN)

    off = off_b
    zb = [proj(off + j * mix_b, mix_b) for j in range(5)]

    if latent:
        kc = _rope(kc, cos, sin)
        qc = _rope(qc, cos, sin)
        store_kv(kc_o, kc)
        store_kv(vc_o, vc)
    else:
        store_kv(kct_o, kc.T)
        store_kv(vct_o, vc.T)
    qc = qc * scale
    lane = lax.broadcasted_iota(jnp.int32, (1, LANES), 1)
    for n in range(2):
        blk = qc[:, n * LANES:(n + 1) * LANES]
        in_half = (lane // HEAD_DIM) == n
        for g in range(2):
            src = blk if g == n else pltpu.roll(blk, HEAD_DIM, 1)
            hc = 2 * n + g
            qc_o[:, hc * LANES:(hc + 1) * LANES] = jnp.where(in_half, src, 0.0).astype(qc_o.dtype)

    hq_o[0] = _silu(zb[0])
    ff_o[0] = forget(zb[1], lower_bound(lbf_ref))
    fb_o[0] = forget(zb[2], lower_bound(lbb_ref))
    hv_o[0] = zb[3]
    hg_o[0] = _silu(zb[4])


def _in_proj(x, mod, mod_row, w_in, lb_f, lb_b, qn, kn, rope, kv_prev, *, li):
    bsz, t, d = x.shape
    latent = rope is not None
    depth, _, n_in = w_in.shape
    tiles = t // ROW_TILE
    mix_a, mix_b, mix_c = d // 2, d // 4, d // 4
    kv_c = mix_c // 2
    x2 = x.reshape(bsz * t, d)

    row = lambda i: (i, 0)
    brow = lambda i: (i // tiles, i % tiles, 0)
    const = lambda i: (0, 0)
    in_specs = [pl.BlockSpec((ROW_TILE, d), row),
                pl.BlockSpec((1, 1, mod.shape[-1]), lambda i: (mod_row(i * ROW_TILE), 0, 0)),
                pl.BlockSpec((None, d, n_in), lambda i: (li, 0, 0)),
                pl.BlockSpec(lb_f.shape, const), pl.BlockSpec(lb_b.shape, const),
                pl.BlockSpec(qn.shape, const), pl.BlockSpec(kn.shape, const)]
    args = [x2, mod, w_in, lb_f, lb_b, qn, kn]
    if latent:
        in_specs += [pl.BlockSpec((ROW_TILE, LANES), lambda i: (i % tiles, 0))] * 2
        args += list(rope)
        slots, slot0 = 1, 0
    else:
        slots, slot0 = (depth, 0) if kv_prev is None else (1, li)
    aliases = {}
    if kv_prev is not None:
        for j, buf in enumerate(kv_prev):
            aliases[len(args)] = 7 + j
            in_specs.append(pl.BlockSpec(memory_space=pl.ANY))
            args.append(buf)
    krow = lambda i: (i // tiles, slot0, i % tiles, 0)
    kcol = lambda i: (i // tiles, slot0, 0, i % tiles)
    heads = mix_a // LANES

    out_specs = [pl.BlockSpec((ROW_TILE, mix_a), row)]
    out_shape = [jax.ShapeDtypeStruct((bsz * t, mix_a), BF16)]
    out_specs += [pl.BlockSpec((1, ROW_TILE, mix_b), brow)] * 5
    out_shape += [jax.ShapeDtypeStruct((bsz, t, mix_b), F32)] * 5
    out_specs.append(pl.BlockSpec((ROW_TILE, 2 * mix_c), row))
    out_shape.append(jax.ShapeDtypeStruct((bsz * t, 2 * mix_c), BF16))
    if latent:
        for width in (mix_a, mix_a, kv_c, kv_c):
            out_specs.append(pl.BlockSpec((1, 1, ROW_TILE, width), krow))
            out_shape.append(jax.ShapeDtypeStruct((bsz, 1, t, width), BF16))
    else:
        out_specs += [pl.BlockSpec((1, slots, mix_a, ROW_TILE), kcol),
                      pl.BlockSpec((1, slots, ROW_TILE * heads, LANES), krow),
                      pl.BlockSpec((1, slots, kv_c, ROW_TILE), kcol),
                      pl.BlockSpec((1, slots, kv_c, ROW_TILE), kcol)]
        out_shape += [jax.ShapeDtypeStruct((bsz, depth, mix_a, t), F32),
                      jax.ShapeDtypeStruct((bsz, depth, t * heads, LANES), F32),
                      jax.ShapeDtypeStruct((bsz, depth, kv_c, t), F32),
                      jax.ShapeDtypeStruct((bsz, depth, kv_c, t), F32)]
    return pl.pallas_call(
        functools.partial(_in_proj_kernel, li=li, d=d, latent=latent, n_alias=len(aliases)),
        grid=(bsz * tiles,),
        in_specs=in_specs, out_specs=out_specs, out_shape=out_shape,
        input_output_aliases=aliases,
        compiler_params=_params(1),
        name="in_proj_latent" if latent else "in_proj_context",
    )(*args)


def _softmax_parts(scores):
    m = functools.reduce(jnp.maximum, [jnp.max(s, axis=-1, keepdims=True) for s in scores])
    es = [jnp.exp2(s - m) for s in scores]
    denom = functools.reduce(lambda a, b: a + b, [jnp.sum(e, axis=-1, keepdims=True) for e in es])
    return es, 1.0 / denom


def _diff_attn_kernel(*refs, li, cached, heads, v_rows):
    if cached:
        (q_ref, k_ref, v_ref, ck_ref, cv_ref, lq1, lk1, lq2, lk2, sub_ref, o_ref) = refs
    else:
        (q_ref, k_ref, v_ref, lq1, lk1, lq2, lk2, sub_ref, o_ref) = refs
    lam_init = 0.8 - 0.6 * math.exp(-0.3 * li)

    def lam_term(a, b):
        return jnp.exp(jnp.sum(a[li:li + 1, :] * b[li:li + 1, :], axis=-1, keepdims=True))

    lam = lam_term(lq1, lk1) - lam_term(lq2, lk2) + lam_init
    tq = q_ref.shape[0]
    lane = lax.broadcasted_iota(jnp.int32, (1, LANES), 1)
    cols = [slice(h * LANES, (h + 1) * LANES) for h in range(heads)]

    scores, vals = [], []
    for c in cols:
        q = q_ref[:, c]
        zero = jnp.zeros_like(q)
        q2 = jnp.concatenate([jnp.where(lane < HEAD_DIM, q, zero),
                              jnp.where(lane >= HEAD_DIM, q, zero)], axis=0)
        head = pl.program_id(1) * heads + c.start // LANES
        if v_rows:
            t = k_ref.shape[3]
            head_scores = [_dot(q2, k_ref[0, 0, c, :].astype(BF16))]
            own_v = v_ref[0, 0, pl.ds(head, t, stride=v_ref.shape[2] // t), :]
        else:
            head_scores = [lax.dot_general(q2, k_ref[0, 0, :, c].astype(BF16), NT,
                                           preferred_element_type=F32)]
            own_v = v_ref[0, 0, :, c]
        vals.append([own_v.astype(BF16)])
        if cached:
            past = ck_ref.shape[-1]
            all_heads = cv_ref.shape[2] // past
            head_scores.append(_dot(q2, ck_ref[0, 0, c, :].astype(BF16)))
            vals[-1].append(cv_ref[0, 0, pl.ds(head, past, stride=all_heads), :].astype(BF16))
        scores.append(head_scores)
    parts = [_softmax_parts(s) for s in scores]
    outs = []
    for (es, r), vs in zip(parts, vals):
        r0 = r[0:tq]
        r1 = r[tq:2 * tq] * lam
        o = None
        for e, v in zip(es, vs):
            part = _dot((e[0:tq] * r0 - e[tq:2 * tq] * r1).astype(BF16), v)
            o = part if o is None else o + part
        outs.append(o)
    gain = sub_ref[li:li + 1, :] * (1.0 - lam_init)
    for c, o in zip(cols, outs):
        ms = jnp.mean(o * o, axis=-1, keepdims=True)
        o_ref[:, c] = (o * lax.rsqrt(ms + RMS_EPS) * gain).astype(o_ref.dtype)


def _diff_attn(q, k, v, cache, lam, subln, *, li, kv_layer, bsz, tq, heads_per_step, v_rows):
    width = q.shape[-1]
    t = k.shape[3] if v_rows else k.shape[2]
    wstep = heads_per_step * LANES
    nq = t // tq
    cached = cache is not None
    if v_rows:
        kv_specs = [pl.BlockSpec((1, 1, wstep, t), lambda b, h, i: (b, kv_layer, h, 0)),
                    pl.BlockSpec((1, 1) + v.shape[2:], lambda b, h, i: (b, kv_layer, 0, 0))]
    else:
        kv_specs = [pl.BlockSpec((1, 1, t, wstep), lambda b, h, i: (b, kv_layer, 0, h))] * 2
    in_specs = [pl.BlockSpec((tq, wstep), lambda b, h, i: (b * nq + i, h))] + kv_specs
    args = [q, k, v]
    if cached:
        ck, cv = cache
        in_specs += [pl.BlockSpec((1, 1, wstep, ck.shape[-1]), lambda b, h, i: (b, li, h, 0)),
                     pl.BlockSpec((1, 1) + cv.shape[2:], lambda b, h, i: (b, li, 0, 0))]
        args += [ck, cv]
    in_specs += [pl.BlockSpec(a.shape, lambda b, h, i: (0, 0)) for a in (*lam, subln)]
    args += [*lam, subln]
    return pl.pallas_call(
        functools.partial(_diff_attn_kernel, li=li, cached=cached, heads=heads_per_step,
                          v_rows=v_rows),
        grid=(bsz, width // wstep, nq),
        in_specs=in_specs,
        out_specs=pl.BlockSpec((tq, wstep), lambda b, h, i: (b * nq + i, h)),
        out_shape=jax.ShapeDtypeStruct((bsz * t, width), BF16),
        compiler_params=_params(3),
        name="diff_attn_latent" if cached else "diff_attn_context",
    )(*args)


def _gqa_kernel(*refs, cached, own_t):
    if cached:
        q_ref, k_ref, v_ref, ck_ref, cv_ref, o_ref = refs
    else:
        q_ref, k_ref, v_ref, o_ref = refs
    tq = q_ref.shape[0]
    heads = q_ref.shape[1] // LANES
    lane = lax.broadcasted_iota(jnp.int32, (1, LANES), 1)
    keys = [(k_ref[0, 0].astype(BF16), own_t)]
    vals = [(v_ref[0, 0].astype(BF16), own_t)]
    if cached:
        keys.append((ck_ref[0, 0].astype(BF16), True))
        vals.append((cv_ref[0, 0].astype(BF16), True))
    groups = range(heads // 2)
    scores = []
    for n in groups:
        q = jnp.concatenate([q_ref[:, (2 * n + g) * LANES:(2 * n + g + 1) * LANES]
                             for g in range(2)], axis=0)
        scores.append([_dot(q, k) if k_t else lax.dot_general(q, k, NT, preferred_element_type=F32)
                       for k, k_t in keys])
    parts = [_softmax_parts(s) for s in scores]
    outs = []
    for es, r in parts:
        o = None
        for e, (v, v_t) in zip(es, vals):
            p = e.astype(BF16)
            part = lax.dot_general(p, v, NT, preferred_element_type=F32) if v_t else _dot(p, v)
            o = part if o is None else o + part
        outs.append(o * r)
    for n in groups:
        first = outs[n][0:tq]
        second = outs[n][tq:2 * tq]
        if n == 0:
            second = pltpu.roll(second, HEAD_DIM, 1)
        else:
            first = pltpu.roll(first, HEAD_DIM, 1)
        o_ref[:, n * LANES:(n + 1) * LANES] = jnp.where(lane < HEAD_DIM, first,
                                                        second).astype(o_ref.dtype)


def _gqa(q, k, v, cache, *, li, kv_layer, bsz, tq, own_t):
    t, kvw = (k.shape[3], k.shape[2]) if own_t else (k.shape[2], k.shape[3])
    nq = t // tq
    cached = cache is not None
    in_specs = [pl.BlockSpec((tq, q.shape[-1]), lambda b, i: (b * nq + i, 0)),
                pl.BlockSpec((1, 1) + k.shape[2:], lambda b, i: (b, kv_layer, 0, 0)),
                pl.BlockSpec((1, 1) + k.shape[2:], lambda b, i: (b, kv_layer, 0, 0))]
    args = [q, k, v]
    if cached:
        in_specs += [pl.BlockSpec((1, 1) + cache[0].shape[2:], lambda b, i: (b, li, 0, 0))] * 2
        args += list(cache)
    return pl.pallas_call(
        functools.partial(_gqa_kernel, cached=cached, own_t=own_t),
        grid=(bsz, nq),
        in_specs=in_specs,
        out_specs=pl.BlockSpec((tq, 2 * kvw), lambda b, i: (b * nq + i, 0)),
        out_shape=jax.ShapeDtypeStruct((bsz * t, 2 * kvw), BF16),
        compiler_params=_params(2),
        name="gqa_latent" if cached else "gqa_context",
    )(*args)


def _head_masks(width):
    lane_head = lax.broadcasted_iota(jnp.int32, (1, width), 1) // HEAD_DIM
    return [lane_head == h for h in range(width // HEAD_DIM)]


def _stack_heads(x, masks):
    return jnp.concatenate([jnp.where(m, x, jnp.zeros_like(x)) for m in masks], axis=0)


def _block_diag_mask(width):
    r = lax.broadcasted_iota(jnp.int32, (width, width), 0) // HEAD_DIM
    c = lax.broadcasted_iota(jnp.int32, (width, width), 1) // HEAD_DIM
    return r == c


def _ref_rows(b, offsets, span):
    width = b.shape[-1]
    return jnp.concatenate([jnp.broadcast_to(b[o:o + 1], (span, width)) for o in offsets], axis=0)


def _hgrn_chunks(problems):
    n = len(problems)
    c, width = problems[0][0].shape
    qs = [p[0] for p in problems]
    vs = [p[2] for p in problems]
    sts = [p[3] for p in problems]
    rev = [p[4] for p in problems]
    chains = range(n)
    masks = _head_masks(width)
    trow =lax.broadcasted_iota(jnp.int32, (c, 1), 0)
    t_full = lax.broadcasted_iota(jnp.int32, (c, width), 0)
    s_full = lax.broadcasted_iota(jnp.int32, (c, width), 1) % c

    ks = [1.0 - p[1] for p in problems]
    b = [jnp.log(p[1]) for p in problems]
    step = 1
    while step < c:
        for j in chains:
            if rev[j]:
                b[j] = b[j] + jnp.where(trow < c - step, pltpu.roll(b[j], c - step, 0), 0.0)
            else:
                b[j] = b[j] + jnp.where(trow >= step, pltpu.roll(b[j], step, 0), 0.0)
        step *= 2
    b_end = [b[j][0:1] if rev[j] else b[j][c - 1:c] for j in chains]

    o = [lax.dot_general((qs[j] * jnp.exp(b[j])).astype(BF16), sts[j].astype(BF16), NT,
                         preferred_element_type=F32) for j in chains]

    a = [None] * n
    m = c // 2
    while m >= DIAG_BLOCK:
        blocks = c // (2 * m)
        same = (t_full // (2 * m)) == (s_full // (2 * m))
        for j in chains:
            ref = _ref_rows(b[j], [i * 2 * m + (m if rev[j] else m - 1) for i in range(blocks)],
                            2 * m)
            is_q = ((trow % (2 * m)) < m) if rev[j] else ((trow % (2 * m)) >= m)
            e = jnp.exp(jnp.where(is_q, b[j] - ref, ref - b[j]))
            ql = jnp.where(is_q, qs[j] * e, 0.0).astype(BF16)
            kl = jnp.where(is_q, 0.0, ks[j] * e).astype(BF16)
            al = lax.dot_general(ql, _stack_heads(kl, masks), NT, preferred_element_type=F32)
            if blocks > 1:
                al = jnp.where(same, al, 0.0)
            a[j] = al if a[j] is None else a[j] + al
        m //= 2
    blocks = c // DIAG_BLOCK
    mid = DIAG_BLOCK // 2
    same = (t_full // DIAG_BLOCK) == (s_full // DIAG_BLOCK)
    for j in chains:
        ref = _ref_rows(b[j], [i * DIAG_BLOCK + (mid if rev[j] else mid - 1) for i in range(blocks)],
                        DIAG_BLOCK)
        d = b[j] - ref
        ql = (qs[j] * jnp.exp(d)).astype(BF16)
        kl = (ks[j] * jnp.exp(-d)).astype(BF16)
        al = lax.dot_general(ql, _stack_heads(kl, masks), NT, preferred_element_type=F32)
        causal = (s_full >= t_full) if rev[j] else (s_full <= t_full)
        a[j] = a[j] + jnp.where(same & causal, al, 0.0)

    v_b = [v.astype(BF16) for v in vs]
    o = [o[j] + _dot(a[j].astype(BF16), _stack_heads(v_b[j], masks)) for j in chains]

    bd = _block_diag_mask(width)
    upd = [lax.dot_general(v_b[j], (ks[j] * jnp.exp(b_end[j] - b[j])).astype(BF16), TN,
                           preferred_element_type=F32) for j in chains]
    st_new = [sts[j] * jnp.exp(b_end[j]) + jnp.where(bd, upd[j], 0.0) for j in chains]
    return list(zip(o, st_new))


def _mxu_transpose(x):
    n = x.shape[1]
    r = lax.broadcasted_iota(jnp.int32, (n, n), 0)
    c = lax.broadcasted_iota(jnp.int32, (n, n), 1)
    eye = (r == c).astype(BF16)
    acc = None
    rem = x
    for _ in range(3):
        piece = rem.astype(BF16)
        rem = rem - piece.astype(F32)
        part = lax.dot_general(eye, piece, NT, preferred_element_type=F32)
        acc = part if acc is None else acc + part
    return acc


def _hgrn_kernel(*refs, li, has_state, want_state, n_alias, heads):
    refs = list(refs)
    q_ref, ff_ref, fb_ref, v_ref, gate_ref, gn_ref = refs[:6]
    pos = 6
    if has_state:
        s0f_ref, s0b_ref = refs[pos:pos + 2]
        pos += 2
    pos += n_alias
    o_ref = refs[pos]
    pos += 1
    if want_state:
        sf_ref, sb_ref = refs[pos:pos + 2]
        pos += 2
    st_ref, of_ref, ob_ref = refs[pos:pos + 3]
    nb, t, width = q_ref.shape
    nc = t // CHUNK
    bd = _block_diag_mask(width)

    for n in range(nb):
        for d in range(2):
            if has_state:
                x = (s0b_ref if d else s0f_ref)[n, 0].reshape(width, HEAD_DIM)
                xt = _mxu_transpose(x)
                st_ref[2 * n + d] = jnp.where(bd, jnp.concatenate([xt] * heads, axis=0), 0.0)
            else:
                st_ref[2 * n + d] = jnp.zeros((width, width), F32)

    def body(ci, carry):
        rows = (pl.ds(pl.multiple_of(ci * CHUNK, CHUNK), CHUNK),
                pl.ds(pl.multiple_of((nc - 1 - ci) * CHUNK, CHUNK), CHUNK))
        loaded = []
        for n in range(nb):
            for d, f_ref in enumerate((ff_ref, fb_ref)):
                r = rows[d]
                loaded.append((q_ref[n, r, :], f_ref[n, r, :], v_ref[n, r, :], st_ref[2 * n + d],
                               bool(d)))
        for j, (o, st) in enumerate(_hgrn_chunks(loaded)):
            n, d = divmod(j, 2)
            (ob_ref if d else of_ref)[n, rows[d], :] = o
            st_ref[j] = st
        return carry

    lax.fori_loop(0, nc, body, 0)

    for n in range(nb):
        o = of_ref[n] + ob_ref[n]
        o_ref[n] = (_group_rms(o, gn_ref[li:li + 1, :], HEAD_DIM) * gate_ref[n]).astype(o_ref.dtype)

    if want_state:
        for n in range(nb):
            for d, dst in enumerate((sf_ref, sb_ref)):
                st = st_ref[2 * n + d]
                rows = st[0:HEAD_DIM]
                for h in range(1, heads):
                    rows = rows + st[h * HEAD_DIM:(h + 1) * HEAD_DIM]
                final = _mxu_transpose(rows).reshape(heads, HEAD_DIM, HEAD_DIM)
                for slot in range(dst.shape[1]):
                    dst[n, slot] = final


def _hgrn(hq, ff, fb, hv, hg, gn, state, state_prev, *, li, depth, want_state, nb):
    bsz, t, width = hq.shape
    heads = width // HEAD_DIM
    has_state = state is not None
    seq = pl.BlockSpec((nb, t, width), lambda b: (b, 0, 0))
    in_specs = [seq] * 5 + [pl.BlockSpec(gn.shape, lambda b: (0, 0))]
    args = [hq, ff, fb, hv, hg, gn]
    if has_state:
        in_specs += [pl.BlockSpec((nb, 1, heads, HEAD_DIM, HEAD_DIM), lambda b: (b, li, 0, 0, 0))] * 2
        args += list(state)
    aliases = {}
    if state_prev is not None:
        for j, buf in enumerate(state_prev):
            aliases[len(args)] = 1 + j
            in_specs.append(pl.BlockSpec(memory_space=pl.ANY))
            args.append(buf)
    out_specs = [seq]
    out_shape = [jax.ShapeDtypeStruct((bsz, t, width), BF16)]
    if want_state:
        slots, slot0 = (depth, 0) if state_prev is None else (1, li)
        out_specs += [pl.BlockSpec((nb, slots, heads, HEAD_DIM, HEAD_DIM),
                                   lambda b: (b, slot0, 0, 0, 0))] * 2
        out_shape += [jax.ShapeDtypeStruct((bsz, depth, heads, HEAD_DIM, HEAD_DIM), F32)] * 2
    return pl.pallas_call(
        functools.partial(_hgrn_kernel, li=li, has_state=has_state, want_state=want_state,
                          n_alias=len(aliases), heads=heads),
        grid=(bsz // nb,),
        in_specs=in_specs, out_specs=out_specs, out_shape=out_shape,
        input_output_aliases=aliases,
        scratch_shapes=[pltpu.VMEM((2 * nb, width, width), F32),
                        pltpu.VMEM((nb, t, width), F32), pltpu.VMEM((nb, t, width), F32)],
        compiler_params=_params(1),
        name="hgrn2_latent" if has_state else "hgrn2_context",
    )(*args)


def _out_mlp_kernel(x_ref, oa_ref, ob_ref, oc_ref, mod_ref, wo_ref, w1_ref, w2_ref,
                    g1_ref, b1_ref, g2_ref, b2_ref, y_ref, *, li, d, alpha, ff_chunk):
    wa, wb = oa_ref.shape[-1], ob_ref.shape[-1]
    layer = slice(li, li + 1)
    gate1 = mod_ref[0, :, 2 * d:3 * d]
    shift2 = mod_ref[0, :, 3 * d:4 * d]
    gain2 = mod_ref[0, :, 4 * d:5 * d]
    gate2 = mod_ref[0, :, 5 * d:6 * d]
    subs = [slice(s * ROW_TILE, (s + 1) * ROW_TILE) for s in range(x_ref.shape[0] // ROW_TILE)]
    wo = wo_ref[...].astype(BF16)
    m = [_dot(oa_ref[r, :], wo[0:wa, :]) + _dot(ob_ref[r, :], wo[wa:wa + wb, :])
         + _dot(oc_ref[r, :], wo[wa + wb:, :]) for r in subs]
    x1 = [_layernorm(alpha * x_ref[r, :] + gate1 * mi, g1_ref[layer, :], b1_ref[layer, :])
          for r, mi in zip(subs, m)]
    h2 = [(xi * (1.0 + gain2) + shift2).astype(BF16) for xi in x1]
    acc = [None] * len(subs)
    for j in range(w1_ref.shape[-1] // ff_chunk):
        cols = slice(j * ff_chunk, (j + 1) * ff_chunk)
        hid = [jnp.maximum(_dot(hi, w1_ref[:, cols]), 0.0) for hi in h2]
        for s, hd in enumerate(hid):
            part = _dot((hd * hd).astype(BF16), w2_ref[cols, :])
            acc[s] = part if acc[s] is None else acc[s] + part
    for r, xi, ai in zip(subs, x1, acc):
        y_ref[r, :] = _layernorm(alpha * xi + gate2 * ai, g2_ref[layer, :], b2_ref[layer, :])


def _out_mlp(x, oa, ob, oc, mod, mod_row, w_out, w_ff1, w_ff2, ln, *, li, alpha):
    bsz, t, d = x.shape
    rows = 2 * ROW_TILE
    row = lambda i: (i, 0)
    const = lambda i: (0, 0)
    resident = lambda a: pl.BlockSpec((None,) + a.shape[1:], lambda i: (li, 0, 0),
                                      pipeline_mode=pl.Buffered(1))
    in_specs = [pl.BlockSpec((rows, d), row),
                pl.BlockSpec((rows, oa.shape[-1]), row),
                pl.BlockSpec((rows, ob.shape[-1]), row),
                pl.BlockSpec((rows, oc.shape[-1]), row),
                pl.BlockSpec((1, 1, mod.shape[-1]), lambda i: (mod_row(i * rows), 0, 0)),
                resident(w_out), resident(w_ff1), resident(w_ff2)]
    in_specs += [pl.BlockSpec(a.shape, const) for a in ln]
    y = pl.pallas_call(
        functools.partial(_out_mlp_kernel, li=li, d=d, alpha=alpha, ff_chunk=1024),
        grid=(bsz * t // rows,),
        in_specs=in_specs,
        out_specs=pl.BlockSpec((rows, d), row),
        out_shape=jax.ShapeDtypeStruct((bsz * t, d), F32),
        compiler_params=_params(1),
        name="out_mlp",
    )(x.reshape(bsz * t, d), oa, ob.reshape(bsz * t, -1), oc, mod, w_out, w_ff1, w_ff2, *ln)
    return y.reshape(bsz, t, d)


def _rope_tables(n_tokens):
    pairs = HEAD_DIM // 4
    tok = np.arange(n_tokens)
    row = (tok // GRID_W).astype(np.float64)
    col = (tok % GRID_W).astype(np.float64)
    inv = ROPE_THETA ** (-np.arange(pairs, dtype=np.float64) / pairs)
    ang = np.concatenate([row[:, None] * inv, col[:, None] * inv], axis=-1)
    lane = np.arange(LANES)
    pair = (lane % HEAD_DIM) // 2
    sign = np.where(lane % 2 == 0, -1.0, 1.0)
    return (jnp.asarray(np.cos(ang)[:, pair], F32), jnp.asarray(np.sin(ang)[:, pair] * sign, F32))


def kernel(x_prompt, x_sample, cache_a_k, cache_a_v, cache_c_k, cache_c_v, state_b_fwd, state_b_bwd, c, c_ctx, w_ada, b_ada, w_in, w_out, lam_q1, lam_k1, lam_q2, lam_k2, subln_g, lb_logits_fwd, lb_logits_bwd, gnorm_g, qnorm_g, knorm_g, ln1_g, ln1_b, ln2_g, ln2_b, w_ff1, w_ff2):
    depth = w_in.shape[0]
    bsz, seq, d = x_prompt.shape
    dec_bsz, dec_seq, _ = x_sample.shape
    past = cache_a_k.shape[2]
    alpha = (2 * depth) ** 0.25
    mix_a, mix_b, mix_c = d // 2, d // 4, d // 4

    mod = _modulation(c_ctx, c, w_ada, b_ada)
    rope = _rope_tables(dec_seq)

    cache = (cache_a_k.transpose(0, 1, 3, 4, 5, 2).reshape(dec_bsz, depth, mix_a, past),
             cache_a_v.reshape(dec_bsz, depth, past * (mix_a // LANES), LANES),
             cache_c_k.transpose(0, 1, 3, 4, 2).reshape(dec_bsz, depth, mix_c // 2, past),
             cache_c_v.transpose(0, 1, 3, 4, 2).reshape(dec_bsz, depth, mix_c // 2, past))
    lam = (lam_q1, lam_k1, lam_q2, lam_k2)

    weights = (w_in, w_out, w_ff1.astype(BF16), w_ff2.astype(BF16))
    qn = jnp.tile(qnorm_g, (1, mix_c // HEAD_DIM))
    kn = jnp.tile(knorm_g, (1, mix_c // 2 // HEAD_DIM))
    gn = jnp.tile(gnorm_g, (1, mix_b // HEAD_DIM))
    ln = (ln1_g, ln1_b, ln2_g, ln2_b)

    def stream(x, li, latent, own_prev):
        w_in_b, w_out_b, w1_b, w2_b = weights
        n, t, _ = x.shape
        mod_row = ((lambda r0: li * MOD_ROWS + 1 + r0 // t) if latent
                   else (lambda r0: li * MOD_ROWS))
        kv_layer = 0 if latent else li
        (qa, hq, ff, fb, hv, hg, qc, ka, va, kc, vc) = _in_proj(
            x, mod, mod_row, w_in_b, lb_logits_fwd, lb_logits_bwd, qn, kn,
            rope if latent else None, None if own_prev is None else own_prev[0:4], li=li)
        tq = 256
        oa = _diff_attn(qa, ka, va, cache[0:2] if latent else None, lam, subln_g,
                        li=li, kv_layer=kv_layer, bsz=n, tq=tq, heads_per_step=2 if latent else 4,
                        v_rows=not latent)
        oc = _gqa(qc, kc, vc, cache[2:4] if latent else None, li=li, kv_layer=kv_layer, bsz=n, tq=tq,
                  own_t=not latent)
        hres = _hgrn(hq, ff, fb, hv, hg, gn, (state_b_fwd, state_b_bwd) if latent else None,
                     None if own_prev is None else own_prev[4:6],
                     li=li, depth=depth, want_state=not latent, nb=2)
        y = _out_mlp(x, oa, hres[0], oc, mod, mod_row, w_out_b, w1_b, w2_b, ln, li=li, alpha=alpha)
        own = None if latent else (ka, va, kc, vc, hres[1], hres[2])
        return y, own

    y_prompt, y_sample = x_prompt, x_sample
    own = None
    for li in range(depth):
        y_prompt, own = stream(y_prompt, li, False, own)
        y_sample, _ = stream(y_sample, li, True, None)

    heads_a = mix_a // (2 * HEAD_DIM)
    new_a_k = own[0].reshape(bsz, depth, heads_a, 2, HEAD_DIM, seq).transpose(0, 1, 5, 2, 3, 4)
    new_a_v = own[1].reshape(bsz, depth, seq, heads_a, 2 * HEAD_DIM)
    kv_heads = mix_c // 2 // HEAD_DIM
    new_c_k = own[2].reshape(bsz, depth, kv_heads, HEAD_DIM, seq).transpose(0, 1, 4, 2, 3)
    new_c_v = own[3].reshape(bsz, depth, kv_heads, HEAD_DIM, seq).transpose(0, 1, 4, 2, 3)
    return (y_prompt, y_sample, new_a_k, new_a_v, new_c_k, new_c_v, own[4], own[5])
```

```python
import functools
import math

import jax
import jax.numpy as jnp
import numpy as np
from jax import lax
from jax.experimental import pallas as pl
from jax.experimental.pallas import tpu as pltpu

GRID_W = 64
HEAD_DIM = 64
ROPE_THETA = 10000.0
LN_EPS = 1e-6
RMS_EPS = 1e-6
F_MIN = 1e-6
CHUNK = 64
DIAG_BLOCK = 8
LANES = 128
ROW_TILE = 256
VMEM_LIMIT = 56 * 1024 * 1024

F32 = jnp.float32
BF16 = jnp.bfloat16
NT = (((1,), (1,)), ((), ()))
TN = (((0,), (0,)), ((), ()))


def _params(n_grid):
    return pltpu.CompilerParams(dimension_semantics=("arbitrary",) * n_grid,
                                vmem_limit_bytes=VMEM_LIMIT)


def _dot(a, b):
    return jnp.dot(a, b, preferred_element_type=F32)


def _split_dot(a, b_bf16, passes, dims=None):
    acc = None
    rem = a
    for _ in range(passes):
        piece = rem.astype(BF16)
        rem = rem - piece.astype(F32)
        part = (_dot(piece, b_bf16) if dims is None
                else lax.dot_general(piece, b_bf16, dims, preferred_element_type=F32))
        acc = part if acc is None else acc + part
    return acc


def _group_ones(n, group):
    r = lax.broadcasted_iota(jnp.int32, (n, n), 0) // group
    c = lax.broadcasted_iota(jnp.int32, (n, n), 1) // group
    return (r == c).astype(BF16)


def _group_mean_square(x, group):
    n = x.shape[-1]
    return _split_dot(x * x, _group_ones(n, group), 2) * (1.0 / group)


def _group_rms(x, g_row, group):
    return x * lax.rsqrt(_group_mean_square(x, group) + RMS_EPS) * g_row


def _pair_swap(x):
    lane = lax.broadcasted_iota(jnp.int32, x.shape, 1)
    return jnp.where(lane % 2 == 0, pltpu.roll(x, LANES - 1, 1), pltpu.roll(x, 1, 1))


def _rope(x, cos, sin):
    blocks = []
    for j in range(x.shape[-1] // LANES):
        blk = x[:, j * LANES:(j + 1) * LANES]
        blocks.append(blk * cos + _pair_swap(blk) * sin)
    return blocks[0] if len(blocks) == 1 else jnp.concatenate(blocks, axis=-1)


def _silu(x):
    return x * jax.nn.sigmoid(x)


def _layernorm(x, g, b):
    mu = jnp.mean(x, axis=-1, keepdims=True)
    xc = x - mu
    var = jnp.mean(xc * xc, axis=-1, keepdims=True)
    return xc * lax.rsqrt(var + LN_EPS) * g + b


MOD_ROWS = 8


def _mod_kernel(cctx_ref, c_ref, w_ref, b_ref, o_ref, s_ref):
    n_req = c_ref.shape[0]
    s_ref[...] = jnp.zeros_like(s_ref)
    s_ref[0:1, :] = _silu(cctx_ref[...])
    s_ref[1:1 + n_req, :] = _silu(c_ref[...])
    layer = pl.program_id(0)
    res = _dot(s_ref[...].astype(BF16), w_ref[0].astype(BF16)) + b_ref[pl.ds(layer, 1), :]
    for r in range(MOD_ROWS):
        o_ref[r] = res[r:r + 1, :]


def _modulation(c_ctx, c, w_ada, b_ada):
    depth, d, n = w_ada.shape
    tn = 1536
    return pl.pallas_call(
        _mod_kernel,
        grid=(depth, n // tn),
        in_specs=[pl.BlockSpec((1, d), lambda l, j: (0, 0)),
                  pl.BlockSpec(c.shape, lambda l, j: (0, 0)),
                  pl.BlockSpec((1, d, tn), lambda l, j: (l, 0, j)),
                  pl.BlockSpec((depth, tn), lambda l, j: (0, j))],
        out_specs=pl.BlockSpec((MOD_ROWS, 1, tn), lambda l, j: (l, 0, j)),
        out_shape=jax.ShapeDtypeStruct((depth * MOD_ROWS, 1, n), F32),
        scratch_shapes=[pltpu.VMEM((MOD_ROWS, d), F32)],
        compiler_params=_params(2),
        name="adaln_modulation",
    )(c_ctx.reshape(1, d), c, w_ada, b_ada)


def _in_proj_kernel(*refs, li, d, latent, n_alias):
    refs = list(refs)
    x_ref, mod_ref, w_ref, lbf_ref, lbb_ref, qn_ref, kn_ref = refs[:7]
    pos = 7
    if latent:
        cos, sin = refs[pos][...], refs[pos + 1][...]
        pos += 2
    pos += n_alias
    qa_o, hq_o, ff_o, fb_o, hv_o, hg_o, qc_o = refs[pos:pos + 7]
    if latent:
        ka_o, va_o, kc_o, vc_o = refs[pos + 7:pos + 11]
    else:
        ka_o, va_rows_o, kct_o, vct_o = refs[pos + 7:pos + 11]

    def store_kv(ref, val):
        for slot in range(ref.shape[1]):
            ref[0, slot] = val.astype(ref.dtype)

    mix_a, mix_b, mix_c = d // 2, d // 4, d // 4
    kv_c = mix_c // 2
    scale = HEAD_DIM ** -0.5 * math.log2(math.e)

    shift = mod_ref[0, :, 0:d]
    gain = mod_ref[0, :, d:2 * d]
    h = x_ref[...] * (1.0 + gain) + shift

    def proj(start, width):
        return _dot(h, w_ref[:, start:start + width])

    off_b = 3 * mix_a
    off_c = off_b + 5 * mix_b

    zq = proj(off_c, mix_c)
    zk = proj(off_c + mix_c, kv_c)
    vc = proj(off_c + mix_c + kv_c, kv_c)
    qa = proj(0, mix_a)
    msq = _group_mean_square(zq, HEAD_DIM)
    msk = _group_mean_square(zk, HEAD_DIM)
    ka = proj(mix_a, mix_a)
    va = proj(2 * mix_a, mix_a)
    qc = zq * lax.rsqrt(msq + RMS_EPS) * qn_ref[li:li + 1, :]
    kc = zk * lax.rsqrt(msk + RMS_EPS) * kn_ref[li:li + 1, :]

    if latent:
        qa = _rope(qa, cos, sin)
        ka = _rope(ka, cos, sin)
    qa_o[...] = (qa * scale).astype(qa_o.dtype)
    if latent:
        store_kv(ka_o, ka)
        store_kv(va_o, va)
    else:
        store_kv(ka_o, ka.T)
        heads = mix_a // LANES
        for slot in range(va_rows_o.shape[1]):
            for hd in range(heads):
                va_rows_o[0, slot, pl.ds(hd, ROW_TILE, stride=heads), :] = (
                    va[:, hd * LANES:(hd + 1) * LANES])

    def lower_bound(ref):
        logits = ref[...]
        e = jnp.exp(logits - jnp.max(logits, axis=0, keepdims=True))
        sm = e / jnp.sum(e, axis=0, keepdims=True)
        return jnp.sum(sm[0:li + 1], axis=0, keepdims=True) - sm[0:1]

    def forget(x, lb):
        return jnp.maximum(lb + (1.0 - lb) * jax.nn.sigmoid(x), F_MIN)

    off = off_b
    zb = [proj(off + j * mix_b, mix_b) for j in range(5)]

    if latent:
        kc = _rope(kc, cos, sin)
        qc = _rope(qc, cos, sin)
        store_kv(kc_o, kc)
        store_kv(vc_o, vc)
    else:
        store_kv(kct_o, kc.T)
        store_kv(vct_o, vc.T)
    qc = qc * scale
    lane = lax.broadcasted_iota(jnp.int32, (1, LANES), 1)
    for n in range(2):
        blk = qc[:, n * LANES:(n + 1) * LANES]
        in_half = (lane // HEAD_DIM) == n
        for g in range(2):
            src = blk if g == n else pltpu.roll(blk, HEAD_DIM, 1)
            hc = 2 * n + g
            qc_o[:, hc * LANES:(hc + 1) * LANES] = jnp.where(in_half, src, 0.0).astype(qc_o.dtype)

    hq_o[0] = _silu(zb[0])
    ff_o[0] = forget(zb[1], lower_bound(lbf_ref))
    fb_o[0] = forget(zb[2], lower_bound(lbb_ref))
    hv_o[0] = zb[3]
    hg_o[0] = _silu(zb[4])


def _in_proj(x, mod, mod_row, w_in, lb_f, lb_b, qn, kn, rope, kv_prev, *, li):
    bsz, t, d = x.shape
    latent = rope is not None
    depth, _, n_in = w_in.shape
    tiles = t // ROW_TILE
    mix_a, mix_b, mix_c = d // 2, d // 4, d // 4
    kv_c = mix_c // 2
    x2 = x.reshape(bsz * t, d)

    row = lambda i: (i, 0)
    brow = lambda i: (i // tiles, i % tiles, 0)
    const = lambda i: (0, 0)
    in_specs = [pl.BlockSpec((ROW_TILE, d), row),
                pl.BlockSpec((1, 1, mod.shape[-1]), lambda i: (mod_row(i * ROW_TILE), 0, 0)),
                pl.BlockSpec((None, d, n_in), lambda i: (li, 0, 0)),
                pl.BlockSpec(lb_f.shape, const), pl.BlockSpec(lb_b.shape, const),
                pl.BlockSpec(qn.shape, const), pl.BlockSpec(kn.shape, const)]
    args = [x2, mod, w_in, lb_f, lb_b, qn, kn]
    if latent:
        in_specs += [pl.BlockSpec((ROW_TILE, LANES), lambda i: (i % tiles, 0))] * 2
        args += list(rope)
        slots, slot0 = 1, 0
    else:
        slots, slot0 = (depth, 0) if kv_prev is None else (1, li)
    aliases = {}
    if kv_prev is not None:
        for j, buf in enumerate(kv_prev):
            aliases[len(args)] = 7 + j
            in_specs.append(pl.BlockSpec(memory_space=pl.ANY))
            args.append(buf)
    krow = lambda i: (i // tiles, slot0, i % tiles, 0)
    kcol = lambda i: (i // tiles, slot0, 0, i % tiles)
    heads = mix_a // LANES

    out_specs = [pl.BlockSpec((ROW_TILE, mix_a), row)]
    out_shape = [jax.ShapeDtypeStruct((bsz * t, mix_a), BF16)]
    out_specs += [pl.BlockSpec((1, ROW_TILE, mix_b), brow)] * 5
    out_shape += [jax.ShapeDtypeStruct((bsz, t, mix_b), F32)] * 5
    out_specs.append(pl.BlockSpec((ROW_TILE, 2 * mix_c), row))
    out_shape.append(jax.ShapeDtypeStruct((bsz * t, 2 * mix_c), BF16))
    if latent:
        for width in (mix_a, mix_a, kv_c, kv_c):
            out_specs.append(pl.BlockSpec((1, 1, ROW_TILE, width), krow))
            out_shape.append(jax.ShapeDtypeStruct((bsz, 1, t, width), BF16))
    else:
        out_specs += [pl.BlockSpec((1, slots, mix_a, ROW_TILE), kcol),
                      pl.BlockSpec((1, slots, ROW_TILE * heads, LANES), krow),
                      pl.BlockSpec((1, slots, kv_c, ROW_TILE), kcol),
                      pl.BlockSpec((1, slots, kv_c, ROW_TILE), kcol)]
        out_shape += [jax.ShapeDtypeStruct((bsz, depth, mix_a, t), F32),
                      jax.ShapeDtypeStruct((bsz, depth, t * heads, LANES), F32),
                      jax.ShapeDtypeStruct((bsz, depth, kv_c, t), F32),
                      jax.ShapeDtypeStruct((bsz, depth, kv_c, t), F32)]
    return pl.pallas_call(
        functools.partial(_in_proj_kernel, li=li, d=d, latent=latent, n_alias=len(aliases)),
        grid=(bsz * tiles,),
        in_specs=in_specs, out_specs=out_specs, out_shape=out_shape,
        input_output_aliases=aliases,
        compiler_params=_params(1),
        name="in_proj_latent" if latent else "in_proj_context",
    )(*args)


def _softmax_parts(scores):
    m = functools.reduce(jnp.maximum, [jnp.max(s, axis=-1, keepdims=True) for s in scores])
    es = [jnp.exp2(s - m) for s in scores]
    denom = functools.reduce(lambda a, b: a + b, [jnp.sum(e, axis=-1, keepdims=True) for e in es])
    return es, 1.0 / denom


def _score_blocks(q, keys):
    return [_dot(q, k) if k_t else lax.dot_general(q, k, NT, preferred_element_type=F32)
            for k, k_t in keys]


def _diff_lambda(lq1, lk1, lq2, lk2, li):
    lam_init = 0.8 - 0.6 * math.exp(-0.3 * li)

    def lam_term(a, b):
        return jnp.exp(jnp.sum(a[li:li + 1, :] * b[li:li + 1, :], axis=-1, keepdims=True))

    return lam_term(lq1, lk1) - lam_term(lq2, lk2) + lam_init, lam_init


def _diff_head_gen(q, keys, vals, lam, gain, store):
    tq = q.shape[0]
    lane = lax.broadcasted_iota(jnp.int32, (1, LANES), 1)
    zero = jnp.zeros_like(q)
    q2 = jnp.concatenate([jnp.where(lane < HEAD_DIM, q, zero),
                          jnp.where(lane >= HEAD_DIM, q, zero)], axis=0)
    scores = _score_blocks(q2, keys)
    yield
    es, r = _softmax_parts(scores)
    yield
    r0 = r[0:tq]
    r1 = r[tq:2 * tq] * lam
    o = None
    for e, v in zip(es, vals):
        part = _dot((e[0:tq] * r0 - e[tq:2 * tq] * r1).astype(BF16), v)
        o = part if o is None else o + part
    yield
    ms = jnp.mean(o * o, axis=-1, keepdims=True)
    store(o * lax.rsqrt(ms + RMS_EPS) * gain)


def _diff_attn_kernel(*refs, li, cached, heads, v_rows):
    if cached:
        (q_ref, k_ref, v_ref, ck_ref, cv_ref, lq1, lk1, lq2, lk2, sub_ref, o_ref) = refs
    else:
        (q_ref, k_ref, v_ref, lq1, lk1, lq2, lk2, sub_ref, o_ref) = refs
    lam, lam_init = _diff_lambda(lq1, lk1, lq2, lk2, li)
    gain = sub_ref[li:li + 1, :] * (1.0 - lam_init)

    def head_chain(j):
        c = slice(j * LANES, (j + 1) * LANES)
        head = pl.program_id(1) * heads + j
        if v_rows:
            t = k_ref.shape[3]
            keys = [(k_ref[0, 0, c, :].astype(BF16), True)]
            vals = [v_ref[0, 0, pl.ds(head, t, stride=v_ref.shape[2] // t), :].astype(BF16)]
        else:
            keys = [(k_ref[0, 0, :, c].astype(BF16), False)]
            vals = [v_ref[0, 0, :, c].astype(BF16)]
        if cached:
            past = ck_ref.shape[-1]
            all_heads = cv_ref.shape[2] // past
            keys.append((ck_ref[0, 0, c, :].astype(BF16), True))
            vals.append(cv_ref[0, 0, pl.ds(head, past, stride=all_heads), :].astype(BF16))

        def store(o):
            o_ref[:, c] = o.astype(o_ref.dtype)

        return _diff_head_gen(q_ref[:, c], keys, vals, lam, gain, store)

    _run_interleaved([head_chain(j) for j in range(heads)])


def _diff_attn(q, k, v, cache, lam, subln, *, li, kv_layer, bsz, tq, heads_per_step, v_rows):
    width = q.shape[-1]
    t = k.shape[3] if v_rows else k.shape[2]
    wstep = heads_per_step * LANES
    nq = t // tq
    cached = cache is not None
    if v_rows:
        kv_specs = [pl.BlockSpec((1, 1, wstep, t), lambda b, h, i: (b, kv_layer, h, 0)),
                    pl.BlockSpec((1, 1) + v.shape[2:], lambda b, h, i: (b, kv_layer, 0, 0))]
    else:
        kv_specs = [pl.BlockSpec((1, 1, t, wstep), lambda b, h, i: (b, kv_layer, 0, h))] * 2
    in_specs = [pl.BlockSpec((tq, wstep), lambda b, h, i: (b * nq + i, h))] + kv_specs
    args = [q, k, v]
    if cached:
        ck, cv = cache
        in_specs += [pl.BlockSpec((1, 1, wstep, ck.shape[-1]), lambda b, h, i: (b, li, h, 0)),
                     pl.BlockSpec((1, 1) + cv.shape[2:], lambda b, h, i: (b, li, 0, 0))]
        args += [ck, cv]
    in_specs += [pl.BlockSpec(a.shape, lambda b, h, i: (0, 0)) for a in (*lam, subln)]
    args += [*lam, subln]
    return pl.pallas_call(
        functools.partial(_diff_attn_kernel, li=li, cached=cached, heads=heads_per_step,
                          v_rows=v_rows),
        grid=(bsz, width // wstep, nq),
        in_specs=in_specs,
        out_specs=pl.BlockSpec((tq, wstep), lambda b, h, i: (b * nq + i, h)),
        out_shape=jax.ShapeDtypeStruct((bsz * t, width), BF16),
        compiler_params=_params(3),
        name="diff_attn_latent" if cached else "diff_attn_context",
    )(*args)


def _gqa_group_gen(n, q, keys, vals, store):
    tq = q.shape[0] // 2
    scores = _score_blocks(q, keys)
    yield
    es, r = _softmax_parts(scores)
    yield
    o = None
    for e, (v, v_t) in zip(es, vals):
        p = e.astype(BF16)
        part = lax.dot_general(p, v, NT, preferred_element_type=F32) if v_t else _dot(p, v)
        o = part if o is None else o + part
    yield
    o = o * r
    first = o[0:tq]
    second = o[tq:2 * tq]
    if n == 0:
        second = pltpu.roll(second, HEAD_DIM, 1)
    else:
        first = pltpu.roll(first, HEAD_DIM, 1)
    lane = lax.broadcasted_iota(jnp.int32, (1, LANES), 1)
    store(jnp.where(lane < HEAD_DIM, first, second))


def _gqa_kernel(*refs, cached, own_t):
    if cached:
        q_ref, k_ref, v_ref, ck_ref, cv_ref, o_ref = refs
    else:
        q_ref, k_ref, v_ref, o_ref = refs
    heads = q_ref.shape[1] // LANES
    keys = [(k_ref[0, 0].astype(BF16), own_t)]
    vals = [(v_ref[0, 0].astype(BF16), own_t)]
    if cached:
        keys.append((ck_ref[0, 0].astype(BF16), True))
        vals.append((cv_ref[0, 0].astype(BF16), True))

    def group_chain(n):
        q = jnp.concatenate([q_ref[:, (2 * n + g) * LANES:(2 * n + g + 1) * LANES]
                             for g in range(2)], axis=0)

        def store(o):
            o_ref[:, n * LANES:(n + 1) * LANES] = o.astype(o_ref.dtype)

        return _gqa_group_gen(n, q, keys, vals, store)

    _run_interleaved([group_chain(n) for n in range(heads // 2)])


def _gqa(q, k, v, cache, *, li, kv_layer, bsz, tq, own_t):
    t, kvw = (k.shape[3], k.shape[2]) if own_t else (k.shape[2], k.shape[3])
    nq = t // tq
    cached = cache is not None
    in_specs = [pl.BlockSpec((tq, q.shape[-1]), lambda b, i: (b * nq + i, 0)),
                pl.BlockSpec((1, 1) + k.shape[2:], lambda b, i: (b, kv_layer, 0, 0)),
                pl.BlockSpec((1, 1) + k.shape[2:], lambda b, i: (b, kv_layer, 0, 0))]
    args = [q, k, v]
    if cached:
        in_specs += [pl.BlockSpec((1, 1) + cache[0].shape[2:], lambda b, i: (b, li, 0, 0))] * 2
        args += list(cache)
    return pl.pallas_call(
        functools.partial(_gqa_kernel, cached=cached, own_t=own_t),
        grid=(bsz, nq),
        in_specs=in_specs,
        out_specs=pl.BlockSpec((tq, 2 * kvw), lambda b, i: (b * nq + i, 0)),
        out_shape=jax.ShapeDtypeStruct((bsz * t, 2 * kvw), BF16),
        compiler_params=_params(2),
        name="gqa_latent" if cached else "gqa_context",
    )(*args)


def _head_masks(width):
    lane_head = lax.broadcasted_iota(jnp.int32, (1, width), 1) // HEAD_DIM
    return [lane_head == h for h in range(width // HEAD_DIM)]


def _stack_heads(x, masks):
    return jnp.concatenate([jnp.where(m, x, jnp.zeros_like(x)) for m in masks], axis=0)


def _block_diag_mask(width):
    r = lax.broadcasted_iota(jnp.int32, (width, width), 0) // HEAD_DIM
    c = lax.broadcasted_iota(jnp.int32, (width, width), 1) // HEAD_DIM
    return r == c


def _ref_rows(b, offsets, span):
    width = b.shape[-1]
    return jnp.concatenate([jnp.broadcast_to(b[o:o + 1], (span, width)) for o in offsets], axis=0)


def _run_interleaved(gens, delays=None):
    live = list(zip(gens, delays or [0] * len(gens)))
    rnd = 0
    while live:
        for item in list(live):
            if item[1] > rnd:
                continue
            try:
                next(item[0])
            except StopIteration:
                live.remove(item)
        rnd += 1


def _hgrn_chunks(problems):
    out = []
    _run_interleaved([_hgrn_chunks_gen(problems, out)])
    return out


def _hgrn_chunks_gen(problems, out):
    n = len(problems)
    c, width = problems[0][0].shape
    qs = [p[0] for p in problems]
    vs = [p[2] for p in problems]
    sts = [p[3] for p in problems]
    rev = [p[4] for p in problems]
    chains = range(n)
    masks = _head_masks(width)
    trow =lax.broadcasted_iota(jnp.int32, (c, 1), 0)
    t_full = lax.broadcasted_iota(jnp.int32, (c, width), 0)
    s_full = lax.broadcasted_iota(jnp.int32, (c, width), 1) % c

    ks = [1.0 - p[1] for p in problems]
    b = [jnp.log(p[1]) for p in problems]
    step = 1
    while step < c:
        for j in chains:
            if rev[j]:
                b[j] = b[j] + jnp.where(trow < c - step, pltpu.roll(b[j], c - step, 0), 0.0)
            else:
                b[j] = b[j] + jnp.where(trow >= step, pltpu.roll(b[j], step, 0), 0.0)
        step *= 2
        yield
    b_end = [b[j][0:1] if rev[j] else b[j][c - 1:c] for j in chains]

    o = [lax.dot_general((qs[j] * jnp.exp(b[j])).astype(BF16), sts[j].astype(BF16), NT,
                         preferred_element_type=F32) for j in chains]
    yield

    a = [None] * n
    m = c // 2
    while m >= DIAG_BLOCK:
        blocks = c // (2 * m)
        same = (t_full // (2 * m)) == (s_full // (2 * m))
        for j in chains:
            ref = _ref_rows(b[j], [i * 2 * m + (m if rev[j] else m - 1) for i in range(blocks)],
                            2 * m)
            is_q = ((trow % (2 * m)) < m) if rev[j] else ((trow % (2 * m)) >= m)
            e = jnp.exp(jnp.where(is_q, b[j] - ref, ref - b[j]))
            ql = jnp.where(is_q, qs[j] * e, 0.0).astype(BF16)
            kl = jnp.where(is_q, 0.0, ks[j] * e).astype(BF16)
            al = lax.dot_general(ql, _stack_heads(kl, masks), NT, preferred_element_type=F32)
            if blocks > 1:
                al = jnp.where(same, al, 0.0)
            a[j] = al if a[j] is None else a[j] + al
        m //= 2
        yield
    blocks = c // DIAG_BLOCK
    mid = DIAG_BLOCK // 2
    same = (t_full // DIAG_BLOCK) == (s_full // DIAG_BLOCK)
    for j in chains:
        ref = _ref_rows(b[j], [i * DIAG_BLOCK + (mid if rev[j] else mid - 1) for i in range(blocks)],
                        DIAG_BLOCK)
        d = b[j] - ref
        ql = (qs[j] * jnp.exp(d)).astype(BF16)
        kl = (ks[j] * jnp.exp(-d)).astype(BF16)
        al = lax.dot_general(ql, _stack_heads(kl, masks), NT, preferred_element_type=F32)
        causal = (s_full >= t_full) if rev[j] else (s_full <= t_full)
        a[j] = a[j] + jnp.where(same & causal, al, 0.0)
    yield

    v_b = [v.astype(BF16) for v in vs]
    o = [o[j] + _dot(a[j].astype(BF16), _stack_heads(v_b[j], masks)) for j in chains]
    yield

    bd = _block_diag_mask(width)
    upd = [lax.dot_general(v_b[j], (ks[j] * jnp.exp(b_end[j] - b[j])).astype(BF16), TN,
                           preferred_element_type=F32) for j in chains]
    st_new = [sts[j] * jnp.exp(b_end[j]) + jnp.where(bd, upd[j], 0.0) for j in chains]
    out.extend(zip(o, st_new))


def _mxu_transpose(x):
    n = x.shape[1]
    r = lax.broadcasted_iota(jnp.int32, (n, n), 0)
    c = lax.broadcasted_iota(jnp.int32, (n, n), 1)
    eye = (r == c).astype(BF16)
    acc = None
    rem = x
    for _ in range(3):
        piece = rem.astype(BF16)
        rem = rem - piece.astype(F32)
        part = lax.dot_general(eye, piece, NT, preferred_element_type=F32)
        acc = part if acc is None else acc + part
    return acc


def _hgrn_kernel(*refs, li, has_state, want_state, n_alias, heads):
    refs = list(refs)
    q_ref, ff_ref, fb_ref, v_ref, gate_ref, gn_ref = refs[:6]
    pos = 6
    if has_state:
        s0f_ref, s0b_ref = refs[pos:pos + 2]
        pos += 2
    pos += n_alias
    o_ref = refs[pos]
    pos += 1
    if want_state:
        sf_ref, sb_ref = refs[pos:pos + 2]
        pos += 2
    st_ref, of_ref, ob_ref = refs[pos:pos + 3]
    nb, t, width = q_ref.shape
    nc = t // CHUNK
    bd = _block_diag_mask(width)

    for n in range(nb):
        for d in range(2):
            if has_state:
                x = (s0b_ref if d else s0f_ref)[n, 0].reshape(width, HEAD_DIM)
                xt = _mxu_transpose(x)
                st_ref[2 * n + d] = jnp.where(bd, jnp.concatenate([xt] * heads, axis=0), 0.0)
            else:
                st_ref[2 * n + d] = jnp.zeros((width, width), F32)

    def body(ci, carry):
        rows = (pl.ds(pl.multiple_of(ci * CHUNK, CHUNK), CHUNK),
                pl.ds(pl.multiple_of((nc - 1 - ci) * CHUNK, CHUNK), CHUNK))
        loaded = []
        for n in range(nb):
            for d, f_ref in enumerate((ff_ref, fb_ref)):
                r = rows[d]
                loaded.append((q_ref[n, r, :], f_ref[n, r, :], v_ref[n, r, :], st_ref[2 * n + d],
                               bool(d)))
        for j, (o, st) in enumerate(_hgrn_chunks(loaded)):
            n, d = divmod(j, 2)
            (ob_ref if d else of_ref)[n, rows[d], :] = o
            st_ref[j] = st
        return carry

    lax.fori_loop(0, nc, body, 0)

    for n in range(nb):
        o = of_ref[n] + ob_ref[n]
        o_ref[n] = (_group_rms(o, gn_ref[li:li + 1, :], HEAD_DIM) * gate_ref[n]).astype(o_ref.dtype)

    if want_state:
        for n in range(nb):
            for d, dst in enumerate((sf_ref, sb_ref)):
                st = st_ref[2 * n + d]
                rows = st[0:HEAD_DIM]
                for h in range(1, heads):
                    rows = rows + st[h * HEAD_DIM:(h + 1) * HEAD_DIM]
                final = _mxu_transpose(rows).reshape(heads, HEAD_DIM, HEAD_DIM)
                for slot in range(dst.shape[1]):
                    dst[n, slot] = final


def _hgrn(hq, ff, fb, hv, hg, gn, state, state_prev, *, li, depth, want_state, nb):
    bsz, t, width = hq.shape
    heads = width // HEAD_DIM
    has_state = state is not None
    seq = pl.BlockSpec((nb, t, width), lambda b: (b, 0, 0))
    in_specs = [seq] * 5 + [pl.BlockSpec(gn.shape, lambda b: (0, 0))]
    args = [hq, ff, fb, hv, hg, gn]
    if has_state:
        in_specs += [pl.BlockSpec((nb, 1, heads, HEAD_DIM, HEAD_DIM), lambda b: (b, li, 0, 0, 0))] * 2
        args += list(state)
    aliases = {}
    if state_prev is not None:
        for j, buf in enumerate(state_prev):
            aliases[len(args)] = 1 + j
            in_specs.append(pl.BlockSpec(memory_space=pl.ANY))
            args.append(buf)
    out_specs = [seq]
    out_shape = [jax.ShapeDtypeStruct((bsz, t, width), BF16)]
    if want_state:
        slots, slot0 = (depth, 0) if state_prev is None else (1, li)
        out_specs += [pl.BlockSpec((nb, slots, heads, HEAD_DIM, HEAD_DIM),
                                   lambda b: (b, slot0, 0, 0, 0))] * 2
        out_shape += [jax.ShapeDtypeStruct((bsz, depth, heads, HEAD_DIM, HEAD_DIM), F32)] * 2
    return pl.pallas_call(
        functools.partial(_hgrn_kernel, li=li, has_state=has_state, want_state=want_state,
                          n_alias=len(aliases), heads=heads),
        grid=(bsz // nb,),
        in_specs=in_specs, out_specs=out_specs, out_shape=out_shape,
        input_output_aliases=aliases,
        scratch_shapes=[pltpu.VMEM((2 * nb, width, width), F32),
                        pltpu.VMEM((nb, t, width), F32), pltpu.VMEM((nb, t, width), F32)],
        compiler_params=_params(1),
        name="hgrn2_latent" if has_state else "hgrn2_context",
    )(*args)


ATTENTION_DELAY = 5


def _mixer_ctx_kernel(*refs, li, n_alias):
    refs = list(refs)
    (qa_ref, kat_ref, va_ref, qc_ref, kct_ref, vct_ref, hq_ref, ff_ref, fb_ref, hv_ref, hg_ref,
     gn_ref, lq1, lk1, lq2, lk2, sub_ref) = refs[:17]
    pos = 17 + n_alias
    oa_ref, oc_ref, ob_ref, sf_ref, sb_ref = refs[pos:pos + 5]
    st_ref, of_ref, obk_ref = refs[pos + 5:pos + 8]
    nb, t, width = hq_ref.shape
    nc = t // CHUNK
    heads_b = width // HEAD_DIM
    heads_a = qa_ref.shape[1] // LANES
    groups_c = qc_ref.shape[1] // LANES // 2
    lam, lam_init = _diff_lambda(lq1, lk1, lq2, lk2, li)
    gain = sub_ref[li:li + 1, :] * (1.0 - lam_init)

    for j in range(2 * nb):
        st_ref[j] = jnp.zeros((width, width), F32)

    def diff_unit(n, h):
        rows = slice(n * t, (n + 1) * t)
        c = slice(h * LANES, (h + 1) * LANES)
        keys = [(kat_ref[n, 0, c, :].astype(BF16), True)]
        vals = [va_ref[n, 0, pl.ds(h, t, stride=heads_a), :].astype(BF16)]

        def store(o):
            oa_ref[rows, c] = o.astype(oa_ref.dtype)

        yield from _diff_head_gen(qa_ref[rows, c], keys, vals, lam, gain, store)

    def gqa_unit(n, g):
        rows = slice(n * t, (n + 1) * t)
        q = jnp.concatenate([qc_ref[rows, (2 * g + j) * LANES:(2 * g + j + 1) * LANES]
                             for j in range(2)], axis=0)
        keys = [(kct_ref[n, 0].astype(BF16), True)]
        vals = [(vct_ref[n, 0].astype(BF16), True)]

        def store(o):
            oc_ref[rows, g * LANES:(g + 1) * LANES] = o.astype(oc_ref.dtype)

        yield from _gqa_group_gen(g, q, keys, vals, store)

    def scan_step(ci):
        rows = (slice(ci * CHUNK, (ci + 1) * CHUNK), slice((nc - 1 - ci) * CHUNK, (nc - ci) * CHUNK))
        loaded = []
        for n in range(nb):
            for d, f_ref in enumerate((ff_ref, fb_ref)):
                r = rows[d]
                loaded.append((hq_ref[n, r, :], f_ref[n, r, :], hv_ref[n, r, :], st_ref[2 * n + d],
                               bool(d)))
        out = []
        yield from _hgrn_chunks_gen(loaded, out)
        for j, (o, st) in enumerate(out):
            n, d = divmod(j, 2)
            (obk_ref if d else of_ref)[n, rows[d], :] = o
            st_ref[j] = st

    units = []
    for n in range(nb):
        units += [diff_unit(n, h) for h in range(heads_a)]
        units += [gqa_unit(n, g) for g in range(groups_c)]
    share = -(-len(units) // nc)
    for ci in range(nc):
        mine = units[ci * share:(ci + 1) * share]
        _run_interleaved([scan_step(ci)] + mine, [0] + [ATTENTION_DELAY] * len(mine))

    for n in range(nb):
        o = of_ref[n] + obk_ref[n]
        ob_ref[n] = (_group_rms(o, gn_ref[li:li + 1, :], HEAD_DIM) * hg_ref[n]).astype(ob_ref.dtype)
        for d, dst in enumerate((sf_ref, sb_ref)):
            st = st_ref[2 * n + d]
            rows = st[0:HEAD_DIM]
            for h in range(1, heads_b):
                rows = rows + st[h * HEAD_DIM:(h + 1) * HEAD_DIM]
            final = _mxu_transpose(rows).reshape(heads_b, HEAD_DIM, HEAD_DIM)
            for slot in range(dst.shape[1]):
                dst[n, slot] = final


def _mixer_ctx(qa, ka_t, va_rows, qc, kc_t, vc_t, hq, ff, fb, hv, hg, gn, lam, subln, state_prev,
               *, li, depth, nb):
    bsz, t, width = hq.shape
    heads_b = width // HEAD_DIM
    rows = lambda b: (b, 0)
    at_layer = lambda b: (b, li, 0, 0)
    seq = pl.BlockSpec((nb, t, width), lambda b: (b, 0, 0))
    const = lambda b: (0, 0)
    in_specs = [pl.BlockSpec((nb * t, qa.shape[1]), rows),
                pl.BlockSpec((nb, 1) + ka_t.shape[2:], at_layer),
                pl.BlockSpec((nb, 1) + va_rows.shape[2:], at_layer),
                pl.BlockSpec((nb * t, qc.shape[1]), rows),
                pl.BlockSpec((nb, 1) + kc_t.shape[2:], at_layer),
                pl.BlockSpec((nb, 1) + vc_t.shape[2:], at_layer),
                seq, seq, seq, seq, seq, pl.BlockSpec(gn.shape, const)]
    in_specs += [pl.BlockSpec(a.shape, const) for a in (*lam, subln)]
    args = [qa, ka_t, va_rows, qc, kc_t, vc_t, hq, ff, fb, hv, hg, gn, *lam, subln]
    aliases = {}
    if state_prev is not None:
        for j, buf in enumerate(state_prev):
            aliases[len(args)] = 3 + j
            in_specs.append(pl.BlockSpec(memory_space=pl.ANY))
            args.append(buf)
    slots, slot0 = (depth, 0) if state_prev is None else (1, li)
    state_spec = pl.BlockSpec((nb, slots, heads_b, HEAD_DIM, HEAD_DIM), lambda b: (b, slot0, 0, 0, 0))
    out_specs = [pl.BlockSpec((nb * t, qa.shape[1]), rows),
                 pl.BlockSpec((nb * t, qc.shape[1] // 2), rows), seq, state_spec, state_spec]
    out_shape = [jax.ShapeDtypeStruct(qa.shape, BF16),
                 jax.ShapeDtypeStruct((qc.shape[0], qc.shape[1] // 2), BF16),
                 jax.ShapeDtypeStruct((bsz, t, width), BF16)]
    out_shape += [jax.ShapeDtypeStruct((bsz, depth, heads_b, HEAD_DIM, HEAD_DIM), F32)] * 2
    return pl.pallas_call(
        functools.partial(_mixer_ctx_kernel, li=li, n_alias=len(aliases)),
        grid=(bsz // nb,),
        in_specs=in_specs, out_specs=out_specs, out_shape=out_shape,
        input_output_aliases=aliases,
        scratch_shapes=[pltpu.VMEM((2 * nb, width, width), F32),
                        pltpu.VMEM((nb, t, width), F32), pltpu.VMEM((nb, t, width), F32)],
        compiler_params=_params(1),
        name="mixers_context",
    )(*args)


def _out_mlp_kernel(x_ref, oa_ref, ob_ref, oc_ref, mod_ref, wo_ref, w1_ref, w2_ref,
                    g1_ref, b1_ref, g2_ref, b2_ref, y_ref, *, li, d, alpha, ff_chunk):
    wa, wb = oa_ref.shape[-1], ob_ref.shape[-1]
    layer = slice(li, li + 1)
    gate1 = mod_ref[0, :, 2 * d:3 * d]
    shift2 = mod_ref[0, :, 3 * d:4 * d]
    gain2 = mod_ref[0, :, 4 * d:5 * d]
    gate2 = mod_ref[0, :, 5 * d:6 * d]
    subs = [slice(s * ROW_TILE, (s + 1) * ROW_TILE) for s in range(x_ref.shape[0] // ROW_TILE)]
    wo = wo_ref[...].astype(BF16)
    m = [_dot(oa_ref[r, :], wo[0:wa, :]) + _dot(ob_ref[r, :], wo[wa:wa + wb, :])
         + _dot(oc_ref[r, :], wo[wa + wb:, :]) for r in subs]
    x1 = [_layernorm(alpha * x_ref[r, :] + gate1 * mi, g1_ref[layer, :], b1_ref[layer, :])
          for r, mi in zip(subs, m)]
    h2 = [(xi * (1.0 + gain2) + shift2).astype(BF16) for xi in x1]
    acc = [None] * len(subs)
    for j in range(w1_ref.shape[-1] // ff_chunk):
        cols = slice(j * ff_chunk, (j + 1) * ff_chunk)
        hid = [jnp.maximum(_dot(hi, w1_ref[:, cols]), 0.0) for hi in h2]
        for s, hd in enumerate(hid):
            part = _dot((hd * hd).astype(BF16), w2_ref[cols, :])
            acc[s] = part if acc[s] is None else acc[s] + part
    for r, xi, ai in zip(subs, x1, acc):
        y_ref[r, :] = _layernorm(alpha * xi + gate2 * ai, g2_ref[layer, :], b2_ref[layer, :])


def _out_mlp(x, oa, ob, oc, mod, mod_row, w_out, w_ff1, w_ff2, ln, *, li, alpha):
    bsz, t, d = x.shape
    rows = 2 * ROW_TILE
    row = lambda i: (i, 0)
    const = lambda i: (0, 0)
    resident = lambda a: pl.BlockSpec((None,) + a.shape[1:], lambda i: (li, 0, 0),
                                      pipeline_mode=pl.Buffered(1))
    in_specs = [pl.BlockSpec((rows, d), row),
                pl.BlockSpec((rows, oa.shape[-1]), row),
                pl.BlockSpec((rows, ob.shape[-1]), row),
                pl.BlockSpec((rows, oc.shape[-1]), row),
                pl.BlockSpec((1, 1, mod.shape[-1]), lambda i: (mod_row(i * rows), 0, 0)),
                resident(w_out), resident(w_ff1), resident(w_ff2)]
    in_specs += [pl.BlockSpec(a.shape, const) for a in ln]
    y = pl.pallas_call(
        functools.partial(_out_mlp_kernel, li=li, d=d, alpha=alpha, ff_chunk=1024),
        grid=(bsz * t // rows,),
        in_specs=in_specs,
        out_specs=pl.BlockSpec((rows, d), row),
        out_shape=jax.ShapeDtypeStruct((bsz * t, d), F32),
        compiler_params=_params(1),
        name="out_mlp",
    )(x.reshape(bsz * t, d), oa, ob.reshape(bsz * t, -1), oc, mod, w_out, w_ff1, w_ff2, *ln)
    return y.reshape(bsz, t, d)


def _rope_tables(n_tokens):
    pairs = HEAD_DIM // 4
    tok = np.arange(n_tokens)
    row = (tok // GRID_W).astype(np.float64)
    col = (tok % GRID_W).astype(np.float64)
    inv = ROPE_THETA ** (-np.arange(pairs, dtype=np.float64) / pairs)
    ang = np.concatenate([row[:, None] * inv, col[:, None] * inv], axis=-1)
    lane = np.arange(LANES)
    pair = (lane % HEAD_DIM) // 2
    sign = np.where(lane % 2 == 0, -1.0, 1.0)
    return (jnp.asarray(np.cos(ang)[:, pair], F32), jnp.asarray(np.sin(ang)[:, pair] * sign, F32))


def kernel(x_prompt, x_sample, cache_a_k, cache_a_v, cache_c_k, cache_c_v, state_b_fwd, state_b_bwd, c, c_ctx, w_ada, b_ada, w_in, w_out, lam_q1, lam_k1, lam_q2, lam_k2, subln_g, lb_logits_fwd, lb_logits_bwd, gnorm_g, qnorm_g, knorm_g, ln1_g, ln1_b, ln2_g, ln2_b, w_ff1, w_ff2):
    depth = w_in.shape[0]
    bsz, seq, d = x_prompt.shape
    dec_bsz, dec_seq, _ = x_sample.shape
    past = cache_a_k.shape[2]
    alpha = (2 * depth) ** 0.25
    mix_a, mix_b, mix_c = d // 2, d // 4, d // 4

    mod = _modulation(c_ctx, c, w_ada, b_ada)
    rope = _rope_tables(dec_seq)

    cache = (cache_a_k.transpose(0, 1, 3, 4, 5, 2).reshape(dec_bsz, depth, mix_a, past),
             cache_a_v.reshape(dec_bsz, depth, past * (mix_a // LANES), LANES),
             cache_c_k.transpose(0, 1, 3, 4, 2).reshape(dec_bsz, depth, mix_c // 2, past),
             cache_c_v.transpose(0, 1, 3, 4, 2).reshape(dec_bsz, depth, mix_c // 2, past))
    lam = (lam_q1, lam_k1, lam_q2, lam_k2)

    weights = (w_in, w_out, w_ff1.astype(BF16), w_ff2.astype(BF16))
    qn = jnp.tile(qnorm_g, (1, mix_c // HEAD_DIM))
    kn = jnp.tile(knorm_g, (1, mix_c // 2 // HEAD_DIM))
    gn = jnp.tile(gnorm_g, (1, mix_b // HEAD_DIM))
    ln = (ln1_g, ln1_b, ln2_g, ln2_b)

    def stream(x, li, latent, own_prev):
        w_in_b, w_out_b, w1_b, w2_b = weights
        n, t, _ = x.shape
        mod_row = ((lambda r0: li * MOD_ROWS + 1 + r0 // t) if latent
                   else (lambda r0: li * MOD_ROWS))
        kv_layer = 0 if latent else li
        (qa, hq, ff, fb, hv, hg, qc, ka, va, kc, vc) = _in_proj(
            x, mod, mod_row, w_in_b, lb_logits_fwd, lb_logits_bwd, qn, kn,
            rope if latent else None, None if own_prev is None else own_prev[0:4], li=li)
        if latent:
            tq = 256
            oa = _diff_attn(qa, ka, va, cache[0:2], lam, subln_g, li=li, kv_layer=0, bsz=n, tq=tq,
                            heads_per_step=2, v_rows=False)
            oc = _gqa(qc, kc, vc, cache[2:4], li=li, kv_layer=0, bsz=n, tq=tq, own_t=False)
            ob = _hgrn(hq, ff, fb, hv, hg, gn, (state_b_fwd, state_b_bwd), None,
                       li=li, depth=depth, want_state=False, nb=2)[0]
            own = None
        else:
            oa, oc, ob, s_f, s_b = _mixer_ctx(
                qa, ka, va, qc, kc, vc, hq, ff, fb, hv, hg, gn, lam, subln_g,
                None if own_prev is None else own_prev[4:6], li=li, depth=depth, nb=2)
            own = (ka, va, kc, vc, s_f, s_b)
        y = _out_mlp(x, oa, ob, oc, mod, mod_row, w_out_b, w1_b, w2_b, ln, li=li, alpha=alpha)
        return y, own

    y_prompt, y_sample = x_prompt, x_sample
    own = None
    for li in range(depth):
        y_prompt, own = stream(y_prompt, li, False, own)
        y_sample, _ = stream(y_sample, li, True, None)

    heads_a = mix_a // (2 * HEAD_DIM)
    new_a_k = own[0].reshape(bsz, depth, heads_a, 2, HEAD_DIM, seq).transpose(0, 1, 5, 2, 3, 4)
    new_a_v = own[1].reshape(bsz, depth, seq, heads_a, 2 * HEAD_DIM)
    kv_heads = mix_c // 2 // HEAD_DIM
    new_c_k = own[2].reshape(bsz, depth, kv_heads, HEAD_DIM, seq).transpose(0, 1, 4, 2, 3)
    new_c_v = own[3].reshape(bsz, depth, kv_heads, HEAD_DIM, seq).transpose(0, 1, 4, 2, 3)
    return (y_prompt, y_sample, new_a_k, new_a_v, new_c_k, new_c_v, own[4], own[5])
```

```python
import functools
import math

import jax
import jax.numpy as jnp
import numpy as np
from jax import lax
from jax.experimental import pallas as pl
from jax.experimental.pallas import tpu as pltpu

GRID_W = 64
HEAD_DIM = 64
ROPE_THETA = 10000.0
LN_EPS = 1e-6
RMS_EPS = 1e-6
F_MIN = 1e-6
CHUNK = 64
DIAG_BLOCK = 8
LANES = 128
ROW_TILE = 256
VMEM_LIMIT = 56 * 1024 * 1024

F32 = jnp.float32
BF16 = jnp.bfloat16
NT = (((1,), (1,)), ((), ()))
TN = (((0,), (0,)), ((), ()))


def _params(n_grid):
    return pltpu.CompilerParams(dimension_semantics=("arbitrary",) * n_grid,
                                vmem_limit_bytes=VMEM_LIMIT)


def _dot(a, b):
    return jnp.dot(a, b, preferred_element_type=F32)


def _split_dot(a, b_bf16, passes, dims=None):
    acc = None
    rem = a
    for _ in range(passes):
        piece = rem.astype(BF16)
        rem = rem - piece.astype(F32)
        part = (_dot(piece, b_bf16) if dims is None
                else lax.dot_general(piece, b_bf16, dims, preferred_element_type=F32))
        acc = part if acc is None else acc + part
    return acc


def _group_ones(n, group):
    r = lax.broadcasted_iota(jnp.int32, (n, n), 0) // group
    c = lax.broadcasted_iota(jnp.int32, (n, n), 1) // group
    return (r == c).astype(BF16)


def _group_mean_square(x, group):
    n = x.shape[-1]
    return _split_dot(x * x, _group_ones(n, group), 2) * (1.0 / group)


def _group_rms(x, g_row, group):
    return x * lax.rsqrt(_group_mean_square(x, group) + RMS_EPS) * g_row


def _pair_swap(x):
    lane = lax.broadcasted_iota(jnp.int32, x.shape, 1)
    return jnp.where(lane % 2 == 0, pltpu.roll(x, LANES - 1, 1), pltpu.roll(x, 1, 1))


def _rope(x, cos, sin):
    blocks = []
    for j in range(x.shape[-1] // LANES):
        blk = x[:, j * LANES:(j + 1) * LANES]
        blocks.append(blk * cos + _pair_swap(blk) * sin)
    return blocks[0] if len(blocks) == 1 else jnp.concatenate(blocks, axis=-1)


def _silu(x):
    return x * jax.nn.sigmoid(x)


def _layernorm(x, g, b):
    mu = jnp.mean(x, axis=-1, keepdims=True)
    xc = x - mu
    var = jnp.mean(xc * xc, axis=-1, keepdims=True)
    return xc * lax.rsqrt(var + LN_EPS) * g + b


MOD_ROWS = 8


def _mod_kernel(cctx_ref, c_ref, w_ref, b_ref, o_ref, s_ref):
    n_req = c_ref.shape[0]
    s_ref[...] = jnp.zeros_like(s_ref)
    s_ref[0:1, :] = _silu(cctx_ref[...])
    s_ref[1:1 + n_req, :] = _silu(c_ref[...])
    layer = pl.program_id(0)
    res = _dot(s_ref[...].astype(BF16), w_ref[0].astype(BF16)) + b_ref[pl.ds(layer, 1), :]
    for r in range(MOD_ROWS):
        o_ref[r] = res[r:r + 1, :]


def _modulation(c_ctx, c, w_ada, b_ada):
    depth, d, n = w_ada.shape
    tn = 1536
    return pl.pallas_call(
        _mod_kernel,
        grid=(depth, n // tn),
        in_specs=[pl.BlockSpec((1, d), lambda l, j: (0, 0)),
                  pl.BlockSpec(c.shape, lambda l, j: (0, 0)),
                  pl.BlockSpec((1, d, tn), lambda l, j: (l, 0, j)),
                  pl.BlockSpec((depth, tn), lambda l, j: (0, j))],
        out_specs=pl.BlockSpec((MOD_ROWS, 1, tn), lambda l, j: (l, 0, j)),
        out_shape=jax.ShapeDtypeStruct((depth * MOD_ROWS, 1, n), F32),
        scratch_shapes=[pltpu.VMEM((MOD_ROWS, d), F32)],
        compiler_params=_params(2),
        name="adaln_modulation",
    )(c_ctx.reshape(1, d), c, w_ada, b_ada)


def _in_proj_kernel(*refs, li, d, latent, n_alias):
    refs = list(refs)
    x_ref, mod_ref, w_ref, lbf_ref, lbb_ref, qn_ref, kn_ref = refs[:7]
    pos = 7
    if latent:
        cos, sin = refs[pos][...], refs[pos + 1][...]
        pos += 2
    pos += n_alias
    qa_o, hq_o, ff_o, fb_o, hv_o, hg_o, qc_o = refs[pos:pos + 7]
    if latent:
        ka_o, va_o, kc_o, vc_o = refs[pos + 7:pos + 11]
    else:
        ka_o, va_rows_o, kct_o, vct_o = refs[pos + 7:pos + 11]

    def store_kv(ref, val):
        for slot in range(ref.shape[1]):
            ref[0, slot] = val.astype(ref.dtype)

    mix_a, mix_b, mix_c = d // 2, d // 4, d // 4
    kv_c = mix_c // 2
    scale = HEAD_DIM ** -0.5 * math.log2(math.e)

    shift = mod_ref[0, :, 0:d]
    gain = mod_ref[0, :, d:2 * d]
    h = x_ref[...] * (1.0 + gain) + shift

    def proj(start, width):
        return _dot(h, w_ref[:, start:start + width])

    off_b = 3 * mix_a
    off_c = off_b + 5 * mix_b

    zq = proj(off_c, mix_c)
    zk = proj(off_c + mix_c, kv_c)
    vc = proj(off_c + mix_c + kv_c, kv_c)
    qa = proj(0, mix_a)
    msq = _group_mean_square(zq, HEAD_DIM)
    msk = _group_mean_square(zk, HEAD_DIM)
    ka = proj(mix_a, mix_a)
    va = proj(2 * mix_a, mix_a)
    qc = zq * lax.rsqrt(msq + RMS_EPS) * qn_ref[li:li + 1, :]
    kc = zk * lax.rsqrt(msk + RMS_EPS) * kn_ref[li:li + 1, :]

    if latent:
        qa = _rope(qa, cos, sin)
        ka = _rope(ka, cos, sin)
    qa_o[...] = (qa * scale).astype(qa_o.dtype)
    if latent:
        store_kv(ka_o, ka)
        store_kv(va_o, va)
    else:
        store_kv(ka_o, ka.T)
        heads = mix_a // LANES
        for slot in range(va_rows_o.shape[1]):
            for hd in range(heads):
                va_rows_o[0, slot, pl.ds(hd, ROW_TILE, stride=heads), :] = (
                    va[:, hd * LANES:(hd + 1) * LANES])

    def lower_bound(ref):
        logits = ref[...]
        e = jnp.exp(logits - jnp.max(logits, axis=0, keepdims=True))
        sm = e / jnp.sum(e, axis=0, keepdims=True)
        return jnp.sum(sm[0:li + 1], axis=0, keepdims=True) - sm[0:1]

    def forget(x, lb):
        return jnp.maximum(lb + (1.0 - lb) * jax.nn.sigmoid(x), F_MIN)

    off = off_b
    zb = [proj(off + j * mix_b, mix_b) for j in range(5)]

    if latent:
        kc = _rope(kc, cos, sin)
        qc = _rope(qc, cos, sin)
        store_kv(kc_o, kc)
        store_kv(vc_o, vc)
    else:
        store_kv(kct_o, kc.T)
        store_kv(vct_o, vc.T)
    qc = qc * scale
    lane = lax.broadcasted_iota(jnp.int32, (1, LANES), 1)
    for n in range(2):
        blk = qc[:, n * LANES:(n + 1) * LANES]
        in_half = (lane // HEAD_DIM) == n
        for g in range(2):
            src = blk if g == n else pltpu.roll(blk, HEAD_DIM, 1)
            hc = 2 * n + g
            qc_o[:, hc * LANES:(hc + 1) * LANES] = jnp.where(in_half, src, 0.0).astype(qc_o.dtype)

    hq_o[0] = _silu(zb[0])
    ff_o[0] = forget(zb[1], lower_bound(lbf_ref))
    fb_o[0] = forget(zb[2], lower_bound(lbb_ref))
    hv_o[0] = zb[3]
    hg_o[0] = _silu(zb[4])


def _in_proj(x, mod, mod_row, w_in, lb_f, lb_b, qn, kn, rope, kv_prev, *, li):
    bsz, t, d = x.shape
    latent = rope is not None
    depth, _, n_in = w_in.shape
    tiles = t // ROW_TILE
    mix_a, mix_b, mix_c = d // 2, d // 4, d // 4
    kv_c = mix_c // 2
    x2 = x.reshape(bsz * t, d)

    row = lambda i: (i, 0)
    brow = lambda i: (i // tiles, i % tiles, 0)
    const = lambda i: (0, 0)
    in_specs = [pl.BlockSpec((ROW_TILE, d), row),
                pl.BlockSpec((1, 1, mod.shape[-1]), lambda i: (mod_row(i * ROW_TILE), 0, 0)),
                pl.BlockSpec((None, d, n_in), lambda i: (li, 0, 0)),
                pl.BlockSpec(lb_f.shape, const), pl.BlockSpec(lb_b.shape, const),
                pl.BlockSpec(qn.shape, const), pl.BlockSpec(kn.shape, const)]
    args = [x2, mod, w_in, lb_f, lb_b, qn, kn]
    if latent:
        in_specs += [pl.BlockSpec((ROW_TILE, LANES), lambda i: (i % tiles, 0))] * 2
        args += list(rope)
        slots, slot0 = 1, 0
    else:
        slots, slot0 = (depth, 0) if kv_prev is None else (1, li)
    aliases = {}
    if kv_prev is not None:
        for j, buf in enumerate(kv_prev):
            aliases[len(args)] = 7 + j
            in_specs.append(pl.BlockSpec(memory_space=pl.ANY))
            args.append(buf)
    krow = lambda i: (i // tiles, slot0, i % tiles, 0)
    kcol = lambda i: (i // tiles, slot0, 0, i % tiles)
    heads = mix_a // LANES

    out_specs = [pl.BlockSpec((ROW_TILE, mix_a), row)]
    out_shape = [jax.ShapeDtypeStruct((bsz * t, mix_a), BF16)]
    out_specs += [pl.BlockSpec((1, ROW_TILE, mix_b), brow)] * 5
    out_shape += [jax.ShapeDtypeStruct((bsz, t, mix_b), F32)] * 5
    out_specs.append(pl.BlockSpec((ROW_TILE, 2 * mix_c), row))
    out_shape.append(jax.ShapeDtypeStruct((bsz * t, 2 * mix_c), BF16))
    if latent:
        for width in (mix_a, mix_a, kv_c, kv_c):
            out_specs.append(pl.BlockSpec((1, 1, ROW_TILE, width), krow))
            out_shape.append(jax.ShapeDtypeStruct((bsz, 1, t, width), BF16))
    else:
        out_specs += [pl.BlockSpec((1, slots, mix_a, ROW_TILE), kcol),
                      pl.BlockSpec((1, slots, ROW_TILE * heads, LANES), krow),
                      pl.BlockSpec((1, slots, kv_c, ROW_TILE), kcol),
                      pl.BlockSpec((1, slots, kv_c, ROW_TILE), kcol)]
        out_shape += [jax.ShapeDtypeStruct((bsz, depth, mix_a, t), F32),
                      jax.ShapeDtypeStruct((bsz, depth, t * heads, LANES), F32),
                      jax.ShapeDtypeStruct((bsz, depth, kv_c, t), F32),
                      jax.ShapeDtypeStruct((bsz, depth, kv_c, t), F32)]
    return pl.pallas_call(
        functools.partial(_in_proj_kernel, li=li, d=d, latent=latent, n_alias=len(aliases)),
        grid=(bsz * tiles,),
        in_specs=in_specs, out_specs=out_specs, out_shape=out_shape,
        input_output_aliases=aliases,
        compiler_params=_params(1),
        name="in_proj_latent" if latent else "in_proj_context",
    )(*args)


def _softmax_parts(scores):
    m = functools.reduce(jnp.maximum, [jnp.max(s, axis=-1, keepdims=True) for s in scores])
    es = [jnp.exp2(s - m) for s in scores]
    denom = functools.reduce(lambda a, b: a + b, [jnp.sum(e, axis=-1, keepdims=True) for e in es])
    return es, 1.0 / denom


def _score_blocks(q, keys):
    return [_dot(q, k) if k_t else lax.dot_general(q, k, NT, preferred_element_type=F32)
            for k, k_t in keys]


def _diff_lambda(lq1, lk1, lq2, lk2, li):
    lam_init = 0.8 - 0.6 * math.exp(-0.3 * li)

    def lam_term(a, b):
        return jnp.exp(jnp.sum(a[li:li + 1, :] * b[li:li + 1, :], axis=-1, keepdims=True))

    return lam_term(lq1, lk1) - lam_term(lq2, lk2) + lam_init, lam_init


def _diff_head_gen(q, keys, vals, lam, gain, store):
    tq = q.shape[0]
    lane = lax.broadcasted_iota(jnp.int32, (1, LANES), 1)
    zero = jnp.zeros_like(q)
    q2 = jnp.concatenate([jnp.where(lane < HEAD_DIM, q, zero),
                          jnp.where(lane >= HEAD_DIM, q, zero)], axis=0)
    scores = _score_blocks(q2, keys)
    yield
    es, r = _softmax_parts(scores)
    yield
    r0 = r[0:tq]
    r1 = r[tq:2 * tq] * lam
    o = None
    for e, v in zip(es, vals):
        part = _dot((e[0:tq] * r0 - e[tq:2 * tq] * r1).astype(BF16), v)
        o = part if o is None else o + part
    yield
    ms = jnp.mean(o * o, axis=-1, keepdims=True)
    store(o * lax.rsqrt(ms + RMS_EPS) * gain)


def _diff_attn_kernel(*refs, li, cached, heads, v_rows):
    if cached:
        (q_ref, k_ref, v_ref, ck_ref, cv_ref, lq1, lk1, lq2, lk2, sub_ref, o_ref) = refs
    else:
        (q_ref, k_ref, v_ref, lq1, lk1, lq2, lk2, sub_ref, o_ref) = refs
    lam, lam_init = _diff_lambda(lq1, lk1, lq2, lk2, li)
    gain = sub_ref[li:li + 1, :] * (1.0 - lam_init)

    def head_chain(j):
        c = slice(j * LANES, (j + 1) * LANES)
        head = pl.program_id(1) * heads + j
        if v_rows:
            t = k_ref.shape[3]
            keys = [(k_ref[0, 0, c, :].astype(BF16), True)]
            vals = [v_ref[0, 0, pl.ds(head, t, stride=v_ref.shape[2] // t), :].astype(BF16)]
        else:
            keys = [(k_ref[0, 0, :, c].astype(BF16), False)]
            vals = [v_ref[0, 0, :, c].astype(BF16)]
        if cached:
            past = ck_ref.shape[-1]
            all_heads = cv_ref.shape[2] // past
            keys.append((ck_ref[0, 0, c, :].astype(BF16), True))
            vals.append(cv_ref[0, 0, pl.ds(head, past, stride=all_heads), :].astype(BF16))

        def store(o):
            o_ref[:, c] = o.astype(o_ref.dtype)

        return _diff_head_gen(q_ref[:, c], keys, vals, lam, gain, store)

    _run_interleaved([head_chain(j) for j in range(heads)])


def _diff_attn(q, k, v, cache, lam, subln, *, li, kv_layer, bsz, tq, heads_per_step, v_rows):
    width = q.shape[-1]
    t = k.shape[3] if v_rows else k.shape[2]
    wstep = heads_per_step * LANES
    nq = t // tq
    cached = cache is not None
    if v_rows:
        kv_specs = [pl.BlockSpec((1, 1, wstep, t), lambda b, h, i: (b, kv_layer, h, 0)),
                    pl.BlockSpec((1, 1) + v.shape[2:], lambda b, h, i: (b, kv_layer, 0, 0))]
    else:
        kv_specs = [pl.BlockSpec((1, 1, t, wstep), lambda b, h, i: (b, kv_layer, 0, h))] * 2
    in_specs = [pl.BlockSpec((tq, wstep), lambda b, h, i: (b * nq + i, h))] + kv_specs
    args = [q, k, v]
    if cached:
        ck, cv = cache
        in_specs += [pl.BlockSpec((1, 1, wstep, ck.shape[-1]), lambda b, h, i: (b, li, h, 0)),
                     pl.BlockSpec((1, 1) + cv.shape[2:], lambda b, h, i: (b, li, 0, 0))]
        args += [ck, cv]
    in_specs += [pl.BlockSpec(a.shape, lambda b, h, i: (0, 0)) for a in (*lam, subln)]
    args += [*lam, subln]
    return pl.pallas_call(
        functools.partial(_diff_attn_kernel, li=li, cached=cached, heads=heads_per_step,
                          v_rows=v_rows),
        grid=(bsz, width // wstep, nq),
        in_specs=in_specs,
        out_specs=pl.BlockSpec((tq, wstep), lambda b, h, i: (b * nq + i, h)),
        out_shape=jax.ShapeDtypeStruct((bsz * t, width), BF16),
        compiler_params=_params(3),
        name="diff_attn_latent" if cached else "diff_attn_context",
    )(*args)


def _gqa_group_gen(n, q, keys, vals, store):
    tq = q.shape[0] // 2
    scores = _score_blocks(q, keys)
    yield
    es, r = _softmax_parts(scores)
    yield
    o = None
    for e, (v, v_t) in zip(es, vals):
        p = e.astype(BF16)
        part = lax.dot_general(p, v, NT, preferred_element_type=F32) if v_t else _dot(p, v)
        o = part if o is None else o + part
    yield
    o = o * r
    first = o[0:tq]
    second = o[tq:2 * tq]
    if n == 0:
        second = pltpu.roll(second, HEAD_DIM, 1)
    else:
        first = pltpu.roll(first, HEAD_DIM, 1)
    lane = lax.broadcasted_iota(jnp.int32, (1, LANES), 1)
    store(jnp.where(lane < HEAD_DIM, first, second))


def _gqa_kernel(*refs, cached, own_t):
    if cached:
        q_ref, k_ref, v_ref, ck_ref, cv_ref, o_ref = refs
    else:
        q_ref, k_ref, v_ref, o_ref = refs
    heads = q_ref.shape[1] // LANES
    keys = [(k_ref[0, 0].astype(BF16), own_t)]
    vals = [(v_ref[0, 0].astype(BF16), own_t)]
    if cached:
        keys.append((ck_ref[0, 0].astype(BF16), True))
        vals.append((cv_ref[0, 0].astype(BF16), True))

    def group_chain(n):
        q = jnp.concatenate([q_ref[:, (2 * n + g) * LANES:(2 * n + g + 1) * LANES]
                             for g in range(2)], axis=0)

        def store(o):
            o_ref[:, n * LANES:(n + 1) * LANES] = o.astype(o_ref.dtype)

        return _gqa_group_gen(n, q, keys, vals, store)

    _run_interleaved([group_chain(n) for n in range(heads // 2)])


def _gqa(q, k, v, cache, *, li, kv_layer, bsz, tq, own_t):
    t, kvw = (k.shape[3], k.shape[2]) if own_t else (k.shape[2], k.shape[3])
    nq = t // tq
    cached = cache is not None
    in_specs = [pl.BlockSpec((tq, q.shape[-1]), lambda b, i: (b * nq + i, 0)),
                pl.BlockSpec((1, 1) + k.shape[2:], lambda b, i: (b, kv_layer, 0, 0)),
                pl.BlockSpec((1, 1) + k.shape[2:], lambda b, i: (b, kv_layer, 0, 0))]
    args = [q, k, v]
    if cached:
        in_specs += [pl.BlockSpec((1, 1) + cache[0].shape[2:], lambda b, i: (b, li, 0, 0))] * 2
        args += list(cache)
    return pl.pallas_call(
        functools.partial(_gqa_kernel, cached=cached, own_t=own_t),
        grid=(bsz, nq),
        in_specs=in_specs,
        out_specs=pl.BlockSpec((tq, 2 * kvw), lambda b, i: (b * nq + i, 0)),
        out_shape=jax.ShapeDtypeStruct((bsz * t, 2 * kvw), BF16),
        compiler_params=_params(2),
        name="gqa_latent" if cached else "gqa_context",
    )(*args)


def _head_masks(width):
    lane_head = lax.broadcasted_iota(jnp.int32, (1, width), 1) // HEAD_DIM
    return [lane_head == h for h in range(width // HEAD_DIM)]


def _stack_heads(x, masks):
    return jnp.concatenate([jnp.where(m, x, jnp.zeros_like(x)) for m in masks], axis=0)


def _block_diag_mask(width):
    r = lax.broadcasted_iota(jnp.int32, (width, width), 0) // HEAD_DIM
    c = lax.broadcasted_iota(jnp.int32, (width, width), 1) // HEAD_DIM
    return r == c


def _ref_rows(b, offsets, span):
    width = b.shape[-1]
    return jnp.concatenate([jnp.broadcast_to(b[o:o + 1], (span, width)) for o in offsets], axis=0)


def _run_interleaved(gens, delays=None):
    live = list(zip(gens, delays or [0] * len(gens)))
    rnd = 0
    while live:
        for item in list(live):
            if item[1] > rnd:
                continue
            try:
                next(item[0])
            except StopIteration:
                live.remove(item)
        rnd += 1


def _hgrn_chunks(problems):
    out = []
    _run_interleaved([_hgrn_chunks_gen(problems, out)])
    return out


def _hgrn_chunks_gen(problems, out):
    n = len(problems)
    c, width = problems[0][0].shape
    qs = [p[0] for p in problems]
    vs = [p[2] for p in problems]
    sts = [p[3] for p in problems]
    rev = [p[4] for p in problems]
    chains = range(n)
    masks = _head_masks(width)
    trow =lax.broadcasted_iota(jnp.int32, (c, 1), 0)
    t_full = lax.broadcasted_iota(jnp.int32, (c, width), 0)
    s_full = lax.broadcasted_iota(jnp.int32, (c, width), 1) % c

    ks = [1.0 - p[1] for p in problems]
    b = [jnp.log(p[1]) for p in problems]
    step = 1
    while step < c:
        for j in chains:
            if rev[j]:
                b[j] = b[j] + jnp.where(trow < c - step, pltpu.roll(b[j], c - step, 0), 0.0)
            else:
                b[j] = b[j] + jnp.where(trow >= step, pltpu.roll(b[j], step, 0), 0.0)
        step *= 2
        yield
    b_end = [b[j][0:1] if rev[j] else b[j][c - 1:c] for j in chains]

    o = [lax.dot_general((qs[j] * jnp.exp(b[j])).astype(BF16), sts[j].astype(BF16), NT,
                         preferred_element_type=F32) for j in chains]
    yield

    a = [None] * n
    m = c // 2
    while m >= DIAG_BLOCK:
        blocks = c // (2 * m)
        same = (t_full // (2 * m)) == (s_full // (2 * m))
        for j in chains:
            ref = _ref_rows(b[j], [i * 2 * m + (m if rev[j] else m - 1) for i in range(blocks)],
                            2 * m)
            is_q = ((trow % (2 * m)) < m) if rev[j] else ((trow % (2 * m)) >= m)
            e = jnp.exp(jnp.where(is_q, b[j] - ref, ref - b[j]))
            ql = jnp.where(is_q, qs[j] * e, 0.0).astype(BF16)
            kl = jnp.where(is_q, 0.0, ks[j] * e).astype(BF16)
            al = lax.dot_general(ql, _stack_heads(kl, masks), NT, preferred_element_type=F32)
            if blocks > 1:
                al = jnp.where(same, al, 0.0)
            a[j] = al if a[j] is None else a[j] + al
        m //= 2
        yield
    blocks = c // DIAG_BLOCK
    mid = DIAG_BLOCK // 2
    same = (t_full // DIAG_BLOCK) == (s_full // DIAG_BLOCK)
    for j in chains:
        ref = _ref_rows(b[j], [i * DIAG_BLOCK + (mid if rev[j] else mid - 1) for i in range(blocks)],
                        DIAG_BLOCK)
        d = b[j] - ref
        ql = (qs[j] * jnp.exp(d)).astype(BF16)
        kl = (ks[j] * jnp.exp(-d)).astype(BF16)
        al = lax.dot_general(ql, _stack_heads(kl, masks), NT, preferred_element_type=F32)
        causal = (s_full >= t_full) if rev[j] else (s_full <= t_full)
        a[j] = a[j] + jnp.where(same & causal, al, 0.0)
    yield

    v_b = [v.astype(BF16) for v in vs]
    o = [o[j] + _dot(a[j].astype(BF16), _stack_heads(v_b[j], masks)) for j in chains]
    yield

    bd = _block_diag_mask(width)
    upd = [lax.dot_general(v_b[j], (ks[j] * jnp.exp(b_end[j] - b[j])).astype(BF16), TN,
                           preferred_element_type=F32) for j in chains]
    st_new = [sts[j] * jnp.exp(b_end[j]) + jnp.where(bd, upd[j], 0.0) for j in chains]
    out.extend(zip(o, st_new))


def _mxu_transpose(x):
    n = x.shape[1]
    r = lax.broadcasted_iota(jnp.int32, (n, n), 0)
    c = lax.broadcasted_iota(jnp.int32, (n, n), 1)
    eye = (r == c).astype(BF16)
    acc = None
    rem = x
    for _ in range(3):
        piece = rem.astype(BF16)
        rem = rem - piece.astype(F32)
        part = lax.dot_general(eye, piece, NT, preferred_element_type=F32)
        acc = part if acc is None else acc + part
    return acc


def _hgrn_kernel(*refs, li, has_state, want_state, n_alias, heads):
    refs = list(refs)
    q_ref, ff_ref, fb_ref, v_ref, gate_ref, gn_ref = refs[:6]
    pos = 6
    if has_state:
        s0f_ref, s0b_ref = refs[pos:pos + 2]
        pos += 2
    pos += n_alias
    o_ref = refs[pos]
    pos += 1
    if want_state:
        sf_ref, sb_ref = refs[pos:pos + 2]
        pos += 2
    st_ref, of_ref, ob_ref = refs[pos:pos + 3]
    nb, t, width = q_ref.shape
    nc = t // CHUNK
    bd = _block_diag_mask(width)

    for n in range(nb):
        for d in range(2):
            if has_state:
                x = (s0b_ref if d else s0f_ref)[n, 0].reshape(width, HEAD_DIM)
                xt = _mxu_transpose(x)
                st_ref[2 * n + d] = jnp.where(bd, jnp.concatenate([xt] * heads, axis=0), 0.0)
            else:
                st_ref[2 * n + d] = jnp.zeros((width, width), F32)

    def body(ci, carry):
        rows = (pl.ds(pl.multiple_of(ci * CHUNK, CHUNK), CHUNK),
                pl.ds(pl.multiple_of((nc - 1 - ci) * CHUNK, CHUNK), CHUNK))
        loaded = []
        for n in range(nb):
            for d, f_ref in enumerate((ff_ref, fb_ref)):
                r = rows[d]
                loaded.append((q_ref[n, r, :], f_ref[n, r, :], v_ref[n, r, :], st_ref[2 * n + d],
                               bool(d)))
        for j, (o, st) in enumerate(_hgrn_chunks(loaded)):
            n, d = divmod(j, 2)
            (ob_ref if d else of_ref)[n, rows[d], :] = o
            st_ref[j] = st
        return carry

    lax.fori_loop(0, nc, body, 0)

    for n in range(nb):
        o = of_ref[n] + ob_ref[n]
        o_ref[n] = (_group_rms(o, gn_ref[li:li + 1, :], HEAD_DIM) * gate_ref[n]).astype(o_ref.dtype)

    if want_state:
        for n in range(nb):
            for d, dst in enumerate((sf_ref, sb_ref)):
                st = st_ref[2 * n + d]
                rows = st[0:HEAD_DIM]
                for h in range(1, heads):
                    rows = rows + st[h * HEAD_DIM:(h + 1) * HEAD_DIM]
                final = _mxu_transpose(rows).reshape(heads, HEAD_DIM, HEAD_DIM)
                for slot in range(dst.shape[1]):
                    dst[n, slot] = final


def _hgrn(hq, ff, fb, hv, hg, gn, state, state_prev, *, li, depth, want_state, nb):
    bsz, t, width = hq.shape
    heads = width // HEAD_DIM
    has_state = state is not None
    seq = pl.BlockSpec((nb, t, width), lambda b: (b, 0, 0))
    in_specs = [seq] * 5 + [pl.BlockSpec(gn.shape, lambda b: (0, 0))]
    args = [hq, ff, fb, hv, hg, gn]
    if has_state:
        in_specs += [pl.BlockSpec((nb, 1, heads, HEAD_DIM, HEAD_DIM), lambda b: (b, li, 0, 0, 0))] * 2
        args += list(state)
    aliases = {}
    if state_prev is not None:
        for j, buf in enumerate(state_prev):
            aliases[len(args)] = 1 + j
            in_specs.append(pl.BlockSpec(memory_space=pl.ANY))
            args.append(buf)
    out_specs = [seq]
    out_shape = [jax.ShapeDtypeStruct((bsz, t, width), BF16)]
    if want_state:
        slots, slot0 = (depth, 0) if state_prev is None else (1, li)
        out_specs += [pl.BlockSpec((nb, slots, heads, HEAD_DIM, HEAD_DIM),
                                   lambda b: (b, slot0, 0, 0, 0))] * 2
        out_shape += [jax.ShapeDtypeStruct((bsz, depth, heads, HEAD_DIM, HEAD_DIM), F32)] * 2
    return pl.pallas_call(
        functools.partial(_hgrn_kernel, li=li, has_state=has_state, want_state=want_state,
                          n_alias=len(aliases), heads=heads),
        grid=(bsz // nb,),
        in_specs=in_specs, out_specs=out_specs, out_shape=out_shape,
        input_output_aliases=aliases,
        scratch_shapes=[pltpu.VMEM((2 * nb, width, width), F32),
                        pltpu.VMEM((nb, t, width), F32), pltpu.VMEM((nb, t, width), F32)],
        compiler_params=_params(1),
        name="hgrn2_latent" if has_state else "hgrn2_context",
    )(*args)


ATTENTION_DELAY = 5


def _mixer_ctx_kernel(*refs, li, n_alias):
    refs = list(refs)
    (qa_ref, kat_ref, va_ref, qc_ref, kct_ref, vct_ref, hq_ref, ff_ref, fb_ref, hv_ref, hg_ref,
     gn_ref, lq1, lk1, lq2, lk2, sub_ref) = refs[:17]
    pos = 17 + n_alias
    oa_ref, oc_ref, ob_ref, sf_ref, sb_ref = refs[pos:pos + 5]
    st_ref, of_ref, obk_ref = refs[pos + 5:pos + 8]
    nb, t, width = hq_ref.shape
    nc = t // CHUNK
    heads_b = width // HEAD_DIM
    heads_a = qa_ref.shape[1] // LANES
    groups_c = qc_ref.shape[1] // LANES // 2
    lam, lam_init = _diff_lambda(lq1, lk1, lq2, lk2, li)
    gain = sub_ref[li:li + 1, :] * (1.0 - lam_init)

    for j in range(2 * nb):
        st_ref[j] = jnp.zeros((width, width), F32)

    def diff_unit(n, h):
        rows = slice(n * t, (n + 1) * t)
        c = slice(h * LANES, (h + 1) * LANES)
        keys = [(kat_ref[n, 0, c, :].astype(BF16), True)]
        vals = [va_ref[n, 0, pl.ds(h, t, stride=heads_a), :].astype(BF16)]

        def store(o):
            oa_ref[rows, c] = o.astype(oa_ref.dtype)

        yield from _diff_head_gen(qa_ref[rows, c], keys, vals, lam, gain, store)

    def gqa_unit(n, g):
        rows = slice(n * t, (n + 1) * t)
        q = jnp.concatenate([qc_ref[rows, (2 * g + j) * LANES:(2 * g + j + 1) * LANES]
                             for j in range(2)], axis=0)
        keys = [(kct_ref[n, 0].astype(BF16), True)]
        vals = [(vct_ref[n, 0].astype(BF16), True)]

        def store(o):
            oc_ref[rows, g * LANES:(g + 1) * LANES] = o.astype(oc_ref.dtype)

        yield from _gqa_group_gen(g, q, keys, vals, store)

    def scan_step(ci):
        rows = (slice(ci * CHUNK, (ci + 1) * CHUNK), slice((nc - 1 - ci) * CHUNK, (nc - ci) * CHUNK))
        loaded = []
        for n in range(nb):
            for d, f_ref in enumerate((ff_ref, fb_ref)):
                r = rows[d]
                loaded.append((hq_ref[n, r, :], f_ref[n, r, :], hv_ref[n, r, :], st_ref[2 * n + d],
                               bool(d)))
        out = []
        yield from _hgrn_chunks_gen(loaded, out)
        for j, (o, st) in enumerate(out):
            n, d = divmod(j, 2)
            (obk_ref if d else of_ref)[n, rows[d], :] = o
            st_ref[j] = st

    units = []
    for n in range(nb):
        units += [diff_unit(n, h) for h in range(heads_a)]
        units += [gqa_unit(n, g) for g in range(groups_c)]
    share = -(-len(units) // nc)
    for ci in range(nc):
        mine = units[ci * share:(ci + 1) * share]
        _run_interleaved([scan_step(ci)] + mine, [0] + [ATTENTION_DELAY] * len(mine))

    for n in range(nb):
        o = of_ref[n] + obk_ref[n]
        ob_ref[n] = (_group_rms(o, gn_ref[li:li + 1, :], HEAD_DIM) * hg_ref[n]).astype(ob_ref.dtype)
        for d, dst in enumerate((sf_ref, sb_ref)):
            st = st_ref[2 * n + d]
            rows = st[0:HEAD_DIM]
            for h in range(1, heads_b):
                rows = rows + st[h * HEAD_DIM:(h + 1) * HEAD_DIM]
            final = _mxu_transpose(rows).reshape(heads_b, HEAD_DIM, HEAD_DIM)
            for slot in range(dst.shape[1]):
                dst[n, slot] = final


def _mixer_ctx(qa, ka_t, va_rows, qc, kc_t, vc_t, hq, ff, fb, hv, hg, gn, lam, subln, state_prev,
               *, li, depth, nb):
    bsz, t, width = hq.shape
    heads_b = width // HEAD_DIM
    rows = lambda b: (b, 0)
    at_layer = lambda b: (b, li, 0, 0)
    seq = pl.BlockSpec((nb, t, width), lambda b: (b, 0, 0))
    const = lambda b: (0, 0)
    in_specs = [pl.BlockSpec((nb * t, qa.shape[1]), rows),
                pl.BlockSpec((nb, 1) + ka_t.shape[2:], at_layer),
                pl.BlockSpec((nb, 1) + va_rows.shape[2:], at_layer),
                pl.BlockSpec((nb * t, qc.shape[1]), rows),
                pl.BlockSpec((nb, 1) + kc_t.shape[2:], at_layer),
                pl.BlockSpec((nb, 1) + vc_t.shape[2:], at_layer),
                seq, seq, seq, seq, seq, pl.BlockSpec(gn.shape, const)]
    in_specs += [pl.BlockSpec(a.shape, const) for a in (*lam, subln)]
    args = [qa, ka_t, va_rows, qc, kc_t, vc_t, hq, ff, fb, hv, hg, gn, *lam, subln]
    aliases = {}
    if state_prev is not None:
        for j, buf in enumerate(state_prev):
            aliases[len(args)] = 3 + j
            in_specs.append(pl.BlockSpec(memory_space=pl.ANY))
            args.append(buf)
    slots, slot0 = (depth, 0) if state_prev is None else (1, li)
    state_spec = pl.BlockSpec((nb, slots, heads_b, HEAD_DIM, HEAD_DIM), lambda b: (b, slot0, 0, 0, 0))
    out_specs = [pl.BlockSpec((nb * t, qa.shape[1]), rows),
                 pl.BlockSpec((nb * t, qc.shape[1] // 2), rows), seq, state_spec, state_spec]
    out_shape = [jax.ShapeDtypeStruct(qa.shape, BF16),
                 jax.ShapeDtypeStruct((qc.shape[0], qc.shape[1] // 2), BF16),
                 jax.ShapeDtypeStruct((bsz, t, width), BF16)]
    out_shape += [jax.ShapeDtypeStruct((bsz, depth, heads_b, HEAD_DIM, HEAD_DIM), F32)] * 2
    return pl.pallas_call(
        functools.partial(_mixer_ctx_kernel, li=li, n_alias=len(aliases)),
        grid=(bsz // nb,),
        in_specs=in_specs, out_specs=out_specs, out_shape=out_shape,
        input_output_aliases=aliases,
        scratch_shapes=[pltpu.VMEM((2 * nb, width, width), F32),
                        pltpu.VMEM((nb, t, width), F32), pltpu.VMEM((nb, t, width), F32)],
        compiler_params=_params(1),
        name="mixers_context",
    )(*args)


def _mixer_lat_kernel(qa_ref, k_ref, v_ref, ck_ref, cv_ref, qc_ref, kc_ref, vc_ref, cck_ref, ccv_ref,
                      hq_ref, ff_ref, fb_ref, hv_ref, hg_ref, gn_ref, s0f_ref, s0b_ref,
                      lq1, lk1, lq2, lk2, sub_ref, oa_ref, oc_ref, ob_ref,
                      st_ref, of_ref, obk_ref, *, li, tq):
    _, t, width = hq_ref.shape
    nc = t // CHUNK
    nq = t // tq
    steps_per_trip = nc // nq
    heads_b = width // HEAD_DIM
    heads_a = qa_ref.shape[1] // LANES
    groups_c = qc_ref.shape[1] // LANES // 2
    past = ck_ref.shape[-1]
    lam, lam_init = _diff_lambda(lq1, lk1, lq2, lk2, li)
    gain = sub_ref[li:li + 1, :] * (1.0 - lam_init)
    bd = _block_diag_mask(width)

    for d, src in enumerate((s0f_ref, s0b_ref)):
        x = src[0, 0].reshape(width, HEAD_DIM)
        xt = _mxu_transpose(x)
        st_ref[d] = jnp.where(bd, jnp.concatenate([xt] * heads_b, axis=0), 0.0)

    def trip(qt, carry):
        q_rows = pl.ds(pl.multiple_of(qt * tq, tq), tq)

        def diff_unit(h):
            c = slice(h * LANES, (h + 1) * LANES)
            keys = [(k_ref[0, 0, :, c], False), (ck_ref[0, 0, c, :].astype(BF16), True)]
            vals = [v_ref[0, 0, :, c], cv_ref[0, 0, pl.ds(h, past, stride=heads_a), :].astype(BF16)]

            def store(o):
                oa_ref[q_rows, c] = o.astype(oa_ref.dtype)

            yield from _diff_head_gen(qa_ref[q_rows, c], keys, vals, lam, gain, store)

        def gqa_unit(g):
            q = jnp.concatenate([qc_ref[q_rows, (2 * g + j) * LANES:(2 * g + j + 1) * LANES]
                                 for j in range(2)], axis=0)
            keys = [(kc_ref[0, 0], False), (cck_ref[0, 0].astype(BF16), True)]
            vals = [(vc_ref[0, 0], False), (ccv_ref[0, 0].astype(BF16), True)]

            def store(o):
                oc_ref[q_rows, g * LANES:(g + 1) * LANES] = o.astype(oc_ref.dtype)

            yield from _gqa_group_gen(g, q, keys, vals, store)

        def scan_step(cj):
            ci = qt * steps_per_trip + cj
            rows = (pl.ds(pl.multiple_of(ci * CHUNK, CHUNK), CHUNK),
                    pl.ds(pl.multiple_of((nc - 1 - ci) * CHUNK, CHUNK), CHUNK))
            loaded = [(hq_ref[0, rows[d], :], f_ref[0, rows[d], :], hv_ref[0, rows[d], :],
                       st_ref[d], bool(d)) for d, f_ref in enumerate((ff_ref, fb_ref))]
            out = []
            yield from _hgrn_chunks_gen(loaded, out)
            for d, (o, st) in enumerate(out):
                (obk_ref if d else of_ref)[rows[d], :] = o
                st_ref[d] = st

        units = [diff_unit(h) for h in range(heads_a)] + [gqa_unit(g) for g in range(groups_c)]
        for cj in range(steps_per_trip):
            mine = units[cj::steps_per_trip]
            _run_interleaved([scan_step(cj)] + mine, [0] + [ATTENTION_DELAY] * len(mine))
        return carry

    lax.fori_loop(0, nq, trip, 0)

    o = of_ref[...] + obk_ref[...]
    ob_ref[0] = (_group_rms(o, gn_ref[li:li + 1, :], HEAD_DIM) * hg_ref[0]).astype(ob_ref.dtype)


def _mixer_lat(qa, ka, va, qc, kc, vc, cache, hq, ff, fb, hv, hg, gn, state, lam, subln, *, li, tq):
    bsz, t, width = hq.shape
    rows = lambda b: (b, 0)
    own = lambda b: (b, 0, 0, 0)
    at_layer = lambda b: (b, li, 0, 0)
    seq = pl.BlockSpec((1, t, width), lambda b: (b, 0, 0))
    const = lambda b: (0, 0)
    in_specs = [pl.BlockSpec((t, qa.shape[1]), rows),
                pl.BlockSpec((1,) + ka.shape[1:], own), pl.BlockSpec((1,) + va.shape[1:], own),
                pl.BlockSpec((1, 1) + cache[0].shape[2:], at_layer),
                pl.BlockSpec((1, 1) + cache[1].shape[2:], at_layer),
                pl.BlockSpec((t, qc.shape[1]), rows),
                pl.BlockSpec((1,) + kc.shape[1:], own), pl.BlockSpec((1,) + vc.shape[1:], own),
                pl.BlockSpec((1, 1) + cache[2].shape[2:], at_layer),
                pl.BlockSpec((1, 1) + cache[3].shape[2:], at_layer),
                seq, seq, seq, seq, seq, pl.BlockSpec(gn.shape, const)]
    in_specs += [pl.BlockSpec((1, 1) + state[0].shape[2:], lambda b: (b, li, 0, 0, 0))] * 2
    in_specs += [pl.BlockSpec(a.shape, const) for a in (*lam, subln)]
    args = [qa, ka, va, cache[0], cache[1], qc, kc, vc, cache[2], cache[3], hq, ff, fb, hv, hg, gn,
            *state, *lam, subln]
    return pl.pallas_call(
        functools.partial(_mixer_lat_kernel, li=li, tq=tq),
        grid=(bsz,),
        in_specs=in_specs,
        out_specs=[pl.BlockSpec((t, qa.shape[1]), rows),
                   pl.BlockSpec((t, qc.shape[1] // 2), rows), seq],
        out_shape=[jax.ShapeDtypeStruct(qa.shape, BF16),
                   jax.ShapeDtypeStruct((qc.shape[0], qc.shape[1] // 2), BF16),
                   jax.ShapeDtypeStruct((bsz, t, width), BF16)],
        scratch_shapes=[pltpu.VMEM((2, width, width), F32),
                        pltpu.VMEM((t, width), F32), pltpu.VMEM((t, width), F32)],
        compiler_params=_params(1),
        name="mixers_latent",
    )(*args)


def _out_mlp_kernel(x_ref, oa_ref, ob_ref, oc_ref, mod_ref, wo_ref, w1_ref, w2_ref,
                    g1_ref, b1_ref, g2_ref, b2_ref, y_ref, *, li, d, alpha, ff_chunk):
    wa, wb = oa_ref.shape[-1], ob_ref.shape[-1]
    layer = slice(li, li + 1)
    gate1 = mod_ref[0, :, 2 * d:3 * d]
    shift2 = mod_ref[0, :, 3 * d:4 * d]
    gain2 = mod_ref[0, :, 4 * d:5 * d]
    gate2 = mod_ref[0, :, 5 * d:6 * d]
    subs = [slice(s * ROW_TILE, (s + 1) * ROW_TILE) for s in range(x_ref.shape[0] // ROW_TILE)]
    wo = wo_ref[...].astype(BF16)
    m = [_dot(oa_ref[r, :], wo[0:wa, :]) + _dot(ob_ref[r, :], wo[wa:wa + wb, :])
         + _dot(oc_ref[r, :], wo[wa + wb:, :]) for r in subs]
    x1 = [_layernorm(alpha * x_ref[r, :] + gate1 * mi, g1_ref[layer, :], b1_ref[layer, :])
          for r, mi in zip(subs, m)]
    h2 = [(xi * (1.0 + gain2) + shift2).astype(BF16) for xi in x1]
    acc = [None] * len(subs)
    for j in range(w1_ref.shape[-1] // ff_chunk):
        cols = slice(j * ff_chunk, (j + 1) * ff_chunk)
        hid = [jnp.maximum(_dot(hi, w1_ref[:, cols]), 0.0) for hi in h2]
        for s, hd in enumerate(hid):
            part = _dot((hd * hd).astype(BF16), w2_ref[cols, :])
            acc[s] = part if acc[s] is None else acc[s] + part
    for r, xi, ai in zip(subs, x1, acc):
        y_ref[r, :] = _layernorm(alpha * xi + gate2 * ai, g2_ref[layer, :], b2_ref[layer, :])


def _out_mlp(x, oa, ob, oc, mod, mod_row, w_out, w_ff1, w_ff2, ln, *, li, alpha):
    bsz, t, d = x.shape
    rows = 2 * ROW_TILE
    row = lambda i: (i, 0)
    const = lambda i: (0, 0)
    resident = lambda a: pl.BlockSpec((None,) + a.shape[1:], lambda i: (li, 0, 0),
                                      pipeline_mode=pl.Buffered(1))
    in_specs = [pl.BlockSpec((rows, d), row),
                pl.BlockSpec((rows, oa.shape[-1]), row),
                pl.BlockSpec((rows, ob.shape[-1]), row),
                pl.BlockSpec((rows, oc.shape[-1]), row),
                pl.BlockSpec((1, 1, mod.shape[-1]), lambda i: (mod_row(i * rows), 0, 0)),
                resident(w_out), resident(w_ff1), resident(w_ff2)]
    in_specs += [pl.BlockSpec(a.shape, const) for a in ln]
    y = pl.pallas_call(
        functools.partial(_out_mlp_kernel, li=li, d=d, alpha=alpha, ff_chunk=1024),
        grid=(bsz * t // rows,),
        in_specs=in_specs,
        out_specs=pl.BlockSpec((rows, d), row),
        out_shape=jax.ShapeDtypeStruct((bsz * t, d), F32),
        compiler_params=_params(1),
        name="out_mlp",
    )(x.reshape(bsz * t, d), oa, ob.reshape(bsz * t, -1), oc, mod, w_out, w_ff1, w_ff2, *ln)
    return y.reshape(bsz, t, d)


def _rope_tables(n_tokens):
    pairs = HEAD_DIM // 4
    tok = np.arange(n_tokens)
    row = (tok // GRID_W).astype(np.float64)
    col = (tok % GRID_W).astype(np.float64)
    inv = ROPE_THETA ** (-np.arange(pairs, dtype=np.float64) / pairs)
    ang = np.concatenate([row[:, None] * inv, col[:, None] * inv], axis=-1)
    lane = np.arange(LANES)
    pair = (lane % HEAD_DIM) // 2
    sign = np.where(lane % 2 == 0, -1.0, 1.0)
    return (jnp.asarray(np.cos(ang)[:, pair], F32), jnp.asarray(np.sin(ang)[:, pair] * sign, F32))


def kernel(x_prompt, x_sample, cache_a_k, cache_a_v, cache_c_k, cache_c_v, state_b_fwd, state_b_bwd, c, c_ctx, w_ada, b_ada, w_in, w_out, lam_q1, lam_k1, lam_q2, lam_k2, subln_g, lb_logits_fwd, lb_logits_bwd, gnorm_g, qnorm_g, knorm_g, ln1_g, ln1_b, ln2_g, ln2_b, w_ff1, w_ff2):
    depth = w_in.shape[0]
    bsz, seq, d = x_prompt.shape
    dec_bsz, dec_seq, _ = x_sample.shape
    past = cache_a_k.shape[2]
    alpha = (2 * depth) ** 0.25
    mix_a, mix_b, mix_c = d // 2, d // 4, d // 4

    mod = _modulation(c_ctx, c, w_ada, b_ada)
    rope = _rope_tables(dec_seq)

    cache = (cache_a_k.transpose(0, 1, 3, 4, 5, 2).reshape(dec_bsz, depth, mix_a, past),
             cache_a_v.reshape(dec_bsz, depth, past * (mix_a // LANES), LANES),
             cache_c_k.transpose(0, 1, 3, 4, 2).reshape(dec_bsz, depth, mix_c // 2, past),
             cache_c_v.transpose(0, 1, 3, 4, 2).reshape(dec_bsz, depth, mix_c // 2, past))
    lam = (lam_q1, lam_k1, lam_q2, lam_k2)

    weights = (w_in, w_out, w_ff1.astype(BF16), w_ff2.astype(BF16))
    qn = jnp.tile(qnorm_g, (1, mix_c // HEAD_DIM))
    kn = jnp.tile(knorm_g, (1, mix_c // 2 // HEAD_DIM))
    gn = jnp.tile(gnorm_g, (1, mix_b // HEAD_DIM))
    ln = (ln1_g, ln1_b, ln2_g, ln2_b)

    def stream(x, li, latent, own_prev):
        w_in_b, w_out_b, w1_b, w2_b = weights
        n, t, _ = x.shape
        mod_row = ((lambda r0: li * MOD_ROWS + 1 + r0 // t) if latent
                   else (lambda r0: li * MOD_ROWS))
        kv_layer = 0 if latent else li
        (qa, hq, ff, fb, hv, hg, qc, ka, va, kc, vc) = _in_proj(
            x, mod, mod_row, w_in_b, lb_logits_fwd, lb_logits_bwd, qn, kn,
            rope if latent else None, None if own_prev is None else own_prev[0:4], li=li)
        if latent:
            oa, oc, ob = _mixer_lat(qa, ka, va, qc, kc, vc, cache, hq, ff, fb, hv, hg, gn,
                                    (state_b_fwd, state_b_bwd), lam, subln_g, li=li, tq=256)
            own = None
        else:
            oa, oc, ob, s_f, s_b = _mixer_ctx(
                qa, ka, va, qc, kc, vc, hq, ff, fb, hv, hg, gn, lam, subln_g,
                None if own_prev is None else own_prev[4:6], li=li, depth=depth, nb=2)
            own = (ka, va, kc, vc, s_f, s_b)
        y = _out_mlp(x, oa, ob, oc, mod, mod_row, w_out_b, w1_b, w2_b, ln, li=li, alpha=alpha)
        return y, own

    y_prompt, y_sample = x_prompt, x_sample
    own = None
    for li in range(depth):
        y_prompt, own = stream(y_prompt, li, False, own)
        y_sample, _ = stream(y_sample, li, True, None)

    heads_a = mix_a // (2 * HEAD_DIM)
    new_a_k = own[0].reshape(bsz, depth, heads_a, 2, HEAD_DIM, seq).transpose(0, 1, 5, 2, 3, 4)
    new_a_v = own[1].reshape(bsz, depth, seq, heads_a, 2 * HEAD_DIM)
    kv_heads = mix_c // 2 // HEAD_DIM
    new_c_k = own[2].reshape(bsz, depth, kv_heads, HEAD_DIM, seq).transpose(0, 1, 4, 2, 3)
    new_c_v = own[3].reshape(bsz, depth, kv_heads, HEAD_DIM, seq).transpose(0, 1, 4, 2, 3)
    return (y_prompt, y_sample, new_a_k, new_a_v, new_c_k, new_c_v, own[4], own[5])
```

```python
import functools
import math

import jax
import jax.numpy as jnp
import numpy as np
from jax import lax
from jax.experimental import pallas as pl
from jax.experimental.pallas import tpu as pltpu

GRID_W = 64
HEAD_DIM = 64
ROPE_THETA = 10000.0
LN_EPS = 1e-6
RMS_EPS = 1e-6
F_MIN = 1e-6
CHUNK = 64
DIAG_BLOCK = 8
LANES = 128
ROW_TILE = 256
VMEM_LIMIT = 56 * 1024 * 1024

F32 = jnp.float32
BF16 = jnp.bfloat16
NT = (((1,), (1,)), ((), ()))
TN = (((0,), (0,)), ((), ()))


def _params(n_grid):
    return pltpu.CompilerParams(dimension_semantics=("arbitrary",) * n_grid,
                                vmem_limit_bytes=VMEM_LIMIT)


def _dot(a, b):
    return jnp.dot(a, b, preferred_element_type=F32)


def _split_dot(a, b_bf16, passes, dims=None):
    acc = None
    rem = a
    for _ in range(passes):
        piece = rem.astype(BF16)
        rem = rem - piece.astype(F32)
        part = (_dot(piece, b_bf16) if dims is None
                else lax.dot_general(piece, b_bf16, dims, preferred_element_type=F32))
        acc = part if acc is None else acc + part
    return acc


def _group_ones(n, group):
    r = lax.broadcasted_iota(jnp.int32, (n, n), 0) // group
    c = lax.broadcasted_iota(jnp.int32, (n, n), 1) // group
    return (r == c).astype(BF16)


def _group_mean_square(x, group):
    n = x.shape[-1]
    return _split_dot(x * x, _group_ones(n, group), 2) * (1.0 / group)


def _group_rms(x, g_row, group):
    return x * lax.rsqrt(_group_mean_square(x, group) + RMS_EPS) * g_row


def _pair_swap(x):
    lane = lax.broadcasted_iota(jnp.int32, x.shape, 1)
    return jnp.where(lane % 2 == 0, pltpu.roll(x, LANES - 1, 1), pltpu.roll(x, 1, 1))


def _rope(x, cos, sin):
    blocks = []
    for j in range(x.shape[-1] // LANES):
        blk = x[:, j * LANES:(j + 1) * LANES]
        blocks.append(blk * cos + _pair_swap(blk) * sin)
    return blocks[0] if len(blocks) == 1 else jnp.concatenate(blocks, axis=-1)


def _silu(x):
    return x * jax.nn.sigmoid(x)


def _layernorm(x, g, b):
    mu = jnp.mean(x, axis=-1, keepdims=True)
    xc = x - mu
    var = jnp.mean(xc * xc, axis=-1, keepdims=True)
    return xc * lax.rsqrt(var + LN_EPS) * g + b


MOD_ROWS = 8


def _mod_kernel(cctx_ref, c_ref, w_ref, b_ref, o_ref, s_ref):
    n_req = c_ref.shape[0]
    s_ref[...] = jnp.zeros_like(s_ref)
    s_ref[0:1, :] = _silu(cctx_ref[...])
    s_ref[1:1 + n_req, :] = _silu(c_ref[...])
    layer = pl.program_id(0)
    res = _dot(s_ref[...].astype(BF16), w_ref[0].astype(BF16)) + b_ref[pl.ds(layer, 1), :]
    for r in range(MOD_ROWS):
        o_ref[r] = res[r:r + 1, :]


def _modulation(c_ctx, c, w_ada, b_ada):
    depth, d, n = w_ada.shape
    tn = 1536
    return pl.pallas_call(
        _mod_kernel,
        grid=(depth, n // tn),
        in_specs=[pl.BlockSpec((1, d), lambda l, j: (0, 0)),
                  pl.BlockSpec(c.shape, lambda l, j: (0, 0)),
                  pl.BlockSpec((1, d, tn), lambda l, j: (l, 0, j)),
                  pl.BlockSpec((depth, tn), lambda l, j: (0, j))],
        out_specs=pl.BlockSpec((MOD_ROWS, 1, tn), lambda l, j: (l, 0, j)),
        out_shape=jax.ShapeDtypeStruct((depth * MOD_ROWS, 1, n), F32),
        scratch_shapes=[pltpu.VMEM((MOD_ROWS, d), F32)],
        compiler_params=_params(2),
        name="adaln_modulation",
    )(c_ctx.reshape(1, d), c, w_ada, b_ada)


def _in_proj_kernel(*refs, li, d, latent, n_alias):
    refs = list(refs)
    x_ref, mod_ref, w_ref, lbf_ref, lbb_ref, qn_ref, kn_ref = refs[:7]
    pos = 7
    if latent:
        cos, sin = refs[pos][...], refs[pos + 1][...]
        pos += 2
    pos += n_alias
    qa_o, hq_o, ff_o, fb_o, hv_o, hg_o, qc_o = refs[pos:pos + 7]
    if latent:
        ka_o, va_o, kc_o, vc_o = refs[pos + 7:pos + 11]
    else:
        ka_o, va_rows_o, kct_o, vct_o = refs[pos + 7:pos + 11]

    def store_kv(ref, val):
        for slot in range(ref.shape[1]):
            ref[0, slot] = val.astype(ref.dtype)

    mix_a, mix_b, mix_c = d // 2, d // 4, d // 4
    kv_c = mix_c // 2
    scale = HEAD_DIM ** -0.5 * math.log2(math.e)

    shift = mod_ref[0, :, 0:d]
    gain = mod_ref[0, :, d:2 * d]
    h = x_ref[...] * (1.0 + gain) + shift

    def proj(start, width):
        return _dot(h, w_ref[:, start:start + width])

    off_b = 3 * mix_a
    off_c = off_b + 5 * mix_b

    zq = proj(off_c, mix_c)
    zk = proj(off_c + mix_c, kv_c)
    vc = proj(off_c + mix_c + kv_c, kv_c)
    qa = proj(0, mix_a)
    msq = _group_mean_square(zq, HEAD_DIM)
    msk = _group_mean_square(zk, HEAD_DIM)
    ka = proj(mix_a, mix_a)
    va = proj(2 * mix_a, mix_a)
    qc = zq * lax.rsqrt(msq + RMS_EPS) * qn_ref[li:li + 1, :]
    kc = zk * lax.rsqrt(msk + RMS_EPS) * kn_ref[li:li + 1, :]

    if latent:
        qa = _rope(qa, cos, sin)
        ka = _rope(ka, cos, sin)
    qa_o[...] = (qa * scale).astype(qa_o.dtype)
    if latent:
        store_kv(ka_o, ka)
        store_kv(va_o, va)
    else:
        store_kv(ka_o, ka.T)
        heads = mix_a // LANES
        for slot in range(va_rows_o.shape[1]):
            for hd in range(heads):
                va_rows_o[0, slot, pl.ds(hd, ROW_TILE, stride=heads), :] = (
                    va[:, hd * LANES:(hd + 1) * LANES])

    def lower_bound(ref):
        logits = ref[...]
        e = jnp.exp(logits - jnp.max(logits, axis=0, keepdims=True))
        sm = e / jnp.sum(e, axis=0, keepdims=True)
        return jnp.sum(sm[0:li + 1], axis=0, keepdims=True) - sm[0:1]

    def forget(x, lb):
        return jnp.maximum(lb + (1.0 - lb) * jax.nn.sigmoid(x), F_MIN)

    off = off_b
    zb = [proj(off + j * mix_b, mix_b) for j in range(5)]

    if latent:
        kc = _rope(kc, cos, sin)
        qc = _rope(qc, cos, sin)
        store_kv(kc_o, kc)
        store_kv(vc_o, vc)
    else:
        store_kv(kct_o, kc.T)
        store_kv(vct_o, vc.T)
    qc = qc * scale
    lane = lax.broadcasted_iota(jnp.int32, (1, LANES), 1)
    for n in range(2):
        blk = qc[:, n * LANES:(n + 1) * LANES]
        in_half = (lane // HEAD_DIM) == n
        for g in range(2):
            src = blk if g == n else pltpu.roll(blk, HEAD_DIM, 1)
            hc = 2 * n + g
            qc_o[:, hc * LANES:(hc + 1) * LANES] = jnp.where(in_half, src, 0.0).astype(qc_o.dtype)

    hq_o[0] = _silu(zb[0])
    ff_o[0] = forget(zb[1], lower_bound(lbf_ref))
    fb_o[0] = forget(zb[2], lower_bound(lbb_ref))
    hv_o[0] = zb[3]
    hg_o[0] = _silu(zb[4])


def _in_proj(x, mod, mod_row, w_in, lb_f, lb_b, qn, kn, rope, kv_prev, *, li):
    bsz, t, d = x.shape
    latent = rope is not None
    depth, _, n_in = w_in.shape
    tiles = t // ROW_TILE
    mix_a, mix_b, mix_c = d // 2, d // 4, d // 4
    kv_c = mix_c // 2
    x2 = x.reshape(bsz * t, d)

    row = lambda i: (i, 0)
    brow = lambda i: (i // tiles, i % tiles, 0)
    const = lambda i: (0, 0)
    in_specs = [pl.BlockSpec((ROW_TILE, d), row),
                pl.BlockSpec((1, 1, mod.shape[-1]), lambda i: (mod_row(i * ROW_TILE), 0, 0)),
                pl.BlockSpec((None, d, n_in), lambda i: (li, 0, 0)),
                pl.BlockSpec(lb_f.shape, const), pl.BlockSpec(lb_b.shape, const),
                pl.BlockSpec(qn.shape, const), pl.BlockSpec(kn.shape, const)]
    args = [x2, mod, w_in, lb_f, lb_b, qn, kn]
    if latent:
        in_specs += [pl.BlockSpec((ROW_TILE, LANES), lambda i: (i % tiles, 0))] * 2
        args += list(rope)
        slots, slot0 = 1, 0
    else:
        slots, slot0 = (depth, 0) if kv_prev is None else (1, li)
    aliases = {}
    if kv_prev is not None:
        for j, buf in enumerate(kv_prev):
            aliases[len(args)] = 7 + j
            in_specs.append(pl.BlockSpec(memory_space=pl.ANY))
            args.append(buf)
    krow = lambda i: (i // tiles, slot0, i % tiles, 0)
    kcol = lambda i: (i // tiles, slot0, 0, i % tiles)
    heads = mix_a // LANES

    out_specs = [pl.BlockSpec((ROW_TILE, mix_a), row)]
    out_shape = [jax.ShapeDtypeStruct((bsz * t, mix_a), BF16)]
    out_specs += [pl.BlockSpec((1, ROW_TILE, mix_b), brow)] * 5
    out_shape += [jax.ShapeDtypeStruct((bsz, t, mix_b), F32)] * 5
    out_specs.append(pl.BlockSpec((ROW_TILE, 2 * mix_c), row))
    out_shape.append(jax.ShapeDtypeStruct((bsz * t, 2 * mix_c), BF16))
    if latent:
        for width in (mix_a, mix_a, kv_c, kv_c):
            out_specs.append(pl.BlockSpec((1, 1, ROW_TILE, width), krow))
            out_shape.append(jax.ShapeDtypeStruct((bsz, 1, t, width), BF16))
    else:
        out_specs += [pl.BlockSpec((1, slots, mix_a, ROW_TILE), kcol),
                      pl.BlockSpec((1, slots, ROW_TILE * heads, LANES), krow),
                      pl.BlockSpec((1, slots, kv_c, ROW_TILE), kcol),
                      pl.BlockSpec((1, slots, kv_c, ROW_TILE), kcol)]
        out_shape += [jax.ShapeDtypeStruct((bsz, depth, mix_a, t), F32),
                      jax.ShapeDtypeStruct((bsz, depth, t * heads, LANES), F32),
                      jax.ShapeDtypeStruct((bsz, depth, kv_c, t), F32),
                      jax.ShapeDtypeStruct((bsz, depth, kv_c, t), F32)]
    return pl.pallas_call(
        functools.partial(_in_proj_kernel, li=li, d=d, latent=latent, n_alias=len(aliases)),
        grid=(bsz * tiles,),
        in_specs=in_specs, out_specs=out_specs, out_shape=out_shape,
        input_output_aliases=aliases,
        compiler_params=_params(1),
        name="in_proj_latent" if latent else "in_proj_context",
    )(*args)


def _softmax_parts(scores):
    m = functools.reduce(jnp.maximum, [jnp.max(s, axis=-1, keepdims=True) for s in scores])
    es = [jnp.exp2(s - m) for s in scores]
    denom = functools.reduce(lambda a, b: a + b, [jnp.sum(e, axis=-1, keepdims=True) for e in es])
    return es, 1.0 / denom


def _score_blocks(q, keys):
    return [_dot(q, k) if k_t else lax.dot_general(q, k, NT, preferred_element_type=F32)
            for k, k_t in keys]


def _diff_lambda(lq1, lk1, lq2, lk2, li):
    lam_init = 0.8 - 0.6 * math.exp(-0.3 * li)

    def lam_term(a, b):
        return jnp.exp(jnp.sum(a[li:li + 1, :] * b[li:li + 1, :], axis=-1, keepdims=True))

    return lam_term(lq1, lk1) - lam_term(lq2, lk2) + lam_init, lam_init


def _diff_head_gen(q, keys, vals, lam, gain, store):
    tq = q.shape[0]
    lane = lax.broadcasted_iota(jnp.int32, (1, LANES), 1)
    zero = jnp.zeros_like(q)
    q2 = jnp.concatenate([jnp.where(lane < HEAD_DIM, q, zero),
                          jnp.where(lane >= HEAD_DIM, q, zero)], axis=0)
    scores = _score_blocks(q2, keys)
    yield
    es, r = _softmax_parts(scores)
    yield
    r0 = r[0:tq]
    ratio = r[tq:2 * tq] * lam / r0
    o = None
    for e, v in zip(es, vals):
        part = _dot((e[0:tq] - e[tq:2 * tq] * ratio).astype(BF16), v)
        o = part if o is None else o + part
    yield
    o = o * r0
    ms = jnp.mean(o * o, axis=-1, keepdims=True)
    store(o * lax.rsqrt(ms + RMS_EPS) * gain)


def _diff_attn_kernel(*refs, li, cached, heads, v_rows):
    if cached:
        (q_ref, k_ref, v_ref, ck_ref, cv_ref, lq1, lk1, lq2, lk2, sub_ref, o_ref) = refs
    else:
        (q_ref, k_ref, v_ref, lq1, lk1, lq2, lk2, sub_ref, o_ref) = refs
    lam, lam_init = _diff_lambda(lq1, lk1, lq2, lk2, li)
    gain = sub_ref[li:li + 1, :] * (1.0 - lam_init)

    def head_chain(j):
        c = slice(j * LANES, (j + 1) * LANES)
        head = pl.program_id(1) * heads + j
        if v_rows:
            t = k_ref.shape[3]
            keys = [(k_ref[0, 0, c, :].astype(BF16), True)]
            vals = [v_ref[0, 0, pl.ds(head, t, stride=v_ref.shape[2] // t), :].astype(BF16)]
        else:
            keys = [(k_ref[0, 0, :, c].astype(BF16), False)]
            vals = [v_ref[0, 0, :, c].astype(BF16)]
        if cached:
            past = ck_ref.shape[-1]
            all_heads = cv_ref.shape[2] // past
            keys.append((ck_ref[0, 0, c, :].astype(BF16), True))
            vals.append(cv_ref[0, 0, pl.ds(head, past, stride=all_heads), :].astype(BF16))

        def store(o):
            o_ref[:, c] = o.astype(o_ref.dtype)

        return _diff_head_gen(q_ref[:, c], keys, vals, lam, gain, store)

    _run_interleaved([head_chain(j) for j in range(heads)])


def _diff_attn(q, k, v, cache, lam, subln, *, li, kv_layer, bsz, tq, heads_per_step, v_rows):
    width = q.shape[-1]
    t = k.shape[3] if v_rows else k.shape[2]
    wstep = heads_per_step * LANES
    nq = t // tq
    cached = cache is not None
    if v_rows:
        kv_specs = [pl.BlockSpec((1, 1, wstep, t), lambda b, h, i: (b, kv_layer, h, 0)),
                    pl.BlockSpec((1, 1) + v.shape[2:], lambda b, h, i: (b, kv_layer, 0, 0))]
    else:
        kv_specs = [pl.BlockSpec((1, 1, t, wstep), lambda b, h, i: (b, kv_layer, 0, h))] * 2
    in_specs = [pl.BlockSpec((tq, wstep), lambda b, h, i: (b * nq + i, h))] + kv_specs
    args = [q, k, v]
    if cached:
        ck, cv = cache
        in_specs += [pl.BlockSpec((1, 1, wstep, ck.shape[-1]), lambda b, h, i: (b, li, h, 0)),
                     pl.BlockSpec((1, 1) + cv.shape[2:], lambda b, h, i: (b, li, 0, 0))]
        args += [ck, cv]
    in_specs += [pl.BlockSpec(a.shape, lambda b, h, i: (0, 0)) for a in (*lam, subln)]
    args += [*lam, subln]
    return pl.pallas_call(
        functools.partial(_diff_attn_kernel, li=li, cached=cached, heads=heads_per_step,
                          v_rows=v_rows),
        grid=(bsz, width // wstep, nq),
        in_specs=in_specs,
        out_specs=pl.BlockSpec((tq, wstep), lambda b, h, i: (b * nq + i, h)),
        out_shape=jax.ShapeDtypeStruct((bsz * t, width), BF16),
        compiler_params=_params(3),
        name="diff_attn_latent" if cached else "diff_attn_context",
    )(*args)


def _gqa_group_gen(n, q, keys, vals, store):
    tq = q.shape[0] // 2
    scores = _score_blocks(q, keys)
    yield
    es, r = _softmax_parts(scores)
    yield
    o = None
    for e, (v, v_t) in zip(es, vals):
        p = e.astype(BF16)
        part = lax.dot_general(p, v, NT, preferred_element_type=F32) if v_t else _dot(p, v)
        o = part if o is None else o + part
    yield
    o = o * r
    first = o[0:tq]
    second = o[tq:2 * tq]
    if n == 0:
        second = pltpu.roll(second, HEAD_DIM, 1)
    else:
        first = pltpu.roll(first, HEAD_DIM, 1)
    lane = lax.broadcasted_iota(jnp.int32, (1, LANES), 1)
    store(jnp.where(lane < HEAD_DIM, first, second))


def _gqa_kernel(*refs, cached, own_t):
    if cached:
        q_ref, k_ref, v_ref, ck_ref, cv_ref, o_ref = refs
    else:
        q_ref, k_ref, v_ref, o_ref = refs
    heads = q_ref.shape[1] // LANES
    keys = [(k_ref[0, 0].astype(BF16), own_t)]
    vals = [(v_ref[0, 0].astype(BF16), own_t)]
    if cached:
        keys.append((ck_ref[0, 0].astype(BF16), True))
        vals.append((cv_ref[0, 0].astype(BF16), True))

    def group_chain(n):
        q = jnp.concatenate([q_ref[:, (2 * n + g) * LANES:(2 * n + g + 1) * LANES]
                             for g in range(2)], axis=0)

        def store(o):
            o_ref[:, n * LANES:(n + 1) * LANES] = o.astype(o_ref.dtype)

        return _gqa_group_gen(n, q, keys, vals, store)

    _run_interleaved([group_chain(n) for n in range(heads // 2)])


def _gqa(q, k, v, cache, *, li, kv_layer, bsz, tq, own_t):
    t, kvw = (k.shape[3], k.shape[2]) if own_t else (k.shape[2], k.shape[3])
    nq = t // tq
    cached = cache is not None
    in_specs = [pl.BlockSpec((tq, q.shape[-1]), lambda b, i: (b * nq + i, 0)),
                pl.BlockSpec((1, 1) + k.shape[2:], lambda b, i: (b, kv_layer, 0, 0)),
                pl.BlockSpec((1, 1) + k.shape[2:], lambda b, i: (b, kv_layer, 0, 0))]
    args = [q, k, v]
    if cached:
        in_specs += [pl.BlockSpec((1, 1) + cache[0].shape[2:], lambda b, i: (b, li, 0, 0))] * 2
        args += list(cache)
    return pl.pallas_call(
        functools.partial(_gqa_kernel, cached=cached, own_t=own_t),
        grid=(bsz, nq),
        in_specs=in_specs,
        out_specs=pl.BlockSpec((tq, 2 * kvw), lambda b, i: (b * nq + i, 0)),
        out_shape=jax.ShapeDtypeStruct((bsz * t, 2 * kvw), BF16),
        compiler_params=_params(2),
        name="gqa_latent" if cached else "gqa_context",
    )(*args)


def _head_masks(width):
    lane_head = lax.broadcasted_iota(jnp.int32, (1, width), 1) // HEAD_DIM
    return [lane_head == h for h in range(width // HEAD_DIM)]


def _stack_heads(x, masks):
    return jnp.concatenate([jnp.where(m, x, jnp.zeros_like(x)) for m in masks], axis=0)


def _block_diag_mask(width):
    r = lax.broadcasted_iota(jnp.int32, (width, width), 0) // HEAD_DIM
    c = lax.broadcasted_iota(jnp.int32, (width, width), 1) // HEAD_DIM
    return r == c


def _ref_rows(b, offsets, span):
    width = b.shape[-1]
    return jnp.concatenate([jnp.broadcast_to(b[o:o + 1], (span, width)) for o in offsets], axis=0)


def _run_interleaved(gens, delays=None):
    live = list(zip(gens, delays or [0] * len(gens)))
    rnd = 0
    while live:
        for item in list(live):
            if item[1] > rnd:
                continue
            try:
                next(item[0])
            except StopIteration:
                live.remove(item)
        rnd += 1


def _hgrn_chunks(problems):
    out = []
    _run_interleaved([_hgrn_chunks_gen(problems, out)])
    return out


def _hgrn_chunks_gen(problems, out):
    n = len(problems)
    c, width = problems[0][0].shape
    qs = [p[0] for p in problems]
    vs = [p[2] for p in problems]
    sts = [p[3] for p in problems]
    rev = [p[4] for p in problems]
    chains = range(n)
    masks = _head_masks(width)
    trow =lax.broadcasted_iota(jnp.int32, (c, 1), 0)
    t_full = lax.broadcasted_iota(jnp.int32, (c, width), 0)
    s_full = lax.broadcasted_iota(jnp.int32, (c, width), 1) % c

    ks = [1.0 - p[1] for p in problems]
    b = [jnp.log(p[1]) for p in problems]
    step = 1
    while step < c:
        for j in chains:
            if rev[j]:
                b[j] = b[j] + jnp.where(trow < c - step, pltpu.roll(b[j], c - step, 0), 0.0)
            else:
                b[j] = b[j] + jnp.where(trow >= step, pltpu.roll(b[j], step, 0), 0.0)
        step *= 2
        yield
    b_end = [b[j][0:1] if rev[j] else b[j][c - 1:c] for j in chains]

    o = [lax.dot_general((qs[j] * jnp.exp(b[j])).astype(BF16), sts[j].astype(BF16), NT,
                         preferred_element_type=F32) for j in chains]
    yield

    a = [None] * n
    m = c // 2
    while m >= DIAG_BLOCK:
        blocks = c // (2 * m)
        same = (t_full // (2 * m)) == (s_full // (2 * m))
        for j in chains:
            ref = _ref_rows(b[j], [i * 2 * m + (m if rev[j] else m - 1) for i in range(blocks)],
                            2 * m)
            is_q = ((trow % (2 * m)) < m) if rev[j] else ((trow % (2 * m)) >= m)
            e = jnp.exp(jnp.where(is_q, b[j] - ref, ref - b[j]))
            ql = jnp.where(is_q, qs[j] * e, 0.0).astype(BF16)
            kl = jnp.where(is_q, 0.0, ks[j] * e).astype(BF16)
            al = lax.dot_general(ql, _stack_heads(kl, masks), NT, preferred_element_type=F32)
            if blocks > 1:
                al = jnp.where(same, al, 0.0)
            a[j] = al if a[j] is None else a[j] + al
        m //= 2
        yield
    blocks = c // DIAG_BLOCK
    mid = DIAG_BLOCK // 2
    same = (t_full // DIAG_BLOCK) == (s_full // DIAG_BLOCK)
    for j in chains:
        ref = _ref_rows(b[j], [i * DIAG_BLOCK + (mid if rev[j] else mid - 1) for i in range(blocks)],
                        DIAG_BLOCK)
        d = b[j] - ref
        ql = (qs[j] * jnp.exp(d)).astype(BF16)
        kl = (ks[j] * jnp.exp(-d)).astype(BF16)
        al = lax.dot_general(ql, _stack_heads(kl, masks), NT, preferred_element_type=F32)
        causal = (s_full >= t_full) if rev[j] else (s_full <= t_full)
        a[j] = a[j] + jnp.where(same & causal, al, 0.0)
    yield

    v_b = [v.astype(BF16) for v in vs]
    o = [o[j] + _dot(a[j].astype(BF16), _stack_heads(v_b[j], masks)) for j in chains]
    yield

    bd = _block_diag_mask(width)
    upd = [lax.dot_general(v_b[j], (ks[j] * jnp.exp(b_end[j] - b[j])).astype(BF16), TN,
                           preferred_element_type=F32) for j in chains]
    st_new = [sts[j] * jnp.exp(b_end[j]) + jnp.where(bd, upd[j], 0.0) for j in chains]
    out.extend(zip(o, st_new))


def _mxu_transpose(x):
    n = x.shape[1]
    r = lax.broadcasted_iota(jnp.int32, (n, n), 0)
    c = lax.broadcasted_iota(jnp.int32, (n, n), 1)
    eye = (r == c).astype(BF16)
    acc = None
    rem = x
    for _ in range(3):
        piece = rem.astype(BF16)
        rem = rem - piece.astype(F32)
        part = lax.dot_general(eye, piece, NT, preferred_element_type=F32)
        acc = part if acc is None else acc + part
    return acc


def _hgrn_kernel(*refs, li, has_state, want_state, n_alias, heads):
    refs = list(refs)
    q_ref, ff_ref, fb_ref, v_ref, gate_ref, gn_ref = refs[:6]
    pos = 6
    if has_state:
        s0f_ref, s0b_ref = refs[pos:pos + 2]
        pos += 2
    pos += n_alias
    o_ref = refs[pos]
    pos += 1
    if want_state:
        sf_ref, sb_ref = refs[pos:pos + 2]
        pos += 2
    st_ref, of_ref, ob_ref = refs[pos:pos + 3]
    nb, t, width = q_ref.shape
    nc = t // CHUNK
    bd = _block_diag_mask(width)

    for n in range(nb):
        for d in range(2):
            if has_state:
                x = (s0b_ref if d else s0f_ref)[n, 0].reshape(width, HEAD_DIM)
                xt = _mxu_transpose(x)
                st_ref[2 * n + d] = jnp.where(bd, jnp.concatenate([xt] * heads, axis=0), 0.0)
            else:
                st_ref[2 * n + d] = jnp.zeros((width, width), F32)

    def body(ci, carry):
        rows = (pl.ds(pl.multiple_of(ci * CHUNK, CHUNK), CHUNK),
                pl.ds(pl.multiple_of((nc - 1 - ci) * CHUNK, CHUNK), CHUNK))
        loaded = []
        for n in range(nb):
            for d, f_ref in enumerate((ff_ref, fb_ref)):
                r = rows[d]
                loaded.append((q_ref[n, r, :], f_ref[n, r, :], v_ref[n, r, :], st_ref[2 * n + d],
                               bool(d)))
        for j, (o, st) in enumerate(_hgrn_chunks(loaded)):
            n, d = divmod(j, 2)
            (ob_ref if d else of_ref)[n, rows[d], :] = o
            st_ref[j] = st
        return carry

    lax.fori_loop(0, nc, body, 0)

    for n in range(nb):
        o = of_ref[n] + ob_ref[n]
        o_ref[n] = (_group_rms(o, gn_ref[li:li + 1, :], HEAD_DIM) * gate_ref[n]).astype(o_ref.dtype)

    if want_state:
        for n in range(nb):
            for d, dst in enumerate((sf_ref, sb_ref)):
                st = st_ref[2 * n + d]
                rows = st[0:HEAD_DIM]
                for h in range(1, heads):
                    rows = rows + st[h * HEAD_DIM:(h + 1) * HEAD_DIM]
                final = _mxu_transpose(rows).reshape(heads, HEAD_DIM, HEAD_DIM)
                for slot in range(dst.shape[1]):
                    dst[n, slot] = final


def _hgrn(hq, ff, fb, hv, hg, gn, state, state_prev, *, li, depth, want_state, nb):
    bsz, t, width = hq.shape
    heads = width // HEAD_DIM
    has_state = state is not None
    seq = pl.BlockSpec((nb, t, width), lambda b: (b, 0, 0))
    in_specs = [seq] * 5 + [pl.BlockSpec(gn.shape, lambda b: (0, 0))]
    args = [hq, ff, fb, hv, hg, gn]
    if has_state:
        in_specs += [pl.BlockSpec((nb, 1, heads, HEAD_DIM, HEAD_DIM), lambda b: (b, li, 0, 0, 0))] * 2
        args += list(state)
    aliases = {}
    if state_prev is not None:
        for j, buf in enumerate(state_prev):
            aliases[len(args)] = 1 + j
            in_specs.append(pl.BlockSpec(memory_space=pl.ANY))
            args.append(buf)
    out_specs = [seq]
    out_shape = [jax.ShapeDtypeStruct((bsz, t, width), BF16)]
    if want_state:
        slots, slot0 = (depth, 0) if state_prev is None else (1, li)
        out_specs += [pl.BlockSpec((nb, slots, heads, HEAD_DIM, HEAD_DIM),
                                   lambda b: (b, slot0, 0, 0, 0))] * 2
        out_shape += [jax.ShapeDtypeStruct((bsz, depth, heads, HEAD_DIM, HEAD_DIM), F32)] * 2
    return pl.pallas_call(
        functools.partial(_hgrn_kernel, li=li, has_state=has_state, want_state=want_state,
                          n_alias=len(aliases), heads=heads),
        grid=(bsz // nb,),
        in_specs=in_specs, out_specs=out_specs, out_shape=out_shape,
        input_output_aliases=aliases,
        scratch_shapes=[pltpu.VMEM((2 * nb, width, width), F32),
                        pltpu.VMEM((nb, t, width), F32), pltpu.VMEM((nb, t, width), F32)],
        compiler_params=_params(1),
        name="hgrn2_latent" if has_state else "hgrn2_context",
    )(*args)


ATTENTION_DELAY = 5


def _mixer_ctx_kernel(*refs, li, n_alias):
    refs = list(refs)
    (qa_ref, kat_ref, va_ref, qc_ref, kct_ref, vct_ref, hq_ref, ff_ref, fb_ref, hv_ref, hg_ref,
     gn_ref, lq1, lk1, lq2, lk2, sub_ref) = refs[:17]
    pos = 17 + n_alias
    oa_ref, oc_ref, ob_ref, sf_ref, sb_ref = refs[pos:pos + 5]
    st_ref, of_ref, obk_ref = refs[pos + 5:pos + 8]
    nb, t, width = hq_ref.shape
    nc = t // CHUNK
    heads_b = width // HEAD_DIM
    heads_a = qa_ref.shape[1] // LANES
    groups_c = qc_ref.shape[1] // LANES // 2
    lam, lam_init = _diff_lambda(lq1, lk1, lq2, lk2, li)
    gain = sub_ref[li:li + 1, :] * (1.0 - lam_init)

    for j in range(2 * nb):
        st_ref[j] = jnp.zeros((width, width), F32)

    def diff_unit(n, h):
        rows = slice(n * t, (n + 1) * t)
        c = slice(h * LANES, (h + 1) * LANES)
        keys = [(kat_ref[n, 0, c, :].astype(BF16), True)]
        vals = [va_ref[n, 0, pl.ds(h, t, stride=heads_a), :].astype(BF16)]

        def store(o):
            oa_ref[rows, c] = o.astype(oa_ref.dtype)

        yield from _diff_head_gen(qa_ref[rows, c], keys, vals, lam, gain, store)

    def gqa_unit(n, g):
        rows = slice(n * t, (n + 1) * t)
        q = jnp.concatenate([qc_ref[rows, (2 * g + j) * LANES:(2 * g + j + 1) * LANES]
                             for j in range(2)], axis=0)
        keys = [(kct_ref[n, 0].astype(BF16), True)]
        vals = [(vct_ref[n, 0].astype(BF16), True)]

        def store(o):
            oc_ref[rows, g * LANES:(g + 1) * LANES] = o.astype(oc_ref.dtype)

        yield from _gqa_group_gen(g, q, keys, vals, store)

    def scan_step(ci):
        rows = (slice(ci * CHUNK, (ci + 1) * CHUNK), slice((nc - 1 - ci) * CHUNK, (nc - ci) * CHUNK))
        loaded = []
        for n in range(nb):
            for d, f_ref in enumerate((ff_ref, fb_ref)):
                r = rows[d]
                loaded.append((hq_ref[n, r, :], f_ref[n, r, :], hv_ref[n, r, :], st_ref[2 * n + d],
                               bool(d)))
        out = []
        yield from _hgrn_chunks_gen(loaded, out)
        for j, (o, st) in enumerate(out):
            n, d = divmod(j, 2)
            (obk_ref if d else of_ref)[n, rows[d], :] = o
            st_ref[j] = st

    units = []
    for n in range(nb):
        units += [diff_unit(n, h) for h in range(heads_a)]
        units += [gqa_unit(n, g) for g in range(groups_c)]
    share = -(-len(units) // nc)
    for ci in range(nc):
        mine = units[ci * share:(ci + 1) * share]
        _run_interleaved([scan_step(ci)] + mine, [0] + [ATTENTION_DELAY] * len(mine))

    for n in range(nb):
        o = of_ref[n] + obk_ref[n]
        ob_ref[n] = (_group_rms(o, gn_ref[li:li + 1, :], HEAD_DIM) * hg_ref[n]).astype(ob_ref.dtype)
        for d, dst in enumerate((sf_ref, sb_ref)):
            st = st_ref[2 * n + d]
            rows = st[0:HEAD_DIM]
            for h in range(1, heads_b):
                rows = rows + st[h * HEAD_DIM:(h + 1) * HEAD_DIM]
            final = _mxu_transpose(rows).reshape(heads_b, HEAD_DIM, HEAD_DIM)
            for slot in range(dst.shape[1]):
                dst[n, slot] = final


def _mixer_ctx(qa, ka_t, va_rows, qc, kc_t, vc_t, hq, ff, fb, hv, hg, gn, lam, subln, state_prev,
               *, li, depth, nb):
    bsz, t, width = hq.shape
    heads_b = width // HEAD_DIM
    rows = lambda b: (b, 0)
    at_layer = lambda b: (b, li, 0, 0)
    seq = pl.BlockSpec((nb, t, width), lambda b: (b, 0, 0))
    const = lambda b: (0, 0)
    in_specs = [pl.BlockSpec((nb * t, qa.shape[1]), rows),
                pl.BlockSpec((nb, 1) + ka_t.shape[2:], at_layer),
                pl.BlockSpec((nb, 1) + va_rows.shape[2:], at_layer),
                pl.BlockSpec((nb * t, qc.shape[1]), rows),
                pl.BlockSpec((nb, 1) + kc_t.shape[2:], at_layer),
                pl.BlockSpec((nb, 1) + vc_t.shape[2:], at_layer),
                seq, seq, seq, seq, seq, pl.BlockSpec(gn.shape, const)]
    in_specs += [pl.BlockSpec(a.shape, const) for a in (*lam, subln)]
    args = [qa, ka_t, va_rows, qc, kc_t, vc_t, hq, ff, fb, hv, hg, gn, *lam, subln]
    aliases = {}
    if state_prev is not None:
        for j, buf in enumerate(state_prev):
            aliases[len(args)] = 3 + j
            in_specs.append(pl.BlockSpec(memory_space=pl.ANY))
            args.append(buf)
    slots, slot0 = (depth, 0) if state_prev is None else (1, li)
    state_spec = pl.BlockSpec((nb, slots, heads_b, HEAD_DIM, HEAD_DIM), lambda b: (b, slot0, 0, 0, 0))
    out_specs = [pl.BlockSpec((nb * t, qa.shape[1]), rows),
                 pl.BlockSpec((nb * t, qc.shape[1] // 2), rows), seq, state_spec, state_spec]
    out_shape = [jax.ShapeDtypeStruct(qa.shape, BF16),
                 jax.ShapeDtypeStruct((qc.shape[0], qc.shape[1] // 2), BF16),
                 jax.ShapeDtypeStruct((bsz, t, width), BF16)]
    out_shape += [jax.ShapeDtypeStruct((bsz, depth, heads_b, HEAD_DIM, HEAD_DIM), F32)] * 2
    return pl.pallas_call(
        functools.partial(_mixer_ctx_kernel, li=li, n_alias=len(aliases)),
        grid=(bsz // nb,),
        in_specs=in_specs, out_specs=out_specs, out_shape=out_shape,
        input_output_aliases=aliases,
        scratch_shapes=[pltpu.VMEM((2 * nb, width, width), F32),
                        pltpu.VMEM((nb, t, width), F32), pltpu.VMEM((nb, t, width), F32)],
        compiler_params=_params(1),
        name="mixers_context",
    )(*args)


def _mixer_lat_kernel(qa_ref, k_ref, v_ref, ck_ref, cv_ref, qc_ref, kc_ref, vc_ref, cck_ref, ccv_ref,
                      hq_ref, ff_ref, fb_ref, hv_ref, hg_ref, gn_ref, s0f_ref, s0b_ref,
                      lq1, lk1, lq2, lk2, sub_ref, oa_ref, oc_ref, ob_ref,
                      st_ref, of_ref, obk_ref, *, li, tq):
    _, t, width = hq_ref.shape
    nc = t // CHUNK
    nq = t // tq
    steps_per_trip = nc // nq
    heads_b = width // HEAD_DIM
    heads_a = qa_ref.shape[1] // LANES
    groups_c = qc_ref.shape[1] // LANES // 2
    past = ck_ref.shape[-1]
    lam, lam_init = _diff_lambda(lq1, lk1, lq2, lk2, li)
    gain = sub_ref[li:li + 1, :] * (1.0 - lam_init)
    bd = _block_diag_mask(width)

    for d, src in enumerate((s0f_ref, s0b_ref)):
        x = src[0, 0].reshape(width, HEAD_DIM)
        xt = _mxu_transpose(x)
        st_ref[d] = jnp.where(bd, jnp.concatenate([xt] * heads_b, axis=0), 0.0)

    def trip(qt, carry):
        q_rows = pl.ds(pl.multiple_of(qt * tq, tq), tq)

        def diff_unit(h):
            c = slice(h * LANES, (h + 1) * LANES)
            keys = [(k_ref[0, 0, :, c], False), (ck_ref[0, 0, c, :].astype(BF16), True)]
            vals = [v_ref[0, 0, :, c], cv_ref[0, 0, pl.ds(h, past, stride=heads_a), :].astype(BF16)]

            def store(o):
                oa_ref[q_rows, c] = o.astype(oa_ref.dtype)

            yield from _diff_head_gen(qa_ref[q_rows, c], keys, vals, lam, gain, store)

        def gqa_unit(g):
            q = jnp.concatenate([qc_ref[q_rows, (2 * g + j) * LANES:(2 * g + j + 1) * LANES]
                                 for j in range(2)], axis=0)
            keys = [(kc_ref[0, 0], False), (cck_ref[0, 0].astype(BF16), True)]
            vals = [(vc_ref[0, 0], False), (ccv_ref[0, 0].astype(BF16), True)]

            def store(o):
                oc_ref[q_rows, g * LANES:(g + 1) * LANES] = o.astype(oc_ref.dtype)

            yield from _gqa_group_gen(g, q, keys, vals, store)

        def scan_step(cj):
            ci = qt * steps_per_trip + cj
            rows = (pl.ds(pl.multiple_of(ci * CHUNK, CHUNK), CHUNK),
                    pl.ds(pl.multiple_of((nc - 1 - ci) * CHUNK, CHUNK), CHUNK))
            loaded = [(hq_ref[0, rows[d], :], f_ref[0, rows[d], :], hv_ref[0, rows[d], :],
                       st_ref[d], bool(d)) for d, f_ref in enumerate((ff_ref, fb_ref))]
            out = []
            yield from _hgrn_chunks_gen(loaded, out)
            for d, (o, st) in enumerate(out):
                (obk_ref if d else of_ref)[rows[d], :] = o
                st_ref[d] = st

        units = [diff_unit(h) for h in range(heads_a)] + [gqa_unit(g) for g in range(groups_c)]
        for cj in range(steps_per_trip):
            mine = units[cj::steps_per_trip]
            _run_interleaved([scan_step(cj)] + mine, [0] + [ATTENTION_DELAY] * len(mine))
        return carry

    lax.fori_loop(0, nq, trip, 0)

    o = of_ref[...] + obk_ref[...]
    ob_ref[0] = (_group_rms(o, gn_ref[li:li + 1, :], HEAD_DIM) * hg_ref[0]).astype(ob_ref.dtype)


def _mixer_lat(qa, ka, va, qc, kc, vc, cache, hq, ff, fb, hv, hg, gn, state, lam, subln, *, li, tq):
    bsz, t, width = hq.shape
    rows = lambda b: (b, 0)
    own = lambda b: (b, 0, 0, 0)
    at_layer = lambda b: (b, li, 0, 0)
    seq = pl.BlockSpec((1, t, width), lambda b: (b, 0, 0))
    const = lambda b: (0, 0)
    in_specs = [pl.BlockSpec((t, qa.shape[1]), rows),
                pl.BlockSpec((1,) + ka.shape[1:], own), pl.BlockSpec((1,) + va.shape[1:], own),
                pl.BlockSpec((1, 1) + cache[0].shape[2:], at_layer),
                pl.BlockSpec((1, 1) + cache[1].shape[2:], at_layer),
                pl.BlockSpec((t, qc.shape[1]), rows),
                pl.BlockSpec((1,) + kc.shape[1:], own), pl.BlockSpec((1,) + vc.shape[1:], own),
                pl.BlockSpec((1, 1) + cache[2].shape[2:], at_layer),
                pl.BlockSpec((1, 1) + cache[3].shape[2:], at_layer),
                seq, seq, seq, seq, seq, pl.BlockSpec(gn.shape, const)]
    in_specs += [pl.BlockSpec((1, 1) + state[0].shape[2:], lambda b: (b, li, 0, 0, 0))] * 2
    in_specs += [pl.BlockSpec(a.shape, const) for a in (*lam, subln)]
    args = [qa, ka, va, cache[0], cache[1], qc, kc, vc, cache[2], cache[3], hq, ff, fb, hv, hg, gn,
            *state, *lam, subln]
    return pl.pallas_call(
        functools.partial(_mixer_lat_kernel, li=li, tq=tq),
        grid=(bsz,),
        in_specs=in_specs,
        out_specs=[pl.BlockSpec((t, qa.shape[1]), rows),
                   pl.BlockSpec((t, qc.shape[1] // 2), rows), seq],
        out_shape=[jax.ShapeDtypeStruct(qa.shape, BF16),
                   jax.ShapeDtypeStruct((qc.shape[0], qc.shape[1] // 2), BF16),
                   jax.ShapeDtypeStruct((bsz, t, width), BF16)],
        scratch_shapes=[pltpu.VMEM((2, width, width), F32),
                        pltpu.VMEM((t, width), F32), pltpu.VMEM((t, width), F32)],
        compiler_params=_params(1),
        name="mixers_latent",
    )(*args)


def _out_mlp_kernel(*refs, li, d, alpha, ff_chunk, cast_next):
    (x_ref, oa_ref, ob_ref, oc_ref, mod_ref, wo_ref, w1_ref, w2_ref,
     g1_ref, b1_ref, g2_ref, b2_ref) = refs[:12]
    if cast_next:
        w1n_ref, w2n_ref, y_ref, w1n_o, w2n_o = refs[12:]
        w1n_o[...] = w1n_ref[...].astype(BF16)
        w2n_o[...] = w2n_ref[...].astype(BF16)
    else:
        y_ref = refs[12]
    wa, wb = oa_ref.shape[-1], ob_ref.shape[-1]
    layer = slice(li, li + 1)
    gate1 = mod_ref[0, :, 2 * d:3 * d]
    shift2 = mod_ref[0, :, 3 * d:4 * d]
    gain2 = mod_ref[0, :, 4 * d:5 * d]
    gate2 = mod_ref[0, :, 5 * d:6 * d]
    subs = [slice(s * ROW_TILE, (s + 1) * ROW_TILE) for s in range(x_ref.shape[0] // ROW_TILE)]
    wo = wo_ref[...].astype(BF16)
    m = [_dot(oa_ref[r, :], wo[0:wa, :]) + _dot(ob_ref[r, :], wo[wa:wa + wb, :])
         + _dot(oc_ref[r, :], wo[wa + wb:, :]) for r in subs]
    x1 = [_layernorm(alpha * x_ref[r, :] + gate1 * mi, g1_ref[layer, :], b1_ref[layer, :])
          for r, mi in zip(subs, m)]
    h2 = [(xi * (1.0 + gain2) + shift2).astype(BF16) for xi in x1]
    acc = [None] * len(subs)
    for j in range(w1_ref.shape[-1] // ff_chunk):
        cols = slice(j * ff_chunk, (j + 1) * ff_chunk)
        hid = [jnp.maximum(_dot(hi, w1_ref[:, cols]), 0.0) for hi in h2]
        for s, hd in enumerate(hid):
            part = _dot((hd * hd).astype(BF16), w2_ref[cols, :])
            acc[s] = part if acc[s] is None else acc[s] + part
    for r, xi, ai in zip(subs, x1, acc):
        y_ref[r, :] = _layernorm(alpha * xi + gate2 * ai, g2_ref[layer, :], b2_ref[layer, :])


def _out_mlp(x, oa, ob, oc, mod, mod_row, w_out, w_ff1, w_ff2, ln, next_ff, *, li, alpha):
    bsz, t, d = x.shape
    rows = 2 * ROW_TILE
    steps = bsz * t // rows
    row = lambda i: (i, 0)
    const = lambda i: (0, 0)
    resident = lambda a: pl.BlockSpec(a.shape, const, pipeline_mode=pl.Buffered(1))
    in_specs = [pl.BlockSpec((rows, d), row),
                pl.BlockSpec((rows, oa.shape[-1]), row),
                pl.BlockSpec((rows, ob.shape[-1]), row),
                pl.BlockSpec((rows, oc.shape[-1]), row),
                pl.BlockSpec((1, 1, mod.shape[-1]), lambda i: (mod_row(i * rows), 0, 0)),
                pl.BlockSpec((None,) + w_out.shape[1:], lambda i: (li, 0, 0),
                             pipeline_mode=pl.Buffered(1)),
                resident(w_ff1), resident(w_ff2)]
    in_specs += [pl.BlockSpec(a.shape, const) for a in ln]
    args = [x.reshape(bsz * t, d), oa, ob.reshape(bsz * t, -1), oc, mod, w_out, w_ff1, w_ff2, *ln]
    out_specs = [pl.BlockSpec((rows, d), row)]
    out_shape = [jax.ShapeDtypeStruct((bsz * t, d), F32)]
    if next_ff is not None:
        for w in next_ff:
            blk = (w.shape[1] // steps, w.shape[2])
            in_specs.append(pl.BlockSpec((None,) + blk, lambda i: (li + 1, i, 0)))
            args.append(w)
            out_specs.append(pl.BlockSpec(blk, row))
            out_shape.append(jax.ShapeDtypeStruct(w.shape[1:], BF16))
    res = pl.pallas_call(
        functools.partial(_out_mlp_kernel, li=li, d=d, alpha=alpha, ff_chunk=1024,
                          cast_next=next_ff is not None),
        grid=(steps,),
        in_specs=in_specs, out_specs=out_specs, out_shape=out_shape,
        compiler_params=_params(1),
        name="out_mlp",
    )(*args)
    return res[0].reshape(bsz, t, d), tuple(res[1:])


def _rope_tables(n_tokens):
    pairs = HEAD_DIM // 4
    tok = np.arange(n_tokens)
    row = (tok // GRID_W).astype(np.float64)
    col = (tok % GRID_W).astype(np.float64)
    inv = ROPE_THETA ** (-np.arange(pairs, dtype=np.float64) / pairs)
    ang = np.concatenate([row[:, None] * inv, col[:, None] * inv], axis=-1)
    lane = np.arange(LANES)
    pair = (lane % HEAD_DIM) // 2
    sign = np.where(lane % 2 == 0, -1.0, 1.0)
    return (jnp.asarray(np.cos(ang)[:, pair], F32), jnp.asarray(np.sin(ang)[:, pair] * sign, F32))


def kernel(x_prompt, x_sample, cache_a_k, cache_a_v, cache_c_k, cache_c_v, state_b_fwd, state_b_bwd, c, c_ctx, w_ada, b_ada, w_in, w_out, lam_q1, lam_k1, lam_q2, lam_k2, subln_g, lb_logits_fwd, lb_logits_bwd, gnorm_g, qnorm_g, knorm_g, ln1_g, ln1_b, ln2_g, ln2_b, w_ff1, w_ff2):
    depth = w_in.shape[0]
    bsz, seq, d = x_prompt.shape
    dec_bsz, dec_seq, _ = x_sample.shape
    past = cache_a_k.shape[2]
    alpha = (2 * depth) ** 0.25
    mix_a, mix_b, mix_c = d // 2, d // 4, d // 4

    mod = _modulation(c_ctx, c, w_ada, b_ada)
    rope = _rope_tables(dec_seq)

    cache = (cache_a_k.transpose(0, 1, 3, 4, 5, 2).reshape(dec_bsz, depth, mix_a, past),
             cache_a_v.reshape(dec_bsz, depth, past * (mix_a // LANES), LANES),
             cache_c_k.transpose(0, 1, 3, 4, 2).reshape(dec_bsz, depth, mix_c // 2, past),
             cache_c_v.transpose(0, 1, 3, 4, 2).reshape(dec_bsz, depth, mix_c // 2, past))
    lam = (lam_q1, lam_k1, lam_q2, lam_k2)

    ff_b = [(w_ff1[0].astype(BF16), w_ff2[0].astype(BF16))]
    qn =jnp.tile(qnorm_g, (1, mix_c // HEAD_DIM))
    kn = jnp.tile(knorm_g, (1, mix_c // 2 // HEAD_DIM))
    gn = jnp.tile(gnorm_g, (1, mix_b // HEAD_DIM))
    ln = (ln1_g, ln1_b, ln2_g, ln2_b)

    def stream(x, li, latent, own_prev):
        n, t, _ = x.shape
        mod_row = ((lambda r0: li * MOD_ROWS + 1 + r0 // t) if latent
                   else (lambda r0: li * MOD_ROWS))
        (qa, hq, ff, fb, hv, hg, qc, ka, va, kc, vc) = _in_proj(
            x, mod, mod_row, w_in, lb_logits_fwd, lb_logits_bwd, qn, kn,
            rope if latent else None, None if own_prev is None else own_prev[0:4], li=li)
        if latent:
            oa, oc, ob = _mixer_lat(qa, ka, va, qc, kc, vc, cache, hq, ff, fb, hv, hg, gn,
                                    (state_b_fwd, state_b_bwd), lam, subln_g, li=li, tq=256)
            own = None
        else:
            oa, oc, ob, s_f, s_b = _mixer_ctx(
                qa, ka, va, qc, kc, vc, hq, ff, fb, hv, hg, gn, lam, subln_g,
                None if own_prev is None else own_prev[4:6], li=li, depth=depth, nb=4)
            own = (ka, va, kc, vc, s_f, s_b)
        cast_next = not latent and li + 1 < depth
        y, next_ff = _out_mlp(x, oa, ob, oc, mod, mod_row, w_out, *ff_b[li], ln,
                              (w_ff1, w_ff2) if cast_next else None, li=li, alpha=alpha)
        if cast_next:
            ff_b.append(next_ff)
        return y, own

    y_prompt, y_sample = x_prompt, x_sample
    own = None
    for li in range(depth):
        y_prompt, own = stream(y_prompt, li, False, own)
        y_sample, _ = stream(y_sample, li, True, None)

    heads_a = mix_a // (2 * HEAD_DIM)
    new_a_k = own[0].reshape(bsz, depth, heads_a, 2, HEAD_DIM, seq).transpose(0, 1, 5, 2, 3, 4)
    new_a_v = own[1].reshape(bsz, depth, seq, heads_a, 2 * HEAD_DIM)
    kv_heads = mix_c // 2 // HEAD_DIM
    new_c_k = own[2].reshape(bsz, depth, kv_heads, HEAD_DIM, seq).transpose(0, 1, 4, 2, 3)
    new_c_v = own[3].reshape(bsz, depth, kv_heads, HEAD_DIM, seq).transpose(0, 1, 4, 2, 3)
    return (y_prompt, y_sample, new_a_k, new_a_v, new_c_k, new_c_v, own[4], own[5])
```

```python
import functools
import math

import jax
import jax.numpy as jnp
import numpy as np
from jax import lax
from jax.experimental import pallas as pl
from jax.experimental.pallas import tpu as pltpu

GRID_W = 64
HEAD_DIM = 64
ROPE_THETA = 10000.0
LN_EPS = 1e-6
RMS_EPS = 1e-6
F_MIN = 1e-6
CHUNK = 64
DIAG_BLOCK = 8
LANES = 128
ROW_TILE = 256
VMEM_LIMIT = 56 * 1024 * 1024

F32 = jnp.float32
BF16 = jnp.bfloat16
NT = (((1,), (1,)), ((), ()))
TN = (((0,), (0,)), ((), ()))


def _params(n_grid):
    return pltpu.CompilerParams(dimension_semantics=("arbitrary",) * n_grid,
                                vmem_limit_bytes=VMEM_LIMIT)


def _dot(a, b):
    return jnp.dot(a, b, preferred_element_type=F32)


def _split_dot(a, b_bf16, passes, dims=None):
    acc = None
    rem = a
    for _ in range(passes):
        piece = rem.astype(BF16)
        rem = rem - piece.astype(F32)
        part = (_dot(piece, b_bf16) if dims is None
                else lax.dot_general(piece, b_bf16, dims, preferred_element_type=F32))
        acc = part if acc is None else acc + part
    return acc


def _group_ones(n, group):
    r = lax.broadcasted_iota(jnp.int32, (n, n), 0) // group
    c = lax.broadcasted_iota(jnp.int32, (n, n), 1) // group
    return (r == c).astype(BF16)


def _group_mean_square(x, group):
    n = x.shape[-1]
    return _split_dot(x * x, _group_ones(n, group), 2) * (1.0 / group)


def _group_rms(x, g_row, group):
    return x * lax.rsqrt(_group_mean_square(x, group) + RMS_EPS) * g_row


def _pair_swap(x):
    lane = lax.broadcasted_iota(jnp.int32, x.shape, 1)
    return jnp.where(lane % 2 == 0, pltpu.roll(x, LANES - 1, 1), pltpu.roll(x, 1, 1))


def _rope(x, cos, sin):
    blocks = []
    for j in range(x.shape[-1] // LANES):
        blk = x[:, j * LANES:(j + 1) * LANES]
        blocks.append(blk * cos + _pair_swap(blk) * sin)
    return blocks[0] if len(blocks) == 1 else jnp.concatenate(blocks, axis=-1)


def _silu(x):
    return x * jax.nn.sigmoid(x)


def _layernorm(x, g, b):
    mu = jnp.mean(x, axis=-1, keepdims=True)
    xc = x - mu
    var = jnp.mean(xc * xc, axis=-1, keepdims=True)
    return xc * lax.rsqrt(var + LN_EPS) * g + b


MOD_ROWS = 8


def _mod_kernel(cctx_ref, c_ref, w_ref, b_ref, o_ref, s_ref):
    n_req = c_ref.shape[0]
    s_ref[...] = jnp.zeros_like(s_ref)
    s_ref[0:1, :] = _silu(cctx_ref[...])
    s_ref[1:1 + n_req, :] = _silu(c_ref[...])
    layer = pl.program_id(0)
    res = _dot(s_ref[...].astype(BF16), w_ref[0].astype(BF16)) + b_ref[pl.ds(layer, 1), :]
    for r in range(MOD_ROWS):
        o_ref[r] = res[r:r + 1, :]


def _modulation(c_ctx, c, w_ada, b_ada):
    depth, d, n = w_ada.shape
    tn = 1536
    return pl.pallas_call(
        _mod_kernel,
        grid=(depth, n // tn),
        in_specs=[pl.BlockSpec((1, d), lambda l, j: (0, 0)),
                  pl.BlockSpec(c.shape, lambda l, j: (0, 0)),
                  pl.BlockSpec((1, d, tn), lambda l, j: (l, 0, j)),
                  pl.BlockSpec((depth, tn), lambda l, j: (0, j))],
        out_specs=pl.BlockSpec((MOD_ROWS, 1, tn), lambda l, j: (l, 0, j)),
        out_shape=jax.ShapeDtypeStruct((depth * MOD_ROWS, 1, n), F32),
        scratch_shapes=[pltpu.VMEM((MOD_ROWS, d), F32)],
        compiler_params=_params(2),
        name="adaln_modulation",
    )(c_ctx.reshape(1, d), c, w_ada, b_ada)


def _in_proj_kernel(*refs, li, d, latent, n_alias):
    refs = list(refs)
    x_ref, mod_ref, w_ref, lbf_ref, lbb_ref, qn_ref, kn_ref = refs[:7]
    pos = 7
    if latent:
        cos, sin = refs[pos][...], refs[pos + 1][...]
        pos += 2
    pos += n_alias
    qa_o, hq_o, ff_o, fb_o, hv_o, hg_o, qc_o = refs[pos:pos + 7]
    if latent:
        ka_o, va_o, kc_o, vc_o = refs[pos + 7:pos + 11]
    else:
        ka_o, va_rows_o, kct_o, vct_o = refs[pos + 7:pos + 11]

    def store_kv(ref, val):
        for slot in range(ref.shape[1]):
            ref[0, slot] = val.astype(ref.dtype)

    mix_a, mix_b, mix_c = d // 2, d // 4, d // 4
    kv_c = mix_c // 2
    scale = HEAD_DIM ** -0.5 * math.log2(math.e)

    shift = mod_ref[0, :, 0:d]
    gain = mod_ref[0, :, d:2 * d]
    h = x_ref[...] * (1.0 + gain) + shift

    def proj(start, width):
        return _dot(h, w_ref[:, start:start + width])

    off_b = 3 * mix_a
    off_c = off_b + 5 * mix_b

    zq = proj(off_c, mix_c)
    zk = proj(off_c + mix_c, kv_c)
    vc = proj(off_c + mix_c + kv_c, kv_c)
    qa = proj(0, mix_a)
    msq = _group_mean_square(zq, HEAD_DIM)
    msk = _group_mean_square(zk, HEAD_DIM)
    ka = proj(mix_a, mix_a)
    va = proj(2 * mix_a, mix_a)
    qc = zq * lax.rsqrt(msq + RMS_EPS) * qn_ref[li:li + 1, :]
    kc = zk * lax.rsqrt(msk + RMS_EPS) * kn_ref[li:li + 1, :]

    if latent:
        qa = _rope(qa, cos, sin)
        ka = _rope(ka, cos, sin)
    qa_o[...] = (qa * scale).astype(qa_o.dtype)
    if latent:
        store_kv(ka_o, ka)
        store_kv(va_o, va)
    else:
        store_kv(ka_o, ka.T)
        heads = mix_a // LANES
        for slot in range(va_rows_o.shape[1]):
            for hd in range(heads):
                va_rows_o[0, slot, pl.ds(hd, ROW_TILE, stride=heads), :] = (
                    va[:, hd * LANES:(hd + 1) * LANES])

    def lower_bound(ref):
        logits = ref[...]
        e = jnp.exp(logits - jnp.max(logits, axis=0, keepdims=True))
        sm = e / jnp.sum(e, axis=0, keepdims=True)
        return jnp.sum(sm[0:li + 1], axis=0, keepdims=True) - sm[0:1]

    def forget(x, lb):
        return jnp.maximum(lb + (1.0 - lb) * jax.nn.sigmoid(x), F_MIN)

    off = off_b
    zb = [proj(off + j * mix_b, mix_b) for j in range(5)]

    if latent:
        kc = _rope(kc, cos, sin)
        qc = _rope(qc, cos, sin)
        store_kv(kc_o, kc)
        store_kv(vc_o, vc)
    else:
        store_kv(kct_o, kc.T)
        store_kv(vct_o, vc.T)
    qc = qc * scale
    lane = lax.broadcasted_iota(jnp.int32, (1, LANES), 1)
    for n in range(2):
        blk = qc[:, n * LANES:(n + 1) * LANES]
        in_half = (lane // HEAD_DIM) == n
        for g in range(2):
            src = blk if g == n else pltpu.roll(blk, HEAD_DIM, 1)
            hc = 2 * n + g
            qc_o[:, hc * LANES:(hc + 1) * LANES] = jnp.where(in_half, src, 0.0).astype(qc_o.dtype)

    hq_o[0] = _silu(zb[0])
    ff_o[0] = forget(zb[1], lower_bound(lbf_ref))
    fb_o[0] = forget(zb[2], lower_bound(lbb_ref))
    hv_o[0] = zb[3]
    hg_o[0] = _silu(zb[4])


def _in_proj(x, mod, mod_row, w_in, lb_f, lb_b, qn, kn, rope, kv_prev, *, li):
    bsz, t, d = x.shape
    latent = rope is not None
    depth, _, n_in = w_in.shape
    tiles = t // ROW_TILE
    mix_a, mix_b, mix_c = d // 2, d // 4, d // 4
    kv_c = mix_c // 2
    x2 = x.reshape(bsz * t, d)

    row = lambda i: (i, 0)
    brow = lambda i: (i // tiles, i % tiles, 0)
    const = lambda i: (0, 0)
    in_specs = [pl.BlockSpec((ROW_TILE, d), row),
                pl.BlockSpec((1, 1, mod.shape[-1]), lambda i: (mod_row(i * ROW_TILE), 0, 0)),
                pl.BlockSpec((None, d, n_in), lambda i: (li, 0, 0)),
                pl.BlockSpec(lb_f.shape, const), pl.BlockSpec(lb_b.shape, const),
                pl.BlockSpec(qn.shape, const), pl.BlockSpec(kn.shape, const)]
    args = [x2, mod, w_in, lb_f, lb_b, qn, kn]
    if latent:
        in_specs += [pl.BlockSpec((ROW_TILE, LANES), lambda i: (i % tiles, 0))] * 2
        args += list(rope)
        slots, slot0 = 1, 0
    else:
        slots, slot0 = (depth, 0) if kv_prev is None else (1, li)
    aliases = {}
    if kv_prev is not None:
        for j, buf in enumerate(kv_prev):
            aliases[len(args)] = 7 + j
            in_specs.append(pl.BlockSpec(memory_space=pl.ANY))
            args.append(buf)
    krow = lambda i: (i // tiles, slot0, i % tiles, 0)
    kcol = lambda i: (i // tiles, slot0, 0, i % tiles)
    heads = mix_a // LANES

    out_specs = [pl.BlockSpec((ROW_TILE, mix_a), row)]
    out_shape = [jax.ShapeDtypeStruct((bsz * t, mix_a), BF16)]
    out_specs += [pl.BlockSpec((1, ROW_TILE, mix_b), brow)] * 5
    out_shape += [jax.ShapeDtypeStruct((bsz, t, mix_b), F32)] * 5
    out_specs.append(pl.BlockSpec((ROW_TILE, 2 * mix_c), row))
    out_shape.append(jax.ShapeDtypeStruct((bsz * t, 2 * mix_c), BF16))
    if latent:
        for width in (mix_a, mix_a, kv_c, kv_c):
            out_specs.append(pl.BlockSpec((1, 1, ROW_TILE, width), krow))
            out_shape.append(jax.ShapeDtypeStruct((bsz, 1, t, width), BF16))
    else:
        out_specs += [pl.BlockSpec((1, slots, mix_a, ROW_TILE), kcol),
                      pl.BlockSpec((1, slots, ROW_TILE * heads, LANES), krow),
                      pl.BlockSpec((1, slots, kv_c, ROW_TILE), kcol),
                      pl.BlockSpec((1, slots, kv_c, ROW_TILE), kcol)]
        out_shape += [jax.ShapeDtypeStruct((bsz, depth, mix_a, t), F32),
                      jax.ShapeDtypeStruct((bsz, depth, t * heads, LANES), F32),
                      jax.ShapeDtypeStruct((bsz, depth, kv_c, t), F32),
                      jax.ShapeDtypeStruct((bsz, depth, kv_c, t), F32)]
    return pl.pallas_call(
        functools.partial(_in_proj_kernel, li=li, d=d, latent=latent, n_alias=len(aliases)),
        grid=(bsz * tiles,),
        in_specs=in_specs, out_specs=out_specs, out_shape=out_shape,
        input_output_aliases=aliases,
        compiler_params=_params(1),
        name="in_proj_latent" if latent else "in_proj_context",
    )(*args)


def _softmax_parts(scores):
    m = functools.reduce(jnp.maximum, [jnp.max(s, axis=-1, keepdims=True) for s in scores])
    es = [jnp.exp2(s - m) for s in scores]
    denom = functools.reduce(lambda a, b: a + b, [jnp.sum(e, axis=-1, keepdims=True) for e in es])
    return es, 1.0 / denom


def _score_blocks(q, keys):
    return [_dot(q, k) if k_t else lax.dot_general(q, k, NT, preferred_element_type=F32)
            for k, k_t in keys]


def _diff_lambda(lq1, lk1, lq2, lk2, li):
    lam_init = 0.8 - 0.6 * math.exp(-0.3 * li)

    def lam_term(a, b):
        return jnp.exp(jnp.sum(a[li:li + 1, :] * b[li:li + 1, :], axis=-1, keepdims=True))

    return lam_term(lq1, lk1) - lam_term(lq2, lk2) + lam_init, lam_init


def _diff_head_gen(q, keys, vals, lam, gain, store):
    tq = q.shape[0]
    lane = lax.broadcasted_iota(jnp.int32, (1, LANES), 1)
    zero = jnp.zeros_like(q)
    q2 = jnp.concatenate([jnp.where(lane < HEAD_DIM, q, zero),
                          jnp.where(lane >= HEAD_DIM, q, zero)], axis=0)
    scores = _score_blocks(q2, keys)
    yield
    es, r = _softmax_parts(scores)
    yield
    r0 = r[0:tq]
    ratio = r[tq:2 * tq] * lam / r0
    o = None
    for e, v in zip(es, vals):
        part = _dot((e[0:tq] - e[tq:2 * tq] * ratio).astype(BF16), v)
        o = part if o is None else o + part
    yield
    o = o * r0
    ms = jnp.mean(o * o, axis=-1, keepdims=True)
    store(o * lax.rsqrt(ms + RMS_EPS) * gain)


def _diff_attn_kernel(*refs, li, cached, heads, v_rows):
    if cached:
        (q_ref, k_ref, v_ref, ck_ref, cv_ref, lq1, lk1, lq2, lk2, sub_ref, o_ref) = refs
    else:
        (q_ref, k_ref, v_ref, lq1, lk1, lq2, lk2, sub_ref, o_ref) = refs
    lam, lam_init = _diff_lambda(lq1, lk1, lq2, lk2, li)
    gain = sub_ref[li:li + 1, :] * (1.0 - lam_init)

    def head_chain(j):
        c = slice(j * LANES, (j + 1) * LANES)
        head = pl.program_id(1) * heads + j
        if v_rows:
            t = k_ref.shape[3]
            keys = [(k_ref[0, 0, c, :].astype(BF16), True)]
            vals = [v_ref[0, 0, pl.ds(head, t, stride=v_ref.shape[2] // t), :].astype(BF16)]
        else:
            keys = [(k_ref[0, 0, :, c].astype(BF16), False)]
            vals = [v_ref[0, 0, :, c].astype(BF16)]
        if cached:
            past = ck_ref.shape[-1]
            all_heads = cv_ref.shape[2] // past
            keys.append((ck_ref[0, 0, c, :].astype(BF16), True))
            vals.append(cv_ref[0, 0, pl.ds(head, past, stride=all_heads), :].astype(BF16))

        def store(o):
            o_ref[:, c] = o.astype(o_ref.dtype)

        return _diff_head_gen(q_ref[:, c], keys, vals, lam, gain, store)

    _run_interleaved([head_chain(j) for j in range(heads)])


def _diff_attn(q, k, v, cache, lam, subln, *, li, kv_layer, bsz, tq, heads_per_step, v_rows):
    width = q.shape[-1]
    t = k.shape[3] if v_rows else k.shape[2]
    wstep = heads_per_step * LANES
    nq = t // tq
    cached = cache is not None
    if v_rows:
        kv_specs = [pl.BlockSpec((1, 1, wstep, t), lambda b, h, i: (b, kv_layer, h, 0)),
                    pl.BlockSpec((1, 1) + v.shape[2:], lambda b, h, i: (b, kv_layer, 0, 0))]
    else:
        kv_specs = [pl.BlockSpec((1, 1, t, wstep), lambda b, h, i: (b, kv_layer, 0, h))] * 2
    in_specs = [pl.BlockSpec((tq, wstep), lambda b, h, i: (b * nq + i, h))] + kv_specs
    args = [q, k, v]
    if cached:
        ck, cv = cache
        in_specs += [pl.BlockSpec((1, 1, wstep, ck.shape[-1]), lambda b, h, i: (b, li, h, 0)),
                     pl.BlockSpec((1, 1) + cv.shape[2:], lambda b, h, i: (b, li, 0, 0))]
        args += [ck, cv]
    in_specs += [pl.BlockSpec(a.shape, lambda b, h, i: (0, 0)) for a in (*lam, subln)]
    args += [*lam, subln]
    return pl.pallas_call(
        functools.partial(_diff_attn_kernel, li=li, cached=cached, heads=heads_per_step,
                          v_rows=v_rows),
        grid=(bsz, width // wstep, nq),
        in_specs=in_specs,
        out_specs=pl.BlockSpec((tq, wstep), lambda b, h, i: (b * nq + i, h)),
        out_shape=jax.ShapeDtypeStruct((bsz * t, width), BF16),
        compiler_params=_params(3),
        name="diff_attn_latent" if cached else "diff_attn_context",
    )(*args)


def _gqa_group_gen(n, q, keys, vals, store):
    tq = q.shape[0] // 2
    scores = _score_blocks(q, keys)
    yield
    es, r = _softmax_parts(scores)
    yield
    o = None
    for e, (v, v_t) in zip(es, vals):
        p = e.astype(BF16)
        part = lax.dot_general(p, v, NT, preferred_element_type=F32) if v_t else _dot(p, v)
        o = part if o is None else o + part
    yield
    o = o * r
    first = o[0:tq]
    second = o[tq:2 * tq]
    if n == 0:
        second = pltpu.roll(second, HEAD_DIM, 1)
    else:
        first = pltpu.roll(first, HEAD_DIM, 1)
    lane = lax.broadcasted_iota(jnp.int32, (1, LANES), 1)
    store(jnp.where(lane < HEAD_DIM, first, second))


def _gqa_kernel(*refs, cached, own_t):
    if cached:
        q_ref, k_ref, v_ref, ck_ref, cv_ref, o_ref = refs
    else:
        q_ref, k_ref, v_ref, o_ref = refs
    heads = q_ref.shape[1] // LANES
    keys = [(k_ref[0, 0].astype(BF16), own_t)]
    vals = [(v_ref[0, 0].astype(BF16), own_t)]
    if cached:
        keys.append((ck_ref[0, 0].astype(BF16), True))
        vals.append((cv_ref[0, 0].astype(BF16), True))

    def group_chain(n):
        q = jnp.concatenate([q_ref[:, (2 * n + g) * LANES:(2 * n + g + 1) * LANES]
                             for g in range(2)], axis=0)

        def store(o):
            o_ref[:, n * LANES:(n + 1) * LANES] = o.astype(o_ref.dtype)

        return _gqa_group_gen(n, q, keys, vals, store)

    _run_interleaved([group_chain(n) for n in range(heads // 2)])


def _gqa(q, k, v, cache, *, li, kv_layer, bsz, tq, own_t):
    t, kvw = (k.shape[3], k.shape[2]) if own_t else (k.shape[2], k.shape[3])
    nq = t // tq
    cached = cache is not None
    in_specs = [pl.BlockSpec((tq, q.shape[-1]), lambda b, i: (b * nq + i, 0)),
                pl.BlockSpec((1, 1) + k.shape[2:], lambda b, i: (b, kv_layer, 0, 0)),
                pl.BlockSpec((1, 1) + k.shape[2:], lambda b, i: (b, kv_layer, 0, 0))]
    args = [q, k, v]
    if cached:
        in_specs += [pl.BlockSpec((1, 1) + cache[0].shape[2:], lambda b, i: (b, li, 0, 0))] * 2
        args += list(cache)
    return pl.pallas_call(
        functools.partial(_gqa_kernel, cached=cached, own_t=own_t),
        grid=(bsz, nq),
        in_specs=in_specs,
        out_specs=pl.BlockSpec((tq, 2 * kvw), lambda b, i: (b * nq + i, 0)),
        out_shape=jax.ShapeDtypeStruct((bsz * t, 2 * kvw), BF16),
        compiler_params=_params(2),
        name="gqa_latent" if cached else "gqa_context",
    )(*args)


def _head_masks(width):
    lane_head = lax.broadcasted_iota(jnp.int32, (1, width), 1) // HEAD_DIM
    return [lane_head == h for h in range(width // HEAD_DIM)]


def _stack_heads(x, masks):
    return jnp.concatenate([jnp.where(m, x, jnp.zeros_like(x)) for m in masks], axis=0)


def _block_diag_mask(width):
    r = lax.broadcasted_iota(jnp.int32, (width, width), 0) // HEAD_DIM
    c = lax.broadcasted_iota(jnp.int32, (width, width), 1) // HEAD_DIM
    return r == c


def _ref_rows(b, offsets, span):
    width = b.shape[-1]
    return jnp.concatenate([jnp.broadcast_to(b[o:o + 1], (span, width)) for o in offsets], axis=0)


def _run_interleaved(gens, delays=None):
    live = list(zip(gens, delays or [0] * len(gens)))
    rnd = 0
    while live:
        for item in list(live):
            if item[1] > rnd:
                continue
            try:
                next(item[0])
            except StopIteration:
                live.remove(item)
        rnd += 1


def _hgrn_chunks(problems):
    out = []
    _run_interleaved([_hgrn_chunks_gen(problems, out)])
    return out


def _hgrn_chunks_gen(problems, out):
    n = len(problems)
    c, width = problems[0][0].shape
    qs = [p[0] for p in problems]
    vs = [p[2] for p in problems]
    sts = [p[3] for p in problems]
    rev = [p[4] for p in problems]
    chains = range(n)
    masks = _head_masks(width)
    trow =lax.broadcasted_iota(jnp.int32, (c, 1), 0)
    t_full = lax.broadcasted_iota(jnp.int32, (c, width), 0)
    s_full = lax.broadcasted_iota(jnp.int32, (c, width), 1) % c

    ks = [1.0 - p[1] for p in problems]
    b = [jnp.log(p[1]) * math.log2(math.e) for p in problems]
    step = 1
    while step < c:
        for j in chains:
            if rev[j]:
                b[j] = b[j] + jnp.where(trow < c - step, pltpu.roll(b[j], c - step, 0), 0.0)
            else:
                b[j] = b[j] + jnp.where(trow >= step, pltpu.roll(b[j], step, 0), 0.0)
        step *= 2
        yield
    b_end = [b[j][0:1] if rev[j] else b[j][c - 1:c] for j in chains]

    o = [lax.dot_general((qs[j] * jnp.exp2(b[j])).astype(BF16), sts[j].astype(BF16), NT,
                         preferred_element_type=F32) for j in chains]
    yield

    a = [None] * n
    m = c // 2
    while m >= DIAG_BLOCK:
        blocks = c // (2 * m)
        same = (t_full // (2 * m)) == (s_full // (2 * m))
        for j in chains:
            ref = _ref_rows(b[j], [i * 2 * m + (m if rev[j] else m - 1) for i in range(blocks)],
                            2 * m)
            is_q = ((trow % (2 * m)) < m) if rev[j] else ((trow % (2 * m)) >= m)
            e = jnp.exp2((b[j] - ref) * jnp.where(is_q, 1.0, -1.0))
            ql = jnp.where(is_q, qs[j] * e, 0.0).astype(BF16)
            kl = jnp.where(is_q, 0.0, ks[j] * e).astype(BF16)
            al = lax.dot_general(ql, _stack_heads(kl, masks), NT, preferred_element_type=F32)
            if blocks > 1:
                al = jnp.where(same, al, 0.0)
            a[j] = al if a[j] is None else a[j] + al
        m //= 2
        yield
    blocks = c // DIAG_BLOCK
    mid = DIAG_BLOCK // 2
    same = (t_full // DIAG_BLOCK) == (s_full // DIAG_BLOCK)
    for j in chains:
        ref = _ref_rows(b[j], [i * DIAG_BLOCK + (mid if rev[j] else mid - 1) for i in range(blocks)],
                        DIAG_BLOCK)
        d = b[j] - ref
        ql = (qs[j] * jnp.exp2(d)).astype(BF16)
        kl = (ks[j] * jnp.exp2(-d)).astype(BF16)
        al = lax.dot_general(ql, _stack_heads(kl, masks), NT, preferred_element_type=F32)
        causal = (s_full >= t_full) if rev[j] else (s_full <= t_full)
        a[j] = a[j] + jnp.where(same & causal, al, 0.0)
    yield

    v_b = [v.astype(BF16) for v in vs]
    o = [o[j] + _dot(a[j].astype(BF16), _stack_heads(v_b[j], masks)) for j in chains]
    yield

    bd = _block_diag_mask(width)
    upd = [lax.dot_general(v_b[j], (ks[j] * jnp.exp2(b_end[j] - b[j])).astype(BF16), TN,
                           preferred_element_type=F32) for j in chains]
    st_new = [sts[j] * jnp.exp2(b_end[j]) + jnp.where(bd, upd[j], 0.0) for j in chains]
    out.extend(zip(o, st_new))


def _mxu_transpose(x):
    n = x.shape[1]
    r = lax.broadcasted_iota(jnp.int32, (n, n), 0)
    c = lax.broadcasted_iota(jnp.int32, (n, n), 1)
    eye = (r == c).astype(BF16)
    acc = None
    rem = x
    for _ in range(3):
        piece = rem.astype(BF16)
        rem = rem - piece.astype(F32)
        part = lax.dot_general(eye, piece, NT, preferred_element_type=F32)
        acc = part if acc is None else acc + part
    return acc


def _hgrn_kernel(*refs, li, has_state, want_state, n_alias, heads):
    refs = list(refs)
    q_ref, ff_ref, fb_ref, v_ref, gate_ref, gn_ref = refs[:6]
    pos = 6
    if has_state:
        s0f_ref, s0b_ref = refs[pos:pos + 2]
        pos += 2
    pos += n_alias
    o_ref = refs[pos]
    pos += 1
    if want_state:
        sf_ref, sb_ref = refs[pos:pos + 2]
        pos += 2
    st_ref, of_ref, ob_ref = refs[pos:pos + 3]
    nb, t, width = q_ref.shape
    nc = t // CHUNK
    bd = _block_diag_mask(width)

    for n in range(nb):
        for d in range(2):
            if has_state:
                x = (s0b_ref if d else s0f_ref)[n, 0].reshape(width, HEAD_DIM)
                xt = _mxu_transpose(x)
                st_ref[2 * n + d] = jnp.where(bd, jnp.concatenate([xt] * heads, axis=0), 0.0)
            else:
                st_ref[2 * n + d] = jnp.zeros((width, width), F32)

    def body(ci, carry):
        rows = (pl.ds(pl.multiple_of(ci * CHUNK, CHUNK), CHUNK),
                pl.ds(pl.multiple_of((nc - 1 - ci) * CHUNK, CHUNK), CHUNK))
        loaded = []
        for n in range(nb):
            for d, f_ref in enumerate((ff_ref, fb_ref)):
                r = rows[d]
                loaded.append((q_ref[n, r, :], f_ref[n, r, :], v_ref[n, r, :], st_ref[2 * n + d],
                               bool(d)))
        for j, (o, st) in enumerate(_hgrn_chunks(loaded)):
            n, d = divmod(j, 2)
            (ob_ref if d else of_ref)[n, rows[d], :] = o
            st_ref[j] = st
        return carry

    lax.fori_loop(0, nc, body, 0)

    for n in range(nb):
        o = of_ref[n] + ob_ref[n]
        o_ref[n] = (_group_rms(o, gn_ref[li:li + 1, :], HEAD_DIM) * gate_ref[n]).astype(o_ref.dtype)

    if want_state:
        for n in range(nb):
            for d, dst in enumerate((sf_ref, sb_ref)):
                st = st_ref[2 * n + d]
                rows = st[0:HEAD_DIM]
                for h in range(1, heads):
                    rows = rows + st[h * HEAD_DIM:(h + 1) * HEAD_DIM]
                final = _mxu_transpose(rows).reshape(heads, HEAD_DIM, HEAD_DIM)
                for slot in range(dst.shape[1]):
                    dst[n, slot] = final


def _hgrn(hq, ff, fb, hv, hg, gn, state, state_prev, *, li, depth, want_state, nb):
    bsz, t, width = hq.shape
    heads = width // HEAD_DIM
    has_state = state is not None
    seq = pl.BlockSpec((nb, t, width), lambda b: (b, 0, 0))
    in_specs = [seq] * 5 + [pl.BlockSpec(gn.shape, lambda b: (0, 0))]
    args = [hq, ff, fb, hv, hg, gn]
    if has_state:
        in_specs += [pl.BlockSpec((nb, 1, heads, HEAD_DIM, HEAD_DIM), lambda b: (b, li, 0, 0, 0))] * 2
        args += list(state)
    aliases = {}
    if state_prev is not None:
        for j, buf in enumerate(state_prev):
            aliases[len(args)] = 1 + j
            in_specs.append(pl.BlockSpec(memory_space=pl.ANY))
            args.append(buf)
    out_specs = [seq]
    out_shape = [jax.ShapeDtypeStruct((bsz, t, width), BF16)]
    if want_state:
        slots, slot0 = (depth, 0) if state_prev is None else (1, li)
        out_specs += [pl.BlockSpec((nb, slots, heads, HEAD_DIM, HEAD_DIM),
                                   lambda b: (b, slot0, 0, 0, 0))] * 2
        out_shape += [jax.ShapeDtypeStruct((bsz, depth, heads, HEAD_DIM, HEAD_DIM), F32)] * 2
    return pl.pallas_call(
        functools.partial(_hgrn_kernel, li=li, has_state=has_state, want_state=want_state,
                          n_alias=len(aliases), heads=heads),
        grid=(bsz // nb,),
        in_specs=in_specs, out_specs=out_specs, out_shape=out_shape,
        input_output_aliases=aliases,
        scratch_shapes=[pltpu.VMEM((2 * nb, width, width), F32),
                        pltpu.VMEM((nb, t, width), F32), pltpu.VMEM((nb, t, width), F32)],
        compiler_params=_params(1),
        name="hgrn2_latent" if has_state else "hgrn2_context",
    )(*args)


ATTENTION_DELAY = 5


def _mixer_ctx_kernel(*refs, li, n_alias):
    refs = list(refs)
    (qa_ref, kat_ref, va_ref, qc_ref, kct_ref, vct_ref, hq_ref, ff_ref, fb_ref, hv_ref, hg_ref,
     gn_ref, lq1, lk1, lq2, lk2, sub_ref) = refs[:17]
    pos = 17 + n_alias
    oa_ref, oc_ref, ob_ref, sf_ref, sb_ref = refs[pos:pos + 5]
    st_ref, of_ref, obk_ref = refs[pos + 5:pos + 8]
    nb, t, width = hq_ref.shape
    nc = t // CHUNK
    heads_b = width // HEAD_DIM
    heads_a = qa_ref.shape[1] // LANES
    groups_c = qc_ref.shape[1] // LANES // 2
    lam, lam_init = _diff_lambda(lq1, lk1, lq2, lk2, li)
    gain = sub_ref[li:li + 1, :] * (1.0 - lam_init)

    for j in range(2 * nb):
        st_ref[j] = jnp.zeros((width, width), F32)

    def diff_unit(n, h):
        rows = slice(n * t, (n + 1) * t)
        c = slice(h * LANES, (h + 1) * LANES)
        keys = [(kat_ref[n, 0, c, :].astype(BF16), True)]
        vals = [va_ref[n, 0, pl.ds(h, t, stride=heads_a), :].astype(BF16)]

        def store(o):
            oa_ref[rows, c] = o.astype(oa_ref.dtype)

        yield from _diff_head_gen(qa_ref[rows, c], keys, vals, lam, gain, store)

    def gqa_unit(n, g):
        rows = slice(n * t, (n + 1) * t)
        q = jnp.concatenate([qc_ref[rows, (2 * g + j) * LANES:(2 * g + j + 1) * LANES]
                             for j in range(2)], axis=0)
        keys = [(kct_ref[n, 0].astype(BF16), True)]
        vals = [(vct_ref[n, 0].astype(BF16), True)]

        def store(o):
            oc_ref[rows, g * LANES:(g + 1) * LANES] = o.astype(oc_ref.dtype)

        yield from _gqa_group_gen(g, q, keys, vals, store)

    def scan_step(ci):
        rows = (slice(ci * CHUNK, (ci + 1) * CHUNK), slice((nc - 1 - ci) * CHUNK, (nc - ci) * CHUNK))
        loaded = []
        for n in range(nb):
            for d, f_ref in enumerate((ff_ref, fb_ref)):
                r = rows[d]
                loaded.append((hq_ref[n, r, :], f_ref[n, r, :], hv_ref[n, r, :], st_ref[2 * n + d],
                               bool(d)))
        out = []
        yield from _hgrn_chunks_gen(loaded, out)
        for j, (o, st) in enumerate(out):
            n, d = divmod(j, 2)
            (obk_ref if d else of_ref)[n, rows[d], :] = o
            st_ref[j] = st

    units = []
    for n in range(nb):
        units += [diff_unit(n, h) for h in range(heads_a)]
        units += [gqa_unit(n, g) for g in range(groups_c)]
    share = -(-len(units) // nc)
    for ci in range(nc):
        mine = units[ci * share:(ci + 1) * share]
        _run_interleaved([scan_step(ci)] + mine, [0] + [ATTENTION_DELAY] * len(mine))

    for n in range(nb):
        o = of_ref[n] + obk_ref[n]
        ob_ref[n] = (_group_rms(o, gn_ref[li:li + 1, :], HEAD_DIM) * hg_ref[n]).astype(ob_ref.dtype)
        for d, dst in enumerate((sf_ref, sb_ref)):
            st = st_ref[2 * n + d]
            rows = st[0:HEAD_DIM]
            for h in range(1, heads_b):
                rows = rows + st[h * HEAD_DIM:(h + 1) * HEAD_DIM]
            final = _mxu_transpose(rows).reshape(heads_b, HEAD_DIM, HEAD_DIM)
            for slot in range(dst.shape[1]):
                dst[n, slot] = final


def _mixer_ctx(qa, ka_t, va_rows, qc, kc_t, vc_t, hq, ff, fb, hv, hg, gn, lam, subln, state_prev,
               *, li, depth, nb):
    bsz, t, width = hq.shape
    heads_b = width // HEAD_DIM
    rows = lambda b: (b, 0)
    at_layer = lambda b: (b, li, 0, 0)
    seq = pl.BlockSpec((nb, t, width), lambda b: (b, 0, 0))
    const = lambda b: (0, 0)
    in_specs = [pl.BlockSpec((nb * t, qa.shape[1]), rows),
                pl.BlockSpec((nb, 1) + ka_t.shape[2:], at_layer),
                pl.BlockSpec((nb, 1) + va_rows.shape[2:], at_layer),
                pl.BlockSpec((nb * t, qc.shape[1]), rows),
                pl.BlockSpec((nb, 1) + kc_t.shape[2:], at_layer),
                pl.BlockSpec((nb, 1) + vc_t.shape[2:], at_layer),
                seq, seq, seq, seq, seq, pl.BlockSpec(gn.shape, const)]
    in_specs += [pl.BlockSpec(a.shape, const) for a in (*lam, subln)]
    args = [qa, ka_t, va_rows, qc, kc_t, vc_t, hq, ff, fb, hv, hg, gn, *lam, subln]
    aliases = {}
    if state_prev is not None:
        for j, buf in enumerate(state_prev):
            aliases[len(args)] = 3 + j
            in_specs.append(pl.BlockSpec(memory_space=pl.ANY))
            args.append(buf)
    slots, slot0 = (depth, 0) if state_prev is None else (1, li)
    state_spec = pl.BlockSpec((nb, slots, heads_b, HEAD_DIM, HEAD_DIM), lambda b: (b, slot0, 0, 0, 0))
    out_specs = [pl.BlockSpec((nb * t, qa.shape[1]), rows),
                 pl.BlockSpec((nb * t, qc.shape[1] // 2), rows), seq, state_spec, state_spec]
    out_shape = [jax.ShapeDtypeStruct(qa.shape, BF16),
                 jax.ShapeDtypeStruct((qc.shape[0], qc.shape[1] // 2), BF16),
                 jax.ShapeDtypeStruct((bsz, t, width), BF16)]
    out_shape += [jax.ShapeDtypeStruct((bsz, depth, heads_b, HEAD_DIM, HEAD_DIM), F32)] * 2
    return pl.pallas_call(
        functools.partial(_mixer_ctx_kernel, li=li, n_alias=len(aliases)),
        grid=(bsz // nb,),
        in_specs=in_specs, out_specs=out_specs, out_shape=out_shape,
        input_output_aliases=aliases,
        scratch_shapes=[pltpu.VMEM((2 * nb, width, width), F32),
                        pltpu.VMEM((nb, t, width), F32), pltpu.VMEM((nb, t, width), F32)],
        compiler_params=_params(1),
        name="mixers_context",
    )(*args)


def _mixer_lat_kernel(qa_ref, k_ref, v_ref, ck_ref, cv_ref, qc_ref, kc_ref, vc_ref, cck_ref, ccv_ref,
                      hq_ref, ff_ref, fb_ref, hv_ref, hg_ref, gn_ref, s0f_ref, s0b_ref,
                      lq1, lk1, lq2, lk2, sub_ref, oa_ref, oc_ref, ob_ref,
                      st_ref, of_ref, obk_ref, *, li, tq):
    _, t, width = hq_ref.shape
    nc = t // CHUNK
    nq = t // tq
    steps_per_trip = nc // nq
    heads_b = width // HEAD_DIM
    heads_a = qa_ref.shape[1] // LANES
    groups_c = qc_ref.shape[1] // LANES // 2
    past = ck_ref.shape[-1]
    lam, lam_init = _diff_lambda(lq1, lk1, lq2, lk2, li)
    gain = sub_ref[li:li + 1, :] * (1.0 - lam_init)
    bd = _block_diag_mask(width)

    for d, src in enumerate((s0f_ref, s0b_ref)):
        x = src[0, 0].reshape(width, HEAD_DIM)
        xt = _mxu_transpose(x)
        st_ref[d] = jnp.where(bd, jnp.concatenate([xt] * heads_b, axis=0), 0.0)

    def trip(qt, carry):
        q_rows = pl.ds(pl.multiple_of(qt * tq, tq), tq)

        def diff_unit(h):
            c = slice(h * LANES, (h + 1) * LANES)
            keys = [(k_ref[0, 0, :, c], False), (ck_ref[0, 0, c, :].astype(BF16), True)]
            vals = [v_ref[0, 0, :, c], cv_ref[0, 0, pl.ds(h, past, stride=heads_a), :].astype(BF16)]

            def store(o):
                oa_ref[q_rows, c] = o.astype(oa_ref.dtype)

            yield from _diff_head_gen(qa_ref[q_rows, c], keys, vals, lam, gain, store)

        def gqa_unit(g):
            q = jnp.concatenate([qc_ref[q_rows, (2 * g + j) * LANES:(2 * g + j + 1) * LANES]
                                 for j in range(2)], axis=0)
            keys = [(kc_ref[0, 0], False), (cck_ref[0, 0].astype(BF16), True)]
            vals = [(vc_ref[0, 0], False), (ccv_ref[0, 0].astype(BF16), True)]

            def store(o):
                oc_ref[q_rows, g * LANES:(g + 1) * LANES] = o.astype(oc_ref.dtype)

            yield from _gqa_group_gen(g, q, keys, vals, store)

        def scan_step(cj):
            ci = qt * steps_per_trip + cj
            rows = (pl.ds(pl.multiple_of(ci * CHUNK, CHUNK), CHUNK),
                    pl.ds(pl.multiple_of((nc - 1 - ci) * CHUNK, CHUNK), CHUNK))
            loaded = [(hq_ref[0, rows[d], :], f_ref[0, rows[d], :], hv_ref[0, rows[d], :],
                       st_ref[d], bool(d)) for d, f_ref in enumerate((ff_ref, fb_ref))]
            out = []
            yield from _hgrn_chunks_gen(loaded, out)
            for d, (o, st) in enumerate(out):
                (obk_ref if d else of_ref)[rows[d], :] = o
                st_ref[d] = st

        units = [diff_unit(h) for h in range(heads_a)] + [gqa_unit(g) for g in range(groups_c)]
        for cj in range(steps_per_trip):
            mine = units[cj::steps_per_trip]
            _run_interleaved([scan_step(cj)] + mine, [0] + [ATTENTION_DELAY] * len(mine))
        return carry

    lax.fori_loop(0, nq, trip, 0)

    o = of_ref[...] + obk_ref[...]
    ob_ref[0] = (_group_rms(o, gn_ref[li:li + 1, :], HEAD_DIM) * hg_ref[0]).astype(ob_ref.dtype)


def _mixer_lat(qa, ka, va, qc, kc, vc, cache, hq, ff, fb, hv, hg, gn, state, lam, subln, *, li, tq):
    bsz, t, width = hq.shape
    rows = lambda b: (b, 0)
    own = lambda b: (b, 0, 0, 0)
    at_layer = lambda b: (b, li, 0, 0)
    seq = pl.BlockSpec((1, t, width), lambda b: (b, 0, 0))
    const = lambda b: (0, 0)
    in_specs = [pl.BlockSpec((t, qa.shape[1]), rows),
                pl.BlockSpec((1,) + ka.shape[1:], own), pl.BlockSpec((1,) + va.shape[1:], own),
                pl.BlockSpec((1, 1) + cache[0].shape[2:], at_layer),
                pl.BlockSpec((1, 1) + cache[1].shape[2:], at_layer),
                pl.BlockSpec((t, qc.shape[1]), rows),
                pl.BlockSpec((1,) + kc.shape[1:], own), pl.BlockSpec((1,) + vc.shape[1:], own),
                pl.BlockSpec((1, 1) + cache[2].shape[2:], at_layer),
                pl.BlockSpec((1, 1) + cache[3].shape[2:], at_layer),
                seq, seq, seq, seq, seq, pl.BlockSpec(gn.shape, const)]
    in_specs += [pl.BlockSpec((1, 1) + state[0].shape[2:], lambda b: (b, li, 0, 0, 0))] * 2
    in_specs += [pl.BlockSpec(a.shape, const) for a in (*lam, subln)]
    args = [qa, ka, va, cache[0], cache[1], qc, kc, vc, cache[2], cache[3], hq, ff, fb, hv, hg, gn,
            *state, *lam, subln]
    return pl.pallas_call(
        functools.partial(_mixer_lat_kernel, li=li, tq=tq),
        grid=(bsz,),
        in_specs=in_specs,
        out_specs=[pl.BlockSpec((t, qa.shape[1]), rows),
                   pl.BlockSpec((t, qc.shape[1] // 2), rows), seq],
        out_shape=[jax.ShapeDtypeStruct(qa.shape, BF16),
                   jax.ShapeDtypeStruct((qc.shape[0], qc.shape[1] // 2), BF16),
                   jax.ShapeDtypeStruct((bsz, t, width), BF16)],
        scratch_shapes=[pltpu.VMEM((2, width, width), F32),
                        pltpu.VMEM((t, width), F32), pltpu.VMEM((t, width), F32)],
        compiler_params=_params(1),
        name="mixers_latent",
    )(*args)


def _out_mlp_kernel(*refs, li, d, alpha, ff_chunk, cast_next):
    (x_ref, oa_ref, ob_ref, oc_ref, mod_ref, wo_ref, w1_ref, w2_ref,
     g1_ref, b1_ref, g2_ref, b2_ref) = refs[:12]
    if cast_next:
        w1n_ref, w2n_ref, y_ref, w1n_o, w2n_o = refs[12:]
        w1n_o[...] = w1n_ref[...].astype(BF16)
        w2n_o[...] = w2n_ref[...].astype(BF16)
    else:
        y_ref = refs[12]
    wa, wb = oa_ref.shape[-1], ob_ref.shape[-1]
    layer = slice(li, li + 1)
    gate1 = mod_ref[0, :, 2 * d:3 * d]
    shift2 = mod_ref[0, :, 3 * d:4 * d]
    gain2 = mod_ref[0, :, 4 * d:5 * d]
    gate2 = mod_ref[0, :, 5 * d:6 * d]
    subs = [slice(s * ROW_TILE, (s + 1) * ROW_TILE) for s in range(x_ref.shape[0] // ROW_TILE)]
    wo = wo_ref[...].astype(BF16)
    m = [_dot(oa_ref[r, :], wo[0:wa, :]) + _dot(ob_ref[r, :], wo[wa:wa + wb, :])
         + _dot(oc_ref[r, :], wo[wa + wb:, :]) for r in subs]
    x1 = [_layernorm(alpha * x_ref[r, :] + gate1 * mi, g1_ref[layer, :], b1_ref[layer, :])
          for r, mi in zip(subs, m)]
    h2 = [(xi * (1.0 + gain2) + shift2).astype(BF16) for xi in x1]
    acc = [None] * len(subs)
    for j in range(w1_ref.shape[-1] // ff_chunk):
        cols = slice(j * ff_chunk, (j + 1) * ff_chunk)
        hid = [jnp.maximum(_dot(hi, w1_ref[:, cols]), 0.0) for hi in h2]
        for s, hd in enumerate(hid):
            part = _dot((hd * hd).astype(BF16), w2_ref[cols, :])
            acc[s] = part if acc[s] is None else acc[s] + part
    for r, xi, ai in zip(subs, x1, acc):
        y_ref[r, :] = _layernorm(alpha * xi + gate2 * ai, g2_ref[layer, :], b2_ref[layer, :])


def _out_mlp(x, oa, ob, oc, mod, mod_row, w_out, w_ff1, w_ff2, ln, next_ff, *, li, alpha):
    bsz, t, d = x.shape
    rows = 2 * ROW_TILE
    steps = bsz * t // rows
    row = lambda i: (i, 0)
    const = lambda i: (0, 0)
    resident = lambda a: pl.BlockSpec(a.shape, const, pipeline_mode=pl.Buffered(1))
    in_specs = [pl.BlockSpec((rows, d), row),
                pl.BlockSpec((rows, oa.shape[-1]), row),
                pl.BlockSpec((rows, ob.shape[-1]), row),
                pl.BlockSpec((rows, oc.shape[-1]), row),
                pl.BlockSpec((1, 1, mod.shape[-1]), lambda i: (mod_row(i * rows), 0, 0)),
                pl.BlockSpec((None,) + w_out.shape[1:], lambda i: (li, 0, 0),
                             pipeline_mode=pl.Buffered(1)),
                resident(w_ff1), resident(w_ff2)]
    in_specs += [pl.BlockSpec(a.shape, const) for a in ln]
    args = [x.reshape(bsz * t, d), oa, ob.reshape(bsz * t, -1), oc, mod, w_out, w_ff1, w_ff2, *ln]
    out_specs = [pl.BlockSpec((rows, d), row)]
    out_shape = [jax.ShapeDtypeStruct((bsz * t, d), F32)]
    if next_ff is not None:
        for w in next_ff:
            blk = (w.shape[1] // steps, w.shape[2])
            in_specs.append(pl.BlockSpec((None,) + blk, lambda i: (li + 1, i, 0)))
            args.append(w)
            out_specs.append(pl.BlockSpec(blk, row))
            out_shape.append(jax.ShapeDtypeStruct(w.shape[1:], BF16))
    res = pl.pallas_call(
        functools.partial(_out_mlp_kernel, li=li, d=d, alpha=alpha, ff_chunk=1024,
                          cast_next=next_ff is not None),
        grid=(steps,),
        in_specs=in_specs, out_specs=out_specs, out_shape=out_shape,
        compiler_params=_params(1),
        name="out_mlp",
    )(*args)
    return res[0].reshape(bsz, t, d), tuple(res[1:])


def _rope_tables(n_tokens):
    pairs = HEAD_DIM // 4
    tok = np.arange(n_tokens)
    row = (tok // GRID_W).astype(np.float64)
    col = (tok % GRID_W).astype(np.float64)
    inv = ROPE_THETA ** (-np.arange(pairs, dtype=np.float64) / pairs)
    ang = np.concatenate([row[:, None] * inv, col[:, None] * inv], axis=-1)
    lane = np.arange(LANES)
    pair = (lane % HEAD_DIM) // 2
    sign = np.where(lane % 2 == 0, -1.0, 1.0)
    return (jnp.asarray(np.cos(ang)[:, pair], F32), jnp.asarray(np.sin(ang)[:, pair] * sign, F32))


def kernel(x_prompt, x_sample, cache_a_k, cache_a_v, cache_c_k, cache_c_v, state_b_fwd, state_b_bwd, c, c_ctx, w_ada, b_ada, w_in, w_out, lam_q1, lam_k1, lam_q2, lam_k2, subln_g, lb_logits_fwd, lb_logits_bwd, gnorm_g, qnorm_g, knorm_g, ln1_g, ln1_b, ln2_g, ln2_b, w_ff1, w_ff2):
    depth = w_in.shape[0]
    bsz, seq, d = x_prompt.shape
    dec_bsz, dec_seq, _ = x_sample.shape
    past = cache_a_k.shape[2]
    alpha = (2 * depth) ** 0.25
    mix_a, mix_b, mix_c = d // 2, d // 4, d // 4

    mod = _modulation(c_ctx, c, w_ada, b_ada)
    rope = _rope_tables(dec_seq)

    cache = (cache_a_k.transpose(0, 1, 3, 4, 5, 2).reshape(dec_bsz, depth, mix_a, past),
             cache_a_v.reshape(dec_bsz, depth, past * (mix_a // LANES), LANES),
             cache_c_k.transpose(0, 1, 3, 4, 2).reshape(dec_bsz, depth, mix_c // 2, past),
             cache_c_v.transpose(0, 1, 3, 4, 2).reshape(dec_bsz, depth, mix_c // 2, past))
    lam = (lam_q1, lam_k1, lam_q2, lam_k2)

    ff_b = [(w_ff1[0].astype(BF16), w_ff2[0].astype(BF16))]
    qn =jnp.tile(qnorm_g, (1, mix_c // HEAD_DIM))
    kn = jnp.tile(knorm_g, (1, mix_c // 2 // HEAD_DIM))
    gn = jnp.tile(gnorm_g, (1, mix_b // HEAD_DIM))
    ln = (ln1_g, ln1_b, ln2_g, ln2_b)

    def stream(x, li, latent, own_prev):
        n, t, _ = x.shape
        mod_row = ((lambda r0: li * MOD_ROWS + 1 + r0 // t) if latent
                   else (lambda r0: li * MOD_ROWS))
        (qa, hq, ff, fb, hv, hg, qc, ka, va, kc, vc) = _in_proj(
            x, mod, mod_row, w_in, lb_logits_fwd, lb_logits_bwd, qn, kn,
            rope if latent else None, None if own_prev is None else own_prev[0:4], li=li)
        if latent:
            oa, oc, ob = _mixer_lat(qa, ka, va, qc, kc, vc, cache, hq, ff, fb, hv, hg, gn,
                                    (state_b_fwd, state_b_bwd), lam, subln_g, li=li, tq=256)
            own = None
        else:
            oa, oc, ob, s_f, s_b = _mixer_ctx(
                qa, ka, va, qc, kc, vc, hq, ff, fb, hv, hg, gn, lam, subln_g,
                None if own_prev is None else own_prev[4:6], li=li, depth=depth, nb=2)
            own = (ka, va, kc, vc, s_f, s_b)
        cast_next = not latent and li + 1 < depth
        y, next_ff = _out_mlp(x, oa, ob, oc, mod, mod_row, w_out, *ff_b[li], ln,
                              (w_ff1, w_ff2) if cast_next else None, li=li, alpha=alpha)
        if cast_next:
            ff_b.append(next_ff)
        return y, own

    y_prompt, y_sample = x_prompt, x_sample
    own = None
    for li in range(depth):
        y_prompt, own = stream(y_prompt, li, False, own)
        y_sample, _ = stream(y_sample, li, True, None)

    heads_a = mix_a // (2 * HEAD_DIM)
    new_a_k = own[0].reshape(bsz, depth, heads_a, 2, HEAD_DIM, seq).transpose(0, 1, 5, 2, 3, 4)
    new_a_v = own[1].reshape(bsz, depth, seq, heads_a, 2 * HEAD_DIM)
    kv_heads = mix_c // 2 // HEAD_DIM
    new_c_k = own[2].reshape(bsz, depth, kv_heads, HEAD_DIM, seq).transpose(0, 1, 4, 2, 3)
    new_c_v = own[3].reshape(bsz, depth, kv_heads, HEAD_DIM, seq).transpose(0, 1, 4, 2, 3)
    return (y_prompt, y_sample, new_a_k, new_a_v, new_c_k, new_c_v, own[4], own[5])
```

```python
import functools
import math

import jax
import jax.numpy as jnp
import numpy as np
from jax import lax
from jax.experimental import pallas as pl
from jax.experimental.pallas import tpu as pltpu

GRID_W = 64
HEAD_DIM = 64
ROPE_THETA = 10000.0
LN_EPS = 1e-6
RMS_EPS = 1e-6
F_MIN = 1e-6
CHUNK = 64
DIAG_BLOCK = 8
LANES = 128
ROW_TILE = 256
VMEM_LIMIT = 56 * 1024 * 1024

F32 = jnp.float32
BF16 = jnp.bfloat16
NT = (((1,), (1,)), ((), ()))
TN = (((0,), (0,)), ((), ()))


def _params(n_grid):
    return pltpu.CompilerParams(dimension_semantics=("arbitrary",) * n_grid,
                                vmem_limit_bytes=VMEM_LIMIT)


def _dot(a, b):
    return jnp.dot(a, b, preferred_element_type=F32)


def _split_dot(a, b_bf16, passes):
    acc = None
    rem = a
    for _ in range(passes):
        piece = rem.astype(BF16)
        rem = rem - piece.astype(F32)
        part = _dot(piece, b_bf16)
        acc = part if acc is None else acc + part
    return acc


def _group_ones(n, group):
    r = lax.broadcasted_iota(jnp.int32, (n, n), 0) // group
    c = lax.broadcasted_iota(jnp.int32, (n, n), 1) // group
    return (r == c).astype(BF16)


def _group_mean_square(x, group):
    n = x.shape[-1]
    return _split_dot(x * x, _group_ones(n, group), 2) * (1.0 / group)


def _group_rms(x, g_row, group):
    return x * lax.rsqrt(_group_mean_square(x, group) + RMS_EPS) * g_row


def _pair_swap(x):
    lane = lax.broadcasted_iota(jnp.int32, x.shape, 1)
    return jnp.where(lane % 2 == 0, pltpu.roll(x, LANES - 1, 1), pltpu.roll(x, 1, 1))


def _rope(x, cos, sin):
    blocks = []
    for j in range(x.shape[-1] // LANES):
        blk = x[:, j * LANES:(j + 1) * LANES]
        blocks.append(blk * cos + _pair_swap(blk) * sin)
    return blocks[0] if len(blocks) == 1 else jnp.concatenate(blocks, axis=-1)


def _silu(x):
    return x * jax.nn.sigmoid(x)


def _layernorm(x, g, b):
    mu = jnp.mean(x, axis=-1, keepdims=True)
    xc = x - mu
    var = jnp.mean(xc * xc, axis=-1, keepdims=True)
    return xc * lax.rsqrt(var + LN_EPS) * g + b


MOD_ROWS = 8


def _mod_kernel(cctx_ref, c_ref, w_ref, b_ref, o_ref, s_ref):
    n_req = c_ref.shape[0]
    s_ref[...] = jnp.zeros_like(s_ref)
    s_ref[0:1, :] = _silu(cctx_ref[...])
    s_ref[1:1 + n_req, :] = _silu(c_ref[...])
    layer = pl.program_id(0)
    res = _dot(s_ref[...].astype(BF16), w_ref[0].astype(BF16)) + b_ref[pl.ds(layer, 1), :]
    for r in range(MOD_ROWS):
        o_ref[r] = res[r:r + 1, :]


def _modulation(c_ctx, c, w_ada, b_ada):
    depth, d, n = w_ada.shape
    tn = 1536
    return pl.pallas_call(
        _mod_kernel,
        grid=(depth, n // tn),
        in_specs=[pl.BlockSpec((1, d), lambda l, j: (0, 0)),
                  pl.BlockSpec(c.shape, lambda l, j: (0, 0)),
                  pl.BlockSpec((1, d, tn), lambda l, j: (l, 0, j)),
                  pl.BlockSpec((depth, tn), lambda l, j: (0, j))],
        out_specs=pl.BlockSpec((MOD_ROWS, 1, tn), lambda l, j: (l, 0, j)),
        out_shape=jax.ShapeDtypeStruct((depth * MOD_ROWS, 1, n), F32),
        scratch_shapes=[pltpu.VMEM((MOD_ROWS, d), F32)],
        compiler_params=_params(2),
        name="adaln_modulation",
    )(c_ctx.reshape(1, d), c, w_ada, b_ada)


def _in_proj_kernel(*refs, li, d, latent, n_alias):
    refs = list(refs)
    x_ref, mod_ref, w_ref, lbf_ref, lbb_ref, qn_ref, kn_ref = refs[:7]
    pos = 7
    if latent:
        cos, sin = refs[pos][...], refs[pos + 1][...]
        pos += 2
    pos += n_alias
    qa_o, hq_o, ff_o, fb_o, hv_o, hg_o, qc_o = refs[pos:pos + 7]
    if latent:
        ka_o, va_o, kc_o, vc_o = refs[pos + 7:pos + 11]
    else:
        ka_o, va_rows_o, kct_o, vct_o = refs[pos + 7:pos + 11]

    def store_kv(ref, val):
        for slot in range(ref.shape[1]):
            ref[0, slot] = val.astype(ref.dtype)

    mix_a, mix_b, mix_c = d // 2, d // 4, d // 4
    kv_c = mix_c // 2
    scale = HEAD_DIM ** -0.5 * math.log2(math.e)

    shift = mod_ref[0, :, 0:d]
    gain = mod_ref[0, :, d:2 * d]
    h = x_ref[...] * (1.0 + gain) + shift

    def proj(start, width):
        return _dot(h, w_ref[:, start:start + width])

    off_b = 3 * mix_a
    off_c = off_b + 5 * mix_b

    zq = proj(off_c, mix_c)
    zk = proj(off_c + mix_c, kv_c)
    vc = proj(off_c + mix_c + kv_c, kv_c)
    qa = proj(0, mix_a)
    msq = _group_mean_square(zq, HEAD_DIM)
    msk = _group_mean_square(zk, HEAD_DIM)
    ka = proj(mix_a, mix_a)
    va = proj(2 * mix_a, mix_a)
    qc = zq * lax.rsqrt(msq + RMS_EPS) * qn_ref[li:li + 1, :]
    kc = zk * lax.rsqrt(msk + RMS_EPS) * kn_ref[li:li + 1, :]

    if latent:
        qa = _rope(qa, cos, sin)
        ka = _rope(ka, cos, sin)
    qa_o[...] = (qa * scale).astype(qa_o.dtype)
    if latent:
        store_kv(ka_o, ka)
        store_kv(va_o, va)
    else:
        store_kv(ka_o, ka.T)
        heads = mix_a // LANES
        for slot in range(va_rows_o.shape[1]):
            for hd in range(heads):
                va_rows_o[0, slot, pl.ds(hd, ROW_TILE, stride=heads), :] = (
                    va[:, hd * LANES:(hd + 1) * LANES])

    def lower_bound(ref):
        logits = ref[...]
        e = jnp.exp(logits - jnp.max(logits, axis=0, keepdims=True))
        sm = e / jnp.sum(e, axis=0, keepdims=True)
        return jnp.sum(sm[0:li + 1], axis=0, keepdims=True) - sm[0:1]

    def forget(x, lb):
        return jnp.maximum(lb + (1.0 - lb) * jax.nn.sigmoid(x), F_MIN)

    off = off_b
    zb = [proj(off + j * mix_b, mix_b) for j in range(5)]

    if latent:
        kc = _rope(kc, cos, sin)
        qc = _rope(qc, cos, sin)
        store_kv(kc_o, kc)
        store_kv(vc_o, vc)
    else:
        store_kv(kct_o, kc.T)
        store_kv(vct_o, vc.T)
    qc = qc * scale
    lane = lax.broadcasted_iota(jnp.int32, (1, LANES), 1)
    for n in range(2):
        blk = qc[:, n * LANES:(n + 1) * LANES]
        in_half = (lane // HEAD_DIM) == n
        for g in range(2):
            src = blk if g == n else pltpu.roll(blk, HEAD_DIM, 1)
            hc = 2 * n + g
            qc_o[:, hc * LANES:(hc + 1) * LANES] = jnp.where(in_half, src, 0.0).astype(qc_o.dtype)

    hq_o[0] = _silu(zb[0])
    ff_o[0] = forget(zb[1], lower_bound(lbf_ref))
    fb_o[0] = forget(zb[2], lower_bound(lbb_ref))
    hv_o[0] = zb[3]
    hg_o[0] = _silu(zb[4])


def _in_proj(x, mod, mod_row, w_in, lb_f, lb_b, qn, kn, rope, kv_prev, *, li):
    bsz, t, d = x.shape
    latent = rope is not None
    depth, _, n_in = w_in.shape
    tiles = t // ROW_TILE
    mix_a, mix_b, mix_c = d // 2, d // 4, d // 4
    kv_c = mix_c // 2
    x2 = x.reshape(bsz * t, d)

    row = lambda i: (i, 0)
    brow = lambda i: (i // tiles, i % tiles, 0)
    const = lambda i: (0, 0)
    in_specs = [pl.BlockSpec((ROW_TILE, d), row),
                pl.BlockSpec((1, 1, mod.shape[-1]), lambda i: (mod_row(i * ROW_TILE), 0, 0)),
                pl.BlockSpec((None, d, n_in), lambda i: (li, 0, 0)),
                pl.BlockSpec(lb_f.shape, const), pl.BlockSpec(lb_b.shape, const),
                pl.BlockSpec(qn.shape, const), pl.BlockSpec(kn.shape, const)]
    args = [x2, mod, w_in, lb_f, lb_b, qn, kn]
    if latent:
        in_specs += [pl.BlockSpec((ROW_TILE, LANES), lambda i: (i % tiles, 0))] * 2
        args += list(rope)
        slots, slot0 = 1, 0
    else:
        slots, slot0 = (depth, 0) if kv_prev is None else (1, li)
    aliases = {}
    if kv_prev is not None:
        for j, buf in enumerate(kv_prev):
            aliases[len(args)] = 7 + j
            in_specs.append(pl.BlockSpec(memory_space=pl.ANY))
            args.append(buf)
    krow = lambda i: (i // tiles, slot0, i % tiles, 0)
    kcol = lambda i: (i // tiles, slot0, 0, i % tiles)
    heads = mix_a // LANES

    out_specs = [pl.BlockSpec((ROW_TILE, mix_a), row)]
    out_shape = [jax.ShapeDtypeStruct((bsz * t, mix_a), BF16)]
    out_specs += [pl.BlockSpec((1, ROW_TILE, mix_b), brow)] * 5
    out_shape += [jax.ShapeDtypeStruct((bsz, t, mix_b), F32)] * 5
    out_specs.append(pl.BlockSpec((ROW_TILE, 2 * mix_c), row))
    out_shape.append(jax.ShapeDtypeStruct((bsz * t, 2 * mix_c), BF16))
    if latent:
        for width in (mix_a, mix_a, kv_c, kv_c):
            out_specs.append(pl.BlockSpec((1, 1, ROW_TILE, width), krow))
            out_shape.append(jax.ShapeDtypeStruct((bsz, 1, t, width), BF16))
    else:
        out_specs += [pl.BlockSpec((1, slots, mix_a, ROW_TILE), kcol),
                      pl.BlockSpec((1, slots, ROW_TILE * heads, LANES), krow),
                      pl.BlockSpec((1, slots, kv_c, ROW_TILE), kcol),
                      pl.BlockSpec((1, slots, kv_c, ROW_TILE), kcol)]
        out_shape += [jax.ShapeDtypeStruct((bsz, depth, mix_a, t), F32),
                      jax.ShapeDtypeStruct((bsz, depth, t * heads, LANES), F32),
                      jax.ShapeDtypeStruct((bsz, depth, kv_c, t), F32),
                      jax.ShapeDtypeStruct((bsz, depth, kv_c, t), F32)]
    return pl.pallas_call(
        functools.partial(_in_proj_kernel, li=li, d=d, latent=latent, n_alias=len(aliases)),
        grid=(bsz * tiles,),
        in_specs=in_specs, out_specs=out_specs, out_shape=out_shape,
        input_output_aliases=aliases,
        compiler_params=_params(1),
        name="in_proj_latent" if latent else "in_proj_context",
    )(*args)


def _softmax_parts(scores):
    m = functools.reduce(jnp.maximum, [jnp.max(s, axis=-1, keepdims=True) for s in scores])
    es = [jnp.exp2(s - m) for s in scores]
    denom = functools.reduce(lambda a, b: a + b, [jnp.sum(e, axis=-1, keepdims=True) for e in es])
    return es, 1.0 / denom


def _score_blocks(q, keys):
    return [_dot(q, k) if k_t else lax.dot_general(q, k, NT, preferred_element_type=F32)
            for k, k_t in keys]


def _diff_lambda(lq1, lk1, lq2, lk2, li):
    lam_init = 0.8 - 0.6 * math.exp(-0.3 * li)

    def lam_term(a, b):
        return jnp.exp(jnp.sum(a[li:li + 1, :] * b[li:li + 1, :], axis=-1, keepdims=True))

    return lam_term(lq1, lk1) - lam_term(lq2, lk2) + lam_init, lam_init


def _diff_head_gen(q, keys, vals, lam, gain, store):
    tq = q.shape[0]
    lane = lax.broadcasted_iota(jnp.int32, (1, LANES), 1)
    zero = jnp.zeros_like(q)
    q2 = jnp.concatenate([jnp.where(lane < HEAD_DIM, q, zero),
                          jnp.where(lane >= HEAD_DIM, q, zero)], axis=0)
    scores = _score_blocks(q2, keys)
    yield
    es, r = _softmax_parts(scores)
    yield
    r0 = r[0:tq]
    ratio = r[tq:2 * tq] * lam / r0
    o = None
    for e, v in zip(es, vals):
        part = _dot((e[0:tq] - e[tq:2 * tq] * ratio).astype(BF16), v)
        o = part if o is None else o + part
    yield
    o = o * r0
    ms = jnp.mean(o * o, axis=-1, keepdims=True)
    store(o * lax.rsqrt(ms + RMS_EPS) * gain)


def _gqa_group_gen(n, q, keys, vals, store):
    tq = q.shape[0] // 2
    scores = _score_blocks(q, keys)
    yield
    es, r = _softmax_parts(scores)
    yield
    o = None
    for e, (v, v_t) in zip(es, vals):
        p = e.astype(BF16)
        part = lax.dot_general(p, v, NT, preferred_element_type=F32) if v_t else _dot(p, v)
        o = part if o is None else o + part
    yield
    o = o * r
    first = o[0:tq]
    second = o[tq:2 * tq]
    if n == 0:
        second = pltpu.roll(second, HEAD_DIM, 1)
    else:
        first = pltpu.roll(first, HEAD_DIM, 1)
    lane = lax.broadcasted_iota(jnp.int32, (1, LANES), 1)
    store(jnp.where(lane < HEAD_DIM, first, second))


def _head_masks(width):
    lane_head = lax.broadcasted_iota(jnp.int32, (1, width), 1) // HEAD_DIM
    return [lane_head == h for h in range(width // HEAD_DIM)]


def _stack_heads(x, masks):
    return jnp.concatenate([jnp.where(m, x, jnp.zeros_like(x)) for m in masks], axis=0)


def _block_diag_mask(width):
    r = lax.broadcasted_iota(jnp.int32, (width, width), 0) // HEAD_DIM
    c = lax.broadcasted_iota(jnp.int32, (width, width), 1) // HEAD_DIM
    return r == c


def _ref_rows(b, offsets, span):
    width = b.shape[-1]
    return jnp.concatenate([jnp.broadcast_to(b[o:o + 1], (span, width)) for o in offsets], axis=0)


def _run_interleaved(gens, delays=None):
    live = list(zip(gens, delays or [0] * len(gens)))
    rnd = 0
    while live:
        for item in list(live):
            if item[1] > rnd:
                continue
            try:
                next(item[0])
            except StopIteration:
                live.remove(item)
        rnd += 1


def _hgrn_chunks_gen(problems, out):
    n = len(problems)
    c, width = problems[0][0].shape
    qs = [p[0] for p in problems]
    vs = [p[2] for p in problems]
    sts = [p[3] for p in problems]
    rev = [p[4] for p in problems]
    chains = range(n)
    masks = _head_masks(width)
    trow =lax.broadcasted_iota(jnp.int32, (c, 1), 0)
    t_full = lax.broadcasted_iota(jnp.int32, (c, width), 0)
    s_full = lax.broadcasted_iota(jnp.int32, (c, width), 1) % c

    ks = [1.0 - p[1] for p in problems]
    b = [jnp.log(p[1]) * math.log2(math.e) for p in problems]
    step = 1
    while step < c:
        for j in chains:
            if rev[j]:
                b[j] = b[j] + jnp.where(trow < c - step, pltpu.roll(b[j], c - step, 0), 0.0)
            else:
                b[j] = b[j] + jnp.where(trow >= step, pltpu.roll(b[j], step, 0), 0.0)
        step *= 2
        yield
    b_end = [b[j][0:1] if rev[j] else b[j][c - 1:c] for j in chains]

    o = [lax.dot_general((qs[j] * jnp.exp2(b[j])).astype(BF16), sts[j].astype(BF16), NT,
                         preferred_element_type=F32) for j in chains]
    yield

    a = [None] * n
    m = c // 2
    while m >= DIAG_BLOCK:
        blocks = c // (2 * m)
        same = (t_full // (2 * m)) == (s_full // (2 * m))
        for j in chains:
            ref = _ref_rows(b[j], [i * 2 * m + (m if rev[j] else m - 1) for i in range(blocks)],
                            2 * m)
            is_q = ((trow % (2 * m)) < m) if rev[j] else ((trow % (2 * m)) >= m)
            e = jnp.exp2((b[j] - ref) * jnp.where(is_q, 1.0, -1.0))
            ql = jnp.where(is_q, qs[j] * e, 0.0).astype(BF16)
            kl = jnp.where(is_q, 0.0, ks[j] * e).astype(BF16)
            al = lax.dot_general(ql, _stack_heads(kl, masks), NT, preferred_element_type=F32)
            if blocks > 1:
                al = jnp.where(same, al, 0.0)
            a[j] = al if a[j] is None else a[j] + al
        m //= 2
        yield
    blocks = c // DIAG_BLOCK
    mid = DIAG_BLOCK // 2
    same = (t_full // DIAG_BLOCK) == (s_full // DIAG_BLOCK)
    for j in chains:
        ref = _ref_rows(b[j], [i * DIAG_BLOCK + (mid if rev[j] else mid - 1) for i in range(blocks)],
                        DIAG_BLOCK)
        d = b[j] - ref
        ql = (qs[j] * jnp.exp2(d)).astype(BF16)
        kl = (ks[j] * jnp.exp2(-d)).astype(BF16)
        al = lax.dot_general(ql, _stack_heads(kl, masks), NT, preferred_element_type=F32)
        causal = (s_full >= t_full) if rev[j] else (s_full <= t_full)
        a[j] = a[j] + jnp.where(same & causal, al, 0.0)
    yield

    v_b = [v.astype(BF16) for v in vs]
    o = [o[j] + _dot(a[j].astype(BF16), _stack_heads(v_b[j], masks)) for j in chains]
    yield

    bd = _block_diag_mask(width)
    upd = [lax.dot_general(v_b[j], (ks[j] * jnp.exp2(b_end[j] - b[j])).astype(BF16), TN,
                           preferred_element_type=F32) for j in chains]
    st_new = [sts[j] * jnp.exp2(b_end[j]) + jnp.where(bd, upd[j], 0.0) for j in chains]
    out.extend(zip(o, st_new))


def _mxu_transpose(x):
    n = x.shape[1]
    r = lax.broadcasted_iota(jnp.int32, (n, n), 0)
    c = lax.broadcasted_iota(jnp.int32, (n, n), 1)
    eye = (r == c).astype(BF16)
    acc = None
    rem = x
    for _ in range(3):
        piece = rem.astype(BF16)
        rem = rem - piece.astype(F32)
        part = lax.dot_general(eye, piece, NT, preferred_element_type=F32)
        acc = part if acc is None else acc + part
    return acc


ATTENTION_DELAY = 5


def _mixer_ctx_kernel(*refs, li, n_alias):
    refs = list(refs)
    (qa_ref, kat_ref, va_ref, qc_ref, kct_ref, vct_ref, hq_ref, ff_ref, fb_ref, hv_ref, hg_ref,
     gn_ref, lq1, lk1, lq2, lk2, sub_ref) = refs[:17]
    pos = 17 + n_alias
    oa_ref, oc_ref, ob_ref, sf_ref, sb_ref = refs[pos:pos + 5]
    st_ref, of_ref, obk_ref = refs[pos + 5:pos + 8]
    nb, t, width = hq_ref.shape
    nc = t // CHUNK
    heads_b = width // HEAD_DIM
    heads_a = qa_ref.shape[1] // LANES
    groups_c = qc_ref.shape[1] // LANES // 2
    lam, lam_init = _diff_lambda(lq1, lk1, lq2, lk2, li)
    gain = sub_ref[li:li + 1, :] * (1.0 - lam_init)

    for j in range(2 * nb):
        st_ref[j] = jnp.zeros((width, width), F32)

    def diff_unit(n, h):
        rows = slice(n * t, (n + 1) * t)
        c = slice(h * LANES, (h + 1) * LANES)
        keys = [(kat_ref[n, 0, c, :].astype(BF16), True)]
        vals = [va_ref[n, 0, pl.ds(h, t, stride=heads_a), :].astype(BF16)]

        def store(o):
            oa_ref[rows, c] = o.astype(oa_ref.dtype)

        yield from _diff_head_gen(qa_ref[rows, c], keys, vals, lam, gain, store)

    def gqa_unit(n, g):
        rows = slice(n * t, (n + 1) * t)
        q = jnp.concatenate([qc_ref[rows, (2 * g + j) * LANES:(2 * g + j + 1) * LANES]
                             for j in range(2)], axis=0)
        keys = [(kct_ref[n, 0].astype(BF16), True)]
        vals = [(vct_ref[n, 0].astype(BF16), True)]

        def store(o):
            oc_ref[rows, g * LANES:(g + 1) * LANES] = o.astype(oc_ref.dtype)

        yield from _gqa_group_gen(g, q, keys, vals, store)

    def scan_step(ci):
        rows = (slice(ci * CHUNK, (ci + 1) * CHUNK), slice((nc - 1 - ci) * CHUNK, (nc - ci) * CHUNK))
        loaded = []
        for n in range(nb):
            for d, f_ref in enumerate((ff_ref, fb_ref)):
                r = rows[d]
                loaded.append((hq_ref[n, r, :], f_ref[n, r, :], hv_ref[n, r, :], st_ref[2 * n + d],
                               bool(d)))
        out = []
        yield from _hgrn_chunks_gen(loaded, out)
        for j, (o, st) in enumerate(out):
            n, d = divmod(j, 2)
            (obk_ref if d else of_ref)[n, rows[d], :] = o
            st_ref[j] = st

    units = []
    for n in range(nb):
        units += [diff_unit(n, h) for h in range(heads_a)]
        units += [gqa_unit(n, g) for g in range(groups_c)]
    share = -(-len(units) // nc)
    for ci in range(nc):
        mine = units[ci * share:(ci + 1) * share]
        _run_interleaved([scan_step(ci)] + mine, [0] + [ATTENTION_DELAY] * len(mine))

    for n in range(nb):
        o = of_ref[n] + obk_ref[n]
        ob_ref[n] = (_group_rms(o, gn_ref[li:li + 1, :], HEAD_DIM) * hg_ref[n]).astype(ob_ref.dtype)
        for d, dst in enumerate((sf_ref, sb_ref)):
            st = st_ref[2 * n + d]
            rows = st[0:HEAD_DIM]
            for h in range(1, heads_b):
                rows = rows + st[h * HEAD_DIM:(h + 1) * HEAD_DIM]
            final = _mxu_transpose(rows).reshape(heads_b, HEAD_DIM, HEAD_DIM)
            for slot in range(dst.shape[1]):
                dst[n, slot] = final


def _mixer_ctx(qa, ka_t, va_rows, qc, kc_t, vc_t, hq, ff, fb, hv, hg, gn, lam, subln, state_prev,
               *, li, depth, nb):
    bsz, t, width = hq.shape
    heads_b = width // HEAD_DIM
    rows = lambda b: (b, 0)
    at_layer = lambda b: (b, li, 0, 0)
    seq = pl.BlockSpec((nb, t, width), lambda b: (b, 0, 0))
    const = lambda b: (0, 0)
    in_specs = [pl.BlockSpec((nb * t, qa.shape[1]), rows),
                pl.BlockSpec((nb, 1) + ka_t.shape[2:], at_layer),
                pl.BlockSpec((nb, 1) + va_rows.shape[2:], at_layer),
                pl.BlockSpec((nb * t, qc.shape[1]), rows),
                pl.BlockSpec((nb, 1) + kc_t.shape[2:], at_layer),
                pl.BlockSpec((nb, 1) + vc_t.shape[2:], at_layer),
                seq, seq, seq, seq, seq, pl.BlockSpec(gn.shape, const)]
    in_specs += [pl.BlockSpec(a.shape, const) for a in (*lam, subln)]
    args = [qa, ka_t, va_rows, qc, kc_t, vc_t, hq, ff, fb, hv, hg, gn, *lam, subln]
    aliases = {}
    if state_prev is not None:
        for j, buf in enumerate(state_prev):
            aliases[len(args)] = 3 + j
            in_specs.append(pl.BlockSpec(memory_space=pl.ANY))
            args.append(buf)
    slots, slot0 = (depth, 0) if state_prev is None else (1, li)
    state_spec = pl.BlockSpec((nb, slots, heads_b, HEAD_DIM, HEAD_DIM), lambda b: (b, slot0, 0, 0, 0))
    out_specs = [pl.BlockSpec((nb * t, qa.shape[1]), rows),
                 pl.BlockSpec((nb * t, qc.shape[1] // 2), rows), seq, state_spec, state_spec]
    out_shape = [jax.ShapeDtypeStruct(qa.shape, BF16),
                 jax.ShapeDtypeStruct((qc.shape[0], qc.shape[1] // 2), BF16),
                 jax.ShapeDtypeStruct((bsz, t, width), BF16)]
    out_shape += [jax.ShapeDtypeStruct((bsz, depth, heads_b, HEAD_DIM, HEAD_DIM), F32)] * 2
    return pl.pallas_call(
        functools.partial(_mixer_ctx_kernel, li=li, n_alias=len(aliases)),
        grid=(bsz // nb,),
        in_specs=in_specs, out_specs=out_specs, out_shape=out_shape,
        input_output_aliases=aliases,
        scratch_shapes=[pltpu.VMEM((2 * nb, width, width), F32),
                        pltpu.VMEM((nb, t, width), F32), pltpu.VMEM((nb, t, width), F32)],
        compiler_params=_params(1),
        name="mixers_context",
    )(*args)


def _mixer_lat_kernel(qa_ref, k_ref, v_ref, ck_ref, cv_ref, qc_ref, kc_ref, vc_ref, cck_ref, ccv_ref,
                      hq_ref, ff_ref, fb_ref, hv_ref, hg_ref, gn_ref, s0f_ref, s0b_ref,
                      lq1, lk1, lq2, lk2, sub_ref, oa_ref, oc_ref, ob_ref,
                      st_ref, of_ref, obk_ref, *, li, tq):
    _, t, width = hq_ref.shape
    nc = t // CHUNK
    nq = t // tq
    steps_per_trip = nc // nq
    heads_b = width // HEAD_DIM
    heads_a = qa_ref.shape[1] // LANES
    groups_c = qc_ref.shape[1] // LANES // 2
    past = ck_ref.shape[-1]
    lam, lam_init = _diff_lambda(lq1, lk1, lq2, lk2, li)
    gain = sub_ref[li:li + 1, :] * (1.0 - lam_init)
    bd = _block_diag_mask(width)

    for d, src in enumerate((s0f_ref, s0b_ref)):
        x = src[0, 0].reshape(width, HEAD_DIM)
        xt = _mxu_transpose(x)
        st_ref[d] = jnp.where(bd, jnp.concatenate([xt] * heads_b, axis=0), 0.0)

    def trip(qt, carry):
        q_rows = pl.ds(pl.multiple_of(qt * tq, tq), tq)

        def diff_unit(h):
            c = slice(h * LANES, (h + 1) * LANES)
            keys = [(k_ref[0, 0, :, c], False), (ck_ref[0, 0, c, :].astype(BF16), True)]
            vals = [v_ref[0, 0, :, c], cv_ref[0, 0, pl.ds(h, past, stride=heads_a), :].astype(BF16)]

            def store(o):
                oa_ref[q_rows, c] = o.astype(oa_ref.dtype)

            yield from _diff_head_gen(qa_ref[q_rows, c], keys, vals, lam, gain, store)

        def gqa_unit(g):
            q = jnp.concatenate([qc_ref[q_rows, (2 * g + j) * LANES:(2 * g + j + 1) * LANES]
                                 for j in range(2)], axis=0)
            keys = [(kc_ref[0, 0], False), (cck_ref[0, 0].astype(BF16), True)]
            vals = [(vc_ref[0, 0], False), (ccv_ref[0, 0].astype(BF16), True)]

            def store(o):
                oc_ref[q_rows, g * LANES:(g + 1) * LANES] = o.astype(oc_ref.dtype)

            yield from _gqa_group_gen(g, q, keys, vals, store)

        def scan_step(cj):
            ci = qt * steps_per_trip + cj
            rows = (pl.ds(pl.multiple_of(ci * CHUNK, CHUNK), CHUNK),
                    pl.ds(pl.multiple_of((nc - 1 - ci) * CHUNK, CHUNK), CHUNK))
            loaded = [(hq_ref[0, rows[d], :], f_ref[0, rows[d], :], hv_ref[0, rows[d], :],
                       st_ref[d], bool(d)) for d, f_ref in enumerate((ff_ref, fb_ref))]
            out = []
            yield from _hgrn_chunks_gen(loaded, out)
            for d, (o, st) in enumerate(out):
                (obk_ref if d else of_ref)[rows[d], :] = o
                st_ref[d] = st

        units = [diff_unit(h) for h in range(heads_a)] + [gqa_unit(g) for g in range(groups_c)]
        for cj in range(steps_per_trip):
            mine = units[cj::steps_per_trip]
            _run_interleaved([scan_step(cj)] + mine, [0] + [ATTENTION_DELAY] * len(mine))
        return carry

    lax.fori_loop(0, nq, trip, 0)

    o = of_ref[...] + obk_ref[...]
    ob_ref[0] = (_group_rms(o, gn_ref[li:li + 1, :], HEAD_DIM) * hg_ref[0]).astype(ob_ref.dtype)


def _mixer_lat(qa, ka, va, qc, kc, vc, cache, hq, ff, fb, hv, hg, gn, state, lam, subln, *, li, tq):
    bsz, t, width = hq.shape
    rows = lambda b: (b, 0)
    own = lambda b: (b, 0, 0, 0)
    at_layer = lambda b: (b, li, 0, 0)
    seq = pl.BlockSpec((1, t, width), lambda b: (b, 0, 0))
    const = lambda b: (0, 0)
    in_specs = [pl.BlockSpec((t, qa.shape[1]), rows),
                pl.BlockSpec((1,) + ka.shape[1:], own), pl.BlockSpec((1,) + va.shape[1:], own),
                pl.BlockSpec((1, 1) + cache[0].shape[2:], at_layer),
                pl.BlockSpec((1, 1) + cache[1].shape[2:], at_layer),
                pl.BlockSpec((t, qc.shape[1]), rows),
                pl.BlockSpec((1,) + kc.shape[1:], own), pl.BlockSpec((1,) + vc.shape[1:], own),
                pl.BlockSpec((1, 1) + cache[2].shape[2:], at_layer),
                pl.BlockSpec((1, 1) + cache[3].shape[2:], at_layer),
                seq, seq, seq, seq, seq, pl.BlockSpec(gn.shape, const)]
    in_specs += [pl.BlockSpec((1, 1) + state[0].shape[2:], lambda b: (b, li, 0, 0, 0))] * 2
    in_specs += [pl.BlockSpec(a.shape, const) for a in (*lam, subln)]
    args = [qa, ka, va, cache[0], cache[1], qc, kc, vc, cache[2], cache[3], hq, ff, fb, hv, hg, gn,
            *state, *lam, subln]
    return pl.pallas_call(
        functools.partial(_mixer_lat_kernel, li=li, tq=tq),
        grid=(bsz,),
        in_specs=in_specs,
        out_specs=[pl.BlockSpec((t, qa.shape[1]), rows),
                   pl.BlockSpec((t, qc.shape[1] // 2), rows), seq],
        out_shape=[jax.ShapeDtypeStruct(qa.shape, BF16),
                   jax.ShapeDtypeStruct((qc.shape[0], qc.shape[1] // 2), BF16),
                   jax.ShapeDtypeStruct((bsz, t, width), BF16)],
        scratch_shapes=[pltpu.VMEM((2, width, width), F32),
                        pltpu.VMEM((t, width), F32), pltpu.VMEM((t, width), F32)],
        compiler_params=_params(1),
        name="mixers_latent",
    )(*args)


def _out_mlp_kernel(*refs, li, d, alpha, ff_chunk, cast_next):
    (x_ref, oa_ref, ob_ref, oc_ref, mod_ref, wo_ref, w1_ref, w2_ref,
     g1_ref, b1_ref, g2_ref, b2_ref) = refs[:12]
    if cast_next:
        w1n_ref, w2n_ref, y_ref, w1n_o, w2n_o = refs[12:]
        w1n_o[...] = w1n_ref[...].astype(BF16)
        w2n_o[...] = w2n_ref[...].astype(BF16)
    else:
        y_ref = refs[12]
    wa, wb = oa_ref.shape[-1], ob_ref.shape[-1]
    layer = slice(li, li + 1)
    gate1 = mod_ref[0, :, 2 * d:3 * d]
    shift2 = mod_ref[0, :, 3 * d:4 * d]
    gain2 = mod_ref[0, :, 4 * d:5 * d]
    gate2 = mod_ref[0, :, 5 * d:6 * d]
    subs = [slice(s * ROW_TILE, (s + 1) * ROW_TILE) for s in range(x_ref.shape[0] // ROW_TILE)]
    wo = wo_ref[...].astype(BF16)
    m = [_dot(oa_ref[r, :], wo[0:wa, :]) + _dot(ob_ref[r, :], wo[wa:wa + wb, :])
         + _dot(oc_ref[r, :], wo[wa + wb:, :]) for r in subs]
    x1 = [_layernorm(alpha * x_ref[r, :] + gate1 * mi, g1_ref[layer, :], b1_ref[layer, :])
          for r, mi in zip(subs, m)]
    h2 = [(xi * (1.0 + gain2) + shift2).astype(BF16) for xi in x1]
    acc = [None] * len(subs)
    for j in range(w1_ref.shape[-1] // ff_chunk):
        cols = slice(j * ff_chunk, (j + 1) * ff_chunk)
        hid = [jnp.maximum(_dot(hi, w1_ref[:, cols]), 0.0) for hi in h2]
        for s, hd in enumerate(hid):
            part = _dot((hd * hd).astype(BF16), w2_ref[cols, :])
            acc[s] = part if acc[s] is None else acc[s] + part
    for r, xi, ai in zip(subs, x1, acc):
        y_ref[r, :] = _layernorm(alpha * xi + gate2 * ai, g2_ref[layer, :], b2_ref[layer, :])


def _out_mlp(x, oa, ob, oc, mod, mod_row, w_out, w_ff1, w_ff2, ln, next_ff, *, li, alpha):
    bsz, t, d = x.shape
    rows = 2 * ROW_TILE
    steps = bsz * t // rows
    row = lambda i: (i, 0)
    const = lambda i: (0, 0)
    resident = lambda a: pl.BlockSpec(a.shape, const, pipeline_mode=pl.Buffered(1))
    in_specs = [pl.BlockSpec((rows, d), row),
                pl.BlockSpec((rows, oa.shape[-1]), row),
                pl.BlockSpec((rows, ob.shape[-1]), row),
                pl.BlockSpec((rows, oc.shape[-1]), row),
                pl.BlockSpec((1, 1, mod.shape[-1]), lambda i: (mod_row(i * rows), 0, 0)),
                pl.BlockSpec((None,) + w_out.shape[1:], lambda i: (li, 0, 0),
                             pipeline_mode=pl.Buffered(1)),
                resident(w_ff1), resident(w_ff2)]
    in_specs += [pl.BlockSpec(a.shape, const) for a in ln]
    args = [x.reshape(bsz * t, d), oa, ob.reshape(bsz * t, -1), oc, mod, w_out, w_ff1, w_ff2, *ln]
    out_specs = [pl.BlockSpec((rows, d), row)]
    out_shape = [jax.ShapeDtypeStruct((bsz * t, d), F32)]
    if next_ff is not None:
        for w in next_ff:
            blk = (w.shape[1] // steps, w.shape[2])
            in_specs.append(pl.BlockSpec((None,) + blk, lambda i: (li + 1, i, 0)))
            args.append(w)
            out_specs.append(pl.BlockSpec(blk, row))
            out_shape.append(jax.ShapeDtypeStruct(w.shape[1:], BF16))
    res = pl.pallas_call(
        functools.partial(_out_mlp_kernel, li=li, d=d, alpha=alpha, ff_chunk=1024,
                          cast_next=next_ff is not None),
        grid=(steps,),
        in_specs=in_specs, out_specs=out_specs, out_shape=out_shape,
        compiler_params=_params(1),
        name="out_mlp",
    )(*args)
    return res[0].reshape(bsz, t, d), tuple(res[1:])


def _rope_tables(n_tokens):
    pairs = HEAD_DIM // 4
    tok = np.arange(n_tokens)
    row = (tok // GRID_W).astype(np.float64)
    col = (tok % GRID_W).astype(np.float64)
    inv = ROPE_THETA ** (-np.arange(pairs, dtype=np.float64) / pairs)
    ang = np.concatenate([row[:, None] * inv, col[:, None] * inv], axis=-1)
    lane = np.arange(LANES)
    pair = (lane % HEAD_DIM) // 2
    sign = np.where(lane % 2 == 0, -1.0, 1.0)
    return (jnp.asarray(np.cos(ang)[:, pair], F32), jnp.asarray(np.sin(ang)[:, pair] * sign, F32))


def kernel(x_prompt, x_sample, cache_a_k, cache_a_v, cache_c_k, cache_c_v, state_b_fwd, state_b_bwd, c, c_ctx, w_ada, b_ada, w_in, w_out, lam_q1, lam_k1, lam_q2, lam_k2, subln_g, lb_logits_fwd, lb_logits_bwd, gnorm_g, qnorm_g, knorm_g, ln1_g, ln1_b, ln2_g, ln2_b, w_ff1, w_ff2):
    depth = w_in.shape[0]
    bsz, seq, d = x_prompt.shape
    dec_bsz, dec_seq, _ = x_sample.shape
    past = cache_a_k.shape[2]
    alpha = (2 * depth) ** 0.25
    mix_a, mix_b, mix_c = d // 2, d // 4, d // 4

    mod = _modulation(c_ctx, c, w_ada, b_ada)
    rope = _rope_tables(dec_seq)

    cache = (cache_a_k.transpose(0, 1, 3, 4, 5, 2).reshape(dec_bsz, depth, mix_a, past),
             cache_a_v.reshape(dec_bsz, depth, past * (mix_a // LANES), LANES),
             cache_c_k.transpose(0, 1, 3, 4, 2).reshape(dec_bsz, depth, mix_c // 2, past),
             cache_c_v.transpose(0, 1, 3, 4, 2).reshape(dec_bsz, depth, mix_c // 2, past))
    lam = (lam_q1, lam_k1, lam_q2, lam_k2)

    ff_b = [(w_ff1[0].astype(BF16), w_ff2[0].astype(BF16))]
    qn =jnp.tile(qnorm_g, (1, mix_c // HEAD_DIM))
    kn = jnp.tile(knorm_g, (1, mix_c // 2 // HEAD_DIM))
    gn = jnp.tile(gnorm_g, (1, mix_b // HEAD_DIM))
    ln = (ln1_g, ln1_b, ln2_g, ln2_b)

    def stream(x, li, latent, own_prev):
        n, t, _ = x.shape
        mod_row = ((lambda r0: li * MOD_ROWS + 1 + r0 // t) if latent
                   else (lambda r0: li * MOD_ROWS))
        (qa, hq, ff, fb, hv, hg, qc, ka, va, kc, vc) = _in_proj(
            x, mod, mod_row, w_in, lb_logits_fwd, lb_logits_bwd, qn, kn,
            rope if latent else None, None if own_prev is None else own_prev[0:4], li=li)
        if latent:
            oa, oc, ob = _mixer_lat(qa, ka, va, qc, kc, vc, cache, hq, ff, fb, hv, hg, gn,
                                    (state_b_fwd, state_b_bwd), lam, subln_g, li=li, tq=256)
            own = None
        else:
            oa, oc, ob, s_f, s_b = _mixer_ctx(
                qa, ka, va, qc, kc, vc, hq, ff, fb, hv, hg, gn, lam, subln_g,
                None if own_prev is None else own_prev[4:6], li=li, depth=depth, nb=2)
            own = (ka, va, kc, vc, s_f, s_b)
        cast_next = not latent and li + 1 < depth
        y, next_ff = _out_mlp(x, oa, ob, oc, mod, mod_row, w_out, *ff_b[li], ln,
                              (w_ff1, w_ff2) if cast_next else None, li=li, alpha=alpha)
        if cast_next:
            ff_b.append(next_ff)
        return y, own

    y_prompt, y_sample = x_prompt, x_sample
    own = None
    for li in range(depth):
        y_prompt, own = stream(y_prompt, li, False, own)
        y_sample, _ = stream(y_sample, li, True, None)

    heads_a = mix_a // (2 * HEAD_DIM)
    new_a_k = own[0].reshape(bsz, depth, heads_a, 2, HEAD_DIM, seq).transpose(0, 1, 5, 2, 3, 4)
    new_a_v = own[1].reshape(bsz, depth, seq, heads_a, 2 * HEAD_DIM)
    kv_heads = mix_c // 2 // HEAD_DIM
    new_c_k = own[2].reshape(bsz, depth, kv_heads, HEAD_DIM, seq).transpose(0, 1, 4, 2, 3)
    new_c_v = own[3].reshape(bsz, depth, kv_heads, HEAD_DIM, seq).transpose(0, 1, 4, 2, 3)
    return (y_prompt, y_sample, new_a_k, new_a_v, new_c_k, new_c_v, own[4], own[5])
```

```python
import functools
import math

import jax
import jax.numpy as jnp
import numpy as np
from jax import lax
from jax.experimental import pallas as pl
from jax.experimental.pallas import tpu as pltpu

GRID_W = 64
HEAD_DIM = 64
ROPE_THETA = 10000.0
LN_EPS = 1e-6
RMS_EPS = 1e-6
F_MIN = 1e-6
CHUNK = 64
DIAG_BLOCK = 8
LANES = 128
ROW_TILE = 256
VMEM_LIMIT = 56 * 1024 * 1024

F32 = jnp.float32
BF16 = jnp.bfloat16
NT = (((1,), (1,)), ((), ()))
TN = (((0,), (0,)), ((), ()))


def _params(n_grid):
    return pltpu.CompilerParams(dimension_semantics=("arbitrary",) * n_grid,
                                vmem_limit_bytes=VMEM_LIMIT)


def _dot(a, b):
    return jnp.dot(a, b, preferred_element_type=F32)


def _split_dot(a, b_bf16, passes):
    acc = None
    rem = a
    for _ in range(passes):
        piece = rem.astype(BF16)
        rem = rem - piece.astype(F32)
        part = _dot(piece, b_bf16)
        acc = part if acc is None else acc + part
    return acc


def _group_ones(n, group):
    r = lax.broadcasted_iota(jnp.int32, (n, n), 0) // group
    c = lax.broadcasted_iota(jnp.int32, (n, n), 1) // group
    return (r == c).astype(BF16)


def _group_mean_square(x, group):
    n = x.shape[-1]
    return _split_dot(x * x, _group_ones(n, group), 2) * (1.0 / group)


def _group_rms(x, g_row, group):
    return x * lax.rsqrt(_group_mean_square(x, group) + RMS_EPS) * g_row


def _pair_swap(x):
    lane = lax.broadcasted_iota(jnp.int32, x.shape, 1)
    return jnp.where(lane % 2 == 0, pltpu.roll(x, LANES - 1, 1), pltpu.roll(x, 1, 1))


def _rope(x, cos, sin):
    blocks = []
    for j in range(x.shape[-1] // LANES):
        blk = x[:, j * LANES:(j + 1) * LANES]
        blocks.append(blk * cos + _pair_swap(blk) * sin)
    return blocks[0] if len(blocks) == 1 else jnp.concatenate(blocks, axis=-1)


def _silu(x):
    return x * jax.nn.sigmoid(x)


def _layernorm(x, g, b):
    mu = jnp.mean(x, axis=-1, keepdims=True)
    xc = x - mu
    var = jnp.mean(xc * xc, axis=-1, keepdims=True)
    return xc * lax.rsqrt(var + LN_EPS) * g + b


MOD_ROWS = 8


MOD_K = 256


def _mod_kernel(cctx_ref, c_ref, w_ref, b_ref, o_ref, s_ref, acc_ref):
    layer, k = pl.program_id(0), pl.program_id(1)
    n_req = c_ref.shape[0]

    @pl.when(k == 0)
    def _():
        acc_ref[...] = jnp.broadcast_to(b_ref[pl.ds(layer, 1), :], acc_ref.shape)

    s_ref[...] = jnp.zeros_like(s_ref)
    s_ref[0:1, :] = _silu(cctx_ref[...])
    s_ref[1:1 + n_req, :] = _silu(c_ref[...])
    acc_ref[...] += _dot(s_ref[...].astype(BF16), w_ref[0].astype(BF16))

    @pl.when(k == pl.num_programs(1) - 1)
    def _():
        for r in range(MOD_ROWS):
            o_ref[r] = acc_ref[r:r + 1, :]


def _modulation(c_ctx, c, w_ada, b_ada):
    depth, d, n = w_ada.shape
    return pl.pallas_call(
        _mod_kernel,
        grid=(depth, d // MOD_K),
        in_specs=[pl.BlockSpec((1, MOD_K), lambda l, k: (0, k)),
                  pl.BlockSpec((c.shape[0], MOD_K), lambda l, k: (0, k)),
                  pl.BlockSpec((1, MOD_K, n), lambda l, k: (l, k, 0)),
                  pl.BlockSpec(b_ada.shape, lambda l, k: (0, 0))],
        out_specs=pl.BlockSpec((MOD_ROWS, 1, n), lambda l, k: (l, 0, 0)),
        out_shape=jax.ShapeDtypeStruct((depth * MOD_ROWS, 1, n), F32),
        scratch_shapes=[pltpu.VMEM((MOD_ROWS, MOD_K), F32), pltpu.VMEM((MOD_ROWS, n), F32)],
        compiler_params=_params(2),
        name="adaln_modulation",
    )(c_ctx.reshape(1, d), c, w_ada, b_ada)


def _in_proj_kernel(*refs, li, d, latent, n_alias):
    refs = list(refs)
    x_ref, mod_ref, w_ref, lbf_ref, lbb_ref, qn_ref, kn_ref = refs[:7]
    pos = 7
    if latent:
        cos, sin = refs[pos][...], refs[pos + 1][...]
        pos += 2
    pos += n_alias
    qa_o, hq_o, ff_o, fb_o, hv_o, hg_o, qc_o = refs[pos:pos + 7]
    if latent:
        ka_o, va_o, kc_o, vc_o = refs[pos + 7:pos + 11]
    else:
        ka_o, va_rows_o, kct_o, vct_o = refs[pos + 7:pos + 11]

    def store_kv(ref, val):
        for slot in range(ref.shape[1]):
            ref[0, slot] = val.astype(ref.dtype)

    mix_a, mix_b, mix_c = d // 2, d // 4, d // 4
    kv_c = mix_c // 2
    scale = HEAD_DIM ** -0.5 * math.log2(math.e)

    shift = mod_ref[0, :, 0:d]
    gain = mod_ref[0, :, d:2 * d]
    h = x_ref[...] * (1.0 + gain) + shift

    def proj(start, width):
        return _dot(h, w_ref[:, start:start + width])

    off_b = 3 * mix_a
    off_c = off_b + 5 * mix_b

    zq = proj(off_c, mix_c)
    zk = proj(off_c + mix_c, kv_c)
    vc = proj(off_c + mix_c + kv_c, kv_c)
    qa = proj(0, mix_a)
    msq = _group_mean_square(zq, HEAD_DIM)
    msk = _group_mean_square(zk, HEAD_DIM)
    ka = proj(mix_a, mix_a)
    va = proj(2 * mix_a, mix_a)
    qc = zq * lax.rsqrt(msq + RMS_EPS) * qn_ref[li:li + 1, :]
    kc = zk * lax.rsqrt(msk + RMS_EPS) * kn_ref[li:li + 1, :]

    if latent:
        qa = _rope(qa, cos, sin)
        ka = _rope(ka, cos, sin)
    qa_o[...] = (qa * scale).astype(qa_o.dtype)
    if latent:
        store_kv(ka_o, ka)
        store_kv(va_o, va)
    else:
        store_kv(ka_o, ka.T)
        heads = mix_a // LANES
        for slot in range(va_rows_o.shape[1]):
            for hd in range(heads):
                va_rows_o[0, slot, pl.ds(hd, ROW_TILE, stride=heads), :] = (
                    va[:, hd * LANES:(hd + 1) * LANES])

    def lower_bound(ref):
        logits = ref[...]
        e = jnp.exp(logits - jnp.max(logits, axis=0, keepdims=True))
        sm = e / jnp.sum(e, axis=0, keepdims=True)
        return jnp.sum(sm[0:li + 1], axis=0, keepdims=True) - sm[0:1]

    def forget(x, lb):
        return jnp.maximum(lb + (1.0 - lb) * jax.nn.sigmoid(x), F_MIN)

    off = off_b
    zb = [proj(off + j * mix_b, mix_b) for j in range(5)]

    if latent:
        kc = _rope(kc, cos, sin)
        qc = _rope(qc, cos, sin)
        store_kv(kc_o, kc)
        store_kv(vc_o, vc)
    else:
        store_kv(kct_o, kc.T)
        store_kv(vct_o, vc.T)
    qc = qc * scale
    lane = lax.broadcasted_iota(jnp.int32, (1, LANES), 1)
    for n in range(2):
        blk = qc[:, n * LANES:(n + 1) * LANES]
        in_half = (lane // HEAD_DIM) == n
        for g in range(2):
            src = blk if g == n else pltpu.roll(blk, HEAD_DIM, 1)
            hc = 2 * n + g
            qc_o[:, hc * LANES:(hc + 1) * LANES] = jnp.where(in_half, src, 0.0).astype(qc_o.dtype)

    hq_o[0] = _silu(zb[0])
    ff_o[0] = forget(zb[1], lower_bound(lbf_ref))
    fb_o[0] = forget(zb[2], lower_bound(lbb_ref))
    hv_o[0] = zb[3]
    hg_o[0] = _silu(zb[4])


def _in_proj(x, mod, mod_row, w_in, lb_f, lb_b, qn, kn, rope, kv_prev, *, li):
    bsz, t, d = x.shape
    latent = rope is not None
    depth, _, n_in = w_in.shape
    tiles = t // ROW_TILE
    mix_a, mix_b, mix_c = d // 2, d // 4, d // 4
    kv_c = mix_c // 2
    x2 = x.reshape(bsz * t, d)

    row = lambda i: (i, 0)
    brow = lambda i: (i // tiles, i % tiles, 0)
    const = lambda i: (0, 0)
    in_specs = [pl.BlockSpec((ROW_TILE, d), row),
                pl.BlockSpec((1, 1, mod.shape[-1]), lambda i: (mod_row(i * ROW_TILE), 0, 0)),
                pl.BlockSpec((None, d, n_in), lambda i: (li, 0, 0)),
                pl.BlockSpec(lb_f.shape, const), pl.BlockSpec(lb_b.shape, const),
                pl.BlockSpec(qn.shape, const), pl.BlockSpec(kn.shape, const)]
    args = [x2, mod, w_in, lb_f, lb_b, qn, kn]
    if latent:
        in_specs += [pl.BlockSpec((ROW_TILE, LANES), lambda i: (i % tiles, 0))] * 2
        args += list(rope)
        slots, slot0 = 1, 0
    else:
        slots, slot0 = (depth, 0) if kv_prev is None else (1, li)
    aliases = {}
    if kv_prev is not None:
        for j, buf in enumerate(kv_prev):
            aliases[len(args)] = 7 + j
            in_specs.append(pl.BlockSpec(memory_space=pl.ANY))
            args.append(buf)
    krow = lambda i: (i // tiles, slot0, i % tiles, 0)
    kcol = lambda i: (i // tiles, slot0, 0, i % tiles)
    heads = mix_a // LANES

    out_specs = [pl.BlockSpec((ROW_TILE, mix_a), row)]
    out_shape = [jax.ShapeDtypeStruct((bsz * t, mix_a), BF16)]
    out_specs += [pl.BlockSpec((1, ROW_TILE, mix_b), brow)] * 5
    out_shape += [jax.ShapeDtypeStruct((bsz, t, mix_b), F32)] * 5
    out_specs.append(pl.BlockSpec((ROW_TILE, 2 * mix_c), row))
    out_shape.append(jax.ShapeDtypeStruct((bsz * t, 2 * mix_c), BF16))
    if latent:
        for width in (mix_a, mix_a, kv_c, kv_c):
            out_specs.append(pl.BlockSpec((1, 1, ROW_TILE, width), krow))
            out_shape.append(jax.ShapeDtypeStruct((bsz, 1, t, width), BF16))
    else:
        out_specs += [pl.BlockSpec((1, slots, mix_a, ROW_TILE), kcol),
                      pl.BlockSpec((1, slots, ROW_TILE * heads, LANES), krow),
                      pl.BlockSpec((1, slots, kv_c, ROW_TILE), kcol),
                      pl.BlockSpec((1, slots, kv_c, ROW_TILE), kcol)]
        out_shape += [jax.ShapeDtypeStruct((bsz, depth, mix_a, t), F32),
                      jax.ShapeDtypeStruct((bsz, depth, t * heads, LANES), F32),
                      jax.ShapeDtypeStruct((bsz, depth, kv_c, t), F32),
                      jax.ShapeDtypeStruct((bsz, depth, kv_c, t), F32)]
    return pl.pallas_call(
        functools.partial(_in_proj_kernel, li=li, d=d, latent=latent, n_alias=len(aliases)),
        grid=(bsz * tiles,),
        in_specs=in_specs, out_specs=out_specs, out_shape=out_shape,
        input_output_aliases=aliases,
        compiler_params=_params(1),
        name="in_proj_latent" if latent else "in_proj_context",
    )(*args)


def _softmax_parts(scores):
    m = functools.reduce(jnp.maximum, [jnp.max(s, axis=-1, keepdims=True) for s in scores])
    es = [jnp.exp2(s - m) for s in scores]
    denom = functools.reduce(lambda a, b: a + b, [jnp.sum(e, axis=-1, keepdims=True) for e in es])
    return es, 1.0 / denom


def _score_blocks(q, keys):
    return [_dot(q, k) if k_t else lax.dot_general(q, k, NT, preferred_element_type=F32)
            for k, k_t in keys]


def _diff_lambda(lq1, lk1, lq2, lk2, li):
    lam_init = 0.8 - 0.6 * math.exp(-0.3 * li)

    def lam_term(a, b):
        return jnp.exp(jnp.sum(a[li:li + 1, :] * b[li:li + 1, :], axis=-1, keepdims=True))

    return lam_term(lq1, lk1) - lam_term(lq2, lk2) + lam_init, lam_init


def _diff_head_gen(q, keys, vals, lam, gain, store):
    tq = q.shape[0]
    lane = lax.broadcasted_iota(jnp.int32, (1, LANES), 1)
    zero = jnp.zeros_like(q)
    q2 = jnp.concatenate([jnp.where(lane < HEAD_DIM, q, zero),
                          jnp.where(lane >= HEAD_DIM, q, zero)], axis=0)
    scores = _score_blocks(q2, keys)
    yield
    es, r = _softmax_parts(scores)
    yield
    r0 = r[0:tq]
    ratio = r[tq:2 * tq] * lam / r0
    o = None
    for e, v in zip(es, vals):
        part = _dot((e[0:tq] - e[tq:2 * tq] * ratio).astype(BF16), v)
        o = part if o is None else o + part
    yield
    o = o * r0
    ms = jnp.mean(o * o, axis=-1, keepdims=True)
    store(o * lax.rsqrt(ms + RMS_EPS) * gain)


def _gqa_group_gen(n, q, keys, vals, store):
    tq = q.shape[0] // 2
    scores = _score_blocks(q, keys)
    yield
    es, r = _softmax_parts(scores)
    yield
    o = None
    for e, (v, v_t) in zip(es, vals):
        p = e.astype(BF16)
        part = lax.dot_general(p, v, NT, preferred_element_type=F32) if v_t else _dot(p, v)
        o = part if o is None else o + part
    yield
    o = o * r
    first = o[0:tq]
    second = o[tq:2 * tq]
    if n == 0:
        second = pltpu.roll(second, HEAD_DIM, 1)
    else:
        first = pltpu.roll(first, HEAD_DIM, 1)
    lane = lax.broadcasted_iota(jnp.int32, (1, LANES), 1)
    store(jnp.where(lane < HEAD_DIM, first, second))


def _head_masks(width):
    lane_head = lax.broadcasted_iota(jnp.int32, (1, width), 1) // HEAD_DIM
    return [lane_head == h for h in range(width // HEAD_DIM)]


def _stack_heads(x, masks):
    return jnp.concatenate([jnp.where(m, x, jnp.zeros_like(x)) for m in masks], axis=0)


def _block_diag_mask(width):
    r = lax.broadcasted_iota(jnp.int32, (width, width), 0) // HEAD_DIM
    c = lax.broadcasted_iota(jnp.int32, (width, width), 1) // HEAD_DIM
    return r == c


def _ref_rows(b, offsets, span):
    width = b.shape[-1]
    return jnp.concatenate([jnp.broadcast_to(b[o:o + 1], (span, width)) for o in offsets], axis=0)


def _run_interleaved(gens):
    live = list(gens)
    while live:
        for g in list(live):
            try:
                next(g)
            except StopIteration:
                live.remove(g)


def _hgrn_chunks_gen(problems, out):
    n = len(problems)
    c, width = problems[0][0].shape
    qs = [p[0] for p in problems]
    vs = [p[2] for p in problems]
    sts = [p[3] for p in problems]
    rev = [p[4] for p in problems]
    chains = range(n)
    masks = _head_masks(width)
    trow =lax.broadcasted_iota(jnp.int32, (c, 1), 0)
    t_full = lax.broadcasted_iota(jnp.int32, (c, width), 0)
    s_full = lax.broadcasted_iota(jnp.int32, (c, width), 1) % c

    ks = [1.0 - p[1] for p in problems]
    b = [jnp.log(p[1]) * math.log2(math.e) for p in problems]
    step = 1
    while step < c:
        for j in chains:
            if rev[j]:
                b[j] = b[j] + jnp.where(trow < c - step, pltpu.roll(b[j], c - step, 0), 0.0)
            else:
                b[j] = b[j] + jnp.where(trow >= step, pltpu.roll(b[j], step, 0), 0.0)
        step *= 2
        yield
    b_end = [b[j][0:1] if rev[j] else b[j][c - 1:c] for j in chains]

    o = [lax.dot_general((qs[j] * jnp.exp2(b[j])).astype(BF16), sts[j].astype(BF16), NT,
                         preferred_element_type=F32) for j in chains]
    yield

    a = [None] * n
    m = c // 2
    while m >= DIAG_BLOCK:
        blocks = c // (2 * m)
        same = (t_full // (2 * m)) == (s_full // (2 * m))
        for j in chains:
            ref = _ref_rows(b[j], [i * 2 * m + (m if rev[j] else m - 1) for i in range(blocks)],
                            2 * m)
            is_q = ((trow % (2 * m)) < m) if rev[j] else ((trow % (2 * m)) >= m)
            e = jnp.exp2((b[j] - ref) * jnp.where(is_q, 1.0, -1.0))
            ql = jnp.where(is_q, qs[j] * e, 0.0).astype(BF16)
            kl = jnp.where(is_q, 0.0, ks[j] * e).astype(BF16)
            al = lax.dot_general(ql, _stack_heads(kl, masks), NT, preferred_element_type=F32)
            if blocks > 1:
                al = jnp.where(same, al, 0.0)
            a[j] = al if a[j] is None else a[j] + al
        m //= 2
        yield
    blocks = c // DIAG_BLOCK
    mid = DIAG_BLOCK // 2
    same = (t_full // DIAG_BLOCK) == (s_full // DIAG_BLOCK)
    for j in chains:
        ref = _ref_rows(b[j], [i * DIAG_BLOCK + (mid if rev[j] else mid - 1) for i in range(blocks)],
                        DIAG_BLOCK)
        d = b[j] - ref
        ql = (qs[j] * jnp.exp2(d)).astype(BF16)
        kl = (ks[j] * jnp.exp2(-d)).astype(BF16)
        al = lax.dot_general(ql, _stack_heads(kl, masks), NT, preferred_element_type=F32)
        causal = (s_full >= t_full) if rev[j] else (s_full <= t_full)
        a[j] = a[j] + jnp.where(same & causal, al, 0.0)
    yield

    v_b = [v.astype(BF16) for v in vs]
    o = [o[j] + _dot(a[j].astype(BF16), _stack_heads(v_b[j], masks)) for j in chains]
    yield

    bd = _block_diag_mask(width)
    upd = [lax.dot_general(v_b[j], (ks[j] * jnp.exp2(b_end[j] - b[j])).astype(BF16), TN,
                           preferred_element_type=F32) for j in chains]
    st_new = [sts[j] * jnp.exp2(b_end[j]) + jnp.where(bd, upd[j], 0.0) for j in chains]
    out.extend(zip(o, st_new))


def _mxu_transpose(x):
    n = x.shape[1]
    r = lax.broadcasted_iota(jnp.int32, (n, n), 0)
    c = lax.broadcasted_iota(jnp.int32, (n, n), 1)
    eye = (r == c).astype(BF16)
    acc = None
    rem = x
    for _ in range(3):
        piece = rem.astype(BF16)
        rem = rem - piece.astype(F32)
        part = lax.dot_general(eye, piece, NT, preferred_element_type=F32)
        acc = part if acc is None else acc + part
    return acc


def _mixer_ctx_kernel(*refs, li, n_alias):
    refs = list(refs)
    (qa_ref, kat_ref, va_ref, qc_ref, kct_ref, vct_ref, hq_ref, ff_ref, fb_ref, hv_ref, hg_ref,
     gn_ref, lq1, lk1, lq2, lk2, sub_ref) = refs[:17]
    pos = 17 + n_alias
    oa_ref, oc_ref, ob_ref, sf_ref, sb_ref = refs[pos:pos + 5]
    st_ref, of_ref, obk_ref = refs[pos + 5:pos + 8]
    nb, t, width = hq_ref.shape
    nc = t // CHUNK
    heads_b = width // HEAD_DIM
    heads_a = qa_ref.shape[1] // LANES
    groups_c = qc_ref.shape[1] // LANES // 2
    lam, lam_init = _diff_lambda(lq1, lk1, lq2, lk2, li)
    gain = sub_ref[li:li + 1, :] * (1.0 - lam_init)

    for j in range(2 * nb):
        st_ref[j] = jnp.zeros((width, width), F32)

    def diff_unit(n, h):
        rows = slice(n * t, (n + 1) * t)
        c = slice(h * LANES, (h + 1) * LANES)
        keys = [(kat_ref[n, 0, c, :].astype(BF16), True)]
        vals = [va_ref[n, 0, pl.ds(h, t, stride=heads_a), :].astype(BF16)]

        def store(o):
            oa_ref[rows, c] = o.astype(oa_ref.dtype)

        yield from _diff_head_gen(qa_ref[rows, c], keys, vals, lam, gain, store)

    def gqa_unit(n, g):
        rows = slice(n * t, (n + 1) * t)
        q = jnp.concatenate([qc_ref[rows, (2 * g + j) * LANES:(2 * g + j + 1) * LANES]
                             for j in range(2)], axis=0)
        keys = [(kct_ref[n, 0].astype(BF16), True)]
        vals = [(vct_ref[n, 0].astype(BF16), True)]

        def store(o):
            oc_ref[rows, g * LANES:(g + 1) * LANES] = o.astype(oc_ref.dtype)

        yield from _gqa_group_gen(g, q, keys, vals, store)

    def scan_step(ci):
        rows = (slice(ci * CHUNK, (ci + 1) * CHUNK), slice((nc - 1 - ci) * CHUNK, (nc - ci) * CHUNK))
        loaded = []
        for n in range(nb):
            for d, f_ref in enumerate((ff_ref, fb_ref)):
                r = rows[d]
                loaded.append((hq_ref[n, r, :], f_ref[n, r, :], hv_ref[n, r, :], st_ref[2 * n + d],
                               bool(d)))
        out = []
        yield from _hgrn_chunks_gen(loaded, out)
        for j, (o, st) in enumerate(out):
            n, d = divmod(j, 2)
            (obk_ref if d else of_ref)[n, rows[d], :] = o
            st_ref[j] = st

    units = []
    for n in range(nb):
        units += [diff_unit(n, h) for h in range(heads_a)]
        units += [gqa_unit(n, g) for g in range(groups_c)]
    share = -(-len(units) // nc)
    for ci in range(nc):
        mine = units[ci * share:(ci + 1) * share]
        _run_interleaved([scan_step(ci)] + mine)

    for n in range(nb):
        o = of_ref[n] + obk_ref[n]
        ob_ref[n] = (_group_rms(o, gn_ref[li:li + 1, :], HEAD_DIM) * hg_ref[n]).astype(ob_ref.dtype)
        for d, dst in enumerate((sf_ref, sb_ref)):
            st = st_ref[2 * n + d]
            rows = st[0:HEAD_DIM]
            for h in range(1, heads_b):
                rows = rows + st[h * HEAD_DIM:(h + 1) * HEAD_DIM]
            final = _mxu_transpose(rows).reshape(heads_b, HEAD_DIM, HEAD_DIM)
            for slot in range(dst.shape[1]):
                dst[n, slot] = final


def _mixer_ctx(qa, ka_t, va_rows, qc, kc_t, vc_t, hq, ff, fb, hv, hg, gn, lam, subln, state_prev,
               *, li, depth, nb):
    bsz, t, width = hq.shape
    heads_b = width // HEAD_DIM
    rows = lambda b: (b, 0)
    at_layer = lambda b: (b, li, 0, 0)
    seq = pl.BlockSpec((nb, t, width), lambda b: (b, 0, 0))
    const = lambda b: (0, 0)
    in_specs = [pl.BlockSpec((nb * t, qa.shape[1]), rows),
                pl.BlockSpec((nb, 1) + ka_t.shape[2:], at_layer),
                pl.BlockSpec((nb, 1) + va_rows.shape[2:], at_layer),
                pl.BlockSpec((nb * t, qc.shape[1]), rows),
                pl.BlockSpec((nb, 1) + kc_t.shape[2:], at_layer),
                pl.BlockSpec((nb, 1) + vc_t.shape[2:], at_layer),
                seq, seq, seq, seq, seq, pl.BlockSpec(gn.shape, const)]
    in_specs += [pl.BlockSpec(a.shape, const) for a in (*lam, subln)]
    args = [qa, ka_t, va_rows, qc, kc_t, vc_t, hq, ff, fb, hv, hg, gn, *lam, subln]
    aliases = {}
    if state_prev is not None:
        for j, buf in enumerate(state_prev):
            aliases[len(args)] = 3 + j
            in_specs.append(pl.BlockSpec(memory_space=pl.ANY))
            args.append(buf)
    slots, slot0 = (depth, 0) if state_prev is None else (1, li)
    state_spec = pl.BlockSpec((nb, slots, heads_b, HEAD_DIM, HEAD_DIM), lambda b: (b, slot0, 0, 0, 0))
    out_specs = [pl.BlockSpec((nb * t, qa.shape[1]), rows),
                 pl.BlockSpec((nb * t, qc.shape[1] // 2), rows), seq, state_spec, state_spec]
    out_shape = [jax.ShapeDtypeStruct(qa.shape, BF16),
                 jax.ShapeDtypeStruct((qc.shape[0], qc.shape[1] // 2), BF16),
                 jax.ShapeDtypeStruct((bsz, t, width), BF16)]
    out_shape += [jax.ShapeDtypeStruct((bsz, depth, heads_b, HEAD_DIM, HEAD_DIM), F32)] * 2
    return pl.pallas_call(
        functools.partial(_mixer_ctx_kernel, li=li, n_alias=len(aliases)),
        grid=(bsz // nb,),
        in_specs=in_specs, out_specs=out_specs, out_shape=out_shape,
        input_output_aliases=aliases,
        scratch_shapes=[pltpu.VMEM((2 * nb, width, width), F32),
                        pltpu.VMEM((nb, t, width), F32), pltpu.VMEM((nb, t, width), F32)],
        compiler_params=_params(1),
        name="mixers_context",
    )(*args)


def _mixer_lat_kernel(qa_ref, k_ref, v_ref, ck_ref, cv_ref, qc_ref, kc_ref, vc_ref, cck_ref, ccv_ref,
                      hq_ref, ff_ref, fb_ref, hv_ref, hg_ref, gn_ref, s0f_ref, s0b_ref,
                      lq1, lk1, lq2, lk2, sub_ref, oa_ref, oc_ref, ob_ref,
                      st_ref, of_ref, obk_ref, *, li, tq):
    _, t, width = hq_ref.shape
    nc = t // CHUNK
    nq = t // tq
    steps_per_trip = nc // nq
    heads_b = width // HEAD_DIM
    heads_a = qa_ref.shape[1] // LANES
    groups_c = qc_ref.shape[1] // LANES // 2
    past = ck_ref.shape[-1]
    lam, lam_init = _diff_lambda(lq1, lk1, lq2, lk2, li)
    gain = sub_ref[li:li + 1, :] * (1.0 - lam_init)
    bd = _block_diag_mask(width)

    for d, src in enumerate((s0f_ref, s0b_ref)):
        x = src[0, 0].reshape(width, HEAD_DIM)
        xt = _mxu_transpose(x)
        st_ref[d] = jnp.where(bd, jnp.concatenate([xt] * heads_b, axis=0), 0.0)

    def trip(qt, carry):
        q_rows = pl.ds(pl.multiple_of(qt * tq, tq), tq)

        def diff_unit(h):
            c = slice(h * LANES, (h + 1) * LANES)
            keys = [(k_ref[0, 0, :, c], False), (ck_ref[0, 0, c, :].astype(BF16), True)]
            vals = [v_ref[0, 0, :, c], cv_ref[0, 0, pl.ds(h, past, stride=heads_a), :].astype(BF16)]

            def store(o):
                oa_ref[q_rows, c] = o.astype(oa_ref.dtype)

            yield from _diff_head_gen(qa_ref[q_rows, c], keys, vals, lam, gain, store)

        def gqa_unit(g):
            q = jnp.concatenate([qc_ref[q_rows, (2 * g + j) * LANES:(2 * g + j + 1) * LANES]
                                 for j in range(2)], axis=0)
            keys = [(kc_ref[0, 0], False), (cck_ref[0, 0].astype(BF16), True)]
            vals = [(vc_ref[0, 0], False), (ccv_ref[0, 0].astype(BF16), True)]

            def store(o):
                oc_ref[q_rows, g * LANES:(g + 1) * LANES] = o.astype(oc_ref.dtype)

            yield from _gqa_group_gen(g, q, keys, vals, store)

        def scan_step(cj):
            ci = qt * steps_per_trip + cj
            rows = (pl.ds(pl.multiple_of(ci * CHUNK, CHUNK), CHUNK),
                    pl.ds(pl.multiple_of((nc - 1 - ci) * CHUNK, CHUNK), CHUNK))
            loaded = [(hq_ref[0, rows[d], :], f_ref[0, rows[d], :], hv_ref[0, rows[d], :],
                       st_ref[d], bool(d)) for d, f_ref in enumerate((ff_ref, fb_ref))]
            out = []
            yield from _hgrn_chunks_gen(loaded, out)
            for d, (o, st) in enumerate(out):
                (obk_ref if d else of_ref)[rows[d], :] = o
                st_ref[d] = st

        units = [diff_unit(h) for h in range(heads_a)] + [gqa_unit(g) for g in range(groups_c)]
        for cj in range(steps_per_trip):
            mine = units[cj::steps_per_trip]
            _run_interleaved([scan_step(cj)] + mine)
        return carry

    lax.fori_loop(0, nq, trip, 0)

    o = of_ref[...] + obk_ref[...]
    ob_ref[0] = (_group_rms(o, gn_ref[li:li + 1, :], HEAD_DIM) * hg_ref[0]).astype(ob_ref.dtype)


def _mixer_lat(qa, ka, va, qc, kc, vc, cache, hq, ff, fb, hv, hg, gn, state, lam, subln, *, li, tq):
    bsz, t, width = hq.shape
    rows = lambda b: (b, 0)
    own = lambda b: (b, 0, 0, 0)
    at_layer = lambda b: (b, li, 0, 0)
    seq = pl.BlockSpec((1, t, width), lambda b: (b, 0, 0))
    const = lambda b: (0, 0)
    in_specs = [pl.BlockSpec((t, qa.shape[1]), rows),
                pl.BlockSpec((1,) + ka.shape[1:], own), pl.BlockSpec((1,) + va.shape[1:], own),
                pl.BlockSpec((1, 1) + cache[0].shape[2:], at_layer),
                pl.BlockSpec((1, 1) + cache[1].shape[2:], at_layer),
                pl.BlockSpec((t, qc.shape[1]), rows),
                pl.BlockSpec((1,) + kc.shape[1:], own), pl.BlockSpec((1,) + vc.shape[1:], own),
                pl.BlockSpec((1, 1) + cache[2].shape[2:], at_layer),
                pl.BlockSpec((1, 1) + cache[3].shape[2:], at_layer),
                seq, seq, seq, seq, seq, pl.BlockSpec(gn.shape, const)]
    in_specs += [pl.BlockSpec((1, 1) + state[0].shape[2:], lambda b: (b, li, 0, 0, 0))] * 2
    in_specs += [pl.BlockSpec(a.shape, const) for a in (*lam, subln)]
    args = [qa, ka, va, cache[0], cache[1], qc, kc, vc, cache[2], cache[3], hq, ff, fb, hv, hg, gn,
            *state, *lam, subln]
    return pl.pallas_call(
        functools.partial(_mixer_lat_kernel, li=li, tq=tq),
        grid=(bsz,),
        in_specs=in_specs,
        out_specs=[pl.BlockSpec((t, qa.shape[1]), rows),
                   pl.BlockSpec((t, qc.shape[1] // 2), rows), seq],
        out_shape=[jax.ShapeDtypeStruct(qa.shape, BF16),
                   jax.ShapeDtypeStruct((qc.shape[0], qc.shape[1] // 2), BF16),
                   jax.ShapeDtypeStruct((bsz, t, width), BF16)],
        scratch_shapes=[pltpu.VMEM((2, width, width), F32),
                        pltpu.VMEM((t, width), F32), pltpu.VMEM((t, width), F32)],
        compiler_params=_params(1),
        name="mixers_latent",
    )(*args)


def _out_mlp_kernel(*refs, li, d, alpha, ff_chunk, cast_next):
    (x_ref, oa_ref, ob_ref, oc_ref, mod_ref, wo_ref, w1_ref, w2_ref,
     g1_ref, b1_ref, g2_ref, b2_ref) = refs[:12]
    if cast_next:
        w1n_ref, w2n_ref, y_ref, w1n_o, w2n_o = refs[12:]
        w1n_o[...] = w1n_ref[...].astype(BF16)
        w2n_o[...] = w2n_ref[...].astype(BF16)
    else:
        y_ref = refs[12]
    wa, wb = oa_ref.shape[-1], ob_ref.shape[-1]
    layer = slice(li, li + 1)
    gate1 = mod_ref[0, :, 2 * d:3 * d]
    shift2 = mod_ref[0, :, 3 * d:4 * d]
    gain2 = mod_ref[0, :, 4 * d:5 * d]
    gate2 = mod_ref[0, :, 5 * d:6 * d]
    subs = [slice(s * ROW_TILE, (s + 1) * ROW_TILE) for s in range(x_ref.shape[0] // ROW_TILE)]
    wo = wo_ref[...].astype(BF16)
    m = [_dot(oa_ref[r, :], wo[0:wa, :]) + _dot(ob_ref[r, :], wo[wa:wa + wb, :])
         + _dot(oc_ref[r, :], wo[wa + wb:, :]) for r in subs]
    x1 = [_layernorm(alpha * x_ref[r, :] + gate1 * mi, g1_ref[layer, :], b1_ref[layer, :])
          for r, mi in zip(subs, m)]
    h2 = [(xi * (1.0 + gain2) + shift2).astype(BF16) for xi in x1]
    acc = [None] * len(subs)
    for j in range(w1_ref.shape[-1] // ff_chunk):
        cols = slice(j * ff_chunk, (j + 1) * ff_chunk)
        hid = [jnp.maximum(_dot(hi, w1_ref[:, cols]), 0.0) for hi in h2]
        for s, hd in enumerate(hid):
            part = _dot((hd * hd).astype(BF16), w2_ref[cols, :])
            acc[s] = part if acc[s] is None else acc[s] + part
    for r, xi, ai in zip(subs, x1, acc):
        y_ref[r, :] = _layernorm(alpha * xi + gate2 * ai, g2_ref[layer, :], b2_ref[layer, :])


def _out_mlp(x, oa, ob, oc, mod, mod_row, w_out, w_ff1, w_ff2, ln, next_ff, *, li, alpha):
    bsz, t, d = x.shape
    rows = 2 * ROW_TILE
    steps = bsz * t // rows
    row = lambda i: (i, 0)
    const = lambda i: (0, 0)
    resident = lambda a: pl.BlockSpec(a.shape, const, pipeline_mode=pl.Buffered(1))
    in_specs = [pl.BlockSpec((rows, d), row),
                pl.BlockSpec((rows, oa.shape[-1]), row),
                pl.BlockSpec((rows, ob.shape[-1]), row),
                pl.BlockSpec((rows, oc.shape[-1]), row),
                pl.BlockSpec((1, 1, mod.shape[-1]), lambda i: (mod_row(i * rows), 0, 0)),
                pl.BlockSpec((None,) + w_out.shape[1:], lambda i: (li, 0, 0),
                             pipeline_mode=pl.Buffered(1)),
                resident(w_ff1), resident(w_ff2)]
    in_specs += [pl.BlockSpec(a.shape, const) for a in ln]
    args = [x.reshape(bsz * t, d), oa, ob.reshape(bsz * t, -1), oc, mod, w_out, w_ff1, w_ff2, *ln]
    out_specs = [pl.BlockSpec((rows, d), row)]
    out_shape = [jax.ShapeDtypeStruct((bsz * t, d), F32)]
    if next_ff is not None:
        for w in next_ff:
            blk = (w.shape[1] // steps, w.shape[2])
            in_specs.append(pl.BlockSpec((None,) + blk, lambda i: (li + 1, i, 0)))
            args.append(w)
            out_specs.append(pl.BlockSpec(blk, row))
            out_shape.append(jax.ShapeDtypeStruct(w.shape[1:], BF16))
    res = pl.pallas_call(
        functools.partial(_out_mlp_kernel, li=li, d=d, alpha=alpha, ff_chunk=1024,
                          cast_next=next_ff is not None),
        grid=(steps,),
        in_specs=in_specs, out_specs=out_specs, out_shape=out_shape,
        compiler_params=_params(1),
        name="out_mlp",
    )(*args)
    return res[0].reshape(bsz, t, d), tuple(res[1:])


def _rope_tables(n_tokens):
    pairs = HEAD_DIM // 4
    tok = np.arange(n_tokens)
    row = (tok // GRID_W).astype(np.float64)
    col = (tok % GRID_W).astype(np.float64)
    inv = ROPE_THETA ** (-np.arange(pairs, dtype=np.float64) / pairs)
    ang = np.concatenate([row[:, None] * inv, col[:, None] * inv], axis=-1)
    lane = np.arange(LANES)
    pair = (lane % HEAD_DIM) // 2
    sign = np.where(lane % 2 == 0, -1.0, 1.0)
    return (jnp.asarray(np.cos(ang)[:, pair], F32), jnp.asarray(np.sin(ang)[:, pair] * sign, F32))


def kernel(x_prompt, x_sample, cache_a_k, cache_a_v, cache_c_k, cache_c_v, state_b_fwd, state_b_bwd, c, c_ctx, w_ada, b_ada, w_in, w_out, lam_q1, lam_k1, lam_q2, lam_k2, subln_g, lb_logits_fwd, lb_logits_bwd, gnorm_g, qnorm_g, knorm_g, ln1_g, ln1_b, ln2_g, ln2_b, w_ff1, w_ff2):
    depth = w_in.shape[0]
    bsz, seq, d = x_prompt.shape
    dec_bsz, dec_seq, _ = x_sample.shape
    past = cache_a_k.shape[2]
    alpha = (2 * depth) ** 0.25
    mix_a, mix_b, mix_c = d // 2, d // 4, d // 4

    mod = _modulation(c_ctx, c, w_ada, b_ada)
    rope = _rope_tables(dec_seq)

    cache = (cache_a_k.transpose(0, 1, 3, 4, 5, 2).reshape(dec_bsz, depth, mix_a, past),
             cache_a_v.reshape(dec_bsz, depth, past * (mix_a // LANES), LANES),
             cache_c_k.transpose(0, 1, 3, 4, 2).reshape(dec_bsz, depth, mix_c // 2, past),
             cache_c_v.transpose(0, 1, 3, 4, 2).reshape(dec_bsz, depth, mix_c // 2, past))
    lam = (lam_q1, lam_k1, lam_q2, lam_k2)

    ff_b = [(w_ff1[0].astype(BF16), w_ff2[0].astype(BF16))]
    qn =jnp.tile(qnorm_g, (1, mix_c // HEAD_DIM))
    kn = jnp.tile(knorm_g, (1, mix_c // 2 // HEAD_DIM))
    gn = jnp.tile(gnorm_g, (1, mix_b // HEAD_DIM))
    ln = (ln1_g, ln1_b, ln2_g, ln2_b)

    def stream(x, li, latent, own_prev):
        n, t, _ = x.shape
        mod_row = ((lambda r0: li * MOD_ROWS + 1 + r0 // t) if latent
                   else (lambda r0: li * MOD_ROWS))
        (qa, hq, ff, fb, hv, hg, qc, ka, va, kc, vc) = _in_proj(
            x, mod, mod_row, w_in, lb_logits_fwd, lb_logits_bwd, qn, kn,
            rope if latent else None, None if own_prev is None else own_prev[0:4], li=li)
        if latent:
            oa, oc, ob = _mixer_lat(qa, ka, va, qc, kc, vc, cache, hq, ff, fb, hv, hg, gn,
                                    (state_b_fwd, state_b_bwd), lam, subln_g, li=li, tq=256)
            own = None
        else:
            oa, oc, ob, s_f, s_b = _mixer_ctx(
                qa, ka, va, qc, kc, vc, hq, ff, fb, hv, hg, gn, lam, subln_g,
                None if own_prev is None else own_prev[4:6], li=li, depth=depth, nb=2)
            own = (ka, va, kc, vc, s_f, s_b)
        cast_next = not latent and li + 1 < depth
        y, next_ff = _out_mlp(x, oa, ob, oc, mod, mod_row, w_out, *ff_b[li], ln,
                              (w_ff1, w_ff2) if cast_next else None, li=li, alpha=alpha)
        if cast_next:
            ff_b.append(next_ff)
        return y, own

    y_prompt, y_sample = x_prompt, x_sample
    own = None
    for li in range(depth):
        y_prompt, own = stream(y_prompt, li, False, own)
        y_sample, _ = stream(y_sample, li, True, None)

    heads_a = mix_a // (2 * HEAD_DIM)
    new_a_k = own[0].reshape(bsz, depth, heads_a, 2, HEAD_DIM, seq).transpose(0, 1, 5, 2, 3, 4)
    new_a_v = own[1].reshape(bsz, depth, seq, heads_a, 2 * HEAD_DIM)
    kv_heads = mix_c // 2 // HEAD_DIM
    new_c_k = own[2].reshape(bsz, depth, kv_heads, HEAD_DIM, seq).transpose(0, 1, 4, 2, 3)
    new_c_v = own[3].reshape(bsz, depth, kv_heads, HEAD_DIM, seq).transpose(0, 1, 4, 2, 3)
    return (y_prompt, y_sample, new_a_k, new_a_v, new_c_k, new_c_v, own[4], own[5])
```

```python
import functools
import math

import jax
import jax.numpy as jnp
import numpy as np
from jax import lax
from jax.experimental import pallas as pl
from jax.experimental.pallas import tpu as pltpu

GRID_W = 64
HEAD_DIM = 64
ROPE_THETA = 10000.0
LN_EPS = 1e-6
RMS_EPS = 1e-6
F_MIN = 1e-6
CHUNK = 64
DIAG_BLOCK = 8
LANES = 128
ROW_TILE = 256
VMEM_LIMIT = 56 * 1024 * 1024

F32 = jnp.float32
BF16 = jnp.bfloat16
NT = (((1,), (1,)), ((), ()))
TN = (((0,), (0,)), ((), ()))


def _params(n_grid):
    return pltpu.CompilerParams(dimension_semantics=("arbitrary",) * n_grid,
                                vmem_limit_bytes=VMEM_LIMIT)


def _dot(a, b):
    return jnp.dot(a, b, preferred_element_type=F32)


def _split_dot(a, b_bf16, passes):
    acc = None
    rem = a
    for _ in range(passes):
        piece = rem.astype(BF16)
        rem = rem - piece.astype(F32)
        part = _dot(piece, b_bf16)
        acc = part if acc is None else acc + part
    return acc


def _group_ones(n, group):
    r = lax.broadcasted_iota(jnp.int32, (n, n), 0) // group
    c = lax.broadcasted_iota(jnp.int32, (n, n), 1) // group
    return (r == c).astype(BF16)


def _group_mean_square(x, group):
    n = x.shape[-1]
    return _split_dot(x * x, _group_ones(n, group), 2) * (1.0 / group)


def _group_rms(x, g_row, group):
    return x * lax.rsqrt(_group_mean_square(x, group) + RMS_EPS) * g_row


def _pair_swap(x):
    lane = lax.broadcasted_iota(jnp.int32, x.shape, 1)
    return jnp.where(lane % 2 == 0, pltpu.roll(x, LANES - 1, 1), pltpu.roll(x, 1, 1))


def _rope(x, cos, sin):
    blocks = []
    for j in range(x.shape[-1] // LANES):
        blk = x[:, j * LANES:(j + 1) * LANES]
        blocks.append(blk * cos + _pair_swap(blk) * sin)
    return blocks[0] if len(blocks) == 1 else jnp.concatenate(blocks, axis=-1)


def _silu(x):
    return x * jax.nn.sigmoid(x)


def _layernorm(x, g, b):
    mu = jnp.mean(x, axis=-1, keepdims=True)
    xc = x - mu
    var = jnp.mean(xc * xc, axis=-1, keepdims=True)
    return xc * lax.rsqrt(var + LN_EPS) * g + b


MOD_ROWS = 8


MOD_K = 256


def _mod_kernel(cctx_ref, c_ref, w_ref, b_ref, o_ref, s_ref, acc_ref):
    layer, k = pl.program_id(0), pl.program_id(1)
    n_req = c_ref.shape[0]

    @pl.when(k == 0)
    def _():
        acc_ref[...] = jnp.broadcast_to(b_ref[pl.ds(layer, 1), :], acc_ref.shape)

    s_ref[...] = jnp.zeros_like(s_ref)
    s_ref[0:1, :] = _silu(cctx_ref[...])
    s_ref[1:1 + n_req, :] = _silu(c_ref[...])
    acc_ref[...] += _dot(s_ref[...].astype(BF16), w_ref[0].astype(BF16))

    @pl.when(k == pl.num_programs(1) - 1)
    def _():
        for r in range(MOD_ROWS):
            o_ref[r] = acc_ref[r:r + 1, :]


def _modulation(c_ctx, c, w_ada, b_ada):
    depth, d, n = w_ada.shape
    return pl.pallas_call(
        _mod_kernel,
        grid=(depth, d // MOD_K),
        in_specs=[pl.BlockSpec((1, MOD_K), lambda l, k: (0, k)),
                  pl.BlockSpec((c.shape[0], MOD_K), lambda l, k: (0, k)),
                  pl.BlockSpec((1, MOD_K, n), lambda l, k: (l, k, 0)),
                  pl.BlockSpec(b_ada.shape, lambda l, k: (0, 0))],
        out_specs=pl.BlockSpec((MOD_ROWS, 1, n), lambda l, k: (l, 0, 0)),
        out_shape=jax.ShapeDtypeStruct((depth * MOD_ROWS, 1, n), F32),
        scratch_shapes=[pltpu.VMEM((MOD_ROWS, MOD_K), F32), pltpu.VMEM((MOD_ROWS, n), F32)],
        compiler_params=_params(2),
        name="adaln_modulation",
    )(c_ctx.reshape(1, d), c, w_ada, b_ada)


def _in_proj_kernel(*refs, li, d, latent, n_alias):
    refs = list(refs)
    x_ref, mod_ref, w_ref, lbf_ref, lbb_ref, gains_ref = refs[:6]
    pos = 6
    if latent:
        cos, sin = refs[pos][...], refs[pos + 1][...]
        pos += 2
    pos += n_alias
    qa_o, hq_o, ff_o, fb_o, hv_o, hg_o, qc_o = refs[pos:pos + 7]
    if latent:
        ka_o, va_o, kc_o, vc_o = refs[pos + 7:pos + 11]
    else:
        ka_o, va_rows_o, kct_o, vct_o = refs[pos + 7:pos + 11]

    def store_kv(ref, val):
        for slot in range(ref.shape[1]):
            ref[0, slot] = val.astype(ref.dtype)

    mix_a, mix_b, mix_c = d // 2, d // 4, d // 4
    kv_c = mix_c // 2
    scale = HEAD_DIM ** -0.5 * math.log2(math.e)

    shift = mod_ref[0, :, 0:d]
    gain = mod_ref[0, :, d:2 * d]
    h = x_ref[...] * (1.0 + gain) + shift

    def proj(start, width):
        return _dot(h, w_ref[:, start:start + width])

    off_b = 3 * mix_a
    off_c = off_b + 5 * mix_b

    zq = proj(off_c, mix_c)
    zk = proj(off_c + mix_c, kv_c)
    vc = proj(off_c + mix_c + kv_c, kv_c)
    qa = proj(0, mix_a)
    msq = _group_mean_square(zq, HEAD_DIM)
    msk = _group_mean_square(zk, HEAD_DIM)
    ka = proj(mix_a, mix_a)
    va = proj(2 * mix_a, mix_a)
    qc = zq * lax.rsqrt(msq + RMS_EPS) * gains_ref[li:li + 1, 0:mix_c]
    kc = zk * lax.rsqrt(msk + RMS_EPS) * gains_ref[li:li + 1, mix_c:mix_c + kv_c]

    if latent:
        qa = _rope(qa, cos, sin)
        ka = _rope(ka, cos, sin)
    qa_o[...] = (qa * scale).astype(qa_o.dtype)
    if latent:
        store_kv(ka_o, ka)
        store_kv(va_o, va)
    else:
        store_kv(ka_o, ka.T)
        heads = mix_a // LANES
        for slot in range(va_rows_o.shape[1]):
            for hd in range(heads):
                va_rows_o[0, slot, pl.ds(hd, ROW_TILE, stride=heads), :] = (
                    va[:, hd * LANES:(hd + 1) * LANES])

    def lower_bound(ref):
        logits = ref[...]
        e = jnp.exp(logits - jnp.max(logits, axis=0, keepdims=True))
        sm = e / jnp.sum(e, axis=0, keepdims=True)
        return jnp.sum(sm[0:li + 1], axis=0, keepdims=True) - sm[0:1]

    def forget(x, lb):
        return jnp.maximum(lb + (1.0 - lb) * jax.nn.sigmoid(x), F_MIN)

    off = off_b
    zb = [proj(off + j * mix_b, mix_b) for j in range(5)]

    if latent:
        kc = _rope(kc, cos, sin)
        qc = _rope(qc, cos, sin)
        store_kv(kc_o, kc)
        store_kv(vc_o, vc)
    else:
        store_kv(kct_o, kc.T)
        store_kv(vct_o, vc.T)
    qc = qc * scale
    lane = lax.broadcasted_iota(jnp.int32, (1, LANES), 1)
    for n in range(2):
        blk = qc[:, n * LANES:(n + 1) * LANES]
        in_half = (lane // HEAD_DIM) == n
        for g in range(2):
            src = blk if g == n else pltpu.roll(blk, HEAD_DIM, 1)
            hc = 2 * n + g
            qc_o[:, hc * LANES:(hc + 1) * LANES] = jnp.where(in_half, src, 0.0).astype(qc_o.dtype)

    hq_o[0] = _silu(zb[0])
    ff_o[0] = forget(zb[1], lower_bound(lbf_ref))
    fb_o[0] = forget(zb[2], lower_bound(lbb_ref))
    hv_o[0] = zb[3]
    hg_o[0] = _silu(zb[4])


def _in_proj(x, mod, mod_row, w_in, lb_f, lb_b, gains, rope, kv_prev, *, li):
    bsz, t, d = x.shape
    latent = rope is not None
    depth, _, n_in = w_in.shape
    tiles = t // ROW_TILE
    mix_a, mix_b, mix_c = d // 2, d // 4, d // 4
    kv_c = mix_c // 2
    x2 = x.reshape(bsz * t, d)

    row = lambda i: (i, 0)
    brow = lambda i: (i // tiles, i % tiles, 0)
    const = lambda i: (0, 0)
    in_specs = [pl.BlockSpec((ROW_TILE, d), row),
                pl.BlockSpec((1, 1, mod.shape[-1]), lambda i: (mod_row(i * ROW_TILE), 0, 0)),
                pl.BlockSpec((None, d, n_in), lambda i: (li, 0, 0)),
                pl.BlockSpec(lb_f.shape, const), pl.BlockSpec(lb_b.shape, const),
                pl.BlockSpec(gains.shape, const)]
    args = [x2, mod, w_in, lb_f, lb_b, gains]
    if latent:
        in_specs += [pl.BlockSpec((ROW_TILE, LANES), lambda i: (i % tiles, 0))] * 2
        args += list(rope)
        slots, slot0 = 1, 0
    else:
        slots, slot0 = (depth, 0) if kv_prev is None else (1, li)
    aliases = {}
    if kv_prev is not None:
        for j, buf in enumerate(kv_prev):
            aliases[len(args)] = 7 + j
            in_specs.append(pl.BlockSpec(memory_space=pl.ANY))
            args.append(buf)
    krow = lambda i: (i // tiles, slot0, i % tiles, 0)
    kcol = lambda i: (i // tiles, slot0, 0, i % tiles)
    heads = mix_a // LANES

    out_specs = [pl.BlockSpec((ROW_TILE, mix_a), row)]
    out_shape = [jax.ShapeDtypeStruct((bsz * t, mix_a), BF16)]
    out_specs += [pl.BlockSpec((1, ROW_TILE, mix_b), brow)] * 5
    out_shape += [jax.ShapeDtypeStruct((bsz, t, mix_b), F32)] * 5
    out_specs.append(pl.BlockSpec((ROW_TILE, 2 * mix_c), row))
    out_shape.append(jax.ShapeDtypeStruct((bsz * t, 2 * mix_c), BF16))
    if latent:
        for width in (mix_a, mix_a, kv_c, kv_c):
            out_specs.append(pl.BlockSpec((1, 1, ROW_TILE, width), krow))
            out_shape.append(jax.ShapeDtypeStruct((bsz, 1, t, width), BF16))
    else:
        out_specs += [pl.BlockSpec((1, slots, mix_a, ROW_TILE), kcol),
                      pl.BlockSpec((1, slots, ROW_TILE * heads, LANES), krow),
                      pl.BlockSpec((1, slots, kv_c, ROW_TILE), kcol),
                      pl.BlockSpec((1, slots, kv_c, ROW_TILE), kcol)]
        out_shape += [jax.ShapeDtypeStruct((bsz, depth, mix_a, t), F32),
                      jax.ShapeDtypeStruct((bsz, depth, t * heads, LANES), F32),
                      jax.ShapeDtypeStruct((bsz, depth, kv_c, t), F32),
                      jax.ShapeDtypeStruct((bsz, depth, kv_c, t), F32)]
    return pl.pallas_call(
        functools.partial(_in_proj_kernel, li=li, d=d, latent=latent, n_alias=len(aliases)),
        grid=(bsz * tiles,),
        in_specs=in_specs, out_specs=out_specs, out_shape=out_shape,
        input_output_aliases=aliases,
        compiler_params=_params(1),
        name="in_proj_latent" if latent else "in_proj_context",
    )(*args)


def _softmax_parts(scores):
    m = functools.reduce(jnp.maximum, [jnp.max(s, axis=-1, keepdims=True) for s in scores])
    es = [jnp.exp2(s - m) for s in scores]
    denom = functools.reduce(lambda a, b: a + b, [jnp.sum(e, axis=-1, keepdims=True) for e in es])
    return es, 1.0 / denom


def _score_blocks(q, keys):
    return [_dot(q, k) if k_t else lax.dot_general(q, k, NT, preferred_element_type=F32)
            for k, k_t in keys]


def _diff_lambda(lq1, lk1, lq2, lk2, li):
    lam_init = 0.8 - 0.6 * math.exp(-0.3 * li)

    def lam_term(a, b):
        return jnp.exp(jnp.sum(a[li:li + 1, :] * b[li:li + 1, :], axis=-1, keepdims=True))

    return lam_term(lq1, lk1) - lam_term(lq2, lk2) + lam_init, lam_init


def _diff_head_gen(q, keys, vals, lam, gain, store):
    tq = q.shape[0]
    lane = lax.broadcasted_iota(jnp.int32, (1, LANES), 1)
    zero = jnp.zeros_like(q)
    q2 = jnp.concatenate([jnp.where(lane < HEAD_DIM, q, zero),
                          jnp.where(lane >= HEAD_DIM, q, zero)], axis=0)
    scores = _score_blocks(q2, keys)
    yield
    es, r = _softmax_parts(scores)
    yield
    r0 = r[0:tq]
    ratio = r[tq:2 * tq] * lam / r0
    o = None
    for e, v in zip(es, vals):
        part = _dot((e[0:tq] - e[tq:2 * tq] * ratio).astype(BF16), v)
        o = part if o is None else o + part
    yield
    o = o * r0
    ms = jnp.mean(o * o, axis=-1, keepdims=True)
    store(o * lax.rsqrt(ms + RMS_EPS) * gain)


def _gqa_group_gen(n, q, keys, vals, store):
    tq = q.shape[0] // 2
    scores = _score_blocks(q, keys)
    yield
    es, r = _softmax_parts(scores)
    yield
    o = None
    for e, (v, v_t) in zip(es, vals):
        p = e.astype(BF16)
        part = lax.dot_general(p, v, NT, preferred_element_type=F32) if v_t else _dot(p, v)
        o = part if o is None else o + part
    yield
    o = o * r
    first = o[0:tq]
    second = o[tq:2 * tq]
    if n == 0:
        second = pltpu.roll(second, HEAD_DIM, 1)
    else:
        first = pltpu.roll(first, HEAD_DIM, 1)
    lane = lax.broadcasted_iota(jnp.int32, (1, LANES), 1)
    store(jnp.where(lane < HEAD_DIM, first, second))


def _head_masks(width):
    lane_head = lax.broadcasted_iota(jnp.int32, (1, width), 1) // HEAD_DIM
    return [lane_head == h for h in range(width // HEAD_DIM)]


def _stack_heads(x, masks):
    return jnp.concatenate([jnp.where(m, x, jnp.zeros_like(x)) for m in masks], axis=0)


def _block_diag_mask(width):
    r = lax.broadcasted_iota(jnp.int32, (width, width), 0) // HEAD_DIM
    c = lax.broadcasted_iota(jnp.int32, (width, width), 1) // HEAD_DIM
    return r == c


def _ref_rows(b, offsets, span):
    width = b.shape[-1]
    return jnp.concatenate([jnp.broadcast_to(b[o:o + 1], (span, width)) for o in offsets], axis=0)


def _run_interleaved(gens):
    live = list(gens)
    while live:
        for g in list(live):
            try:
                next(g)
            except StopIteration:
                live.remove(g)


def _hgrn_chunks_gen(problems, out):
    n = len(problems)
    c, width = problems[0][0].shape
    qs = [p[0] for p in problems]
    vs = [p[2] for p in problems]
    sts = [p[3] for p in problems]
    rev = [p[4] for p in problems]
    chains = range(n)
    masks = _head_masks(width)
    trow =lax.broadcasted_iota(jnp.int32, (c, 1), 0)
    t_full = lax.broadcasted_iota(jnp.int32, (c, width), 0)
    s_full = lax.broadcasted_iota(jnp.int32, (c, width), 1) % c

    ks = [1.0 - p[1] for p in problems]
    b = [jnp.log(p[1]) * math.log2(math.e) for p in problems]
    step = 1
    while step < c:
        for j in chains:
            if rev[j]:
                b[j] = b[j] + jnp.where(trow < c - step, pltpu.roll(b[j], c - step, 0), 0.0)
            else:
                b[j] = b[j] + jnp.where(trow >= step, pltpu.roll(b[j], step, 0), 0.0)
        step *= 2
        yield
    b_end = [b[j][0:1] if rev[j] else b[j][c - 1:c] for j in chains]

    o = [lax.dot_general((qs[j] * jnp.exp2(b[j])).astype(BF16), sts[j].astype(BF16), NT,
                         preferred_element_type=F32) for j in chains]
    yield

    a = [None] * n
    m = c // 2
    while m >= DIAG_BLOCK:
        blocks = c // (2 * m)
        same = (t_full // (2 * m)) == (s_full // (2 * m))
        for j in chains:
            ref = _ref_rows(b[j], [i * 2 * m + (m if rev[j] else m - 1) for i in range(blocks)],
                            2 * m)
            is_q = ((trow % (2 * m)) < m) if rev[j] else ((trow % (2 * m)) >= m)
            e = jnp.exp2((b[j] - ref) * jnp.where(is_q, 1.0, -1.0))
            ql = jnp.where(is_q, qs[j] * e, 0.0).astype(BF16)
            kl = jnp.where(is_q, 0.0, ks[j] * e).astype(BF16)
            al = lax.dot_general(ql, _stack_heads(kl, masks), NT, preferred_element_type=F32)
            if blocks > 1:
                al = jnp.where(same, al, 0.0)
            a[j] = al if a[j] is None else a[j] + al
        m //= 2
        yield
    blocks = c // DIAG_BLOCK
    mid = DIAG_BLOCK // 2
    same = (t_full // DIAG_BLOCK) == (s_full // DIAG_BLOCK)
    for j in chains:
        ref = _ref_rows(b[j], [i * DIAG_BLOCK + (mid if rev[j] else mid - 1) for i in range(blocks)],
                        DIAG_BLOCK)
        d = b[j] - ref
        ql = (qs[j] * jnp.exp2(d)).astype(BF16)
        kl = (ks[j] * jnp.exp2(-d)).astype(BF16)
        al = lax.dot_general(ql, _stack_heads(kl, masks), NT, preferred_element_type=F32)
        causal = (s_full >= t_full) if rev[j] else (s_full <= t_full)
        a[j] = a[j] + jnp.where(same & causal, al, 0.0)
    yield

    v_b = [v.astype(BF16) for v in vs]
    o = [o[j] + _dot(a[j].astype(BF16), _stack_heads(v_b[j], masks)) for j in chains]
    yield

    bd = _block_diag_mask(width)
    upd = [lax.dot_general(v_b[j], (ks[j] * jnp.exp2(b_end[j] - b[j])).astype(BF16), TN,
                           preferred_element_type=F32) for j in chains]
    st_new = [sts[j] * jnp.exp2(b_end[j]) + jnp.where(bd, upd[j], 0.0) for j in chains]
    out.extend(zip(o, st_new))


def _mxu_transpose(x):
    n = x.shape[1]
    r = lax.broadcasted_iota(jnp.int32, (n, n), 0)
    c = lax.broadcasted_iota(jnp.int32, (n, n), 1)
    eye = (r == c).astype(BF16)
    acc = None
    rem = x
    for _ in range(3):
        piece = rem.astype(BF16)
        rem = rem - piece.astype(F32)
        part = lax.dot_general(eye, piece, NT, preferred_element_type=F32)
        acc = part if acc is None else acc + part
    return acc


def _scan_gain(gains_ref, li, width):
    return gains_ref[li:li + 1, gains_ref.shape[1] - width:]


def _mixer_ctx_kernel(*refs, li, n_alias):
    refs = list(refs)
    (qa_ref, kat_ref, va_ref, qc_ref, kct_ref, vct_ref, hq_ref, ff_ref, fb_ref, hv_ref, hg_ref,
     gn_ref, lq1, lk1, lq2, lk2, sub_ref, w1_ref, w2_ref) = refs[:19]
    pos = 19 + n_alias
    oa_ref, oc_ref, ob_ref, sf_ref, sb_ref, w1_o, w2_o = refs[pos:pos + 7]
    st_ref, of_ref, obk_ref = refs[pos + 7:pos + 10]
    w1_o[...] = w1_ref[...].astype(BF16)
    w2_o[...] = w2_ref[...].astype(BF16)
    nb, t, width = hq_ref.shape
    nc = t // CHUNK
    heads_b = width // HEAD_DIM
    heads_a = qa_ref.shape[1] // LANES
    groups_c = qc_ref.shape[1] // LANES // 2
    lam, lam_init = _diff_lambda(lq1, lk1, lq2, lk2, li)
    gain = sub_ref[li:li + 1, :] * (1.0 - lam_init)

    for j in range(2 * nb):
        st_ref[j] = jnp.zeros((width, width), F32)

    def diff_unit(n, h):
        rows = slice(n * t, (n + 1) * t)
        c = slice(h * LANES, (h + 1) * LANES)
        keys = [(kat_ref[n, 0, c, :].astype(BF16), True)]
        vals = [va_ref[n, 0, pl.ds(h, t, stride=heads_a), :].astype(BF16)]

        def store(o):
            oa_ref[rows, c] = o.astype(oa_ref.dtype)

        yield from _diff_head_gen(qa_ref[rows, c], keys, vals, lam, gain, store)

    def gqa_unit(n, g):
        rows = slice(n * t, (n + 1) * t)
        q = jnp.concatenate([qc_ref[rows, (2 * g + j) * LANES:(2 * g + j + 1) * LANES]
                             for j in range(2)], axis=0)
        keys = [(kct_ref[n, 0].astype(BF16), True)]
        vals = [(vct_ref[n, 0].astype(BF16), True)]

        def store(o):
            oc_ref[rows, g * LANES:(g + 1) * LANES] = o.astype(oc_ref.dtype)

        yield from _gqa_group_gen(g, q, keys, vals, store)

    def scan_step(ci):
        rows = (slice(ci * CHUNK, (ci + 1) * CHUNK), slice((nc - 1 - ci) * CHUNK, (nc - ci) * CHUNK))
        loaded = []
        for n in range(nb):
            for d, f_ref in enumerate((ff_ref, fb_ref)):
                r = rows[d]
                loaded.append((hq_ref[n, r, :], f_ref[n, r, :], hv_ref[n, r, :], st_ref[2 * n + d],
                               bool(d)))
        out = []
        yield from _hgrn_chunks_gen(loaded, out)
        for j, (o, st) in enumerate(out):
            n, d = divmod(j, 2)
            (obk_ref if d else of_ref)[n, rows[d], :] = o
            st_ref[j] = st

    units = []
    for n in range(nb):
        units += [diff_unit(n, h) for h in range(heads_a)]
        units += [gqa_unit(n, g) for g in range(groups_c)]
    share = -(-len(units) // nc)
    for ci in range(nc):
        mine = units[ci * share:(ci + 1) * share]
        _run_interleaved([scan_step(ci)] + mine)

    for n in range(nb):
        o = of_ref[n] + obk_ref[n]
        ob_ref[n] = (_group_rms(o, _scan_gain(gn_ref, li, width), HEAD_DIM)
                     * hg_ref[n]).astype(ob_ref.dtype)
        for d, dst in enumerate((sf_ref, sb_ref)):
            st = st_ref[2 * n + d]
            rows = st[0:HEAD_DIM]
            for h in range(1, heads_b):
                rows = rows + st[h * HEAD_DIM:(h + 1) * HEAD_DIM]
            final = _mxu_transpose(rows).reshape(heads_b, HEAD_DIM, HEAD_DIM)
            for slot in range(dst.shape[1]):
                dst[n, slot] = final


def _mixer_ctx(qa, ka_t, va_rows, qc, kc_t, vc_t, hq, ff, fb, hv, hg, gn, lam, subln, ff_weights,
               state_prev, *, li, depth, nb):
    bsz, t, width = hq.shape
    steps = bsz // nb
    heads_b = width // HEAD_DIM
    rows = lambda b: (b, 0)
    at_layer = lambda b: (b, li, 0, 0)
    seq = pl.BlockSpec((nb, t, width), lambda b: (b, 0, 0))
    const = lambda b: (0, 0)
    in_specs = [pl.BlockSpec((nb * t, qa.shape[1]), rows),
                pl.BlockSpec((nb, 1) + ka_t.shape[2:], at_layer),
                pl.BlockSpec((nb, 1) + va_rows.shape[2:], at_layer),
                pl.BlockSpec((nb * t, qc.shape[1]), rows),
                pl.BlockSpec((nb, 1) + kc_t.shape[2:], at_layer),
                pl.BlockSpec((nb, 1) + vc_t.shape[2:], at_layer),
                seq, seq, seq, seq, seq, pl.BlockSpec(gn.shape, const)]
    in_specs += [pl.BlockSpec(a.shape, const) for a in (*lam, subln)]
    ff_blocks = [(w.shape[1] // steps, w.shape[2]) for w in ff_weights]
    in_specs += [pl.BlockSpec((None,) + blk, lambda b: (li, b, 0)) for blk in ff_blocks]
    args = [qa, ka_t, va_rows, qc, kc_t, vc_t, hq, ff, fb, hv, hg, gn, *lam, subln, *ff_weights]
    aliases = {}
    if state_prev is not None:
        for j, buf in enumerate(state_prev):
            aliases[len(args)] = 3 + j
            in_specs.append(pl.BlockSpec(memory_space=pl.ANY))
            args.append(buf)
    slots, slot0 = (depth, 0) if state_prev is None else (1, li)
    state_spec = pl.BlockSpec((nb, slots, heads_b, HEAD_DIM, HEAD_DIM), lambda b: (b, slot0, 0, 0, 0))
    out_specs = [pl.BlockSpec((nb * t, qa.shape[1]), rows),
                 pl.BlockSpec((nb * t, qc.shape[1] // 2), rows), seq, state_spec, state_spec]
    out_specs += [pl.BlockSpec(blk, rows) for blk in ff_blocks]
    out_shape = [jax.ShapeDtypeStruct(qa.shape, BF16),
                 jax.ShapeDtypeStruct((qc.shape[0], qc.shape[1] // 2), BF16),
                 jax.ShapeDtypeStruct((bsz, t, width), BF16)]
    out_shape += [jax.ShapeDtypeStruct((bsz, depth, heads_b, HEAD_DIM, HEAD_DIM), F32)] * 2
    out_shape += [jax.ShapeDtypeStruct(w.shape[1:], BF16) for w in ff_weights]
    return pl.pallas_call(
        functools.partial(_mixer_ctx_kernel, li=li, n_alias=len(aliases)),
        grid=(steps,),
        in_specs=in_specs, out_specs=out_specs, out_shape=out_shape,
        input_output_aliases=aliases,
        scratch_shapes=[pltpu.VMEM((2 * nb, width, width), F32),
                        pltpu.VMEM((nb, t, width), F32), pltpu.VMEM((nb, t, width), F32)],
        compiler_params=_params(1),
        name="mixers_context",
    )(*args)


def _mixer_lat_kernel(qa_ref, k_ref, v_ref, ck_ref, cv_ref, qc_ref, kc_ref, vc_ref, cck_ref, ccv_ref,
                      hq_ref, ff_ref, fb_ref, hv_ref, hg_ref, gn_ref, s0f_ref, s0b_ref,
                      lq1, lk1, lq2, lk2, sub_ref, oa_ref, oc_ref, ob_ref,
                      st_ref, of_ref, obk_ref, *, li, tq):
    _, t, width = hq_ref.shape
    nc = t // CHUNK
    nq = t // tq
    steps_per_trip = nc // nq
    heads_b = width // HEAD_DIM
    heads_a = qa_ref.shape[1] // LANES
    groups_c = qc_ref.shape[1] // LANES // 2
    past = ck_ref.shape[-1]
    lam, lam_init = _diff_lambda(lq1, lk1, lq2, lk2, li)
    gain = sub_ref[li:li + 1, :] * (1.0 - lam_init)
    bd = _block_diag_mask(width)

    for d, src in enumerate((s0f_ref, s0b_ref)):
        x = src[0, 0].reshape(width, HEAD_DIM)
        xt = _mxu_transpose(x)
        st_ref[d] = jnp.where(bd, jnp.concatenate([xt] * heads_b, axis=0), 0.0)

    def trip(qt, carry):
        q_rows = pl.ds(pl.multiple_of(qt * tq, tq), tq)

        def diff_unit(h):
            c = slice(h * LANES, (h + 1) * LANES)
            keys = [(k_ref[0, 0, :, c], False), (ck_ref[0, 0, c, :].astype(BF16), True)]
            vals = [v_ref[0, 0, :, c], cv_ref[0, 0, pl.ds(h, past, stride=heads_a), :].astype(BF16)]

            def store(o):
                oa_ref[q_rows, c] = o.astype(oa_ref.dtype)

            yield from _diff_head_gen(qa_ref[q_rows, c], keys, vals, lam, gain, store)

        def gqa_unit(g):
            q = jnp.concatenate([qc_ref[q_rows, (2 * g + j) * LANES:(2 * g + j + 1) * LANES]
                                 for j in range(2)], axis=0)
            keys = [(kc_ref[0, 0], False), (cck_ref[0, 0].astype(BF16), True)]
            vals = [(vc_ref[0, 0], False), (ccv_ref[0, 0].astype(BF16), True)]

            def store(o):
                oc_ref[q_rows, g * LANES:(g + 1) * LANES] = o.astype(oc_ref.dtype)

            yield from _gqa_group_gen(g, q, keys, vals, store)

        def scan_step(cj):
            ci = qt * steps_per_trip + cj
            rows = (pl.ds(pl.multiple_of(ci * CHUNK, CHUNK), CHUNK),
                    pl.ds(pl.multiple_of((nc - 1 - ci) * CHUNK, CHUNK), CHUNK))
            loaded = [(hq_ref[0, rows[d], :], f_ref[0, rows[d], :], hv_ref[0, rows[d], :],
                       st_ref[d], bool(d)) for d, f_ref in enumerate((ff_ref, fb_ref))]
            out = []
            yield from _hgrn_chunks_gen(loaded, out)
            for d, (o, st) in enumerate(out):
                (obk_ref if d else of_ref)[rows[d], :] = o
                st_ref[d] = st

        units = [diff_unit(h) for h in range(heads_a)] + [gqa_unit(g) for g in range(groups_c)]
        for cj in range(steps_per_trip):
            mine = units[cj::steps_per_trip]
            _run_interleaved([scan_step(cj)] + mine)
        return carry

    lax.fori_loop(0, nq, trip, 0)

    o = of_ref[...] + obk_ref[...]
    ob_ref[0] = (_group_rms(o, _scan_gain(gn_ref, li, width), HEAD_DIM)
                 * hg_ref[0]).astype(ob_ref.dtype)


def _mixer_lat(qa, ka, va, qc, kc, vc, cache, hq, ff, fb, hv, hg, gn, state, lam, subln, *, li, tq):
    bsz, t, width = hq.shape
    rows = lambda b: (b, 0)
    own = lambda b: (b, 0, 0, 0)
    at_layer = lambda b: (b, li, 0, 0)
    seq = pl.BlockSpec((1, t, width), lambda b: (b, 0, 0))
    const = lambda b: (0, 0)
    in_specs = [pl.BlockSpec((t, qa.shape[1]), rows),
                pl.BlockSpec((1,) + ka.shape[1:], own), pl.BlockSpec((1,) + va.shape[1:], own),
                pl.BlockSpec((1, 1) + cache[0].shape[2:], at_layer),
                pl.BlockSpec((1, 1) + cache[1].shape[2:], at_layer),
                pl.BlockSpec((t, qc.shape[1]), rows),
                pl.BlockSpec((1,) + kc.shape[1:], own), pl.BlockSpec((1,) + vc.shape[1:], own),
                pl.BlockSpec((1, 1) + cache[2].shape[2:], at_layer),
                pl.BlockSpec((1, 1) + cache[3].shape[2:], at_layer),
                seq, seq, seq, seq, seq, pl.BlockSpec(gn.shape, const)]
    in_specs += [pl.BlockSpec((1, 1) + state[0].shape[2:], lambda b: (b, li, 0, 0, 0))] * 2
    in_specs += [pl.BlockSpec(a.shape, const) for a in (*lam, subln)]
    args = [qa, ka, va, cache[0], cache[1], qc, kc, vc, cache[2], cache[3], hq, ff, fb, hv, hg, gn,
            *state, *lam, subln]
    return pl.pallas_call(
        functools.partial(_mixer_lat_kernel, li=li, tq=tq),
        grid=(bsz,),
        in_specs=in_specs,
        out_specs=[pl.BlockSpec((t, qa.shape[1]), rows),
                   pl.BlockSpec((t, qc.shape[1] // 2), rows), seq],
        out_shape=[jax.ShapeDtypeStruct(qa.shape, BF16),
                   jax.ShapeDtypeStruct((qc.shape[0], qc.shape[1] // 2), BF16),
                   jax.ShapeDtypeStruct((bsz, t, width), BF16)],
        scratch_shapes=[pltpu.VMEM((2, width, width), F32),
                        pltpu.VMEM((t, width), F32), pltpu.VMEM((t, width), F32)],
        compiler_params=_params(1),
        name="mixers_latent",
    )(*args)


def _out_mlp_kernel(x_ref, oa_ref, ob_ref, oc_ref, mod_ref, wo_ref, w1_ref, w2_ref,
                    g1_ref, b1_ref, g2_ref, b2_ref, y_ref, *, li, d, alpha, ff_chunk):
    wa, wb = oa_ref.shape[-1], ob_ref.shape[-1]
    layer = slice(li, li + 1)
    gate1 = mod_ref[0, :, 2 * d:3 * d]
    shift2 = mod_ref[0, :, 3 * d:4 * d]
    gain2 = mod_ref[0, :, 4 * d:5 * d]
    gate2 = mod_ref[0, :, 5 * d:6 * d]
    subs = [slice(s * ROW_TILE, (s + 1) * ROW_TILE) for s in range(x_ref.shape[0] // ROW_TILE)]
    wo = wo_ref[...].astype(BF16)
    m = [_dot(oa_ref[r, :], wo[0:wa, :]) + _dot(ob_ref[r, :], wo[wa:wa + wb, :])
         + _dot(oc_ref[r, :], wo[wa + wb:, :]) for r in subs]
    x1 = [_layernorm(alpha * x_ref[r, :] + gate1 * mi, g1_ref[layer, :], b1_ref[layer, :])
          for r, mi in zip(subs, m)]
    h2 = [(xi * (1.0 + gain2) + shift2).astype(BF16) for xi in x1]
    acc = [None] * len(subs)
    for j in range(w1_ref.shape[-1] // ff_chunk):
        cols = slice(j * ff_chunk, (j + 1) * ff_chunk)
        hid = [jnp.maximum(_dot(hi, w1_ref[:, cols]), 0.0) for hi in h2]
        for s, hd in enumerate(hid):
            part = _dot((hd * hd).astype(BF16), w2_ref[cols, :])
            acc[s] = part if acc[s] is None else acc[s] + part
    for r, xi, ai in zip(subs, x1, acc):
        y_ref[r, :] = _layernorm(alpha * xi + gate2 * ai, g2_ref[layer, :], b2_ref[layer, :])


def _out_mlp(x, oa, ob, oc, mod, mod_row, w_out, w_ff1, w_ff2, ln, *, li, alpha):
    bsz, t, d = x.shape
    rows = 2 * ROW_TILE
    row = lambda i: (i, 0)
    const = lambda i: (0, 0)
    resident = lambda a: pl.BlockSpec(a.shape, const, pipeline_mode=pl.Buffered(1))
    in_specs = [pl.BlockSpec((rows, d), row),
                pl.BlockSpec((rows, oa.shape[-1]), row),
                pl.BlockSpec((rows, ob.shape[-1]), row),
                pl.BlockSpec((rows, oc.shape[-1]), row),
                pl.BlockSpec((1, 1, mod.shape[-1]), lambda i: (mod_row(i * rows), 0, 0)),
                pl.BlockSpec((None,) + w_out.shape[1:], lambda i: (li, 0, 0),
                             pipeline_mode=pl.Buffered(1)),
                resident(w_ff1), resident(w_ff2)]
    in_specs += [pl.BlockSpec(a.shape, const) for a in ln]
    y = pl.pallas_call(
        functools.partial(_out_mlp_kernel, li=li, d=d, alpha=alpha, ff_chunk=1024),
        grid=(bsz * t // rows,),
        in_specs=in_specs,
        out_specs=pl.BlockSpec((rows, d), row),
        out_shape=jax.ShapeDtypeStruct((bsz * t, d), F32),
        compiler_params=_params(1),
        name="out_mlp",
    )(x.reshape(bsz * t, d), oa, ob.reshape(bsz * t, -1), oc, mod, w_out, w_ff1, w_ff2, *ln)
    return y.reshape(bsz, t, d)


def _rope_tables(n_tokens):
    pairs = HEAD_DIM // 4
    tok = np.arange(n_tokens)
    row = (tok // GRID_W).astype(np.float64)
    col = (tok % GRID_W).astype(np.float64)
    inv = ROPE_THETA ** (-np.arange(pairs, dtype=np.float64) / pairs)
    ang = np.concatenate([row[:, None] * inv, col[:, None] * inv], axis=-1)
    lane = np.arange(LANES)
    pair = (lane % HEAD_DIM) // 2
    sign = np.where(lane % 2 == 0, -1.0, 1.0)
    return (jnp.asarray(np.cos(ang)[:, pair], F32), jnp.asarray(np.sin(ang)[:, pair] * sign, F32))


def kernel(x_prompt, x_sample, cache_a_k, cache_a_v, cache_c_k, cache_c_v, state_b_fwd, state_b_bwd, c, c_ctx, w_ada, b_ada, w_in, w_out, lam_q1, lam_k1, lam_q2, lam_k2, subln_g, lb_logits_fwd, lb_logits_bwd, gnorm_g, qnorm_g, knorm_g, ln1_g, ln1_b, ln2_g, ln2_b, w_ff1, w_ff2):
    depth = w_in.shape[0]
    bsz, seq, d = x_prompt.shape
    dec_bsz, dec_seq, _ = x_sample.shape
    past = cache_a_k.shape[2]
    alpha = (2 * depth) ** 0.25
    mix_a, mix_b, mix_c = d // 2, d // 4, d // 4

    mod = _modulation(c_ctx, c, w_ada, b_ada)
    rope = _rope_tables(dec_seq)

    cache = (cache_a_k.transpose(0, 1, 3, 4, 5, 2).reshape(dec_bsz, depth, mix_a, past),
             cache_a_v.reshape(dec_bsz, depth, past * (mix_a // LANES), LANES),
             cache_c_k.transpose(0, 1, 3, 4, 2).reshape(dec_bsz, depth, mix_c // 2, past),
             cache_c_v.transpose(0, 1, 3, 4, 2).reshape(dec_bsz, depth, mix_c // 2, past))
    lam = (lam_q1, lam_k1, lam_q2, lam_k2)

    ff_b = []
    tiles = (mix_c // HEAD_DIM, mix_c // 2 // HEAD_DIM, mix_b // HEAD_DIM)
    index = np.concatenate([g * HEAD_DIM + np.tile(np.arange(HEAD_DIM), n)
                            for g, n in enumerate(tiles)])
    gn = jnp.take(jnp.concatenate([qnorm_g, knorm_g, gnorm_g], axis=1), index, axis=1)
    ln = (ln1_g, ln1_b, ln2_g, ln2_b)

    def stream(x, li, latent, own_prev):
        n, t, _ = x.shape
        mod_row = ((lambda r0: li * MOD_ROWS + 1 + r0 // t) if latent
                   else (lambda r0: li * MOD_ROWS))
        (qa, hq, ff, fb, hv, hg, qc, ka, va, kc, vc) = _in_proj(
            x, mod, mod_row, w_in, lb_logits_fwd, lb_logits_bwd, gn,
            rope if latent else None, None if own_prev is None else own_prev[0:4], li=li)
        if latent:
            oa, oc, ob = _mixer_lat(qa, ka, va, qc, kc, vc, cache, hq, ff, fb, hv, hg, gn,
                                    (state_b_fwd, state_b_bwd), lam, subln_g, li=li, tq=256)
            own = None
        else:
            oa, oc, ob, s_f, s_b, w1_b, w2_b = _mixer_ctx(
                qa, ka, va, qc, kc, vc, hq, ff, fb, hv, hg, gn, lam, subln_g, (w_ff1, w_ff2),
                None if own_prev is None else own_prev[4:6], li=li, depth=depth, nb=2)
            own = (ka, va, kc, vc, s_f, s_b)
            ff_b.append((w1_b, w2_b))
        y = _out_mlp(x, oa, ob, oc, mod, mod_row, w_out, *ff_b[li], ln, li=li, alpha=alpha)
        return y, own

    y_prompt, y_sample = x_prompt, x_sample
    own = None
    for li in range(depth):
        y_prompt, own = stream(y_prompt, li, False, own)
        y_sample, _ = stream(y_sample, li, True, None)

    heads_a = mix_a // (2 * HEAD_DIM)
    new_a_k = own[0].reshape(bsz, depth, heads_a, 2, HEAD_DIM, seq).transpose(0, 1, 5, 2, 3, 4)
    new_a_v = own[1].reshape(bsz, depth, seq, heads_a, 2 * HEAD_DIM)
    kv_heads = mix_c // 2 // HEAD_DIM
    new_c_k = own[2].reshape(bsz, depth, kv_heads, HEAD_DIM, seq).transpose(0, 1, 4, 2, 3)
    new_c_v = own[3].reshape(bsz, depth, kv_heads, HEAD_DIM, seq).transpose(0, 1, 4, 2, 3)
    return (y_prompt, y_sample, new_a_k, new_a_v, new_c_k, new_c_v, own[4], own[5])
```
